```python
import math
import jax
import jax.numpy as jnp
from jax import lax
import numpy as np

D_MODEL = 1024
BATCH = 32
SEQ = 256
DEPTH = 4
DEC_BATCH = 2
DEC_SEQ = 1024
PAST_LEN = 256

GRID_W = 64
N_MIXERS = 2
N_HGRN_LAYERS = (DEPTH + 1) // 2
N_ATTN_LAYERS = DEPTH // 2
HGRN_HEADS = D_MODEL // 128
HGRN_DK = D_MODEL // HGRN_HEADS
HGRN_DV = D_MODEL // HGRN_HEADS
HGRN_CHUNK = 32
DIFF_HEADS = 8
DIFF_QK_DIM = D_MODEL // (2 * DIFF_HEADS)
DIFF_V_DIM = D_MODEL // DIFF_HEADS
ROPE_THETA = 10000.0
Q_BLOCK = 128
MOE_GROUPS = 4
MOE_EXPERTS_PER_GROUP = 4
MOE_N_EXPERTS = MOE_GROUPS * MOE_EXPERTS_PER_GROUP
MOE_TOP_K = 2
MOE_D_FF = D_MODEL // 2
EPS = 1e-6

kernel_name = "hybrid_hgrn2_diffattn_hmoe_diffusion_step"


def rmsnorm(x, g):
    xf = x.astype(jnp.float32)
    y = xf * lax.rsqrt(jnp.mean(xf * xf, axis=-1, keepdims=True) + EPS)
    return (y * g.astype(jnp.float32)).astype(x.dtype)


def modulation(cvec, w_ada, b_ada):
    m = jax.nn.silu(cvec) @ w_ada + b_ada
    return jnp.split(m, 6, axis=-1)


def hgrn2_chunk_scan(q, k, v, logf, s0):
    T = q.shape[-2]
    n = T // HGRN_CHUNK

    def chunks(a):
        a = a.reshape(a.shape[:-2] + (n, HGRN_CHUNK, a.shape[-1]))
        return jnp.moveaxis(a, -3, 0)

    causal = jnp.tril(jnp.ones((HGRN_CHUNK, HGRN_CHUNK), dtype=bool))[:, :, None]

    def step(S, inp):
        qc, kc, vc, gc = inp
        b = jnp.cumsum(gc, axis=-2)
        rel = b[..., :, None, :] - b[..., None, :, :]
        dec = jnp.exp(jnp.where(causal, rel, -jnp.inf))
        att = jnp.einsum('...td,...sd,...tsd->...ts', qc, kc, dec)
        o = att @ vc + jnp.einsum('...td,...dv->...tv', qc * jnp.exp(b), S)
        b_last = b[..., -1:, :]
        S_new = (jnp.exp(b_last[..., 0, :])[..., None] * S
                 + jnp.einsum('...sd,...sv->...dv', kc * jnp.exp(b_last - b), vc))
        return S_new, o

    s_fin, o = lax.scan(step, s0, (chunks(q), chunks(k), chunks(v), chunks(logf)))
    o = jnp.moveaxis(o, 0, -3)
    return o.reshape(o.shape[:-3] + (T, o.shape[-1])), s_fin


def hgrn2_mixer(u, w_in, lb, onorm_g, w_out, s0):
    Bn, T, _ = u.shape
    q, inp, zf, zb, g = jnp.split(u @ w_in, 5, axis=-1)

    def heads(a):
        return a.reshape(Bn, T, HGRN_HEADS, -1).transpose(0, 2, 1, 3).astype(jnp.float32)

    def bidir(a_f, a_b):
        return jnp.stack([heads(a_f), jnp.flip(heads(a_b), axis=2)], axis=1)

    z = bidir(zf, zb)
    lbd = lb.astype(jnp.float32).reshape(2, HGRN_HEADS, 1, HGRN_DK)
    logf = jnp.log(lbd + (1.0 - lbd) * jax.nn.sigmoid(z))
    k = (1.0 - lbd) * jax.nn.sigmoid(-z)
    o, s_fin = hgrn2_chunk_scan(bidir(q, q), k, bidir(inp, inp), logf, s0.astype(jnp.float32))
    o = o[:, 0] + jnp.flip(o[:, 1], axis=2)
    o = o.transpose(0, 2, 1, 3).astype(u.dtype)
    o = rmsnorm(o, onorm_g) * jax.nn.silu(g.reshape(Bn, T, HGRN_HEADS, HGRN_DV))
    return o.reshape(Bn, T, D_MODEL) @ w_out, s_fin.astype(u.dtype)


def diff_attn_qkv(u, w_qkv, qn_g, kn_g):
    Bn, T, _ = u.shape
    q, k, v = jnp.split(u @ w_qkv, 3, axis=-1)
    q = rmsnorm(q.reshape(Bn, T, DIFF_HEADS, 2, DIFF_QK_DIM), qn_g)
    k = rmsnorm(k.reshape(Bn, T, DIFF_HEADS, 2, DIFF_QK_DIM), kn_g)
    v = v.reshape(Bn, T, DIFF_HEADS, DIFF_V_DIM)
    return q, k, v


def diff_lambda(lam_p, layer_idx):
    lf = lam_p.astype(jnp.float32)
    lam_init = 0.8 - 0.6 * math.exp(-0.3 * layer_idx)
    lam = jnp.exp(jnp.sum(lf[0] * lf[1])) - jnp.exp(jnp.sum(lf[2] * lf[3])) + lam_init
    return lam, lam_init


def diff_attention(q, k, v, lam):
    Bn, Tq = q.shape[0], q.shape[1]
    nb = Tq // Q_BLOCK
    qb = jnp.moveaxis(q.reshape(Bn, nb, Q_BLOCK, DIFF_HEADS, 2, DIFF_QK_DIM), 1, 0)
    scale = DIFF_QK_DIM ** -0.5

    def block(qi):
        s = jnp.einsum('bqhcd,bkhcd->bhcqk', qi, k).astype(jnp.float32) * scale
        p = jax.nn.softmax(s, axis=-1)
        a = (p[:, :, 0] - lam * p[:, :, 1]).astype(v.dtype)
        return jnp.einsum('bhqk,bkhv->bqhv', a, v)

    o = lax.map(block, qb)
    return jnp.moveaxis(o, 0, 1).reshape(Bn, Tq, DIFF_HEADS, DIFF_V_DIM)


def diff_attn_out(o, subln_g, lam_init, w_out):
    Bn, T = o.shape[0], o.shape[1]
    o = rmsnorm(o, subln_g) * (1.0 - lam_init)
    return o.reshape(Bn, T, D_MODEL) @ w_out


def rope_1d(x, ang):
    x1, x2 = jnp.split(x, 2, axis=-1)
    c = jnp.cos(ang).astype(x.dtype)
    s = jnp.sin(ang).astype(x.dtype)
    return jnp.concatenate([x1 * c - x2 * s, x2 * c + x1 * s], axis=-1)


def axial_rope(x, ang_row, ang_col):
    xr, xc = jnp.split(x, 2, axis=-1)
    return jnp.concatenate([rope_1d(xr, ang_row), rope_1d(xc, ang_col)], axis=-1)


def hier_moe(u, w_group, b_group, w_expert, b_expert, w_gate, w_up, w_down):
    shp = u.shape
    xt = u.reshape(-1, D_MODEL)
    g_prob = jax.nn.softmax((xt @ w_group + b_group).astype(jnp.float32), axis=-1)
    g_w, g_idx = lax.top_k(g_prob, 1)
    e_logits = (xt @ w_expert + b_expert).astype(jnp.float32).reshape(-1, MOE_GROUPS, MOE_EXPERTS_PER_GROUP)
    e_sel = jnp.take_along_axis(e_logits, g_idx[:, :, None], axis=1)[:, 0]
    e_w, e_idx = lax.top_k(jax.nn.softmax(e_sel, axis=-1), MOE_TOP_K)
    e_w = e_w / jnp.sum(e_w, axis=-1, keepdims=True)
    expert_id = g_idx * MOE_EXPERTS_PER_GROUP + e_idx
    combine = jnp.sum(jax.nn.one_hot(expert_id, MOE_N_EXPERTS, dtype=jnp.float32)
                      * (g_w * e_w)[..., None], axis=1)
    h = jax.nn.silu(jnp.einsum('nd,edf->nef', xt, w_gate)) * jnp.einsum('nd,edf->nef', xt, w_up)
    h = h * combine.astype(h.dtype)[:, :, None]
    y = jnp.einsum('nef,efd->nd', h, w_down)
    return y.reshape(shp)


def setup_inputs(seed: int = 0) -> dict:
    key = jax.random.key(seed)
    ks = jax.random.split(key, 32)
    f32 = jnp.float32
    D = D_MODEL

    def nrm(k, shape, scale):
        return jax.random.normal(k, shape, f32) * scale

    return {
        "x_prompt": nrm(ks[0], (BATCH, SEQ, D), 1.0),
        "x_sample": nrm(ks[1], (DEC_BATCH, DEC_SEQ, D), 1.0),
        "c": nrm(ks[2], (DEC_BATCH, D), 1.0),
        "cache_k": nrm(ks[3], (DEC_BATCH, N_ATTN_LAYERS, PAST_LEN, DIFF_HEADS, 2, DIFF_QK_DIM), 1.0),
        "cache_v": nrm(ks[4], (DEC_BATCH, N_ATTN_LAYERS, PAST_LEN, DIFF_HEADS, DIFF_V_DIM), 1.0),
        "state_hgrn": nrm(ks[5], (DEC_BATCH, N_HGRN_LAYERS, 2, HGRN_HEADS, HGRN_DK, HGRN_DV), 0.5),
        "c_ctx": nrm(ks[6], (D,), 1.0),
        "norm_g": 1.0 + nrm(ks[7], (DEPTH, 2, D), 0.01),
        "w_ada": nrm(ks[8], (DEPTH, D, 6 * D), 0.5 * D ** -0.5),
        "b_ada": nrm(ks[9], (DEPTH, 6 * D), 0.01),
        "hgrn_w_in": nrm(ks[10], (N_HGRN_LAYERS, D, 5 * D), D ** -0.5),
        "hgrn_lb_logits": nrm(ks[11], (N_HGRN_LAYERS, 2, D), 0.5),
        "hgrn_onorm_g": 1.0 + nrm(ks[12], (N_HGRN_LAYERS, HGRN_DV), 0.01),
        "hgrn_w_out": nrm(ks[13], (N_HGRN_LAYERS, D, D), D ** -0.5),
        "attn_w_qkv": nrm(ks[14], (N_ATTN_LAYERS, D, 3 * D), D ** -0.5),
        "attn_qn_g": 1.0 + nrm(ks[15], (N_ATTN_LAYERS, DIFF_QK_DIM), 0.01),
        "attn_kn_g": 1.0 + nrm(ks[16], (N_ATTN_LAYERS, DIFF_QK_DIM), 0.01),
        "attn_lambda": nrm(ks[17], (N_ATTN_LAYERS, 4, DIFF_QK_DIM), 0.1),
        "attn_subln_g": 1.0 + nrm(ks[18], (N_ATTN_LAYERS, DIFF_V_DIM), 0.01),
        "attn_w_out": nrm(ks[19], (N_ATTN_LAYERS, D, D), D ** -0.5),
        "moe_w_group": nrm(ks[20], (DEPTH, D, MOE_GROUPS), D ** -0.5),
        "moe_b_group": nrm(ks[21], (DEPTH, MOE_GROUPS), 0.01),
        "moe_w_expert": nrm(ks[22], (DEPTH, D, MOE_N_EXPERTS), D ** -0.5),
        "moe_b_expert": nrm(ks[23], (DEPTH, MOE_N_EXPERTS), 0.01),
        "moe_w_gate": nrm(ks[24], (DEPTH, MOE_N_EXPERTS, D, MOE_D_FF), D ** -0.5),
        "moe_w_up": nrm(ks[25], (DEPTH, MOE_N_EXPERTS, D, MOE_D_FF), D ** -0.5),
        "moe_w_down": nrm(ks[26], (DEPTH, MOE_N_EXPERTS, MOE_D_FF, D), MOE_D_FF ** -0.5),
    }


def reference(x_prompt, x_sample, c, cache_k, cache_v, state_hgrn, c_ctx,
              norm_g, w_ada, b_ada,
              hgrn_w_in, hgrn_lb_logits, hgrn_onorm_g, hgrn_w_out,
              attn_w_qkv, attn_qn_g, attn_kn_g, attn_lambda, attn_subln_g, attn_w_out,
              moe_w_group, moe_b_group, moe_w_expert, moe_b_expert,
              moe_w_gate, moe_w_up, moe_w_down):
    Bp = x_prompt.shape[0]
    t_lat = x_sample.shape[1]
    rows = t_lat // GRID_W
    row = jnp.repeat(jnp.arange(rows), GRID_W).astype(jnp.float32)
    col = jnp.tile(jnp.arange(GRID_W), rows).astype(jnp.float32)
    half = DIFF_QK_DIM // 2
    inv_freq = ROPE_THETA ** (-jnp.arange(0, half, 2, dtype=jnp.float32) / half)
    ang_row = (row[:, None] * inv_freq).reshape(t_lat, 1, 1, -1)
    ang_col = (col[:, None] * inv_freq).reshape(t_lat, 1, 1, -1)
    lbs = jnp.cumsum(jax.nn.softmax(hgrn_lb_logits.astype(jnp.float32), axis=0), axis=0)
    lbs = lbs - lbs[0:1]

    xp, xs = x_prompt, x_sample
    new_k, new_v, new_s = [], [], []
    for i in range(DEPTH):
        j = i // N_MIXERS
        sh1p, sc1p, g1p, sh2p, sc2p, g2p = modulation(c_ctx, w_ada[i], b_ada[i])
        sh1s, sc1s, g1s, sh2s, sc2s, g2s = [m[:, None, :] for m in modulation(c, w_ada[i], b_ada[i])]
        hp = rmsnorm(xp, norm_g[i, 0]) * (1.0 + sc1p) + sh1p
        hs = rmsnorm(xs, norm_g[i, 0]) * (1.0 + sc1s) + sh1s
        if i % N_MIXERS == 0:
            s0 = jnp.zeros((Bp, 2, HGRN_HEADS, HGRN_DK, HGRN_DV), jnp.float32)
            yp, sp = hgrn2_mixer(hp, hgrn_w_in[j], lbs[j], hgrn_onorm_g[j], hgrn_w_out[j], s0)
            ys, _ = hgrn2_mixer(hs, hgrn_w_in[j], lbs[j], hgrn_onorm_g[j], hgrn_w_out[j], state_hgrn[:, j])
            new_s.append(sp)
        else:
            lam, lam_init = diff_lambda(attn_lambda[j], i)
            qp, kp, vp = diff_attn_qkv(hp, attn_w_qkv[j], attn_qn_g[j], attn_kn_g[j])
            yp = diff_attn_out(diff_attention(qp, kp, vp, lam), attn_subln_g[j], lam_init, attn_w_out[j])
            qs, ks_, vs = diff_attn_qkv(hs, attn_w_qkv[j], attn_qn_g[j], attn_kn_g[j])
            qs = axial_rope(qs, ang_row, ang_col)
            ks_ = axial_rope(ks_, ang_row, ang_col)
            k_all = jnp.concatenate([ks_, cache_k[:, j]], axis=1)
            v_all = jnp.concatenate([vs, cache_v[:, j]], axis=1)
            ys = diff_attn_out(diff_attention(qs, k_all, v_all, lam), attn_subln_g[j], lam_init, attn_w_out[j])
            new_k.append(kp)
            new_v.append(vp)
        xp = xp + g1p * yp
        xs = xs + g1s * ys
        hp = rmsnorm(xp, norm_g[i, 1]) * (1.0 + sc2p) + sh2p
        hs = rmsnorm(xs, norm_g[i, 1]) * (1.0 + sc2s) + sh2s
        xp = xp + g2p * hier_moe(hp, moe_w_group[i], moe_b_group[i], moe_w_expert[i], moe_b_expert[i],
                                 moe_w_gate[i], moe_w_up[i], moe_w_down[i])
        xs = xs + g2s * hier_moe(hs, moe_w_group[i], moe_b_group[i], moe_w_expert[i], moe_b_expert[i],
                                 moe_w_gate[i], moe_w_up[i], moe_w_down[i])

    return (xp, xs, jnp.stack(new_k, axis=1), jnp.stack(new_v, axis=1), jnp.stack(new_s, axis=1))
```

```python
import functools
import math

import numpy as np
import jax
import jax.numpy as jnp
from jax import lax
from jax.experimental import pallas as pl
from jax.experimental.pallas import tpu as pltpu

F32 = jnp.float32
BF16 = jnp.bfloat16

D_MODEL = 1024
DEPTH = 4
GRID_W = 64
HEADS = 8
HEAD_DIM = 128
QK_DIM = 64
ROPE_THETA = 10000.0
MOE_GROUPS = 4
MOE_EPG = 4
MOE_D_FF = 512
EPS = 1e-6
N_COND = 8
N_MOD = 6

LANES = 128
BLK = 256
CHUNK = 32
N_CHUNK = BLK // CHUNK
SUPER = 8
SUPER_ROWS = SUPER * BLK
EXP_CLAMP = 80.0
VMEM_LIMIT = 56 * 1024 * 1024


def _cparams(sem):
    return pltpu.CompilerParams(dimension_semantics=sem, vmem_limit_bytes=VMEM_LIMIT)


def _silu(x):
    return x * jax.nn.sigmoid(x)


def _dot(a, b):
    return jnp.dot(a, b, preferred_element_type=F32)


def _dot_nt(a, b):
    return lax.dot_general(a, b, (((1,), (1,)), ((), ())), preferred_element_type=F32)


def _dot_tn(a, b):
    return lax.dot_general(a, b, (((0,), (0,)), ((), ())), preferred_element_type=F32)


def _mod_kernel(c_ref, w_ref, b_ref, o_ref):
    o_ref[...] = _dot(_silu(c_ref[...]), w_ref[...]) + b_ref[...]


def _modulation(cond, w_ada, b_ada):
    tn = 1536
    nj = (N_MOD * D_MODEL) // tn
    return pl.pallas_call(
        _mod_kernel,
        grid=(DEPTH, nj),
        in_specs=[
            pl.BlockSpec((N_COND, D_MODEL), lambda l, j: (0, 0)),
            pl.BlockSpec((None, D_MODEL, tn), lambda l, j: (l, 0, j)),
            pl.BlockSpec((None, 1, tn), lambda l, j: (l, 0, j)),
        ],
        out_specs=pl.BlockSpec((None, N_COND, tn), lambda l, j: (l, 0, j)),
        out_shape=jax.ShapeDtypeStruct((DEPTH, N_COND, N_MOD * D_MODEL), F32),
        compiler_params=_cparams(("parallel", "parallel")),
        name="modulation",
    )(cond, w_ada, b_ada.reshape(DEPTH, 1, N_MOD * D_MODEL))


def _cond_of_row(row, n_prompt, dec_seq):
    return jnp.where(row < n_prompt, 0, 1 + (row - n_prompt) // dec_seq)


def _norm_mod(x, g, sc, sh):
    ms = jnp.mean(x * x, axis=-1, keepdims=True)
    return (x * lax.rsqrt(ms + EPS) * g) * (1.0 + sc) + sh


def _nm_matmul_kernel(x_ref, g_ref, sh_ref, sc_ref, w_ref, o_ref, h_ref):
    @pl.when(pl.program_id(1) == 0)
    def _():
        h_ref[...] = _norm_mod(x_ref[...], g_ref[...], sc_ref[...], sh_ref[...]).astype(BF16)

    o_ref[...] = _dot(h_ref[...], w_ref[...])


def _nm_matmul(x, g, modr, w, n_prompt, dec_seq, which_shift):
    n, d = x.shape
    f = w.shape[1]
    tm, tn = 1024, 512
    cond = lambda i: _cond_of_row(i * tm, n_prompt, dec_seq)
    return pl.pallas_call(
        _nm_matmul_kernel,
        grid=(n // tm, f // tn),
        in_specs=[
            pl.BlockSpec((tm, d), lambda i, j: (i, 0)),
            pl.BlockSpec((1, d), lambda i, j: (0, 0)),
            pl.BlockSpec((None, 1, d), lambda i, j: (cond(i) * N_MOD + which_shift, 0, 0)),
            pl.BlockSpec((None, 1, d), lambda i, j: (cond(i) * N_MOD + which_shift + 1, 0, 0)),
            pl.BlockSpec((d, tn), lambda i, j: (0, j)),
        ],
        out_specs=pl.BlockSpec((tm, tn), lambda i, j: (i, j)),
        out_shape=jax.ShapeDtypeStruct((n, f), F32),
        scratch_shapes=[pltpu.VMEM((tm, d), BF16)],
        compiler_params=_cparams(("parallel", "arbitrary")),
        name="norm_mod_matmul",
    )(x, g.reshape(1, d), modr, modr, w)


def _out_matmul_kernel(x_ref, op_ref, os_ref, gate_ref, w_ref, o_ref, *, n_prompt_tiles):
    i = pl.program_id(0)

    def run(src_ref):
        o_ref[...] = x_ref[...] + gate_ref[...] * _dot(src_ref[...], w_ref[...])

    pl.when(i < n_prompt_tiles)(lambda: run(op_ref))
    pl.when(i >= n_prompt_tiles)(lambda: run(os_ref))


def _out_matmul(x, o_p, o_s, modr, w, n_prompt, dec_seq):
    n, d = x.shape
    tm = 512
    npt = n_prompt // tm
    cond = lambda i: _cond_of_row(i * tm, n_prompt, dec_seq)
    return pl.pallas_call(
        functools.partial(_out_matmul_kernel, n_prompt_tiles=npt),
        grid=(n // tm,),
        in_specs=[
            pl.BlockSpec((tm, d), lambda i: (i, 0)),
            pl.BlockSpec((tm, d), lambda i: (jnp.minimum(i, npt - 1), 0)),
            pl.BlockSpec((tm, d), lambda i: (jnp.maximum(i - npt, 0), 0)),
            pl.BlockSpec((None, 1, d), lambda i: (cond(i) * N_MOD + 2, 0, 0)),
            pl.BlockSpec((d, d), lambda i: (0, 0)),
        ],
        out_specs=pl.BlockSpec((tm, d), lambda i: (i, 0)),
        out_shape=jax.ShapeDtypeStruct((n, d), F32),
        compiler_params=_cparams(("parallel",)),
        name="out_matmul_residual",
    )(x, o_p, o_s, modr, w)


def _scan_constants():
    t = np.arange(BLK)
    out = []
    for rev in (False, True):
        u = (BLK - 1 - t) if rev else t
        ut, us = u[:, None], u[None, :]
        cums = (us <= ut).astype(np.float32)
        lev = np.where(us > ut, 0,
              np.where(ut // CHUNK == us // CHUNK, 1,
              np.where(ut // 64 == us // 64, 2,
              np.where(ut // 128 == us // 128, 3, 4)))).astype(np.int32)
        out.append((jnp.asarray(cums, BF16), jnp.asarray(lev)))
    return out


def _rows_to_block(rows, rev):
    order = rows[::-1] if rev else rows
    return jnp.concatenate([jnp.broadcast_to(r, (CHUNK, LANES)) for r in order], axis=0)


def _scan_dir(q, v, z, lb, st_prev, cums, lev, rev):
    sig = jax.nn.sigmoid(z)
    f = lb + (1.0 - lb) * sig
    logf = jnp.log(f)
    k = (1.0 - lb) * (1.0 - sig)
    hi = logf.astype(BF16)
    lo = (logf - hi.astype(F32)).astype(BF16)
    bb = _dot(cums, jnp.concatenate([hi, lo], axis=1))
    b = bb[:, :LANES] + bb[:, LANES:]

    e_row, m_row = (0, CHUNK // 2) if rev else (CHUNK - 1, CHUNK // 2 - 1)
    ends, mids = [], []
    for j in range(N_CHUNK):
        ends.append(b[j * CHUNK + e_row:j * CHUNK + e_row + 1, :])
        mids.append(b[j * CHUNK + m_row:j * CHUNK + m_row + 1, :])
    if rev:
        ends, mids = ends[::-1], mids[::-1]
    zero = jnp.zeros((1, LANES), F32)
    pres = [zero] + ends[:-1]
    b_pre = _rows_to_block(pres, rev)
    b_end = _rows_to_block(ends, rev)
    b_mid = _rows_to_block(mids, rev)

    qd = q * jnp.exp(b - b_pre)
    ku = k * jnp.exp(b_end - b)
    qm = q * jnp.exp(jnp.clip(b - b_mid, -EXP_CLAMP, EXP_CLAMP))
    km = k * jnp.exp(jnp.clip(b_mid - b, -EXP_CLAMP, EXP_CLAMP))

    att = jnp.where(lev == 1, _dot_nt(qm.astype(BF16), km.astype(BF16)), 0.0)
    att = jnp.where(lev == 2, _dot_nt(qd.astype(BF16), ku.astype(BF16)), att)
    for level, nc in ((3, 4), (4, 8)):
        fq, fk = [], []
        for ju in range(N_CHUNK):
            r = (ju // nc) * nc + nc // 2 - 1
            if ju % nc >= nc // 2:
                fq.append(pres[ju] - ends[r])
                fk.append(None)
            else:
                fq.append(None)
                fk.append(ends[r] - ends[ju])
        fq = [zero if a is None else jnp.exp(a) for a in fq]
        fk = [zero if a is None else jnp.exp(a) for a in fk]
        ql = qd * _rows_to_block(fq, rev)
        kl = ku * _rows_to_block(fk, rev)
        att = jnp.where(lev == level, _dot_nt(ql.astype(BF16), kl.astype(BF16)), att)

    last = ends[-1]
    o_inter = None
    if st_prev is not None:
        qh = qd * _rows_to_block([jnp.exp(p) for p in pres], rev)
        o_inter = _dot_nt(qh.astype(BF16), st_prev.astype(BF16))
    kh = ku * _rows_to_block([jnp.exp(last - e) for e in ends], rev)
    ut = _dot_tn(v.astype(BF16), kh.astype(BF16))
    st_new = ut if st_prev is None else st_prev * jnp.exp(last) + ut
    return att, o_inter, st_new


def _scan_block(q, v, zf, zb, lb, st_f, st_b, consts):
    (cums_f, lev_f), (cums_b, lev_b) = consts
    att_f, oi_f, st_f = _scan_dir(q, v, zf, lb[0:1, :], st_f, cums_f, lev_f, False)
    att_b, oi_b, st_b = _scan_dir(q, v, zb, lb[1:2, :], st_b, cums_b, lev_b, True)
    o = _dot((att_f + att_b).astype(BF16), v.astype(BF16))
    if oi_f is not None:
        o = o + oi_f + oi_b
    return o, st_f, st_b


def _scan_finish(o, g, on):
    ms = jnp.mean(o * o, axis=-1, keepdims=True)
    return ((o * lax.rsqrt(ms + EPS) * on) * _silu(g)).astype(BF16)


def _hgrn_prompt_kernel(q_ref, v_ref, zf_ref, zb_ref, g_ref, lb_ref, on_ref,
                        cf_ref, lf_ref, cb_ref, lvb_ref, og_ref, sfin_ref):
    consts = ((cf_ref[...], lf_ref[...]), (cb_ref[...], lvb_ref[...]))
    o, st_f, st_b = _scan_block(q_ref[...], v_ref[...], zf_ref[...], zb_ref[...], lb_ref[...], None, None, consts)
    og_ref[...] = _scan_finish(o, g_ref[...], on_ref[...])
    sfin_ref[0] = st_f.T
    sfin_ref[1] = st_b.T


def _hgrn_sample_kernel(q_ref, v_ref, zf_ref, zb_ref, g_ref, lb_ref, on_ref,
                        cf_ref, lf_ref, cb_ref, lvb_ref, s0_ref, og_ref, oacc_ref, st_ref, *, n_blocks):
    consts = ((cf_ref[...], lf_ref[...]), (cb_ref[...], lvb_ref[...]))
    st_ref[0] = s0_ref[0].T
    st_ref[1] = s0_ref[1].T
    lb = lb_ref[...]

    def body(tb, carry):
        rf = pl.multiple_of(tb * BLK, BLK)
        rb = pl.multiple_of((n_blocks - 1 - tb) * BLK, BLK)
        (cums_f, lev_f), (cums_b, lev_b) = consts
        v_f = v_ref[pl.ds(rf, BLK), :]
        att_f, oi_f, st_f = _scan_dir(q_ref[pl.ds(rf, BLK), :], v_f, zf_ref[pl.ds(rf, BLK), :],
                                      lb[0:1, :], st_ref[0], cums_f, lev_f, False)
        v_b = v_ref[pl.ds(rb, BLK), :]
        att_b, oi_b, st_b = _scan_dir(q_ref[pl.ds(rb, BLK), :], v_b, zb_ref[pl.ds(rb, BLK), :],
                                      lb[1:2, :], st_ref[1], cums_b, lev_b, True)
        st_ref[0] = st_f
        st_ref[1] = st_b
        of = _dot(att_f.astype(BF16), v_f.astype(BF16)) + oi_f
        ob = _dot(att_b.astype(BF16), v_b.astype(BF16)) + oi_b

        @pl.when(2 * tb < n_blocks)
        def _():
            oacc_ref[pl.ds(rf, BLK), :] = of
            oacc_ref[pl.ds(rb, BLK), :] = ob

        @pl.when(2 * tb >= n_blocks)
        def _():
            oacc_ref[pl.ds(rf, BLK), :] += of
            oacc_ref[pl.ds(rb, BLK), :] += ob

        return carry

    lax.fori_loop(0, n_blocks, body, 0)
    og_ref[...] = _scan_finish(oacc_ref[...], g_ref[...], on_ref[...])


def _hgrn_mixer(proj, lbs_j, onorm_g, s0_sample, n_prompt_seq, seq, dec_batch, dec_seq):
    consts = _scan_constants()
    (cums_f, lev_f), (cums_b, lev_b) = consts
    n_prompt = n_prompt_seq * seq
    on = onorm_g.reshape(1, HEAD_DIM)
    const_specs = [pl.BlockSpec((BLK, BLK), lambda *_: (0, 0)) for _ in range(4)]
    const_args = (cums_f, lev_f, cums_b, lev_b)

    def part_spec(rows, part, row_off):
        return pl.BlockSpec((rows, HEAD_DIM), lambda b, h: (b + row_off, part * HEADS + h))

    og_p, sfin = pl.pallas_call(
        _hgrn_prompt_kernel,
        grid=(n_prompt_seq, HEADS),
        in_specs=[part_spec(seq, p, 0) for p in range(5)] + [
            pl.BlockSpec((2, HEAD_DIM), lambda b, h: (0, h)),
            pl.BlockSpec((1, HEAD_DIM), lambda b, h: (0, 0)),
        ] + const_specs,
        out_specs=[
            pl.BlockSpec((seq, HEAD_DIM), lambda b, h: (b, h)),
            pl.BlockSpec((None, 2, None, HEAD_DIM, HEAD_DIM), lambda b, h: (b, 0, h, 0, 0)),
        ],
        out_shape=[
            jax.ShapeDtypeStruct((n_prompt, D_MODEL), BF16),
            jax.ShapeDtypeStruct((n_prompt_seq, 2, HEADS, HEAD_DIM, HEAD_DIM), F32),
        ],
        compiler_params=_cparams(("parallel", "parallel")),
        name="hgrn_scan_prompt",
    )(proj, proj, proj, proj, proj, lbs_j, on, *const_args)

    n_blocks = dec_seq // BLK
    row_off = n_prompt // dec_seq
    og_s = pl.pallas_call(
        functools.partial(_hgrn_sample_kernel, n_blocks=n_blocks),
        grid=(dec_batch, HEADS),
        in_specs=[part_spec(dec_seq, p, row_off) for p in range(5)] + [
            pl.BlockSpec((2, HEAD_DIM), lambda b, h: (0, h)),
            pl.BlockSpec((1, HEAD_DIM), lambda b, h: (0, 0)),
        ] + const_specs + [
            pl.BlockSpec((None, 2, None, HEAD_DIM, HEAD_DIM), lambda b, h: (b, 0, h, 0, 0)),
        ],
        out_specs=pl.BlockSpec((dec_seq, HEAD_DIM), lambda b, h: (b, h)),
        out_shape=jax.ShapeDtypeStruct((dec_batch * dec_seq, D_MODEL), BF16),
        scratch_shapes=[pltpu.VMEM((dec_seq, HEAD_DIM), F32), pltpu.VMEM((2, HEAD_DIM, HEAD_DIM), F32)],
        compiler_params=_cparams(("parallel", "parallel")),
        name="hgrn_scan_sample",
    )(proj, proj, proj, proj, proj, lbs_j, on, *const_args, s0_sample)
    return og_p, og_s, sfin


def _rope_tables(t_lat):
    rows = t_lat // GRID_W
    row = jnp.repeat(jnp.arange(rows), GRID_W).astype(F32)
    col = jnp.tile(jnp.arange(GRID_W), rows).astype(F32)
    half = QK_DIM // 2
    inv_freq = ROPE_THETA ** (-jnp.arange(0, half, 2, dtype=F32) / half)
    ang_row = row[:, None] * inv_freq
    ang_col = col[:, None] * inv_freq

    def part(ang):
        c, s = jnp.cos(ang), jnp.sin(ang)
        return jnp.concatenate([c, c], axis=1), jnp.concatenate([-s, s], axis=1)

    cr, sr = part(ang_row)
    cc, sc = part(ang_col)
    cos64 = jnp.concatenate([cr, cc], axis=1)
    sin64 = jnp.concatenate([sr, sc], axis=1)
    return jnp.concatenate([cos64, cos64], axis=1), jnp.concatenate([sin64, sin64], axis=1)


def _rope(x, cos, sin):
    lane = lax.broadcasted_iota(jnp.int32, x.shape, 1)
    first = (lane % (QK_DIM // 2)) < (QK_DIM // 4)
    swapped = jnp.where(first, pltpu.roll(x, LANES - QK_DIM // 4, 1), pltpu.roll(x, QK_DIM // 4, 1))
    return x * cos + swapped * sin


def _attn_kernel(*refs, n_qblk, rope, cache, emit_kv, lam_init):
    it = iter(refs)
    q_ref, k_ref, v_ref, qg_ref, kg_ref, lam_ref, sg_ref, seg_ref = (next(it) for _ in range(8))
    cos_ref = sin_ref = ck_ref = cv_ref = nk_ref = nv_ref = None
    if rope:
        cos_ref, sin_ref = next(it), next(it)
    if cache:
        ck_ref, cv_ref = next(it), next(it)
    oa_ref = next(it)
    if emit_kv:
        nk_ref, nv_ref = next(it), next(it)

    seg = seg_ref[...]
    k = k_ref[...]
    v = v_ref[...]
    kn = k * lax.rsqrt(_dot(k * k, seg) + EPS) * kg_ref[...]
    if emit_kv:
        nk_ref[...] = kn
        nv_ref[...] = v
    if rope:
        kn = _rope(kn, cos_ref[...], sin_ref[...])
    lp = lam_ref[...]
    lam = (jnp.exp(jnp.sum(lp[0:1, :] * lp[1:2, :], axis=-1, keepdims=True))
           - jnp.exp(jnp.sum(lp[2:3, :] * lp[3:4, :], axis=-1, keepdims=True)) + lam_init)

    lane = lax.broadcasted_iota(jnp.int32, (1, LANES), 1)
    comp0 = lane < QK_DIM
    ks = [jnp.where(comp0, kn, 0.0).astype(BF16), jnp.where(comp0, 0.0, kn).astype(BF16)]
    vb = v.astype(BF16)
    if cache:
        ck = ck_ref[...]
        cks = [jnp.where(comp0, ck, 0.0).astype(BF16), jnp.where(comp0, 0.0, ck).astype(BF16)]
        cvb = cv_ref[...].astype(BF16)

    scale = QK_DIM ** -0.5
    for qi in range(n_qblk):
        rows = slice(qi * BLK, (qi + 1) * BLK)
        q = q_ref[rows, :]
        qn = q * lax.rsqrt(_dot(q * q, seg) + EPS) * qg_ref[...]
        if rope:
            qn = _rope(qn, cos_ref[rows, :], sin_ref[rows, :])
        qb = (qn * scale).astype(BF16)
        a_self, a_cache = None, None
        for c in range(2):
            s = _dot_nt(qb, ks[c])
            m = jnp.max(s, axis=-1, keepdims=True)
            if cache:
                sc = _dot_nt(qb, cks[c])
                m = jnp.maximum(m, jnp.max(sc, axis=-1, keepdims=True))
            p = jnp.exp(s - m)
            den = jnp.sum(p, axis=-1, keepdims=True)
            if cache:
                pc = jnp.exp(sc - m)
                den = den + jnp.sum(pc, axis=-1, keepdims=True)
            w = (1.0 / den) if c == 0 else (-lam / den)
            a_self = p * w if c == 0 else a_self + p * w
            if cache:
                a_cache = pc * w if c == 0 else a_cache + pc * w
        o = _dot(a_self.astype(BF16), vb)
        if cache:
            o = o + _dot(a_cache.astype(BF16), cvb)
        ms = jnp.mean(o * o, axis=-1, keepdims=True)
        o = (o * lax.rsqrt(ms + EPS) * sg_ref[...]) * (1.0 - lam_init)
        oa_ref[rows, :] = o.astype(BF16)


def _attn_mixer(qkv, qn_g, kn_g, lam_p, subln_g, cache_k_j, cache_v_j, layer_idx,
                n_prompt_seq, seq, dec_batch, dec_seq):
    n_prompt = n_prompt_seq * seq
    lam_init = 0.8 - 0.6 * math.exp(-0.3 * layer_idx)
    qg = jnp.tile(qn_g.reshape(1, QK_DIM), (1, 2))
    kg = jnp.tile(kn_g.reshape(1, QK_DIM), (1, 2))
    sg = subln_g.reshape(1, HEAD_DIM)
    li = np.arange(LANES)
    seg = jnp.asarray((li[:, None] // QK_DIM == li[None, :] // QK_DIM).astype(np.float32) / QK_DIM)
    small_specs = [
        pl.BlockSpec((1, LANES), lambda b, h: (0, 0)),
        pl.BlockSpec((1, LANES), lambda b, h: (0, 0)),
        pl.BlockSpec((4, QK_DIM), lambda b, h: (0, 0)),
        pl.BlockSpec((1, HEAD_DIM), lambda b, h: (0, 0)),
        pl.BlockSpec((LANES, LANES), lambda b, h: (0, 0)),
    ]
    small_args = (qg, kg, lam_p, sg, seg)

    def part_spec(rows, part, row_off):
        return pl.BlockSpec((rows, HEAD_DIM), lambda b, h: (b + row_off, part * HEADS + h))

    oa_p, nk, nv = pl.pallas_call(
        functools.partial(_attn_kernel, n_qblk=seq // BLK, rope=False, cache=False, emit_kv=True, lam_init=lam_init),
        grid=(n_prompt_seq, HEADS),
        in_specs=[part_spec(seq, p, 0) for p in range(3)] + small_specs,
        out_specs=[
            pl.BlockSpec((seq, HEAD_DIM), lambda b, h: (b, h)),
            pl.BlockSpec((None, seq, HEAD_DIM), lambda b, h: (b, 0, h)),
            pl.BlockSpec((None, seq, HEAD_DIM), lambda b, h: (b, 0, h)),
        ],
        out_shape=[
            jax.ShapeDtypeStruct((n_prompt, D_MODEL), BF16),
            jax.ShapeDtypeStruct((n_prompt_seq, seq, D_MODEL), F32),
            jax.ShapeDtypeStruct((n_prompt_seq, seq, D_MODEL), F32),
        ],
        compiler_params=_cparams(("parallel", "parallel")),
        name="diff_attn_prompt",
    )(qkv, qkv, qkv, *small_args)

    cos, sin = _rope_tables(dec_seq)
    past = cache_k_j.shape[1]
    row_off = n_prompt // dec_seq
    oa_s = pl.pallas_call(
        functools.partial(_attn_kernel, n_qblk=dec_seq // BLK, rope=True, cache=True, emit_kv=False, lam_init=lam_init),
        grid=(dec_batch, HEADS),
        in_specs=[part_spec(dec_seq, p, row_off) for p in range(3)] + small_specs + [
            pl.BlockSpec((dec_seq, LANES), lambda b, h: (0, 0)),
            pl.BlockSpec((dec_seq, LANES), lambda b, h: (0, 0)),
            pl.BlockSpec((None, past, HEAD_DIM), lambda b, h: (b, 0, h)),
            pl.BlockSpec((None, past, HEAD_DIM), lambda b, h: (b, 0, h)),
        ],
        out_specs=pl.BlockSpec((dec_seq, HEAD_DIM), lambda b, h: (b, h)),
        out_shape=jax.ShapeDtypeStruct((dec_batch * dec_seq, D_MODEL), BF16),
        compiler_params=_cparams(("parallel", "parallel")),
        name="diff_attn_sample",
    )(qkv, qkv, qkv, *small_args, cos, sin,
      cache_k_j.reshape(dec_batch, past, D_MODEL), cache_v_j.reshape(dec_batch, past, D_MODEL))
    return oa_p, oa_s, nk, nv


ROUTE_ROWS = 32
INFO_GID, INFO_RANK = 8, 9


def _route_kernel(x_ref, g_ref, sh_ref, sc_ref, wr_ref, br_ref, tri_ref,
                  h_ref, il_ref, it_ref, cum_ref, tot_ref, carry_ref):
    i = pl.program_id(0)

    @pl.when(i == 0)
    def _():
        carry_ref[...] = jnp.zeros_like(carry_ref)

    hb = _norm_mod(x_ref[...], g_ref[...], sc_ref[...], sh_ref[...]).astype(BF16)
    h_ref[...] = hb
    logit = _dot_nt(wr_ref[...], hb) + br_ref[...]
    gl = [logit[g:g + 1, :] for g in range(MOE_GROUPS)]
    gmax = functools.reduce(jnp.maximum, gl)
    gz = functools.reduce(lambda a, b: a + b, [jnp.exp(x - gmax) for x in gl])
    g_w = 1.0 / gz
    gid = jnp.full_like(gmax, MOE_GROUPS - 1)
    for g in range(MOE_GROUPS - 2, -1, -1):
        gid = jnp.where(gl[g] == gmax, float(g), gid)
    el = []
    for j in range(MOE_EPG):
        e = logit[MOE_GROUPS + j:MOE_GROUPS + j + 1, :]
        for g in range(1, MOE_GROUPS):
            r = MOE_GROUPS + g * MOE_EPG + j
            e = jnp.where(gid == float(g), logit[r:r + 1, :], e)
        el.append(e)
    emax = functools.reduce(jnp.maximum, el)
    pe = [jnp.exp(e - emax) for e in el]
    idx1 = jnp.full_like(emax, MOE_EPG - 1)
    for j in range(MOE_EPG - 2, -1, -1):
        idx1 = jnp.where(el[j] == emax, float(j), idx1)
    el2 = [jnp.where(idx1 == float(j), -jnp.inf, el[j]) for j in range(MOE_EPG)]
    emax2 = functools.reduce(jnp.maximum, el2)
    idx2 = jnp.full_like(emax, MOE_EPG - 1)
    for j in range(MOE_EPG - 2, -1, -1):
        idx2 = jnp.where(el2[j] == emax2, float(j), idx2)
    sel = [(idx1 == float(j)) | (idx2 == float(j)) for j in range(MOE_EPG)]
    den = functools.reduce(lambda a, b: a + b, [jnp.where(sel[j], pe[j], 0.0) for j in range(MOE_EPG)])
    cw = [jnp.where(sel[j], pe[j] * (g_w / den), 0.0) for j in range(MOE_EPG)]

    row8 = lax.broadcasted_iota(jnp.int32, (8, BLK), 0)
    onehot = jnp.where(row8.astype(F32) == gid, 1.0, 0.0)
    within = _dot(onehot.astype(BF16), tri_ref[...])
    carry = carry_ref[...]
    rank = jnp.sum(onehot * (within + carry[:, 0:1]), axis=0, keepdims=True)
    cum_ref[...] = carry
    new_carry = carry + jnp.sum(onehot, axis=1, keepdims=True)
    carry_ref[...] = new_carry
    tot_ref[...] = new_carry

    il_ref[...] = jnp.where(row8 == 0, gid, jnp.where(row8 == 1, rank, 0.0))
    rowl = lax.broadcasted_iota(jnp.int32, (LANES, BLK), 0)
    m = jnp.zeros((LANES, BLK), F32)
    for j in range(MOE_EPG):
        m = jnp.where(rowl == j, cw[j], m)
    m = jnp.where(rowl == INFO_GID, gid, m)
    m = jnp.where(rowl == INFO_RANK, rank, m)
    it_ref[...] = m.T


def _moe_route(x, g, modr, wr_t, br, n_prompt, dec_seq):
    n, d = x.shape
    nt = n // BLK
    cond = lambda i: _cond_of_row(i * BLK, n_prompt, dec_seq)
    tri = jnp.asarray(np.triu(np.ones((BLK, BLK), np.float32), 1), BF16)
    return pl.pallas_call(
        _route_kernel,
        grid=(nt,),
        in_specs=[
            pl.BlockSpec((BLK, d), lambda i: (i, 0)),
            pl.BlockSpec((1, d), lambda i: (0, 0)),
            pl.BlockSpec((None, 1, d), lambda i: (cond(i) * N_MOD + 3, 0, 0)),
            pl.BlockSpec((None, 1, d), lambda i: (cond(i) * N_MOD + 4, 0, 0)),
            pl.BlockSpec((ROUTE_ROWS, d), lambda i: (0, 0)),
            pl.BlockSpec((ROUTE_ROWS, 1), lambda i: (0, 0)),
            pl.BlockSpec((BLK, BLK), lambda i: (0, 0)),
        ],
        out_specs=[
            pl.BlockSpec((BLK, d), lambda i: (i, 0)),
            pl.BlockSpec((None, 8, BLK), lambda i: (i, 0, 0)),
            pl.BlockSpec((BLK, LANES), lambda i: (i, 0)),
            pl.BlockSpec((None, 8, LANES), lambda i: (i, 0, 0)),
            pl.BlockSpec((8, LANES), lambda i: (0, 0)),
        ],
        out_shape=[
            jax.ShapeDtypeStruct((n, d), BF16),
            jax.ShapeDtypeStruct((nt, 8, BLK), F32),
            jax.ShapeDtypeStruct((n, LANES), F32),
            jax.ShapeDtypeStruct((nt, 8, LANES), F32),
            jax.ShapeDtypeStruct((8, LANES), F32),
        ],
        scratch_shapes=[pltpu.VMEM((8, LANES), F32)],
        compiler_params=_cparams(("arbitrary",)),
        name="moe_route",
    )(x, g.reshape(1, d), modr, modr, wr_t, br, tri)


def _sorted_pos(gid, rank, rstart_ref):
    p = rank
    for g in range(MOE_GROUPS):
        p = p + jnp.where(gid == float(g), rstart_ref[g].astype(F32), 0.0)
    return p


def _gather_kernel(valid_ref, clo_ref, chi_ref, rstart_ref, h_ref, il_ref, it_ref,
                   hs_ref, cws_ref, acch_ref, accc_ref):
    a = pl.program_id(0)

    @pl.when(valid_ref[a] == 0)
    def _():
        hs_ref[...] = jnp.zeros_like(hs_ref)
        cws_ref[...] = jnp.zeros_like(cws_ref)

    @pl.when(valid_ref[a] != 0)
    def _():
        acch_ref[...] = jnp.zeros_like(acch_ref)
        accc_ref[...] = jnp.zeros_like(accc_ref)
        dest = (lax.broadcasted_iota(jnp.int32, (BLK, 1), 0) + a * BLK).astype(F32)

        def body(c, carry):
            info = il_ref[c]
            p = _sorted_pos(info[0:1, :], info[1:2, :], rstart_ref)
            onehot = jnp.where(dest == p, 1.0, 0.0).astype(BF16)
            acch_ref[...] += _dot(onehot, h_ref[c])
            cw = it_ref[c]
            hi = cw.astype(BF16)
            lo = (cw - hi.astype(F32)).astype(BF16)
            accc_ref[...] += _dot(onehot, hi) + _dot(onehot, lo)
            return carry

        lax.fori_loop(clo_ref[a], chi_ref[a] + 1, body, 0)
        hs_ref[...] = acch_ref[...].astype(BF16)
        cws_ref[...] = accc_ref[...]


def _moe_gather(h, info_lane, info_tok, valid, clo, chi, rstart, n_dest_tiles):
    n, d = h.shape
    nt = n // BLK
    resident = dict(pipeline_mode=pl.Buffered(1))
    return pl.pallas_call(
        _gather_kernel,
        grid_spec=pltpu.PrefetchScalarGridSpec(
            num_scalar_prefetch=4,
            grid=(n_dest_tiles,),
            in_specs=[
                pl.BlockSpec((nt, BLK, d), lambda a, *_: (0, 0, 0), **resident),
                pl.BlockSpec((nt, 8, BLK), lambda a, *_: (0, 0, 0), **resident),
                pl.BlockSpec((nt, BLK, LANES), lambda a, *_: (0, 0, 0), **resident),
            ],
            out_specs=[
                pl.BlockSpec((BLK, d), lambda a, *_: (a, 0)),
                pl.BlockSpec((BLK, LANES), lambda a, *_: (a, 0)),
            ],
            scratch_shapes=[pltpu.VMEM((BLK, d), F32), pltpu.VMEM((BLK, LANES), F32)],
        ),
        out_shape=[
            jax.ShapeDtypeStruct((n_dest_tiles * BLK, d), BF16),
            jax.ShapeDtypeStruct((n_dest_tiles * BLK, LANES), F32),
        ],
        compiler_params=_cparams(("arbitrary",)),
        name="moe_gather",
    )(valid, clo, chi, rstart, h.reshape(nt, BLK, d), info_lane, info_tok.reshape(nt, BLK, LANES))


def _moe_mlp_kernel(sgroup_ref, snt_ref, hs_ref, cws_ref, wg_ref, wu_ref, wd_ref, ys_ref,
                    acc_ref, wgb_ref, wub_ref, wdb_ref):
    s = pl.program_id(0)
    k = pl.program_id(1)
    nt = snt_ref[s]
    wgb_ref[...] = wg_ref[...].astype(BF16)
    wub_ref[...] = wu_ref[...].astype(BF16)
    wdb_ref[...] = wd_ref[...].astype(BF16)
    for j in range(SUPER):
        rows = slice(j * BLK, (j + 1) * BLK)

        @pl.when(j < nt)
        def _():
            hsub = hs_ref[rows, :]
            gate = _dot(hsub, wgb_ref[...])
            up = _dot(hsub, wub_ref[...])
            cws = cws_ref[rows, :]
            cwk = jnp.zeros((BLK, 1), F32)
            for kk in range(MOE_EPG):
                cwk = jnp.where(k == kk, cws[:, kk:kk + 1], cwk)
            act = ((_silu(gate) * up) * cwk).astype(BF16)
            y = _dot(act, wdb_ref[...])

            @pl.when(k == 0)
            def _():
                acc_ref[rows, :] = y

            @pl.when(k != 0)
            def _():
                acc_ref[rows, :] += y

            @pl.when(k == MOE_EPG - 1)
            def _():
                ys_ref[rows, :] = acc_ref[rows, :].astype(BF16)

        @pl.when((j >= nt) & (k == MOE_EPG - 1))
        def _():
            ys_ref[rows, :] = jnp.zeros((BLK, D_MODEL), BF16)


def _moe_mlp(hs, cws, w_gate, w_up, w_down, sgroup, snt, n_super):
    d = hs.shape[1]
    widx = lambda s, k, sg, sn: (sg[s] * MOE_EPG + k, 0, 0)
    return pl.pallas_call(
        _moe_mlp_kernel,
        grid_spec=pltpu.PrefetchScalarGridSpec(
            num_scalar_prefetch=2,
            grid=(n_super, MOE_EPG),
            in_specs=[
                pl.BlockSpec((SUPER_ROWS, d), lambda s, k, *_: (s, 0)),
                pl.BlockSpec((SUPER_ROWS, LANES), lambda s, k, *_: (s, 0)),
                pl.BlockSpec((None, d, MOE_D_FF), widx),
                pl.BlockSpec((None, d, MOE_D_FF), widx),
                pl.BlockSpec((None, MOE_D_FF, d), widx),
            ],
            out_specs=pl.BlockSpec((SUPER_ROWS, d), lambda s, k, *_: (s, 0)),
            scratch_shapes=[
                pltpu.VMEM((SUPER_ROWS, d), F32),
                pltpu.VMEM((d, MOE_D_FF), BF16),
                pltpu.VMEM((d, MOE_D_FF), BF16),
                pltpu.VMEM((MOE_D_FF, d), BF16),
            ],
        ),
        out_shape=jax.ShapeDtypeStruct((n_super * SUPER_ROWS, d), BF16),
        compiler_params=_cparams(("arbitrary", "arbitrary")),
        name="moe_experts",
    )(sgroup, snt, hs, cws, w_gate, w_up, w_down)


N_SRC = 2 * MOE_GROUPS


def _unsort_kernel(ut_ref, rstart_ref, x_ref, it_ref, gate_ref, *rest):
    ys_refs, o_ref, acc_ref = rest[:N_SRC], rest[N_SRC], rest[N_SRC + 1]
    t = pl.program_id(0)
    info = it_ref[...]
    p = _sorted_pos(info[:, INFO_GID:INFO_GID + 1], info[:, INFO_RANK:INFO_RANK + 1], rstart_ref)
    lane = lax.broadcasted_iota(jnp.int32, (1, BLK), 1).astype(F32)
    acc_ref[...] = jnp.zeros_like(acc_ref)
    for m in range(N_SRC):
        a = ut_ref[t * N_SRC + m]

        @pl.when(a >= 0)
        def _():
            onehot = jnp.where(p - (a * BLK).astype(F32) == lane, 1.0, 0.0).astype(BF16)
            acc_ref[...] += _dot(onehot, ys_refs[m][...])

    o_ref[...] = x_ref[...] + gate_ref[...] * acc_ref[...]


def _moe_unsort(x, info_tok, modr, ys, ut, rstart, n_prompt, dec_seq):
    n, d = x.shape
    nt = n // BLK
    cond = lambda i: _cond_of_row(i * BLK, n_prompt, dec_seq)

    def ys_spec(m):
        return pl.BlockSpec((BLK, d), lambda t, ut_r, rs_r: (jnp.maximum(ut_r[t * N_SRC + m], 0), 0))

    return pl.pallas_call(
        _unsort_kernel,
        grid_spec=pltpu.PrefetchScalarGridSpec(
            num_scalar_prefetch=2,
            grid=(nt,),
            in_specs=[
                pl.BlockSpec((BLK, d), lambda t, *_: (t, 0)),
                pl.BlockSpec((BLK, LANES), lambda t, *_: (t, 0)),
                pl.BlockSpec((None, 1, d), lambda t, *_: (cond(t) * N_MOD + 5, 0, 0)),
            ] + [ys_spec(m) for m in range(N_SRC)],
            out_specs=pl.BlockSpec((BLK, d), lambda t, *_: (t, 0)),
            scratch_shapes=[pltpu.VMEM((BLK, d), F32)],
        ),
        out_shape=jax.ShapeDtypeStruct((n, d), F32),
        compiler_params=_cparams(("arbitrary",)),
        name="moe_unsort_residual",
    )(ut, rstart, x, info_tok, modr, *([ys] * N_SRC))


def _moe_tables(cum, tot, n_tiles, n_super):
    cumc = jnp.concatenate([cum[:, :MOE_GROUPS, 0], tot[None, :MOE_GROUPS, 0]], axis=0).astype(jnp.int32)
    total = cumc[-1]
    n_sup_g = (total + SUPER_ROWS - 1) // SUPER_ROWS
    sup_end = jnp.cumsum(n_sup_g)
    sup_start = sup_end - n_sup_g
    rstart = sup_start * SUPER_ROWS
    s = jnp.arange(n_super)
    sgroup = jnp.minimum(jnp.sum(s[:, None] >= sup_end[None, :], axis=1), MOE_GROUPS - 1)
    rows_left = total[sgroup] - (s - sup_start[sgroup]) * SUPER_ROWS
    snt = jnp.where(s < sup_end[-1], jnp.clip((rows_left + BLK - 1) // BLK, 0, SUPER), 0)
    a = jnp.arange(n_super * SUPER)
    sa, ja = a // SUPER, a % SUPER
    valid = (ja < snt[sa]).astype(jnp.int32)
    ga = sgroup[sa]
    r0 = ((sa - sup_start[ga]) * SUPER + ja) * BLK
    cg = cumc[:, ga]
    clo = jnp.sum(cg[1:] <= r0[None, :], axis=0)
    chi = jnp.sum(cg[:-1] < (r0 + BLK)[None, :], axis=0) - 1
    clo = jnp.clip(clo, 0, n_tiles - 1)
    chi = jnp.clip(chi, 0, n_tiles - 1)
    first = rstart[None, :] + cumc[:-1]
    last = rstart[None, :] + cumc[1:] - 1
    has = cumc[1:] > cumc[:-1]
    a0 = jnp.where(has, first // BLK, -1)
    a1 = jnp.where(has & (last // BLK != first // BLK), last // BLK, -1)
    ut = jnp.stack([a0, a1], axis=-1).reshape(-1)
    i32 = lambda v: v.astype(jnp.int32)
    return i32(rstart), i32(sgroup), i32(snt), valid, i32(clo), i32(chi), i32(ut)


def _moe_layer(x, g, modr, w_group, b_group, w_expert, b_expert, w_gate, w_up, w_down, n_prompt, dec_seq):
    n, d = x.shape
    nt = n // BLK
    n_super = (n + SUPER_ROWS - 1) // SUPER_ROWS + MOE_GROUPS
    wr_t = jnp.zeros((ROUTE_ROWS, d), F32)
    wr_t = wr_t.at[:MOE_GROUPS].set(w_group.T).at[MOE_GROUPS:MOE_GROUPS * (1 + MOE_EPG)].set(w_expert.T)
    br = jnp.zeros((ROUTE_ROWS, 1), F32)
    br = br.at[:MOE_GROUPS, 0].set(b_group).at[MOE_GROUPS:MOE_GROUPS * (1 + MOE_EPG), 0].set(b_expert)
    h, info_lane, info_tok, cum, tot = _moe_route(x, g, modr, wr_t.astype(BF16), br, n_prompt, dec_seq)
    rstart, sgroup, snt, valid, clo, chi, ut = _moe_tables(cum, tot, nt, n_super)
    hs, cws = _moe_gather(h, info_lane, info_tok, valid, clo, chi, rstart, n_super * SUPER)
    ys = _moe_mlp(hs, cws, w_gate, w_up, w_down, sgroup, snt, n_super)
    return _moe_unsort(x, info_tok, modr, ys, ut, rstart, n_prompt, dec_seq)


def kernel(x_prompt, x_sample, c, cache_k, cache_v, state_hgrn, c_ctx, norm_g, w_ada, b_ada, hgrn_w_in, hgrn_lb_logits, hgrn_onorm_g, hgrn_w_out, attn_w_qkv, attn_qn_g, attn_kn_g, attn_lambda, attn_subln_g, attn_w_out, moe_w_group, moe_b_group, moe_w_expert, moe_b_expert, moe_w_gate, moe_w_up, moe_w_down):
    n_prompt_seq, seq, d = x_prompt.shape
    dec_batch, dec_seq, _ = x_sample.shape
    n_prompt = n_prompt_seq * seq
    assert d == D_MODEL and seq == BLK and dec_seq % BLK == 0 and n_prompt % dec_seq == 0
    assert 1 + dec_batch <= N_COND

    x = jnp.concatenate([x_prompt.reshape(n_prompt, d), x_sample.reshape(dec_batch * dec_seq, d)], axis=0)
    cond = jnp.zeros((N_COND, d), F32).at[0].set(c_ctx).at[1:1 + dec_batch].set(c)
    mod = _modulation(cond, w_ada, b_ada)

    lbs = jnp.cumsum(jax.nn.softmax(hgrn_lb_logits.astype(F32), axis=0), axis=0)
    lbs = lbs - lbs[0:1]

    new_k, new_v, new_s = [], [], []
    for i in range(DEPTH):
        j = i // 2
        modr = mod[i].reshape(N_COND * N_MOD, 1, d)
        if i % 2 == 0:
            proj = _nm_matmul(x, norm_g[i, 0], modr, hgrn_w_in[j].astype(BF16), n_prompt, dec_seq, 0)
            o_p, o_s, sfin = _hgrn_mixer(proj, lbs[j], hgrn_onorm_g[j], state_hgrn[:, j],
                                         n_prompt_seq, seq, dec_batch, dec_seq)
            new_s.append(sfin)
            w_out = hgrn_w_out[j]
        else:
            qkv = _nm_matmul(x, norm_g[i, 0], modr, attn_w_qkv[j].astype(BF16), n_prompt, dec_seq, 0)
            o_p, o_s, nk, nv = _attn_mixer(qkv, attn_qn_g[j], attn_kn_g[j], attn_lambda[j], attn_subln_g[j],
                                           cache_k[:, j], cache_v[:, j], i, n_prompt_seq, seq, dec_batch, dec_seq)
            new_k.append(nk.reshape(n_prompt_seq, seq, HEADS, 2, QK_DIM))
            new_v.append(nv.reshape(n_prompt_seq, seq, HEADS, HEAD_DIM))
            w_out = attn_w_out[j]
        x = _out_matmul(x, o_p, o_s, modr, w_out.astype(BF16), n_prompt, dec_seq)
        x = _moe_layer(x, norm_g[i, 1], modr, moe_w_group[i], moe_b_group[i], moe_w_expert[i], moe_b_expert[i],
                       moe_w_gate[i], moe_w_up[i], moe_w_down[i], n_prompt, dec_seq)

    y_prompt = x[:n_prompt].reshape(n_prompt_seq, seq, d)
    y_sample = x[n_prompt:].reshape(dec_batch, dec_seq, d)
    return (y_prompt, y_sample, jnp.stack(new_k, axis=1), jnp.stack(new_v, axis=1), jnp.stack(new_s, axis=1))
```

```python
import functools
import math

import numpy as np
import jax
import jax.numpy as jnp
from jax import lax
from jax.experimental import pallas as pl
from jax.experimental.pallas import tpu as pltpu

F32 = jnp.float32
BF16 = jnp.bfloat16

D_MODEL = 1024
DEPTH = 4
GRID_W = 64
HEADS = 8
HEAD_DIM = 128
QK_DIM = 64
ROPE_THETA = 10000.0
MOE_GROUPS = 4
MOE_EPG = 4
MOE_EXPERTS = MOE_GROUPS * MOE_EPG
MOE_D_FF = 512
EPS = 1e-6
N_COND = 8
N_MOD = 6
HGRN_PARTS = 5

LANES = 128
BLK = 256
CHUNK = 32
N_CHUNK = BLK // CHUNK
HEADS_PER_STEP = 2
PAIR_W = HEADS_PER_STEP * HEAD_DIM
SUPER = 8
SUPER_ROWS = SUPER * BLK
GATHER_WIN = 6
EXP_CLAMP = 80.0
VMEM_LIMIT = 56 * 1024 * 1024
N_SLOTS = DEPTH // 2
_RESIDENT = dict(pipeline_mode=pl.Buffered(1))


def _cparams(sem):
    return pltpu.CompilerParams(dimension_semantics=sem, vmem_limit_bytes=VMEM_LIMIT)


def _silu(x):
    return x * jax.nn.sigmoid(x)


def _dot(a, b):
    return jnp.dot(a, b, preferred_element_type=F32)


def _dot_nt(a, b):
    return lax.dot_general(a, b, (((1,), (1,)), ((), ())), preferred_element_type=F32)


def _dot_tn(a, b):
    return lax.dot_general(a, b, (((0,), (0,)), ((), ())), preferred_element_type=F32)


def _lane_block(i, width):
    return pl.ds(pl.multiple_of(i * width, width), width)


def _mod_kernel(c_ref, w_ref, b_ref, o_ref):
    o_ref[...] = _dot(_silu(c_ref[...]), w_ref[...]) + b_ref[...]


def _modulation(cond, w_ada, b_ada):
    tn = 1536
    nj = (N_MOD * D_MODEL) // tn
    return pl.pallas_call(
        _mod_kernel,
        grid=(DEPTH, nj),
        in_specs=[
            pl.BlockSpec((N_COND, D_MODEL), lambda l, j: (0, 0)),
            pl.BlockSpec((None, D_MODEL, tn), lambda l, j: (l, 0, j)),
            pl.BlockSpec((None, 1, tn), lambda l, j: (l, 0, j)),
        ],
        out_specs=pl.BlockSpec((None, N_COND, tn), lambda l, j: (l, 0, j)),
        out_shape=jax.ShapeDtypeStruct((DEPTH, N_COND, N_MOD * D_MODEL), F32),
        compiler_params=_cparams(("parallel", "parallel")),
        name="modulation",
    )(cond, w_ada, b_ada.reshape(DEPTH, 1, N_MOD * D_MODEL))


def _norm_mod(x, g, sc, sh):
    ms = jnp.mean(x * x, axis=-1, keepdims=True)
    return (x * lax.rsqrt(ms + EPS) * g) * (1.0 + sc) + sh


def _mod_spec(cond_of_step, which):
    return pl.BlockSpec((None, 1, D_MODEL), lambda i, *_: (cond_of_step(i) * N_MOD + which, 0, 0))


def _scan_constants():
    t = np.arange(BLK)
    out = []
    for rev in (False, True):
        u = (BLK - 1 - t) if rev else t
        ut, us = u[:, None], u[None, :]
        cums = (us <= ut).astype(np.float32)
        lev = np.where(us > ut, 0,
              np.where(ut // CHUNK == us // CHUNK, 1,
              np.where(ut // 64 == us // 64, 2,
              np.where(ut // 128 == us // 128, 3, 4)))).astype(np.int32)
        out += [jnp.asarray(cums, BF16), jnp.asarray(lev)]
    return out


def _rows_to_block(rows, rev):
    order = rows[::-1] if rev else rows
    return jnp.concatenate([jnp.broadcast_to(r, (CHUNK, LANES)) for r in order], axis=0)


def _scan_dir(q, v, z, lb, st_prev, cums, lev, rev):
    sig = jax.nn.sigmoid(z)
    f = lb + (1.0 - lb) * sig
    logf = jnp.log(f)
    k = (1.0 - lb) * (1.0 - sig)
    hi = logf.astype(BF16)
    lo = (logf - hi.astype(F32)).astype(BF16)
    bb = _dot(cums, jnp.concatenate([hi, lo], axis=1))
    b = bb[:, :LANES] + bb[:, LANES:]

    e_row, m_row = (0, CHUNK // 2) if rev else (CHUNK - 1, CHUNK // 2 - 1)
    ends, mids = [], []
    for j in range(N_CHUNK):
        ends.append(b[j * CHUNK + e_row:j * CHUNK + e_row + 1, :])
        mids.append(b[j * CHUNK + m_row:j * CHUNK + m_row + 1, :])
    if rev:
        ends, mids = ends[::-1], mids[::-1]
    zero = jnp.zeros((1, LANES), F32)
    pres = [zero] + ends[:-1]
    b_pre = _rows_to_block(pres, rev)
    b_end = _rows_to_block(ends, rev)
    b_mid = _rows_to_block(mids, rev)

    qd = q * jnp.exp(b - b_pre)
    ku = k * jnp.exp(b_end - b)
    qm = q * jnp.exp(jnp.clip(b - b_mid, -EXP_CLAMP, EXP_CLAMP))
    km = k * jnp.exp(jnp.clip(b_mid - b, -EXP_CLAMP, EXP_CLAMP))

    att = jnp.where(lev == 1, _dot_nt(qm.astype(BF16), km.astype(BF16)), 0.0)
    att = jnp.where(lev == 2, _dot_nt(qd.astype(BF16), ku.astype(BF16)), att)
    for level, nc in ((3, 4), (4, 8)):
        fq, fk = [], []
        for ju in range(N_CHUNK):
            r = (ju // nc) * nc + nc // 2 - 1
            if ju % nc >= nc // 2:
                fq.append(pres[ju] - ends[r])
                fk.append(None)
            else:
                fq.append(None)
                fk.append(ends[r] - ends[ju])
        fq = [zero if a is None else jnp.exp(a) for a in fq]
        fk = [zero if a is None else jnp.exp(a) for a in fk]
        ql = qd * _rows_to_block(fq, rev)
        kl = ku * _rows_to_block(fk, rev)
        att = jnp.where(lev == level, _dot_nt(ql.astype(BF16), kl.astype(BF16)), att)

    last = ends[-1]
    o_inter = None
    if st_prev is not None:
        qh = qd * _rows_to_block([jnp.exp(p) for p in pres], rev)
        o_inter = _dot_nt(qh.astype(BF16), st_prev.astype(BF16))
    kh = ku * _rows_to_block([jnp.exp(last - e) for e in ends], rev)
    ut = _dot_tn(v.astype(BF16), kh.astype(BF16))
    st_new = ut if st_prev is None else st_prev * jnp.exp(last) + ut
    return att, o_inter, st_new


def _scan_finish(o, g, on):
    ms = jnp.mean(o * o, axis=-1, keepdims=True)
    return ((o * lax.rsqrt(ms + EPS) * on) * _silu(g)).astype(BF16)


def _slot_view(ref, slot, owns_all_slots):
    if not owns_all_slots:
        return ref
    for s in range(ref.shape[0]):
        if s != slot:
            ref[s] = jnp.zeros(ref.shape[1:], ref.dtype)
    return ref.at[slot]


def _hgrn_kernel(*refs, n_blocks, has_state, slot, owns_all_slots):
    it = iter(refs)
    x_ref, g_ref, sh_ref, sc_ref, gate_ref, win_ref, lb_ref, on_ref = (next(it) for _ in range(8))
    cf_ref, lf_ref, cb_ref, lvb_ref = (next(it) for _ in range(4))
    s0_ref = next(it) if has_state else None
    wout_ref, xo_ref = next(it), next(it)
    sfin_ref = None if has_state else next(it)
    h_ref, proj_ref, og_ref = next(it), next(it), next(it)
    oacc_ref, st_ref = (next(it), next(it)) if has_state else (None, None)

    h_ref[...] = _norm_mod(x_ref[...], g_ref[...], sc_ref[...], sh_ref[...]).astype(BF16)
    on = on_ref[...]
    if not has_state:
        sfin_ref = _slot_view(sfin_ref, slot, owns_all_slots)

    def part(i, p, rows=slice(None)):
        return proj_ref[rows, p * PAIR_W + i * HEAD_DIM:p * PAIR_W + (i + 1) * HEAD_DIM]

    def pair(hp, carry):
        proj_ref[...] = _dot(h_ref[...], win_ref[:, _lane_block(hp, HGRN_PARTS * PAIR_W)])
        heads = [hp * HEADS_PER_STEP + i for i in range(HEADS_PER_STEP)]
        lbs = [lb_ref[:, _lane_block(hd, HEAD_DIM)] for hd in heads]
        if not has_state:
            for i, hd in enumerate(heads):
                q, v = part(i, 0), part(i, 1)
                att_f, _, st_f = _scan_dir(q, v, part(i, 2), lbs[i][0:1, :], None, cf_ref[...], lf_ref[...], False)
                att_b, _, st_b = _scan_dir(q, v, part(i, 3), lbs[i][1:2, :], None, cb_ref[...], lvb_ref[...], True)
                o = _dot((att_f + att_b).astype(BF16), v.astype(BF16))
                sfin_ref[0, hd] = st_f.T
                sfin_ref[1, hd] = st_b.T
                og_ref[:, _lane_block(hd, HEAD_DIM)] = _scan_finish(o, part(i, 4), on)
        else:
            for i, hd in enumerate(heads):
                st_ref[2 * i] = s0_ref[0, hd].T
                st_ref[2 * i + 1] = s0_ref[1, hd].T

            def body(tb, c2):
                rf = pl.ds(pl.multiple_of(tb * BLK, BLK), BLK)
                rb = pl.ds(pl.multiple_of((n_blocks - 1 - tb) * BLK, BLK), BLK)
                for i in range(HEADS_PER_STEP):
                    cols = slice(i * HEAD_DIM, (i + 1) * HEAD_DIM)
                    v_f, v_b = part(i, 1, rf), part(i, 1, rb)
                    att_f, oi_f, st_f = _scan_dir(part(i, 0, rf), v_f, part(i, 2, rf), lbs[i][0:1, :],
                                                  st_ref[2 * i], cf_ref[...], lf_ref[...], False)
                    att_b, oi_b, st_b = _scan_dir(part(i, 0, rb), v_b, part(i, 3, rb), lbs[i][1:2, :],
                                                  st_ref[2 * i + 1], cb_ref[...], lvb_ref[...], True)
                    st_ref[2 * i] = st_f
                    st_ref[2 * i + 1] = st_b
                    of = _dot(att_f.astype(BF16), v_f.astype(BF16)) + oi_f
                    ob = _dot(att_b.astype(BF16), v_b.astype(BF16)) + oi_b

                    @pl.when(2 * tb < n_blocks)
                    def _():
                        oacc_ref[rf, cols] = of
                        oacc_ref[rb, cols] = ob

                    @pl.when(2 * tb >= n_blocks)
                    def _():
                        oacc_ref[rf, cols] += of
                        oacc_ref[rb, cols] += ob

                return c2

            lax.fori_loop(0, n_blocks, body, 0)
            for i, hd in enumerate(heads):
                cols = slice(i * HEAD_DIM, (i + 1) * HEAD_DIM)
                og_ref[:, _lane_block(hd, HEAD_DIM)] = _scan_finish(oacc_ref[:, cols], part(i, 4), on)
        return carry

    lax.fori_loop(0, HEADS // HEADS_PER_STEP, pair, 0)
    xo_ref[...] = x_ref[...] + gate_ref[...] * _dot(og_ref[...], wout_ref[...])


def _hgrn_layer(x, n_seq, t, cond_of_seq, g, modr, w_in, lbs_j, onorm_g, w_out, s0, sfin_prev, slot):
    d = D_MODEL
    has_state = s0 is not None
    n_blocks = t // BLK
    assert n_blocks == 1 or n_blocks % 2 == 0
    full = lambda shape, **kw: pl.BlockSpec(shape, lambda b, *_: (0,) * len(shape), **kw)
    in_specs = [
        pl.BlockSpec((t, d), lambda b: (b, 0)),
        full((1, d)),
        _mod_spec(cond_of_seq, 0), _mod_spec(cond_of_seq, 1), _mod_spec(cond_of_seq, 2),
        full((d, HGRN_PARTS * d), **_RESIDENT),
        full((2, d)),
        full((1, HEAD_DIM)),
    ] + [full((BLK, BLK))] * 4
    args = [x, g.reshape(1, d), modr, modr, modr, w_in, lbs_j, onorm_g.reshape(1, HEAD_DIM)] + _scan_constants()
    state_block = (None, 2, HEADS, HEAD_DIM, HEAD_DIM)
    if has_state:
        in_specs.append(pl.BlockSpec(state_block, lambda b: (b, 0, 0, 0, 0)))
        args.append(s0)
    in_specs.append(full((d, d), **_RESIDENT))
    args.append(w_out)
    out_specs = [pl.BlockSpec((t, d), lambda b: (b, 0))]
    out_shape = [jax.ShapeDtypeStruct((n_seq * t, d), F32)]
    scratch = [pltpu.VMEM((t, d), BF16), pltpu.VMEM((t, HGRN_PARTS * PAIR_W), F32), pltpu.VMEM((t, d), BF16)]
    aliases = {}
    if has_state:
        scratch += [pltpu.VMEM((t, PAIR_W), F32), pltpu.VMEM((2 * HEADS_PER_STEP, HEAD_DIM, HEAD_DIM), F32)]
    else:
        state_dims = (2, HEADS, HEAD_DIM, HEAD_DIM)
        if sfin_prev is None:
            out_specs.append(pl.BlockSpec((None, N_SLOTS) + state_dims, lambda b: (b, 0, 0, 0, 0, 0)))
        else:
            out_specs.append(pl.BlockSpec((None, None) + state_dims, lambda b: (b, slot, 0, 0, 0, 0)))
            in_specs.append(pl.BlockSpec(memory_space=pl.ANY))
            args.append(sfin_prev)
            aliases = {len(args) - 1: 1}
        out_shape.append(jax.ShapeDtypeStruct((n_seq, N_SLOTS) + state_dims, F32))

    def body(*refs):
        if sfin_prev is not None:
            n_in = len(args)
            refs = refs[:n_in - 1] + refs[n_in:]
        _hgrn_kernel(*refs, n_blocks=n_blocks, has_state=has_state, slot=slot, owns_all_slots=sfin_prev is None)

    out = pl.pallas_call(
        body,
        grid=(n_seq,),
        in_specs=in_specs,
        out_specs=out_specs,
        out_shape=out_shape,
        scratch_shapes=scratch,
        input_output_aliases=aliases,
        compiler_params=_cparams(("parallel",)),
        name="hgrn_layer_sample" if has_state else "hgrn_layer_prompt",
    )(*args)
    return out if not has_state else (out[0], None)


def _rope_tables(t_lat):
    rows = t_lat // GRID_W
    row = jnp.repeat(jnp.arange(rows), GRID_W).astype(F32)
    col = jnp.tile(jnp.arange(GRID_W), rows).astype(F32)
    half = QK_DIM // 2
    inv_freq = ROPE_THETA ** (-jnp.arange(0, half, 2, dtype=F32) / half)
    ang_row = row[:, None] * inv_freq
    ang_col = col[:, None] * inv_freq

    def part(ang):
        c, s = jnp.cos(ang), jnp.sin(ang)
        return jnp.concatenate([c, c], axis=1), jnp.concatenate([-s, s], axis=1)

    cr, sr = part(ang_row)
    cc, sc = part(ang_col)
    cos64 = jnp.concatenate([cr, cc], axis=1)
    sin64 = jnp.concatenate([sr, sc], axis=1)
    return jnp.concatenate([cos64, cos64], axis=1), jnp.concatenate([sin64, sin64], axis=1)


def _rope(x, cos, sin):
    lane = lax.broadcasted_iota(jnp.int32, x.shape, 1)
    first = (lane % (QK_DIM // 2)) < (QK_DIM // 4)
    swapped = jnp.where(first, pltpu.roll(x, LANES - QK_DIM // 4, 1), pltpu.roll(x, QK_DIM // 4, 1))
    return x * cos + swapped * sin


def _attn_kernel(*refs, n_qblk, rope, cache, emit_kv, lam_init, slot, owns_all_slots):
    it = iter(refs)
    x_ref, g_ref, sh_ref, sc_ref, gate_ref, wqkv_ref = (next(it) for _ in range(6))
    qg_ref, kg_ref, lam_ref, sg_ref, seg_ref = (next(it) for _ in range(5))
    cos_ref = sin_ref = ck_ref = cv_ref = nk_ref = nv_ref = None
    if rope:
        cos_ref, sin_ref = next(it), next(it)
    if cache:
        ck_ref, cv_ref = next(it), next(it)
    wout_ref, xo_ref = next(it), next(it)
    if emit_kv:
        nk_ref = _slot_view(next(it), slot, owns_all_slots)
        nv_ref = _slot_view(next(it), slot, owns_all_slots)
    h_ref, qkv_ref, oa_ref = next(it), next(it), next(it)

    d = D_MODEL
    h_ref[...] = _norm_mod(x_ref[...], g_ref[...], sc_ref[...], sh_ref[...]).astype(BF16)
    qkv_ref[...] = _dot(h_ref[...], wqkv_ref[...])
    seg = seg_ref[...]
    lp = lam_ref[...]
    lam = (jnp.exp(jnp.sum(lp[0:1, :] * lp[1:2, :], axis=-1, keepdims=True))
           - jnp.exp(jnp.sum(lp[2:3, :] * lp[3:4, :], axis=-1, keepdims=True)) + lam_init)
    lane = lax.broadcasted_iota(jnp.int32, (1, LANES), 1)
    comp0 = lane < QK_DIM
    scale = QK_DIM ** -0.5

    def split(a):
        return [jnp.where(comp0, a, 0.0).astype(BF16), jnp.where(comp0, 0.0, a).astype(BF16)]

    def one_head(hd):
        hcol = _lane_block(hd, HEAD_DIM)
        k = qkv_ref[:, pl.ds(pl.multiple_of(d + hd * HEAD_DIM, HEAD_DIM), HEAD_DIM)]
        v = qkv_ref[:, pl.ds(pl.multiple_of(2 * d + hd * HEAD_DIM, HEAD_DIM), HEAD_DIM)]
        kn = k * lax.rsqrt(_dot(k * k, seg) + EPS) * kg_ref[...]
        if emit_kv:
            nk_ref[:, hcol] = kn
            nv_ref[:, hcol] = v
        if rope:
            kn = _rope(kn, cos_ref[...], sin_ref[...])
        ks = split(kn)
        vb = v.astype(BF16)
        if cache:
            cks = split(ck_ref[:, hcol])
            cvb = cv_ref[:, hcol].astype(BF16)
        for qi in range(n_qblk):
            rows = slice(qi * BLK, (qi + 1) * BLK)
            q = qkv_ref[rows, hcol]
            qn = q * lax.rsqrt(_dot(q * q, seg) + EPS) * qg_ref[...]
            if rope:
                qn = _rope(qn, cos_ref[rows, :], sin_ref[rows, :])
            qb = (qn * scale).astype(BF16)
            a_self, a_cache = None, None
            for c in range(2):
                s = _dot_nt(qb, ks[c])
                m = jnp.max(s, axis=-1, keepdims=True)
                if cache:
                    sc = _dot_nt(qb, cks[c])
                    m = jnp.maximum(m, jnp.max(sc, axis=-1, keepdims=True))
                p = jnp.exp(s - m)
                den = jnp.sum(p, axis=-1, keepdims=True)
                if cache:
                    pc = jnp.exp(sc - m)
                    den = den + jnp.sum(pc, axis=-1, keepdims=True)
                w = (1.0 / den) if c == 0 else (-lam / den)
                a_self = p * w if c == 0 else a_self + p * w
                if cache:
                    a_cache = pc * w if c == 0 else a_cache + pc * w
            o = _dot(a_self.astype(BF16), vb)
            if cache:
                o = o + _dot(a_cache.astype(BF16), cvb)
            ms = jnp.mean(o * o, axis=-1, keepdims=True)
            o = (o * lax.rsqrt(ms + EPS) * sg_ref[...]) * (1.0 - lam_init)
            oa_ref[rows, hcol] = o.astype(BF16)

    def pair(hp, carry):
        for i in range(HEADS_PER_STEP):
            one_head(hp * HEADS_PER_STEP + i)
        return carry

    lax.fori_loop(0, HEADS // HEADS_PER_STEP, pair, 0)
    xo_ref[...] = x_ref[...] + gate_ref[...] * _dot(oa_ref[...], wout_ref[...])


def _attn_layer(x, n_seq, t, cond_of_seq, g, modr, w_qkv, qn_g, kn_g, lam_p, subln_g, w_out, layer_idx,
                cache_k_j, cache_v_j, kv_prev, slot):
    d = D_MODEL
    cache = cache_k_j is not None
    lam_init = 0.8 - 0.6 * math.exp(-0.3 * layer_idx)
    qg = jnp.tile(qn_g.reshape(1, QK_DIM), (1, 2))
    kg = jnp.tile(kn_g.reshape(1, QK_DIM), (1, 2))
    li = np.arange(LANES)
    seg = jnp.asarray((li[:, None] // QK_DIM == li[None, :] // QK_DIM).astype(np.float32) / QK_DIM)
    full = lambda shape, **kw: pl.BlockSpec(shape, lambda b, *_: (0,) * len(shape), **kw)
    in_specs = [
        pl.BlockSpec((t, d), lambda b: (b, 0)),
        full((1, d)),
        _mod_spec(cond_of_seq, 0), _mod_spec(cond_of_seq, 1), _mod_spec(cond_of_seq, 2),
        full((d, 3 * d), **_RESIDENT),
        full((1, LANES)), full((1, LANES)), full((4, QK_DIM)), full((1, HEAD_DIM)), full((LANES, LANES)),
    ]
    args = [x, g.reshape(1, d), modr, modr, modr, w_qkv, qg, kg, lam_p, subln_g.reshape(1, HEAD_DIM), seg]
    if cache:
        cos, sin = _rope_tables(t)
        past = cache_k_j.shape[1]
        in_specs += [full((t, LANES)), full((t, LANES)),
                     pl.BlockSpec((None, past, d), lambda b: (b, 0, 0)),
                     pl.BlockSpec((None, past, d), lambda b: (b, 0, 0))]
        args += [cos, sin, cache_k_j.reshape(n_seq, past, d), cache_v_j.reshape(n_seq, past, d)]
    in_specs.append(full((d, d), **_RESIDENT))
    args.append(w_out)
    out_specs = [pl.BlockSpec((t, d), lambda b: (b, 0))]
    out_shape = [jax.ShapeDtypeStruct((n_seq * t, d), F32)]
    aliases = {}
    n_carried = 0
    if not cache:
        if kv_prev is None:
            kv_spec = pl.BlockSpec((None, N_SLOTS, t, d), lambda b: (b, 0, 0, 0))
        else:
            kv_spec = pl.BlockSpec((None, None, t, d), lambda b: (b, slot, 0, 0))
            in_specs += [pl.BlockSpec(memory_space=pl.ANY)] * 2
            args += list(kv_prev)
            aliases = {len(args) - 2: 1, len(args) - 1: 2}
            n_carried = 2
        out_specs += [kv_spec, kv_spec]
        out_shape += [jax.ShapeDtypeStruct((n_seq, N_SLOTS, t, d), F32)] * 2

    def body(*refs):
        n_in = len(args)
        refs = refs[:n_in - n_carried] + refs[n_in:]
        _attn_kernel(*refs, n_qblk=t // BLK, rope=cache, cache=cache, emit_kv=not cache, lam_init=lam_init,
                     slot=slot, owns_all_slots=kv_prev is None)

    out = pl.pallas_call(
        body,
        grid=(n_seq,),
        in_specs=in_specs,
        out_specs=out_specs,
        out_shape=out_shape,
        scratch_shapes=[pltpu.VMEM((t, d), BF16), pltpu.VMEM((t, 3 * d), F32), pltpu.VMEM((t, d), BF16)],
        input_output_aliases=aliases,
        compiler_params=_cparams(("parallel",)),
        name="attn_layer_sample" if cache else "attn_layer_prompt",
    )(*args)
    return (out[0], None) if cache else (out[0], (out[1], out[2]))


ROUTE_ROWS = 32
INFO_GID, INFO_RANK = 8, 9


def _two_stream_specs(n_prompt_tiles, width):
    return [pl.BlockSpec((BLK, width), lambda i, *_: (jnp.minimum(i, n_prompt_tiles - 1), 0)),
            pl.BlockSpec((BLK, width), lambda i, *_: (jnp.maximum(i - n_prompt_tiles, 0), 0))]


def _route_kernel(xp_ref, xs_ref, g_ref, sh_ref, sc_ref, wr_ref, br_ref, tri_ref,
                  h_ref, il_ref, it_ref, cum_ref, tot_ref, carry_ref, *, n_prompt_tiles):
    i = pl.program_id(0)

    @pl.when(i == 0)
    def _():
        carry_ref[...] = jnp.zeros_like(carry_ref)

    @pl.when(i < n_prompt_tiles)
    def _():
        h_ref[...] = _norm_mod(xp_ref[...], g_ref[...], sc_ref[...], sh_ref[...]).astype(BF16)

    @pl.when(i >= n_prompt_tiles)
    def _():
        h_ref[...] = _norm_mod(xs_ref[...], g_ref[...], sc_ref[...], sh_ref[...]).astype(BF16)

    logit = _dot_nt(wr_ref[...], h_ref[...]) + br_ref[...]
    gl = [logit[g:g + 1, :] for g in range(MOE_GROUPS)]
    gmax = functools.reduce(jnp.maximum, gl)
    gz = functools.reduce(lambda a, b: a + b, [jnp.exp(x - gmax) for x in gl])
    g_w = 1.0 / gz
    gid = jnp.full_like(gmax, MOE_GROUPS - 1)
    for g in range(MOE_GROUPS - 2, -1, -1):
        gid = jnp.where(gl[g] == gmax, float(g), gid)
    el = []
    for j in range(MOE_EPG):
        e = logit[MOE_GROUPS + j:MOE_GROUPS + j + 1, :]
        for g in range(1, MOE_GROUPS):
            r = MOE_GROUPS + g * MOE_EPG + j
            e = jnp.where(gid == float(g), logit[r:r + 1, :], e)
        el.append(e)
    emax = functools.reduce(jnp.maximum, el)
    pe = [jnp.exp(e - emax) for e in el]
    idx1 = jnp.full_like(emax, MOE_EPG - 1)
    for j in range(MOE_EPG - 2, -1, -1):
        idx1 = jnp.where(el[j] == emax, float(j), idx1)
    el2 = [jnp.where(idx1 == float(j), -jnp.inf, el[j]) for j in range(MOE_EPG)]
    emax2 = functools.reduce(jnp.maximum, el2)
    idx2 = jnp.full_like(emax, MOE_EPG - 1)
    for j in range(MOE_EPG - 2, -1, -1):
        idx2 = jnp.where(el2[j] == emax2, float(j), idx2)
    sel = [(idx1 == float(j)) | (idx2 == float(j)) for j in range(MOE_EPG)]
    den = functools.reduce(lambda a, b: a + b, [jnp.where(sel[j], pe[j], 0.0) for j in range(MOE_EPG)])
    cw = [jnp.where(sel[j], pe[j] * (g_w / den), 0.0) for j in range(MOE_EPG)]

    row8 = lax.broadcasted_iota(jnp.int32, (8, BLK), 0)
    onehot = jnp.where(row8.astype(F32) == gid, 1.0, 0.0)
    within = _dot(onehot.astype(BF16), tri_ref[...])
    carry = carry_ref[...]
    rank = jnp.sum(onehot * (within + carry[:, 0:1]), axis=0, keepdims=True)
    cum_ref[...] = carry
    new_carry = carry + jnp.sum(onehot, axis=1, keepdims=True)
    carry_ref[...] = new_carry
    tot_ref[...] = new_carry

    il_ref[...] = jnp.where(row8 == 0, gid, jnp.where(row8 == 1, rank, 0.0))
    rowl = lax.broadcasted_iota(jnp.int32, (LANES, BLK), 0)
    m = jnp.zeros((LANES, BLK), F32)
    for j in range(MOE_EPG):
        m = jnp.where(rowl == j, cw[j], m)
    m = jnp.where(rowl == INFO_GID, gid, m)
    m = jnp.where(rowl == INFO_RANK, rank, m)
    it_ref[...] = m.T


def _moe_route(xp, xs, g, modr, wr_t, br, cond_of_tile):
    d = D_MODEL
    npt = xp.shape[0] // BLK
    nt = npt + xs.shape[0] // BLK
    n = nt * BLK
    tri = jnp.asarray(np.triu(np.ones((BLK, BLK), np.float32), 1), BF16)
    full = lambda shape: pl.BlockSpec(shape, lambda i: (0,) * len(shape))
    return pl.pallas_call(
        functools.partial(_route_kernel, n_prompt_tiles=npt),
        grid=(nt,),
        in_specs=_two_stream_specs(npt, d) + [
            full((1, d)),
            _mod_spec(cond_of_tile, 3), _mod_spec(cond_of_tile, 4),
            full((ROUTE_ROWS, d)), full((ROUTE_ROWS, 1)), full((BLK, BLK)),
        ],
        out_specs=[
            pl.BlockSpec((BLK, d), lambda i: (i, 0)),
            pl.BlockSpec((8, BLK), lambda i: (0, i)),
            pl.BlockSpec((BLK, LANES), lambda i: (i, 0)),
            pl.BlockSpec((None, 8, LANES), lambda i: (i, 0, 0)),
            pl.BlockSpec((8, LANES), lambda i: (0, 0)),
        ],
        out_shape=[
            jax.ShapeDtypeStruct((n, d), BF16),
            jax.ShapeDtypeStruct((8, n), F32),
            jax.ShapeDtypeStruct((n, LANES), F32),
            jax.ShapeDtypeStruct((nt, 8, LANES), F32),
            jax.ShapeDtypeStruct((8, LANES), F32),
        ],
        scratch_shapes=[pltpu.VMEM((8, LANES), F32)],
        compiler_params=_cparams(("arbitrary",)),
        name="moe_route",
    )(xp, xs, g.reshape(1, d), modr, modr, wr_t, br, tri)


def _sorted_pos(gid, rank, rstart_ref):
    p = rank
    for g in range(MOE_GROUPS):
        p = p + jnp.where(gid == float(g), rstart_ref[g].astype(F32), 0.0)
    return p


def _gather_kernel(valid_ref, clo_ref, chi_ref, rstart_ref, h_ref, il_ref, it_ref,
                   hs_ref, cws_ref, acch_ref, accc_ref, *, n_tiles):
    a = pl.program_id(0)
    win = GATHER_WIN * BLK

    @pl.when(valid_ref[a] == 0)
    def _():
        hs_ref[...] = jnp.zeros_like(hs_ref)
        cws_ref[...] = jnp.zeros_like(cws_ref)

    @pl.when(valid_ref[a] != 0)
    def _():
        acch_ref[...] = jnp.zeros_like(acch_ref)
        accc_ref[...] = jnp.zeros_like(accc_ref)
        dest = (lax.broadcasted_iota(jnp.int32, (BLK, 1), 0) + a * BLK).astype(F32)
        src_tile = lax.broadcasted_iota(jnp.int32, (1, win), 1) // BLK
        clo = clo_ref[a]
        n_win = (chi_ref[a] - clo + GATHER_WIN) // GATHER_WIN

        def body(w, carry):
            first = clo + w * GATHER_WIN
            c0 = jnp.minimum(first, n_tiles - GATHER_WIN)
            rows = pl.ds(pl.multiple_of(c0 * BLK, BLK), win)
            info = il_ref[:, rows]
            p = _sorted_pos(info[0:1, :], info[1:2, :], rstart_ref)
            p = jnp.where(src_tile + c0 >= first, p, -1.0)
            onehot = jnp.where(dest == p, 1.0, 0.0).astype(BF16)
            acch_ref[...] += _dot(onehot, h_ref[rows, :])
            cw = it_ref[rows, :]
            hi = cw.astype(BF16)
            lo = (cw - hi.astype(F32)).astype(BF16)
            accc_ref[...] += _dot(onehot, hi) + _dot(onehot, lo)
            return carry

        lax.fori_loop(0, n_win, body, 0)
        hs_ref[...] = acch_ref[...].astype(BF16)
        cws_ref[...] = accc_ref[...]


def _moe_gather(h, info_lane, info_tok, valid, clo, chi, rstart, n_dest_tiles):
    n, d = h.shape
    nt = n // BLK
    assert nt >= GATHER_WIN
    resident = dict(pipeline_mode=pl.Buffered(1))
    return pl.pallas_call(
        functools.partial(_gather_kernel, n_tiles=nt),
        grid_spec=pltpu.PrefetchScalarGridSpec(
            num_scalar_prefetch=4,
            grid=(n_dest_tiles,),
            in_specs=[
                pl.BlockSpec((n, d), lambda a, *_: (0, 0), **resident),
                pl.BlockSpec((8, n), lambda a, *_: (0, 0), **resident),
                pl.BlockSpec((n, LANES), lambda a, *_: (0, 0), **resident),
            ],
            out_specs=[
                pl.BlockSpec((BLK, d), lambda a, *_: (a, 0)),
                pl.BlockSpec((BLK, LANES), lambda a, *_: (a, 0)),
            ],
            scratch_shapes=[pltpu.VMEM((BLK, d), F32), pltpu.VMEM((BLK, LANES), F32)],
        ),
        out_shape=[
            jax.ShapeDtypeStruct((n_dest_tiles * BLK, d), BF16),
            jax.ShapeDtypeStruct((n_dest_tiles * BLK, LANES), F32),
        ],
        compiler_params=_cparams(("arbitrary",)),
        name="moe_gather",
    )(valid, clo, chi, rstart, h, info_lane, info_tok)


def _moe_mlp_kernel(sgroup_ref, snt_ref, hs_ref, cws_ref, wg_ref, wu_ref, wd_ref, ys_ref,
                    acc_ref, wgb_ref, wub_ref, wdb_ref):
    s = pl.program_id(0)
    k = pl.program_id(1)
    nt = snt_ref[s]
    wgb_ref[...] = wg_ref[...].astype(BF16)
    wub_ref[...] = wu_ref[...].astype(BF16)
    wdb_ref[...] = wd_ref[...].astype(BF16)
    for j in range(SUPER):
        rows = slice(j * BLK, (j + 1) * BLK)

        @pl.when(j < nt)
        def _():
            hsub = hs_ref[rows, :]
            gate = _dot(hsub, wgb_ref[...])
            up = _dot(hsub, wub_ref[...])
            cws = cws_ref[rows, :]
            cwk = jnp.zeros((BLK, 1), F32)
            for kk in range(MOE_EPG):
                cwk = jnp.where(k == kk, cws[:, kk:kk + 1], cwk)
            act = ((_silu(gate) * up) * cwk).astype(BF16)
            y = _dot(act, wdb_ref[...])

            @pl.when(k == 0)
            def _():
                acc_ref[rows, :] = y

            @pl.when(k != 0)
            def _():
                acc_ref[rows, :] += y

            @pl.when(k == MOE_EPG - 1)
            def _():
                ys_ref[rows, :] = acc_ref[rows, :].astype(BF16)

        @pl.when((j >= nt) & (k == MOE_EPG - 1))
        def _():
            ys_ref[rows, :] = jnp.zeros((BLK, D_MODEL), BF16)


def _moe_mlp(hs, cws, w_gate, w_up, w_down, layer, sgroup, snt, n_super):
    d = hs.shape[1]
    widx = lambda s, k, sg, sn: (layer * MOE_EXPERTS + sg[s] * MOE_EPG + k, 0, 0)
    return pl.pallas_call(
        _moe_mlp_kernel,
        grid_spec=pltpu.PrefetchScalarGridSpec(
            num_scalar_prefetch=2,
            grid=(n_super, MOE_EPG),
            in_specs=[
                pl.BlockSpec((SUPER_ROWS, d), lambda s, k, *_: (s, 0)),
                pl.BlockSpec((SUPER_ROWS, LANES), lambda s, k, *_: (s, 0)),
                pl.BlockSpec((None, d, MOE_D_FF), widx),
                pl.BlockSpec((None, d, MOE_D_FF), widx),
                pl.BlockSpec((None, MOE_D_FF, d), widx),
            ],
            out_specs=pl.BlockSpec((SUPER_ROWS, d), lambda s, k, *_: (s, 0)),
            scratch_shapes=[
                pltpu.VMEM((SUPER_ROWS, d), F32),
                pltpu.VMEM((d, MOE_D_FF), BF16),
                pltpu.VMEM((d, MOE_D_FF), BF16),
                pltpu.VMEM((MOE_D_FF, d), BF16),
            ],
        ),
        out_shape=jax.ShapeDtypeStruct((n_super * SUPER_ROWS, d), BF16),
        compiler_params=_cparams(("arbitrary", "arbitrary")),
        name="moe_experts",
    )(sgroup, snt, hs, cws, w_gate, w_up, w_down)


N_SRC = 2 * MOE_GROUPS


def _unsort_kernel(ut_ref, rstart_ref, xp_ref, xs_ref, it_ref, gate_ref, *rest, n_prompt_tiles):
    ys_refs, op_ref, os_ref = rest[:N_SRC], rest[N_SRC], rest[N_SRC + 1]
    t = pl.program_id(0)
    info = it_ref[...]
    p = _sorted_pos(info[:, INFO_GID:INFO_GID + 1], info[:, INFO_RANK:INFO_RANK + 1], rstart_ref)
    lane = lax.broadcasted_iota(jnp.int32, (1, BLK), 1).astype(F32)
    y = None
    for m in range(N_SRC):
        a = ut_ref[t * N_SRC + m]
        onehot = jnp.where(p - (a * BLK).astype(F32) == lane, 1.0, 0.0).astype(BF16)
        part = _dot(onehot, ys_refs[m][...])
        y = part if y is None else y + part

    @pl.when(t < n_prompt_tiles)
    def _():
        op_ref[...] = xp_ref[...] + gate_ref[...] * y

    @pl.when(t >= n_prompt_tiles)
    def _():
        os_ref[...] = xs_ref[...] + gate_ref[...] * y


def _moe_unsort(xp, xs, info_tok, modr, ys, ut, rstart, cond_of_tile):
    d = D_MODEL
    npt = xp.shape[0] // BLK
    nt = npt + xs.shape[0] // BLK

    def ys_spec(m):
        return pl.BlockSpec((BLK, d), lambda t, ut_r, rs_r: (jnp.maximum(ut_r[t * N_SRC + m], 0), 0))

    return pl.pallas_call(
        functools.partial(_unsort_kernel, n_prompt_tiles=npt),
        grid_spec=pltpu.PrefetchScalarGridSpec(
            num_scalar_prefetch=2,
            grid=(nt,),
            in_specs=_two_stream_specs(npt, d) + [
                pl.BlockSpec((BLK, LANES), lambda t, *_: (t, 0)),
                _mod_spec(cond_of_tile, 5),
            ] + [ys_spec(m) for m in range(N_SRC)],
            out_specs=_two_stream_specs(npt, d),
        ),
        out_shape=[jax.ShapeDtypeStruct(xp.shape, F32), jax.ShapeDtypeStruct(xs.shape, F32)],
        compiler_params=_cparams(("arbitrary",)),
        name="moe_unsort_residual",
    )(ut, rstart, xp, xs, info_tok, modr, *([ys] * N_SRC))


def _moe_tables(cum, tot, n_tiles, n_super):
    cumc = jnp.concatenate([cum[:, :MOE_GROUPS, 0], tot[None, :MOE_GROUPS, 0]], axis=0).astype(jnp.int32)
    total = cumc[-1]
    n_sup_g = (total + SUPER_ROWS - 1) // SUPER_ROWS
    sup_end = jnp.cumsum(n_sup_g)
    sup_start = sup_end - n_sup_g
    rstart = sup_start * SUPER_ROWS
    s = jnp.arange(n_super)
    sgroup = jnp.minimum(jnp.sum(s[:, None] >= sup_end[None, :], axis=1), MOE_GROUPS - 1)
    rows_left = total[sgroup] - (s - sup_start[sgroup]) * SUPER_ROWS
    snt = jnp.where(s < sup_end[-1], jnp.clip((rows_left + BLK - 1) // BLK, 0, SUPER), 0)
    a = jnp.arange(n_super * SUPER)
    sa, ja = a // SUPER, a % SUPER
    valid = (ja < snt[sa]).astype(jnp.int32)
    ga = sgroup[sa]
    r0 = ((sa - sup_start[ga]) * SUPER + ja) * BLK
    cg = cumc[:, ga]
    clo = jnp.sum(cg[1:] <= r0[None, :], axis=0)
    chi = jnp.sum(cg[:-1] < (r0 + BLK)[None, :], axis=0) - 1
    clo = jnp.clip(clo, 0, n_tiles - 1)
    chi = jnp.clip(chi, clo, n_tiles - 1)
    first = rstart[None, :] + cumc[:-1]
    last = rstart[None, :] + cumc[1:] - 1
    has = cumc[1:] > cumc[:-1]
    a0 = jnp.where(has, first // BLK, -1)
    a1 = jnp.where(has & (last // BLK != first // BLK), last // BLK, -1)
    ut = jnp.stack([a0, a1], axis=-1).reshape(-1)
    i32 = lambda v: v.astype(jnp.int32)
    return i32(rstart), i32(sgroup), i32(snt), valid, i32(clo), i32(chi), i32(ut)


def _moe_layer(xp, xs, g, modr, wr_t, br, w_gate, w_up, w_down, layer, cond_of_tile):
    nt = (xp.shape[0] + xs.shape[0]) // BLK
    n_super = (nt * BLK + SUPER_ROWS - 1) // SUPER_ROWS + MOE_GROUPS
    h, info_lane, info_tok, cum, tot = _moe_route(xp, xs, g, modr, wr_t, br, cond_of_tile)
    rstart, sgroup, snt, valid, clo, chi, ut = _moe_tables(cum, tot, nt, n_super)
    hs, cws = _moe_gather(h, info_lane, info_tok, valid, clo, chi, rstart, n_super * SUPER)
    ys = _moe_mlp(hs, cws, w_gate, w_up, w_down, layer, sgroup, snt, n_super)
    return _moe_unsort(xp, xs, info_tok, modr, ys, ut, rstart, cond_of_tile)


def kernel(x_prompt, x_sample, c, cache_k, cache_v, state_hgrn, c_ctx, norm_g, w_ada, b_ada, hgrn_w_in, hgrn_lb_logits, hgrn_onorm_g, hgrn_w_out, attn_w_qkv, attn_qn_g, attn_kn_g, attn_lambda, attn_subln_g, attn_w_out, moe_w_group, moe_b_group, moe_w_expert, moe_b_expert, moe_w_gate, moe_w_up, moe_w_down):
    n_prompt_seq, seq, d = x_prompt.shape
    dec_batch, dec_seq, _ = x_sample.shape
    n_prompt = n_prompt_seq * seq
    assert d == D_MODEL and seq == BLK and dec_seq % BLK == 0
    assert 1 + dec_batch <= N_COND

    xp = x_prompt.reshape(n_prompt, d)
    xs = x_sample.reshape(dec_batch * dec_seq, d)
    cond = jnp.zeros((N_COND, d), F32).at[0].set(c_ctx).at[1:1 + dec_batch].set(c)
    mod = _modulation(cond, w_ada, b_ada)

    lbs = jnp.cumsum(jax.nn.softmax(hgrn_lb_logits.astype(F32), axis=0), axis=0)
    lbs = lbs - lbs[0:1]

    cond_prompt = lambda b: 0
    cond_sample = lambda b: 1 + b
    npt, spt = n_prompt // BLK, dec_seq // BLK
    cond_tile = lambda i: jnp.where(i < npt, 0, 1 + (i - npt) // spt)

    n_pair = HEADS // HEADS_PER_STEP
    w_in = hgrn_w_in.astype(BF16).reshape(-1, d, HGRN_PARTS, n_pair, PAIR_W)
    w_in = w_in.transpose(0, 1, 3, 2, 4).reshape(-1, d, HGRN_PARTS * d)
    w_hout = hgrn_w_out.astype(BF16)
    w_qkv = attn_w_qkv.astype(BF16)
    w_aout = attn_w_out.astype(BF16)
    wr_t = jnp.zeros((DEPTH, ROUTE_ROWS, d), F32)
    wr_t = wr_t.at[:, :MOE_GROUPS].set(moe_w_group.transpose(0, 2, 1))
    wr_t = wr_t.at[:, MOE_GROUPS:MOE_GROUPS + MOE_EXPERTS].set(moe_w_expert.transpose(0, 2, 1)).astype(BF16)
    br = jnp.zeros((DEPTH, ROUTE_ROWS, 1), F32)
    br = br.at[:, :MOE_GROUPS, 0].set(moe_b_group).at[:, MOE_GROUPS:MOE_GROUPS + MOE_EXPERTS, 0].set(moe_b_expert)
    w_gate = moe_w_gate.reshape(DEPTH * MOE_EXPERTS, d, MOE_D_FF)
    w_up = moe_w_up.reshape(DEPTH * MOE_EXPERTS, d, MOE_D_FF)
    w_down = moe_w_down.reshape(DEPTH * MOE_EXPERTS, MOE_D_FF, d)

    sfin, kv = None, None
    for i in range(DEPTH):
        j = i // 2
        modr = mod[i].reshape(N_COND * N_MOD, 1, d)
        if i % 2 == 0:
            common = (norm_g[i, 0], modr, w_in[j], lbs[j], hgrn_onorm_g[j], w_hout[j])
            xp, sfin = _hgrn_layer(xp, n_prompt_seq, seq, cond_prompt, *common, None, sfin, j)
            xs, _ = _hgrn_layer(xs, dec_batch, dec_seq, cond_sample, *common, state_hgrn[:, j], None, j)
        else:
            common = (norm_g[i, 0], modr, w_qkv[j], attn_qn_g[j], attn_kn_g[j], attn_lambda[j], attn_subln_g[j],
                      w_aout[j], i)
            xp, kv = _attn_layer(xp, n_prompt_seq, seq, cond_prompt, *common, None, None, kv, j)
            xs, _ = _attn_layer(xs, dec_batch, dec_seq, cond_sample, *common, cache_k[:, j], cache_v[:, j], None, j)
        xp, xs = _moe_layer(xp, xs, norm_g[i, 1], modr, wr_t[i], br[i], w_gate, w_up, w_down, i, cond_tile)

    new_k = kv[0].reshape(n_prompt_seq, DEPTH // 2, seq, HEADS, 2, QK_DIM)
    new_v = kv[1].reshape(n_prompt_seq, DEPTH // 2, seq, HEADS, HEAD_DIM)
    return (xp.reshape(n_prompt_seq, seq, d), xs.reshape(dec_batch, dec_seq, d), new_k, new_v, sfin)
```

```python
import functools
import math

import numpy as np
import jax
import jax.numpy as jnp
from jax import lax
from jax.experimental import pallas as pl
from jax.experimental.pallas import tpu as pltpu

F32 = jnp.float32
BF16 = jnp.bfloat16

D_MODEL = 1024
DEPTH = 4
GRID_W = 64
HEADS = 8
HEAD_DIM = 128
QK_DIM = 64
ROPE_THETA = 10000.0
MOE_GROUPS = 4
MOE_EPG = 4
MOE_EXPERTS = MOE_GROUPS * MOE_EPG
MOE_D_FF = 512
EPS = 1e-6
N_COND = 8
N_MOD = 6
HGRN_PARTS = 5

LANES = 128
BLK = 256
CHUNK = 32
N_CHUNK = BLK // CHUNK
HEADS_PER_STEP = 2
SCAN_HEADS_PER_BODY = 4
ATTN_HEADS_PER_BODY = 8
SUPER = 8
SUPER_ROWS = SUPER * BLK
EXPERT_TILES = 2
FF_PART = 256
GATHER_WIN = 6
EXP_CLAMP = 80.0
VMEM_LIMIT = 56 * 1024 * 1024
N_SLOTS = DEPTH // 2
_RESIDENT = dict(pipeline_mode=pl.Buffered(1))


def _cparams(sem):
    return pltpu.CompilerParams(dimension_semantics=sem, vmem_limit_bytes=VMEM_LIMIT)


def _silu(x):
    return x * jax.nn.sigmoid(x)


def _dot(a, b):
    return jnp.dot(a, b, preferred_element_type=F32)


def _dot_nt(a, b):
    return lax.dot_general(a, b, (((1,), (1,)), ((), ())), preferred_element_type=F32)


def _dot_tn(a, b):
    return lax.dot_general(a, b, (((0,), (0,)), ((), ())), preferred_element_type=F32)


def _lane_block(i, width):
    return pl.ds(pl.multiple_of(i * width, width), width)


def _mod_kernel(c_ref, w_ref, b_ref, o_ref):
    o_ref[...] = _dot(_silu(c_ref[...]), w_ref[...]) + b_ref[...]


def _modulation(cond, w_ada, b_ada):
    tn = 1536
    nj = (N_MOD * D_MODEL) // tn
    return pl.pallas_call(
        _mod_kernel,
        grid=(DEPTH, nj),
        in_specs=[
            pl.BlockSpec((N_COND, D_MODEL), lambda l, j: (0, 0)),
            pl.BlockSpec((None, D_MODEL, tn), lambda l, j: (l, 0, j)),
            pl.BlockSpec((None, 1, tn), lambda l, j: (l, 0, j)),
        ],
        out_specs=pl.BlockSpec((None, N_COND, tn), lambda l, j: (l, 0, j)),
        out_shape=jax.ShapeDtypeStruct((DEPTH, N_COND, N_MOD * D_MODEL), F32),
        compiler_params=_cparams(("parallel", "parallel")),
        name="modulation",
    )(cond, w_ada, b_ada.reshape(DEPTH, 1, N_MOD * D_MODEL))


def _norm_mod(x, g, sc, sh):
    ms = jnp.mean(x * x, axis=-1, keepdims=True)
    return (x * lax.rsqrt(ms + EPS) * g) * (1.0 + sc) + sh


def _mod_spec(cond_of_step, which):
    return pl.BlockSpec((None, 1, D_MODEL), lambda i, *_: (cond_of_step(i) * N_MOD + which, 0, 0))


def _scan_constants():
    t = np.arange(BLK)
    out = []
    for rev in (False, True):
        u = (BLK - 1 - t) if rev else t
        ut, us = u[:, None], u[None, :]
        cums = (us <= ut).astype(np.float32)
        lev = np.where(us > ut, 0,
              np.where(ut // CHUNK == us // CHUNK, 1,
              np.where(ut // 64 == us // 64, 2,
              np.where(ut // 128 == us // 128, 3, 4)))).astype(np.int32)
        out += [jnp.asarray(cums, BF16), jnp.asarray(lev)]
    return out


def _rows_to_block(rows, rev):
    order = rows[::-1] if rev else rows
    return jnp.concatenate([jnp.broadcast_to(r, (CHUNK, LANES)) for r in order], axis=0)


def _scan_prep(q, z, lb, cums, rev, with_inter):
    sig = jax.nn.sigmoid(z)
    f = lb + (1.0 - lb) * sig
    logf = jnp.log(f)
    k = (1.0 - lb) * (1.0 - sig)
    hi = logf.astype(BF16)
    lo = (logf - hi.astype(F32)).astype(BF16)
    bb = _dot(cums, jnp.concatenate([hi, lo], axis=1))
    b = bb[:, :LANES] + bb[:, LANES:]

    e_row, m_row = (0, CHUNK // 2) if rev else (CHUNK - 1, CHUNK // 2 - 1)
    ends, mids = [], []
    for j in range(N_CHUNK):
        ends.append(b[j * CHUNK + e_row:j * CHUNK + e_row + 1, :])
        mids.append(b[j * CHUNK + m_row:j * CHUNK + m_row + 1, :])
    if rev:
        ends, mids = ends[::-1], mids[::-1]
    zero = jnp.zeros((1, LANES), F32)
    one = jnp.ones((1, LANES), F32)
    pres = [zero] + ends[:-1]
    b_pre = _rows_to_block(pres, rev)
    b_end = _rows_to_block(ends, rev)
    b_mid = _rows_to_block(mids, rev)

    qd = q * jnp.exp(b - b_pre)
    ku = k * jnp.exp(b_end - b)
    qm = q * jnp.exp(jnp.clip(b - b_mid, -EXP_CLAMP, EXP_CLAMP))
    km = k * jnp.exp(jnp.clip(b_mid - b, -EXP_CLAMP, EXP_CLAMP))

    levels = []
    for nc in (2, 4, 8):
        fq, fk = [], []
        for ju in range(N_CHUNK):
            r = (ju // nc) * nc + nc // 2 - 1
            if ju % nc >= nc // 2:
                fq.append(one if nc == 2 else jnp.exp(pres[ju] - ends[r]))
                fk.append(zero)
            else:
                fq.append(zero)
                fk.append(one if nc == 2 else jnp.exp(ends[r] - ends[ju]))
        levels.append(((qd * _rows_to_block(fq, rev)).astype(BF16), (ku * _rows_to_block(fk, rev)).astype(BF16)))

    last = ends[-1]
    qh = (qd * _rows_to_block([jnp.exp(p) for p in pres], rev)).astype(BF16) if with_inter else None
    kh = (ku * _rows_to_block([jnp.exp(last - e) for e in ends], rev)).astype(BF16)
    return (qm.astype(BF16), km.astype(BF16)), levels, qh, kh, last


def _scan_att(prep, lev):
    att = jnp.where(lev == 1, _dot_nt(*prep[0]), 0.0)
    for level, (ql, kl) in enumerate(prep[1], start=2):
        att = jnp.where(lev == level, _dot_nt(ql, kl), att)
    return att


def _scan_att_bidir(prep_f, prep_b, lev_f, lev_b):
    att = jnp.where(lev_f == 1, _dot_nt(*prep_f[0]), 0.0) + jnp.where(lev_b == 1, _dot_nt(*prep_b[0]), 0.0)
    lev = jnp.maximum(lev_f, lev_b)
    for level, ((qf, kf), (qb, kb)) in enumerate(zip(prep_f[1], prep_b[1]), start=2):
        both = _dot_nt(jnp.concatenate([qf, qb], axis=1), jnp.concatenate([kf, kb], axis=1))
        att = jnp.where(lev == level, both, att)
    return att


def _scan_state(prep, v, st_prev):
    ut = _dot_tn(v.astype(BF16), prep[3])
    return ut if st_prev is None else st_prev * jnp.exp(prep[4]) + ut


def _scan_finish(o, g, on):
    ms = jnp.mean(o * o, axis=-1, keepdims=True)
    return ((o * lax.rsqrt(ms + EPS) * on) * _silu(g)).astype(BF16)


def _slot_view(ref, slot, owns_all_slots):
    if not owns_all_slots:
        return ref
    for s in range(ref.shape[0]):
        if s != slot:
            ref[s] = jnp.zeros(ref.shape[1:], ref.dtype)
    return ref.at[slot]


def _hgrn_kernel(*refs, n_blocks, has_state, slot, owns_all_slots, hps):
    it = iter(refs)
    x_ref, g_ref, sh_ref, sc_ref, gate_ref, win_ref, lb_ref, on_ref = (next(it) for _ in range(8))
    cf_ref, lf_ref, cb_ref, lvb_ref = (next(it) for _ in range(4))
    s0_ref = next(it) if has_state else None
    wout_ref, xo_ref = next(it), next(it)
    sfin_ref = None if has_state else next(it)
    h_ref, proj_ref, og_ref = next(it), next(it), next(it)
    oacc_ref, st_ref = (next(it), next(it)) if has_state else (None, None)

    h_ref[...] = _norm_mod(x_ref[...], g_ref[...], sc_ref[...], sh_ref[...]).astype(BF16)
    on = on_ref[...]
    if not has_state:
        sfin_ref = _slot_view(sfin_ref, slot, owns_all_slots)

    group_w = hps * HEAD_DIM

    def part(i, p, rows=slice(None)):
        return proj_ref[rows, p * group_w + i * HEAD_DIM:p * group_w + (i + 1) * HEAD_DIM]

    def pair(hp, carry):
        for p in range(HGRN_PARTS):
            cols = pl.ds(pl.multiple_of(p * D_MODEL + hp * group_w, group_w), group_w)
            proj_ref[:, p * group_w:(p + 1) * group_w] = _dot(h_ref[...], win_ref[:, cols])
        heads = [hp * hps + i for i in range(hps)]
        lbs = [lb_ref[:, _lane_block(hd, HEAD_DIM)] for hd in heads]
        if not has_state:
            for i, hd in enumerate(heads):
                q, v = part(i, 0), part(i, 1)
                prep_f = _scan_prep(q, part(i, 2), lbs[i][0:1, :], cf_ref[...], False, False)
                prep_b = _scan_prep(q, part(i, 3), lbs[i][1:2, :], cb_ref[...], True, False)
                att = _scan_att_bidir(prep_f, prep_b, lf_ref[...], lvb_ref[...])
                o = _dot(att.astype(BF16), v.astype(BF16))
                sfin_ref[0, hd] = _scan_state(prep_f, v, None).T
                sfin_ref[1, hd] = _scan_state(prep_b, v, None).T
                og_ref[:, _lane_block(hd, HEAD_DIM)] = _scan_finish(o, part(i, 4), on)
        else:
            for i, hd in enumerate(heads):
                st_ref[2 * i] = s0_ref[0, hd].T
                st_ref[2 * i + 1] = s0_ref[1, hd].T

            def body(tb, c2):
                rf = pl.ds(pl.multiple_of(tb * BLK, BLK), BLK)
                rb = pl.ds(pl.multiple_of((n_blocks - 1 - tb) * BLK, BLK), BLK)
                for i in range(hps):
                    cols = slice(i * HEAD_DIM, (i + 1) * HEAD_DIM)
                    v_f, v_b = part(i, 1, rf), part(i, 1, rb)
                    prep_f = _scan_prep(part(i, 0, rf), part(i, 2, rf), lbs[i][0:1, :], cf_ref[...], False, True)
                    prep_b = _scan_prep(part(i, 0, rb), part(i, 3, rb), lbs[i][1:2, :], cb_ref[...], True, True)
                    st_f, st_b = st_ref[2 * i], st_ref[2 * i + 1]
                    of = (_dot(_scan_att(prep_f, lf_ref[...]).astype(BF16), v_f.astype(BF16))
                          + _dot_nt(prep_f[2], st_f.astype(BF16)))
                    ob = (_dot(_scan_att(prep_b, lvb_ref[...]).astype(BF16), v_b.astype(BF16))
                          + _dot_nt(prep_b[2], st_b.astype(BF16)))
                    st_ref[2 * i] = _scan_state(prep_f, v_f, st_f)
                    st_ref[2 * i + 1] = _scan_state(prep_b, v_b, st_b)

                    @pl.when(2 * tb < n_blocks)
                    def _():
                        oacc_ref[rf, cols] = of
                        oacc_ref[rb, cols] = ob

                    @pl.when(2 * tb >= n_blocks)
                    def _():
                        oacc_ref[rf, cols] += of
                        oacc_ref[rb, cols] += ob

                return c2

            lax.fori_loop(0, n_blocks, body, 0)
            for i, hd in enumerate(heads):
                cols = slice(i * HEAD_DIM, (i + 1) * HEAD_DIM)
                og_ref[:, _lane_block(hd, HEAD_DIM)] = _scan_finish(oacc_ref[:, cols], part(i, 4), on)
        return carry

    lax.fori_loop(0, HEADS // hps, pair, 0)
    xo_ref[...] = x_ref[...] + gate_ref[...] * _dot(og_ref[...], wout_ref[...])


def _hgrn_layer(x, n_seq, t, cond_of_seq, g, modr, w_in, lbs_j, onorm_g, w_out, s0, sfin_prev, slot):
    d = D_MODEL
    has_state = s0 is not None
    n_blocks = t // BLK
    assert n_blocks == 1 or n_blocks % 2 == 0
    full = lambda shape, **kw: pl.BlockSpec(shape, lambda b, *_: (0,) * len(shape), **kw)
    in_specs = [
        pl.BlockSpec((t, d), lambda b: (b, 0)),
        full((1, d)),
        _mod_spec(cond_of_seq, 0), _mod_spec(cond_of_seq, 1), _mod_spec(cond_of_seq, 2),
        full((d, HGRN_PARTS * d), **_RESIDENT),
        full((2, d)),
        full((1, HEAD_DIM)),
    ] + [full((BLK, BLK))] * 4
    args = [x, g.reshape(1, d), modr, modr, modr, w_in, lbs_j, onorm_g.reshape(1, HEAD_DIM)] + _scan_constants()
    state_block = (None, 2, HEADS, HEAD_DIM, HEAD_DIM)
    if has_state:
        in_specs.append(pl.BlockSpec(state_block, lambda b: (b, 0, 0, 0, 0)))
        args.append(s0)
    in_specs.append(full((d, d), **_RESIDENT))
    args.append(w_out)
    out_specs = [pl.BlockSpec((t, d), lambda b: (b, 0))]
    out_shape = [jax.ShapeDtypeStruct((n_seq * t, d), F32)]
    hps = HEADS_PER_STEP if has_state else SCAN_HEADS_PER_BODY
    group_w = hps * HEAD_DIM
    scratch = [pltpu.VMEM((t, d), BF16), pltpu.VMEM((t, HGRN_PARTS * group_w), F32), pltpu.VMEM((t, d), BF16)]
    aliases = {}
    if has_state:
        scratch += [pltpu.VMEM((t, group_w), F32), pltpu.VMEM((2 * hps, HEAD_DIM, HEAD_DIM), F32)]
    else:
        state_dims = (2, HEADS, HEAD_DIM, HEAD_DIM)
        if sfin_prev is None:
            out_specs.append(pl.BlockSpec((None, N_SLOTS) + state_dims, lambda b: (b, 0, 0, 0, 0, 0)))
        else:
            out_specs.append(pl.BlockSpec((None, None) + state_dims, lambda b: (b, slot, 0, 0, 0, 0)))
            in_specs.append(pl.BlockSpec(memory_space=pl.ANY))
            args.append(sfin_prev)
            aliases = {len(args) - 1: 1}
        out_shape.append(jax.ShapeDtypeStruct((n_seq, N_SLOTS) + state_dims, F32))

    def body(*refs):
        if sfin_prev is not None:
            n_in = len(args)
            refs = refs[:n_in - 1] + refs[n_in:]
        _hgrn_kernel(*refs, n_blocks=n_blocks, has_state=has_state, slot=slot, owns_all_slots=sfin_prev is None,
                     hps=hps)

    out = pl.pallas_call(
        body,
        grid=(n_seq,),
        in_specs=in_specs,
        out_specs=out_specs,
        out_shape=out_shape,
        scratch_shapes=scratch,
        input_output_aliases=aliases,
        compiler_params=_cparams(("parallel",)),
        name="hgrn_layer_sample" if has_state else "hgrn_layer_prompt",
    )(*args)
    return out if not has_state else (out[0], None)


def _rope_tables(t_lat):
    rows = t_lat // GRID_W
    row = jnp.repeat(jnp.arange(rows), GRID_W).astype(F32)
    col = jnp.tile(jnp.arange(GRID_W), rows).astype(F32)
    half = QK_DIM // 2
    inv_freq = ROPE_THETA ** (-jnp.arange(0, half, 2, dtype=F32) / half)
    ang_row = row[:, None] * inv_freq
    ang_col = col[:, None] * inv_freq

    def part(ang):
        c, s = jnp.cos(ang), jnp.sin(ang)
        return jnp.concatenate([c, c], axis=1), jnp.concatenate([-s, s], axis=1)

    cr, sr = part(ang_row)
    cc, sc = part(ang_col)
    cos64 = jnp.concatenate([cr, cc], axis=1)
    sin64 = jnp.concatenate([sr, sc], axis=1)
    return jnp.concatenate([cos64, cos64], axis=1), jnp.concatenate([sin64, sin64], axis=1)


def _rope(x, cos, sin):
    lane = lax.broadcasted_iota(jnp.int32, x.shape, 1)
    first = (lane % (QK_DIM // 2)) < (QK_DIM // 4)
    swapped = jnp.where(first, pltpu.roll(x, LANES - QK_DIM // 4, 1), pltpu.roll(x, QK_DIM // 4, 1))
    return x * cos + swapped * sin


def _attn_kernel(*refs, n_qblk, rope, cache, emit_kv, lam_init, slot, owns_all_slots):
    it = iter(refs)
    x_ref, g_ref, sh_ref, sc_ref, gate_ref, wqkv_ref = (next(it) for _ in range(6))
    qg_ref, kg_ref, lam_ref, sg_ref, seg_ref = (next(it) for _ in range(5))
    cos_ref = sin_ref = ck_ref = cv_ref = nk_ref = nv_ref = None
    if rope:
        cos_ref, sin_ref = next(it), next(it)
    if cache:
        ck_ref, cv_ref = next(it), next(it)
    wout_ref, xo_ref = next(it), next(it)
    if emit_kv:
        nk_ref = _slot_view(next(it), slot, owns_all_slots)
        nv_ref = _slot_view(next(it), slot, owns_all_slots)
    h_ref, qkv_ref, oa_ref = next(it), next(it), next(it)

    d = D_MODEL
    h_ref[...] = _norm_mod(x_ref[...], g_ref[...], sc_ref[...], sh_ref[...]).astype(BF16)
    qkv_ref[...] = _dot(h_ref[...], wqkv_ref[...])
    seg = seg_ref[...]
    lp = lam_ref[...]
    lam = (jnp.exp(jnp.sum(lp[0:1, :] * lp[1:2, :], axis=-1, keepdims=True))
           - jnp.exp(jnp.sum(lp[2:3, :] * lp[3:4, :], axis=-1, keepdims=True)) + lam_init)
    lane = lax.broadcasted_iota(jnp.int32, (1, LANES), 1)
    comp0 = lane < QK_DIM
    scale = QK_DIM ** -0.5

    def split(a):
        return [jnp.where(comp0, a, 0.0).astype(BF16), jnp.where(comp0, 0.0, a).astype(BF16)]

    def one_head(hd):
        hcol = _lane_block(hd, HEAD_DIM)
        k = qkv_ref[:, pl.ds(pl.multiple_of(d + hd * HEAD_DIM, HEAD_DIM), HEAD_DIM)]
        v = qkv_ref[:, pl.ds(pl.multiple_of(2 * d + hd * HEAD_DIM, HEAD_DIM), HEAD_DIM)]
        kn = k * lax.rsqrt(_dot(k * k, seg) + EPS) * kg_ref[...]
        if emit_kv:
            nk_ref[:, hcol] = kn
            nv_ref[:, hcol] = v
        if rope:
            kn = _rope(kn, cos_ref[...], sin_ref[...])
        ks = split(kn)
        vb = v.astype(BF16)
        if cache:
            cks = split(ck_ref[:, hcol])
            cvb = cv_ref[:, hcol].astype(BF16)
        for qi in range(n_qblk):
            rows = slice(qi * BLK, (qi + 1) * BLK)
            q = qkv_ref[rows, hcol]
            qn = q * lax.rsqrt(_dot(q * q, seg) + EPS) * qg_ref[...]
            if rope:
                qn = _rope(qn, cos_ref[rows, :], sin_ref[rows, :])
            qb = (qn * scale).astype(BF16)
            a_self, a_cache = None, None
            for c in range(2):
                s = _dot_nt(qb, ks[c])
                m = jnp.max(s, axis=-1, keepdims=True)
                if cache:
                    sc = _dot_nt(qb, cks[c])
                    m = jnp.maximum(m, jnp.max(sc, axis=-1, keepdims=True))
                p = jnp.exp(s - m)
                den = jnp.sum(p, axis=-1, keepdims=True)
                if cache:
                    pc = jnp.exp(sc - m)
                    den = den + jnp.sum(pc, axis=-1, keepdims=True)
                w = (1.0 / den) if c == 0 else (-lam / den)
                a_self = p * w if c == 0 else a_self + p * w
                if cache:
                    a_cache = pc * w if c == 0 else a_cache + pc * w
            o = _dot(a_self.astype(BF16), vb)
            if cache:
                o = o + _dot(a_cache.astype(BF16), cvb)
            ms = jnp.mean(o * o, axis=-1, keepdims=True)
            o = (o * lax.rsqrt(ms + EPS) * sg_ref[...]) * (1.0 - lam_init)
            oa_ref[rows, hcol] = o.astype(BF16)

    heads_per_body = ATTN_HEADS_PER_BODY if n_qblk == 1 else HEADS_PER_STEP

    def group(hg, carry):
        for i in range(heads_per_body):
            one_head(hg * heads_per_body + i)
        return carry

    lax.fori_loop(0, HEADS // heads_per_body, group, 0)
    xo_ref[...] = x_ref[...] + gate_ref[...] * _dot(oa_ref[...], wout_ref[...])


def _attn_layer(x, n_seq, t, cond_of_seq, g, modr, w_qkv, qn_g, kn_g, lam_p, subln_g, w_out, layer_idx,
                cache_k_j, cache_v_j, kv_prev, slot):
    d = D_MODEL
    cache = cache_k_j is not None
    lam_init = 0.8 - 0.6 * math.exp(-0.3 * layer_idx)
    qg = jnp.tile(qn_g.reshape(1, QK_DIM), (1, 2))
    kg = jnp.tile(kn_g.reshape(1, QK_DIM), (1, 2))
    li = np.arange(LANES)
    seg = jnp.asarray((li[:, None] // QK_DIM == li[None, :] // QK_DIM).astype(np.float32) / QK_DIM)
    full = lambda shape, **kw: pl.BlockSpec(shape, lambda b, *_: (0,) * len(shape), **kw)
    in_specs = [
        pl.BlockSpec((t, d), lambda b: (b, 0)),
        full((1, d)),
        _mod_spec(cond_of_seq, 0), _mod_spec(cond_of_seq, 1), _mod_spec(cond_of_seq, 2),
        full((d, 3 * d), **_RESIDENT),
        full((1, LANES)), full((1, LANES)), full((4, QK_DIM)), full((1, HEAD_DIM)), full((LANES, LANES)),
    ]
    args = [x, g.reshape(1, d), modr, modr, modr, w_qkv, qg, kg, lam_p, subln_g.reshape(1, HEAD_DIM), seg]
    if cache:
        cos, sin = _rope_tables(t)
        past = cache_k_j.shape[1]
        in_specs += [full((t, LANES)), full((t, LANES)),
                     pl.BlockSpec((None, past, d), lambda b: (b, 0, 0)),
                     pl.BlockSpec((None, past, d), lambda b: (b, 0, 0))]
        args += [cos, sin, cache_k_j.reshape(n_seq, past, d), cache_v_j.reshape(n_seq, past, d)]
    in_specs.append(full((d, d), **_RESIDENT))
    args.append(w_out)
    out_specs = [pl.BlockSpec((t, d), lambda b: (b, 0))]
    out_shape = [jax.ShapeDtypeStruct((n_seq * t, d), F32)]
    aliases = {}
    n_carried = 0
    if not cache:
        if kv_prev is None:
            kv_spec = pl.BlockSpec((None, N_SLOTS, t, d), lambda b: (b, 0, 0, 0))
        else:
            kv_spec = pl.BlockSpec((None, None, t, d), lambda b: (b, slot, 0, 0))
            in_specs += [pl.BlockSpec(memory_space=pl.ANY)] * 2
            args += list(kv_prev)
            aliases = {len(args) - 2: 1, len(args) - 1: 2}
            n_carried = 2
        out_specs += [kv_spec, kv_spec]
        out_shape += [jax.ShapeDtypeStruct((n_seq, N_SLOTS, t, d), F32)] * 2

    def body(*refs):
        n_in = len(args)
        refs = refs[:n_in - n_carried] + refs[n_in:]
        _attn_kernel(*refs, n_qblk=t // BLK, rope=cache, cache=cache, emit_kv=not cache, lam_init=lam_init,
                     slot=slot, owns_all_slots=kv_prev is None)

    out = pl.pallas_call(
        body,
        grid=(n_seq,),
        in_specs=in_specs,
        out_specs=out_specs,
        out_shape=out_shape,
        scratch_shapes=[pltpu.VMEM((t, d), BF16), pltpu.VMEM((t, 3 * d), F32), pltpu.VMEM((t, d), BF16)],
        input_output_aliases=aliases,
        compiler_params=_cparams(("parallel",)),
        name="attn_layer_sample" if cache else "attn_layer_prompt",
    )(*args)
    return (out[0], None) if cache else (out[0], (out[1], out[2]))


ROUTE_ROWS = 32
INFO_GID, INFO_RANK = 8, 9


def _two_stream_specs(n_prompt_tiles, width):
    return [pl.BlockSpec((BLK, width), lambda i, *_: (jnp.minimum(i, n_prompt_tiles - 1), 0)),
            pl.BlockSpec((BLK, width), lambda i, *_: (jnp.maximum(i - n_prompt_tiles, 0), 0))]


def _route_kernel(xp_ref, xs_ref, g_ref, sh_ref, sc_ref, wr_ref, br_ref, tri_ref,
                  h_ref, il_ref, it_ref, cum_ref, tot_ref, carry_ref, *, n_prompt_tiles):
    i = pl.program_id(0)

    @pl.when(i == 0)
    def _():
        carry_ref[...] = jnp.zeros_like(carry_ref)

    @pl.when(i < n_prompt_tiles)
    def _():
        h_ref[...] = _norm_mod(xp_ref[...], g_ref[...], sc_ref[...], sh_ref[...]).astype(BF16)

    @pl.when(i >= n_prompt_tiles)
    def _():
        h_ref[...] = _norm_mod(xs_ref[...], g_ref[...], sc_ref[...], sh_ref[...]).astype(BF16)

    logit = _dot_nt(wr_ref[...], h_ref[...]) + br_ref[...]
    gl = [logit[g:g + 1, :] for g in range(MOE_GROUPS)]
    gmax = functools.reduce(jnp.maximum, gl)
    gz = functools.reduce(lambda a, b: a + b, [jnp.exp(x - gmax) for x in gl])
    g_w = 1.0 / gz
    gid = jnp.full_like(gmax, MOE_GROUPS - 1)
    for g in range(MOE_GROUPS - 2, -1, -1):
        gid = jnp.where(gl[g] == gmax, float(g), gid)
    el = []
    for j in range(MOE_EPG):
        e = logit[MOE_GROUPS + j:MOE_GROUPS + j + 1, :]
        for g in range(1, MOE_GROUPS):
            r = MOE_GROUPS + g * MOE_EPG + j
            e = jnp.where(gid == float(g), logit[r:r + 1, :], e)
        el.append(e)
    emax = functools.reduce(jnp.maximum, el)
    pe = [jnp.exp(e - emax) for e in el]
    idx1 = jnp.full_like(emax, MOE_EPG - 1)
    for j in range(MOE_EPG - 2, -1, -1):
        idx1 = jnp.where(el[j] == emax, float(j), idx1)
    el2 = [jnp.where(idx1 == float(j), -jnp.inf, el[j]) for j in range(MOE_EPG)]
    emax2 = functools.reduce(jnp.maximum, el2)
    idx2 = jnp.full_like(emax, MOE_EPG - 1)
    for j in range(MOE_EPG - 2, -1, -1):
        idx2 = jnp.where(el2[j] == emax2, float(j), idx2)
    sel = [(idx1 == float(j)) | (idx2 == float(j)) for j in range(MOE_EPG)]
    den = functools.reduce(lambda a, b: a + b, [jnp.where(sel[j], pe[j], 0.0) for j in range(MOE_EPG)])
    cw = [jnp.where(sel[j], pe[j] * (g_w / den), 0.0) for j in range(MOE_EPG)]

    row8 = lax.broadcasted_iota(jnp.int32, (8, BLK), 0)
    onehot = jnp.where(row8.astype(F32) == gid, 1.0, 0.0)
    within = _dot(onehot.astype(BF16), tri_ref[...])
    carry = carry_ref[...]
    rank = jnp.sum(onehot * (within + carry[:, 0:1]), axis=0, keepdims=True)
    cum_ref[...] = carry
    new_carry = carry + jnp.sum(onehot, axis=1, keepdims=True)
    carry_ref[...] = new_carry
    tot_ref[...] = new_carry

    il_ref[...] = jnp.where(row8 == 0, gid, jnp.where(row8 == 1, rank, 0.0))
    rowl = lax.broadcasted_iota(jnp.int32, (LANES, BLK), 0)
    m = jnp.zeros((LANES, BLK), F32)
    for j in range(MOE_EPG):
        hi = cw[j].astype(BF16).astype(F32)
        m = jnp.where(rowl == j, hi, m)
        m = jnp.where(rowl == MOE_EPG + j, cw[j] - hi, m)
    m = jnp.where(rowl == INFO_GID, gid, m)
    m = jnp.where(rowl == INFO_RANK, rank, m)
    it_ref[...] = m.T


def _moe_route(xp, xs, g, modr, wr_t, br, cond_of_tile):
    d = D_MODEL
    npt = xp.shape[0] // BLK
    nt = npt + xs.shape[0] // BLK
    n = nt * BLK
    tri = jnp.asarray(np.triu(np.ones((BLK, BLK), np.float32), 1), BF16)
    full = lambda shape: pl.BlockSpec(shape, lambda i: (0,) * len(shape))
    return pl.pallas_call(
        functools.partial(_route_kernel, n_prompt_tiles=npt),
        grid=(nt,),
        in_specs=_two_stream_specs(npt, d) + [
            full((1, d)),
            _mod_spec(cond_of_tile, 3), _mod_spec(cond_of_tile, 4),
            full((ROUTE_ROWS, d)), full((ROUTE_ROWS, 1)), full((BLK, BLK)),
        ],
        out_specs=[
            pl.BlockSpec((BLK, d), lambda i: (i, 0)),
            pl.BlockSpec((8, BLK), lambda i: (0, i)),
            pl.BlockSpec((BLK, LANES), lambda i: (i, 0)),
            pl.BlockSpec((None, 8, LANES), lambda i: (i, 0, 0)),
            pl.BlockSpec((8, LANES), lambda i: (0, 0)),
        ],
        out_shape=[
            jax.ShapeDtypeStruct((n, d), BF16),
            jax.ShapeDtypeStruct((8, n), F32),
            jax.ShapeDtypeStruct((n, LANES), F32),
            jax.ShapeDtypeStruct((nt, 8, LANES), F32),
            jax.ShapeDtypeStruct((8, LANES), F32),
        ],
        scratch_shapes=[pltpu.VMEM((8, LANES), F32)],
        compiler_params=_cparams(("arbitrary",)),
        name="moe_route",
    )(xp, xs, g.reshape(1, d), modr, modr, wr_t, br, tri)


def _sorted_pos(gid, rank, rstart_ref):
    p = rank
    for g in range(MOE_GROUPS):
        p = p + jnp.where(gid == float(g), rstart_ref[g].astype(F32), 0.0)
    return p


def _gather_kernel(valid_ref, clo_ref, chi_ref, rstart_ref, h_ref, il_ref, it_ref, hs_ref, cws_ref, *, n_tiles):
    a = pl.program_id(0)
    win = GATHER_WIN * BLK

    @pl.when(valid_ref[a] == 0)
    def _():
        hs_ref[...] = jnp.zeros_like(hs_ref)
        cws_ref[...] = jnp.zeros_like(cws_ref)

    @pl.when(valid_ref[a] != 0)
    def _():
        dest = (lax.broadcasted_iota(jnp.int32, (BLK, 1), 0) + a * BLK).astype(F32)
        src_tile = lax.broadcasted_iota(jnp.int32, (1, win), 1) // BLK
        clo = clo_ref[a]
        n_win = (chi_ref[a] - clo + GATHER_WIN) // GATHER_WIN

        def window(w):
            first = clo + w * GATHER_WIN
            c0 = jnp.minimum(first, n_tiles - GATHER_WIN)
            rows = pl.ds(pl.multiple_of(c0 * BLK, BLK), win)
            info = il_ref[:, rows]
            p = _sorted_pos(info[0:1, :], info[1:2, :], rstart_ref)
            p = jnp.where(src_tile + c0 >= first, p, -1.0)
            onehot = jnp.where(dest == p, 1.0, 0.0).astype(BF16)
            r = _dot(onehot, it_ref[rows, :].astype(BF16))
            return _dot(onehot, h_ref[rows, :]), r + pltpu.roll(r, LANES - MOE_EPG, 1)

        dh, dc = window(0)
        hs_ref[...] = dh.astype(BF16)
        cws_ref[...] = dc

        def body(w, carry):
            dh, dc = window(w)
            hs_ref[...] = (hs_ref[...].astype(F32) + dh).astype(BF16)
            cws_ref[...] += dc
            return carry

        lax.fori_loop(1, n_win, body, 0)


def _moe_gather(h, info_lane, info_tok, valid, clo, chi, rstart, n_dest_tiles):
    n, d = h.shape
    nt = n // BLK
    assert nt >= GATHER_WIN
    resident = dict(pipeline_mode=pl.Buffered(1))
    return pl.pallas_call(
        functools.partial(_gather_kernel, n_tiles=nt),
        grid_spec=pltpu.PrefetchScalarGridSpec(
            num_scalar_prefetch=4,
            grid=(n_dest_tiles,),
            in_specs=[
                pl.BlockSpec((n, d), lambda a, *_: (0, 0), **resident),
                pl.BlockSpec((8, n), lambda a, *_: (0, 0), **resident),
                pl.BlockSpec((n, LANES), lambda a, *_: (0, 0), **resident),
            ],
            out_specs=[
                pl.BlockSpec((BLK, d), lambda a, *_: (a, 0)),
                pl.BlockSpec((BLK, LANES), lambda a, *_: (a, 0)),
            ],
        ),
        out_shape=[
            jax.ShapeDtypeStruct((n_dest_tiles * BLK, d), BF16),
            jax.ShapeDtypeStruct((n_dest_tiles * BLK, LANES), F32),
        ],
        compiler_params=_cparams(("arbitrary",)),
        name="moe_gather",
    )(valid, clo, chi, rstart, h, info_lane, info_tok)


def _moe_mlp_kernel(sgroup_ref, snt_ref, hs_ref, cws_ref, wg_ref, wu_ref, wd_ref, ys_ref,
                    acc_ref, wgb_ref, wub_ref, wdb_ref):
    s = pl.program_id(0)
    k = pl.program_id(1)
    nt = snt_ref[s]
    wgb_ref[...] = wg_ref[...].astype(BF16)
    wub_ref[...] = wu_ref[...].astype(BF16)
    wdb_ref[...] = wd_ref[...].astype(BF16)

    @pl.when(k == 0)
    def _():
        acc_ref[...] = jnp.zeros_like(acc_ref)

    def block(j):
        rows = slice(j * EXPERT_TILES * BLK, (j + 1) * EXPERT_TILES * BLK)
        hsub = hs_ref[rows, :]
        cws = cws_ref[rows, :]
        cwk = jnp.zeros((EXPERT_TILES * BLK, 1), F32)
        for kk in range(MOE_EPG):
            cwk = jnp.where(k == kk, cws[:, kk:kk + 1], cwk)
        y = None
        for fh in range(MOE_D_FF // FF_PART):
            fc = slice(fh * FF_PART, (fh + 1) * FF_PART)
            gate = _dot(hsub, wgb_ref[:, fc])
            up = _dot(hsub, wub_ref[:, fc])
            act = ((_silu(gate) * up) * cwk).astype(BF16)
            part = _dot(act, wdb_ref[fc, :])
            y = part if y is None else y + part
        acc_ref[rows, :] += y

    n_blocks = SUPER // EXPERT_TILES

    @pl.when(nt == SUPER)
    def _():
        for j in range(n_blocks):
            block(j)

    @pl.when(nt < SUPER)
    def _():
        for j in range(n_blocks):
            pl.when(j * EXPERT_TILES < nt)(functools.partial(block, j))

    @pl.when(k == MOE_EPG - 1)
    def _():
        ys_ref[...] = acc_ref[...].astype(BF16)


def _moe_mlp(hs, cws, w_gate, w_up, w_down, layer, sgroup, snt, n_super):
    d = hs.shape[1]
    widx = lambda s, k, sg, sn: (layer * MOE_EXPERTS + sg[s] * MOE_EPG + k, 0, 0)
    return pl.pallas_call(
        _moe_mlp_kernel,
        grid_spec=pltpu.PrefetchScalarGridSpec(
            num_scalar_prefetch=2,
            grid=(n_super, MOE_EPG),
            in_specs=[
                pl.BlockSpec((SUPER_ROWS, d), lambda s, k, *_: (s, 0)),
                pl.BlockSpec((SUPER_ROWS, LANES), lambda s, k, *_: (s, 0)),
                pl.BlockSpec((None, d, MOE_D_FF), widx),
                pl.BlockSpec((None, d, MOE_D_FF), widx),
                pl.BlockSpec((None, MOE_D_FF, d), widx),
            ],
            out_specs=pl.BlockSpec((SUPER_ROWS, d), lambda s, k, *_: (s, 0)),
            scratch_shapes=[
                pltpu.VMEM((SUPER_ROWS, d), F32),
                pltpu.VMEM((d, MOE_D_FF), BF16),
                pltpu.VMEM((d, MOE_D_FF), BF16),
                pltpu.VMEM((MOE_D_FF, d), BF16),
            ],
        ),
        out_shape=jax.ShapeDtypeStruct((n_super * SUPER_ROWS, d), BF16),
        compiler_params=_cparams(("arbitrary", "arbitrary")),
        name="moe_experts",
    )(sgroup, snt, hs, cws, w_gate, w_up, w_down)


N_SRC = 2 * MOE_GROUPS


def _unsort_kernel(ut_ref, rstart_ref, xp_ref, xs_ref, it_ref, gate_ref, *rest, n_prompt_tiles):
    ys_refs, op_ref, os_ref, acc_ref = rest[:N_SRC], rest[N_SRC], rest[N_SRC + 1], rest[N_SRC + 2]
    t = pl.program_id(0)
    info = it_ref[...]
    p = _sorted_pos(info[:, INFO_GID:INFO_GID + 1], info[:, INFO_RANK:INFO_RANK + 1], rstart_ref)
    lane = lax.broadcasted_iota(jnp.int32, (1, BLK), 1).astype(F32)

    def take(m):
        a = ut_ref[t * N_SRC + m]
        onehot = jnp.where(p - (a * BLK).astype(F32) == lane, 1.0, 0.0).astype(BF16)
        return _dot(onehot, ys_refs[m][...])

    acc_ref[...] = functools.reduce(lambda u, w: u + w, [take(m) for m in range(0, N_SRC, 2)])
    for m in range(1, N_SRC, 2):
        @pl.when(ut_ref[t * N_SRC + m] >= 0)
        def _():
            acc_ref[...] += take(m)

    @pl.when(t < n_prompt_tiles)
    def _():
        op_ref[...] = xp_ref[...] + gate_ref[...] * acc_ref[...]

    @pl.when(t >= n_prompt_tiles)
    def _():
        os_ref[...] = xs_ref[...] + gate_ref[...] * acc_ref[...]


def _moe_unsort(xp, xs, info_tok, modr, ys, ut, rstart, cond_of_tile):
    d = D_MODEL
    npt = xp.shape[0] // BLK
    nt = npt + xs.shape[0] // BLK

    def ys_spec(m):
        return pl.BlockSpec((BLK, d), lambda t, ut_r, rs_r: (jnp.maximum(ut_r[t * N_SRC + m], 0), 0))

    return pl.pallas_call(
        functools.partial(_unsort_kernel, n_prompt_tiles=npt),
        grid_spec=pltpu.PrefetchScalarGridSpec(
            num_scalar_prefetch=2,
            grid=(nt,),
            in_specs=_two_stream_specs(npt, d) + [
                pl.BlockSpec((BLK, LANES), lambda t, *_: (t, 0)),
                _mod_spec(cond_of_tile, 5),
            ] + [ys_spec(m) for m in range(N_SRC)],
            out_specs=_two_stream_specs(npt, d),
            scratch_shapes=[pltpu.VMEM((BLK, d), F32)],
        ),
        out_shape=[jax.ShapeDtypeStruct(xp.shape, F32), jax.ShapeDtypeStruct(xs.shape, F32)],
        compiler_params=_cparams(("arbitrary",)),
        name="moe_unsort_residual",
    )(ut, rstart, xp, xs, info_tok, modr, *([ys] * N_SRC))


def _moe_tables(cum, tot, n_tiles, n_super):
    cumc = jnp.concatenate([cum[:, :MOE_GROUPS, 0], tot[None, :MOE_GROUPS, 0]], axis=0).astype(jnp.int32)
    total = cumc[-1]
    n_sup_g = (total + SUPER_ROWS - 1) // SUPER_ROWS
    sup_end = jnp.cumsum(n_sup_g)
    sup_start = sup_end - n_sup_g
    rstart = sup_start * SUPER_ROWS
    s = jnp.arange(n_super)
    sgroup = jnp.minimum(jnp.sum(s[:, None] >= sup_end[None, :], axis=1), MOE_GROUPS - 1)
    rows_left = total[sgroup] - (s - sup_start[sgroup]) * SUPER_ROWS
    snt = jnp.where(s < sup_end[-1], jnp.clip((rows_left + BLK - 1) // BLK, 0, SUPER), 0)
    a = jnp.arange(n_super * SUPER)
    sa, ja = a // SUPER, a % SUPER
    valid = (ja < snt[sa]).astype(jnp.int32)
    ga = sgroup[sa]
    r0 = ((sa - sup_start[ga]) * SUPER + ja) * BLK
    cg = cumc[:, ga]
    clo = jnp.sum(cg[1:] <= r0[None, :], axis=0)
    chi = jnp.sum(cg[:-1] < (r0 + BLK)[None, :], axis=0) - 1
    clo = jnp.clip(clo, 0, n_tiles - 1)
    chi = jnp.clip(chi, clo, n_tiles - 1)
    first = rstart[None, :] + cumc[:-1]
    last = rstart[None, :] + cumc[1:] - 1
    has = cumc[1:] > cumc[:-1]
    a0 = jnp.where(has, first // BLK, -1)
    a1 = jnp.where(has & (last // BLK != first // BLK), last // BLK, -1)
    ut = jnp.stack([a0, a1], axis=-1).reshape(-1)
    i32 = lambda v: v.astype(jnp.int32)
    return i32(rstart), i32(sgroup), i32(snt), valid, i32(clo), i32(chi), i32(ut)


def _moe_layer(xp, xs, g, modr, wr_t, br, w_gate, w_up, w_down, layer, cond_of_tile):
    nt = (xp.shape[0] + xs.shape[0]) // BLK
    n_super = (nt * BLK + SUPER_ROWS - 1) // SUPER_ROWS + MOE_GROUPS
    h, info_lane, info_tok, cum, tot = _moe_route(xp, xs, g, modr, wr_t, br, cond_of_tile)
    rstart, sgroup, snt, valid, clo, chi, ut = _moe_tables(cum, tot, nt, n_super)
    hs, cws = _moe_gather(h, info_lane, info_tok, valid, clo, chi, rstart, n_super * SUPER)
    ys = _moe_mlp(hs, cws, w_gate, w_up, w_down, layer, sgroup, snt, n_super)
    return _moe_unsort(xp, xs, info_tok, modr, ys, ut, rstart, cond_of_tile)


def kernel(x_prompt, x_sample, c, cache_k, cache_v, state_hgrn, c_ctx, norm_g, w_ada, b_ada, hgrn_w_in, hgrn_lb_logits, hgrn_onorm_g, hgrn_w_out, attn_w_qkv, attn_qn_g, attn_kn_g, attn_lambda, attn_subln_g, attn_w_out, moe_w_group, moe_b_group, moe_w_expert, moe_b_expert, moe_w_gate, moe_w_up, moe_w_down):
    n_prompt_seq, seq, d = x_prompt.shape
    dec_batch, dec_seq, _ = x_sample.shape
    n_prompt = n_prompt_seq * seq
    assert d == D_MODEL and seq == BLK and dec_seq % BLK == 0
    assert 1 + dec_batch <= N_COND

    xp = x_prompt.reshape(n_prompt, d)
    xs = x_sample.reshape(dec_batch * dec_seq, d)
    cond = jnp.zeros((N_COND, d), F32).at[0].set(c_ctx).at[1:1 + dec_batch].set(c)
    mod = _modulation(cond, w_ada, b_ada)

    lbs = jnp.cumsum(jax.nn.softmax(hgrn_lb_logits.astype(F32), axis=0), axis=0)
    lbs = lbs - lbs[0:1]

    cond_prompt = lambda b: 0
    cond_sample = lambda b: 1 + b
    npt, spt = n_prompt // BLK, dec_seq // BLK
    cond_tile = lambda i: jnp.where(i < npt, 0, 1 + (i - npt) // spt)

    w_in = hgrn_w_in.astype(BF16)
    w_hout = hgrn_w_out.astype(BF16)
    w_qkv = attn_w_qkv.astype(BF16)
    w_aout = attn_w_out.astype(BF16)
    wr_t = jnp.zeros((DEPTH, ROUTE_ROWS, d), F32)
    wr_t = wr_t.at[:, :MOE_GROUPS].set(moe_w_group.transpose(0, 2, 1))
    wr_t = wr_t.at[:, MOE_GROUPS:MOE_GROUPS + MOE_EXPERTS].set(moe_w_expert.transpose(0, 2, 1)).astype(BF16)
    br = jnp.zeros((DEPTH, ROUTE_ROWS, 1), F32)
    br = br.at[:, :MOE_GROUPS, 0].set(moe_b_group).at[:, MOE_GROUPS:MOE_GROUPS + MOE_EXPERTS, 0].set(moe_b_expert)
    w_gate = moe_w_gate.reshape(DEPTH * MOE_EXPERTS, d, MOE_D_FF)
    w_up = moe_w_up.reshape(DEPTH * MOE_EXPERTS, d, MOE_D_FF)
    w_down = moe_w_down.reshape(DEPTH * MOE_EXPERTS, MOE_D_FF, d)

    sfin, kv = None, None
    for i in range(DEPTH):
        j = i // 2
        modr = mod[i].reshape(N_COND * N_MOD, 1, d)
        if i % 2 == 0:
            common = (norm_g[i, 0], modr, w_in[j], lbs[j], hgrn_onorm_g[j], w_hout[j])
            xp, sfin = _hgrn_layer(xp, n_prompt_seq, seq, cond_prompt, *common, None, sfin, j)
            xs, _ = _hgrn_layer(xs, dec_batch, dec_seq, cond_sample, *common, state_hgrn[:, j], None, j)
        else:
            common = (norm_g[i, 0], modr, w_qkv[j], attn_qn_g[j], attn_kn_g[j], attn_lambda[j], attn_subln_g[j],
                      w_aout[j], i)
            xp, kv = _attn_layer(xp, n_prompt_seq, seq, cond_prompt, *common, None, None, kv, j)
            xs, _ = _attn_layer(xs, dec_batch, dec_seq, cond_sample, *common, cache_k[:, j], cache_v[:, j], None, j)
        xp, xs = _moe_layer(xp, xs, norm_g[i, 1], modr, wr_t[i], br[i], w_gate, w_up, w_down, i, cond_tile)

    new_k = kv[0].reshape(n_prompt_seq, DEPTH // 2, seq, HEADS, 2, QK_DIM)
    new_v = kv[1].reshape(n_prompt_seq, DEPTH // 2, seq, HEADS, HEAD_DIM)
    return (xp.reshape(n_prompt_seq, seq, d), xs.reshape(dec_batch, dec_seq, d), new_k, new_v, sfin)
```

```python
import functools
import math

import numpy as np
import jax
import jax.numpy as jnp
from jax import lax
from jax.experimental import pallas as pl
from jax.experimental.pallas import tpu as pltpu

F32 = jnp.float32
BF16 = jnp.bfloat16

D_MODEL = 1024
DEPTH = 4
GRID_W = 64
HEADS = 8
HEAD_DIM = 128
QK_DIM = 64
ROPE_THETA = 10000.0
MOE_GROUPS = 4
MOE_EPG = 4
MOE_EXPERTS = MOE_GROUPS * MOE_EPG
MOE_D_FF = 512
EPS = 1e-6
N_COND = 8
N_MOD = 6
HGRN_PARTS = 5

LANES = 128
BLK = 256
CHUNK = 32
N_CHUNK = BLK // CHUNK
HEADS_PER_STEP = 2
SCAN_HEADS_PER_BODY = 4
ATTN_HEADS_PER_BODY = 8
SUPER = 12
SUPER_ROWS = SUPER * BLK
STRAIGHT_TILES = 8
TBL_RSTART, TBL_SGROUP, TBL_SNT, TBL_SBLK, TBL_CODE, TBL_CLO, TBL_CHI, TBL_OBLK, TBL_UT = range(9)
CODE_SKIP, CODE_GATHER, CODE_ZERO = 0, 1, 2
EXPERT_TILES = 2
FF_PART = 256
GATHER_WIN = 6
EXP_CLAMP = 80.0
VMEM_LIMIT = 56 * 1024 * 1024
N_SLOTS = DEPTH // 2
_RESIDENT = dict(pipeline_mode=pl.Buffered(1))


def _cparams(sem):
    return pltpu.CompilerParams(dimension_semantics=sem, vmem_limit_bytes=VMEM_LIMIT)


def _silu(x):
    return x * jax.nn.sigmoid(x)


def _dot(a, b):
    return jnp.dot(a, b, preferred_element_type=F32)


def _dot_nt(a, b):
    return lax.dot_general(a, b, (((1,), (1,)), ((), ())), preferred_element_type=F32)


def _dot_tn(a, b):
    return lax.dot_general(a, b, (((0,), (0,)), ((), ())), preferred_element_type=F32)


def _lane_block(i, width):
    return pl.ds(pl.multiple_of(i * width, width), width)


def _mod_kernel(c_ref, w_ref, b_ref, o_ref):
    o_ref[...] = _dot(_silu(c_ref[...]), w_ref[...]) + b_ref[...]


def _modulation(cond, w_ada, b_ada):
    tn = 1536
    nj = (N_MOD * D_MODEL) // tn
    return pl.pallas_call(
        _mod_kernel,
        grid=(DEPTH, nj),
        in_specs=[
            pl.BlockSpec((N_COND, D_MODEL), lambda l, j: (0, 0)),
            pl.BlockSpec((None, D_MODEL, tn), lambda l, j: (l, 0, j)),
            pl.BlockSpec((None, 1, tn), lambda l, j: (l, 0, j)),
        ],
        out_specs=pl.BlockSpec((None, N_COND, tn), lambda l, j: (l, 0, j)),
        out_shape=jax.ShapeDtypeStruct((DEPTH, N_COND, N_MOD * D_MODEL), F32),
        compiler_params=_cparams(("parallel", "parallel")),
        name="modulation",
    )(cond, w_ada, b_ada.reshape(DEPTH, 1, N_MOD * D_MODEL))


def _norm_mod(x, g, sc, sh):
    ms = jnp.mean(x * x, axis=-1, keepdims=True)
    return (x * lax.rsqrt(ms + EPS) * g) * (1.0 + sc) + sh


def _mod_spec(cond_of_step, which):
    return pl.BlockSpec((None, 1, D_MODEL), lambda i, *_: (cond_of_step(i) * N_MOD + which, 0, 0))


def _scan_constants():
    t = np.arange(BLK)
    out = []
    for rev in (False, True):
        u = (BLK - 1 - t) if rev else t
        ut, us = u[:, None], u[None, :]
        cums = (us <= ut).astype(np.float32)
        lev = np.where(us > ut, 0,
              np.where(ut // CHUNK == us // CHUNK, 1,
              np.where(ut // 64 == us // 64, 2,
              np.where(ut // 128 == us // 128, 3, 4)))).astype(np.int32)
        out += [jnp.asarray(cums, BF16), jnp.asarray(lev)]
    return out


def _rows_to_block(rows, rev):
    order = rows[::-1] if rev else rows
    return jnp.concatenate([jnp.broadcast_to(r, (CHUNK, LANES)) for r in order], axis=0)


def _scan_prep(q, z, lb, cums, rev, with_inter):
    sig = jax.nn.sigmoid(z)
    f = lb + (1.0 - lb) * sig
    logf = jnp.log(f)
    k = (1.0 - lb) * (1.0 - sig)
    hi = logf.astype(BF16)
    lo = (logf - hi.astype(F32)).astype(BF16)
    bb = _dot(cums, jnp.concatenate([hi, lo], axis=1))
    b = bb[:, :LANES] + bb[:, LANES:]

    e_row, m_row = (0, CHUNK // 2) if rev else (CHUNK - 1, CHUNK // 2 - 1)
    ends, mids = [], []
    for j in range(N_CHUNK):
        ends.append(b[j * CHUNK + e_row:j * CHUNK + e_row + 1, :])
        mids.append(b[j * CHUNK + m_row:j * CHUNK + m_row + 1, :])
    if rev:
        ends, mids = ends[::-1], mids[::-1]
    zero = jnp.zeros((1, LANES), F32)
    one = jnp.ones((1, LANES), F32)
    pres = [zero] + ends[:-1]
    b_pre = _rows_to_block(pres, rev)
    b_end = _rows_to_block(ends, rev)
    b_mid = _rows_to_block(mids, rev)

    qd = q * jnp.exp(b - b_pre)
    ku = k * jnp.exp(b_end - b)
    qm = q * jnp.exp(jnp.clip(b - b_mid, -EXP_CLAMP, EXP_CLAMP))
    km = k * jnp.exp(jnp.clip(b_mid - b, -EXP_CLAMP, EXP_CLAMP))

    levels = []
    for nc in (2, 4, 8):
        fq, fk = [], []
        for ju in range(N_CHUNK):
            r = (ju // nc) * nc + nc // 2 - 1
            if ju % nc >= nc // 2:
                fq.append(one if nc == 2 else jnp.exp(pres[ju] - ends[r]))
                fk.append(zero)
            else:
                fq.append(zero)
                fk.append(one if nc == 2 else jnp.exp(ends[r] - ends[ju]))
        levels.append(((qd * _rows_to_block(fq, rev)).astype(BF16), (ku * _rows_to_block(fk, rev)).astype(BF16)))

    last = ends[-1]
    qh = (qd * _rows_to_block([jnp.exp(p) for p in pres], rev)).astype(BF16) if with_inter else None
    kh = (ku * _rows_to_block([jnp.exp(last - e) for e in ends], rev)).astype(BF16)
    return (qm.astype(BF16), km.astype(BF16)), levels, qh, kh, last


def _scan_att(prep, lev):
    att = jnp.where(lev == 1, _dot_nt(*prep[0]), 0.0)
    for level, (ql, kl) in enumerate(prep[1], start=2):
        att = jnp.where(lev == level, _dot_nt(ql, kl), att)
    return att


def _scan_att_bidir(prep_f, prep_b, lev_f, lev_b):
    att = jnp.where(lev_f == 1, _dot_nt(*prep_f[0]), 0.0) + jnp.where(lev_b == 1, _dot_nt(*prep_b[0]), 0.0)
    lev = jnp.maximum(lev_f, lev_b)
    for level, ((qf, kf), (qb, kb)) in enumerate(zip(prep_f[1], prep_b[1]), start=2):
        both = _dot_nt(jnp.concatenate([qf, qb], axis=1), jnp.concatenate([kf, kb], axis=1))
        att = jnp.where(lev == level, both, att)
    return att


def _scan_state(prep, v, st_prev):
    ut = _dot_tn(v.astype(BF16), prep[3])
    return ut if st_prev is None else st_prev * jnp.exp(prep[4]) + ut


def _scan_finish(o, g, on):
    ms = jnp.mean(o * o, axis=-1, keepdims=True)
    return ((o * lax.rsqrt(ms + EPS) * on) * _silu(g)).astype(BF16)


def _slot_view(ref, slot, owns_all_slots):
    if not owns_all_slots:
        return ref
    for s in range(ref.shape[0]):
        if s != slot:
            ref[s] = jnp.zeros(ref.shape[1:], ref.dtype)
    return ref.at[slot]


def _hgrn_kernel(*refs, n_blocks, has_state, slot, owns_all_slots, hps):
    it = iter(refs)
    x_ref, g_ref, sh_ref, sc_ref, gate_ref, win_ref, lb_ref, on_ref = (next(it) for _ in range(8))
    cf_ref, lf_ref, cb_ref, lvb_ref = (next(it) for _ in range(4))
    s0_ref = next(it) if has_state else None
    wout_ref, xo_ref = next(it), next(it)
    sfin_ref = None if has_state else next(it)
    h_ref, proj_ref, og_ref = next(it), next(it), next(it)
    oacc_ref, st_ref = (next(it), next(it)) if has_state else (None, None)

    h_ref[...] = _norm_mod(x_ref[...], g_ref[...], sc_ref[...], sh_ref[...]).astype(BF16)
    on = on_ref[...]
    if not has_state:
        sfin_ref = _slot_view(sfin_ref, slot, owns_all_slots)

    group_w = hps * HEAD_DIM

    def part(i, p, rows=slice(None)):
        return proj_ref[rows, p * group_w + i * HEAD_DIM:p * group_w + (i + 1) * HEAD_DIM]

    def pair(hp, carry):
        for p in range(HGRN_PARTS):
            cols = pl.ds(pl.multiple_of(p * D_MODEL + hp * group_w, group_w), group_w)
            proj_ref[:, p * group_w:(p + 1) * group_w] = _dot(h_ref[...], win_ref[:, cols])
        heads = [hp * hps + i for i in range(hps)]
        lbs = [lb_ref[:, _lane_block(hd, HEAD_DIM)] for hd in heads]
        if not has_state:
            for i, hd in enumerate(heads):
                q, v = part(i, 0), part(i, 1)
                prep_f = _scan_prep(q, part(i, 2), lbs[i][0:1, :], cf_ref[...], False, False)
                prep_b = _scan_prep(q, part(i, 3), lbs[i][1:2, :], cb_ref[...], True, False)
                att = _scan_att_bidir(prep_f, prep_b, lf_ref[...], lvb_ref[...])
                o = _dot(att.astype(BF16), v.astype(BF16))
                sfin_ref[0, hd] = _scan_state(prep_f, v, None).T
                sfin_ref[1, hd] = _scan_state(prep_b, v, None).T
                og_ref[:, _lane_block(hd, HEAD_DIM)] = _scan_finish(o, part(i, 4), on)
        else:
            for i, hd in enumerate(heads):
                st_ref[2 * i] = s0_ref[0, hd].T
                st_ref[2 * i + 1] = s0_ref[1, hd].T

            def body(tb, c2):
                rf = pl.ds(pl.multiple_of(tb * BLK, BLK), BLK)
                rb = pl.ds(pl.multiple_of((n_blocks - 1 - tb) * BLK, BLK), BLK)
                for i in range(hps):
                    cols = slice(i * HEAD_DIM, (i + 1) * HEAD_DIM)
                    v_f, v_b = part(i, 1, rf), part(i, 1, rb)
                    prep_f = _scan_prep(part(i, 0, rf), part(i, 2, rf), lbs[i][0:1, :], cf_ref[...], False, True)
                    prep_b = _scan_prep(part(i, 0, rb), part(i, 3, rb), lbs[i][1:2, :], cb_ref[...], True, True)
                    st_f, st_b = st_ref[2 * i], st_ref[2 * i + 1]
                    of = (_dot(_scan_att(prep_f, lf_ref[...]).astype(BF16), v_f.astype(BF16))
                          + _dot_nt(prep_f[2], st_f.astype(BF16)))
                    ob = (_dot(_scan_att(prep_b, lvb_ref[...]).astype(BF16), v_b.astype(BF16))
                          + _dot_nt(prep_b[2], st_b.astype(BF16)))
                    st_ref[2 * i] = _scan_state(prep_f, v_f, st_f)
                    st_ref[2 * i + 1] = _scan_state(prep_b, v_b, st_b)

                    @pl.when(2 * tb < n_blocks)
                    def _():
                        oacc_ref[rf, cols] = of
                        oacc_ref[rb, cols] = ob

                    @pl.when(2 * tb >= n_blocks)
                    def _():
                        oacc_ref[rf, cols] += of
                        oacc_ref[rb, cols] += ob

                return c2

            lax.fori_loop(0, n_blocks, body, 0)
            for i, hd in enumerate(heads):
                cols = slice(i * HEAD_DIM, (i + 1) * HEAD_DIM)
                og_ref[:, _lane_block(hd, HEAD_DIM)] = _scan_finish(oacc_ref[:, cols], part(i, 4), on)
        return carry

    lax.fori_loop(0, HEADS // hps, pair, 0)
    xo_ref[...] = x_ref[...] + gate_ref[...] * _dot(og_ref[...], wout_ref[...])


def _hgrn_layer(x, n_seq, t, cond_of_seq, g, modr, w_in, lbs_j, onorm_g, w_out, s0, sfin_prev, slot):
    d = D_MODEL
    has_state = s0 is not None
    n_blocks = t // BLK
    assert n_blocks == 1 or n_blocks % 2 == 0
    full = lambda shape, **kw: pl.BlockSpec(shape, lambda b, *_: (0,) * len(shape), **kw)
    in_specs = [
        pl.BlockSpec((t, d), lambda b: (b, 0)),
        full((1, d)),
        _mod_spec(cond_of_seq, 0), _mod_spec(cond_of_seq, 1), _mod_spec(cond_of_seq, 2),
        full((d, HGRN_PARTS * d), **_RESIDENT),
        full((2, d)),
        full((1, HEAD_DIM)),
    ] + [full((BLK, BLK))] * 4
    args = [x, g.reshape(1, d), modr, modr, modr, w_in, lbs_j, onorm_g.reshape(1, HEAD_DIM)] + _scan_constants()
    state_block = (None, 2, HEADS, HEAD_DIM, HEAD_DIM)
    if has_state:
        in_specs.append(pl.BlockSpec(state_block, lambda b: (b, 0, 0, 0, 0)))
        args.append(s0)
    in_specs.append(full((d, d), **_RESIDENT))
    args.append(w_out)
    out_specs = [pl.BlockSpec((t, d), lambda b: (b, 0))]
    out_shape = [jax.ShapeDtypeStruct((n_seq * t, d), F32)]
    hps = HEADS_PER_STEP if has_state else SCAN_HEADS_PER_BODY
    group_w = hps * HEAD_DIM
    scratch = [pltpu.VMEM((t, d), BF16), pltpu.VMEM((t, HGRN_PARTS * group_w), F32), pltpu.VMEM((t, d), BF16)]
    aliases = {}
    if has_state:
        scratch += [pltpu.VMEM((t, group_w), F32), pltpu.VMEM((2 * hps, HEAD_DIM, HEAD_DIM), F32)]
    else:
        state_dims = (2, HEADS, HEAD_DIM, HEAD_DIM)
        if sfin_prev is None:
            out_specs.append(pl.BlockSpec((None, N_SLOTS) + state_dims, lambda b: (b, 0, 0, 0, 0, 0)))
        else:
            out_specs.append(pl.BlockSpec((None, None) + state_dims, lambda b: (b, slot, 0, 0, 0, 0)))
            in_specs.append(pl.BlockSpec(memory_space=pl.ANY))
            args.append(sfin_prev)
            aliases = {len(args) - 1: 1}
        out_shape.append(jax.ShapeDtypeStruct((n_seq, N_SLOTS) + state_dims, F32))

    def body(*refs):
        if sfin_prev is not None:
            n_in = len(args)
            refs = refs[:n_in - 1] + refs[n_in:]
        _hgrn_kernel(*refs, n_blocks=n_blocks, has_state=has_state, slot=slot, owns_all_slots=sfin_prev is None,
                     hps=hps)

    out = pl.pallas_call(
        body,
        grid=(n_seq,),
        in_specs=in_specs,
        out_specs=out_specs,
        out_shape=out_shape,
        scratch_shapes=scratch,
        input_output_aliases=aliases,
        compiler_params=_cparams(("parallel",)),
        name="hgrn_layer_sample" if has_state else "hgrn_layer_prompt",
    )(*args)
    return out if not has_state else (out[0], None)


def _rope_tables(t_lat):
    rows = t_lat // GRID_W
    row = jnp.repeat(jnp.arange(rows), GRID_W).astype(F32)
    col = jnp.tile(jnp.arange(GRID_W), rows).astype(F32)
    half = QK_DIM // 2
    inv_freq = ROPE_THETA ** (-jnp.arange(0, half, 2, dtype=F32) / half)
    ang_row = row[:, None] * inv_freq
    ang_col = col[:, None] * inv_freq

    def part(ang):
        c, s = jnp.cos(ang), jnp.sin(ang)
        return jnp.concatenate([c, c], axis=1), jnp.concatenate([-s, s], axis=1)

    cr, sr = part(ang_row)
    cc, sc = part(ang_col)
    cos64 = jnp.concatenate([cr, cc], axis=1)
    sin64 = jnp.concatenate([sr, sc], axis=1)
    return jnp.concatenate([cos64, cos64], axis=1), jnp.concatenate([sin64, sin64], axis=1)


def _rope(x, cos, sin):
    lane = lax.broadcasted_iota(jnp.int32, x.shape, 1)
    first = (lane % (QK_DIM // 2)) < (QK_DIM // 4)
    swapped = jnp.where(first, pltpu.roll(x, LANES - QK_DIM // 4, 1), pltpu.roll(x, QK_DIM // 4, 1))
    return x * cos + swapped * sin


def _attn_kernel(*refs, n_qblk, rope, cache, emit_kv, lam_init, slot, owns_all_slots):
    it = iter(refs)
    x_ref, g_ref, sh_ref, sc_ref, gate_ref, wqkv_ref = (next(it) for _ in range(6))
    qg_ref, kg_ref, lam_ref, sg_ref, seg_ref = (next(it) for _ in range(5))
    cos_ref = sin_ref = ck_ref = cv_ref = nk_ref = nv_ref = None
    if rope:
        cos_ref, sin_ref = next(it), next(it)
    if cache:
        ck_ref, cv_ref = next(it), next(it)
    wout_ref, xo_ref = next(it), next(it)
    if emit_kv:
        nk_ref = _slot_view(next(it), slot, owns_all_slots)
        nv_ref = _slot_view(next(it), slot, owns_all_slots)
    h_ref, qkv_ref, oa_ref = next(it), next(it), next(it)

    d = D_MODEL
    h_ref[...] = _norm_mod(x_ref[...], g_ref[...], sc_ref[...], sh_ref[...]).astype(BF16)
    qkv_ref[...] = _dot(h_ref[...], wqkv_ref[...])
    seg = seg_ref[...]
    lp = lam_ref[...]
    lam = (jnp.exp(jnp.sum(lp[0:1, :] * lp[1:2, :], axis=-1, keepdims=True))
           - jnp.exp(jnp.sum(lp[2:3, :] * lp[3:4, :], axis=-1, keepdims=True)) + lam_init)
    lane = lax.broadcasted_iota(jnp.int32, (1, LANES), 1)
    comp0 = lane < QK_DIM
    scale = QK_DIM ** -0.5

    def split(a):
        return [jnp.where(comp0, a, 0.0).astype(BF16), jnp.where(comp0, 0.0, a).astype(BF16)]

    def one_head(hd):
        hcol = _lane_block(hd, HEAD_DIM)
        k = qkv_ref[:, pl.ds(pl.multiple_of(d + hd * HEAD_DIM, HEAD_DIM), HEAD_DIM)]
        v = qkv_ref[:, pl.ds(pl.multiple_of(2 * d + hd * HEAD_DIM, HEAD_DIM), HEAD_DIM)]
        kn = k * lax.rsqrt(_dot(k * k, seg) + EPS) * kg_ref[...]
        if emit_kv:
            nk_ref[:, hcol] = kn
            nv_ref[:, hcol] = v
        if rope:
            kn = _rope(kn, cos_ref[...], sin_ref[...])
        ks = split(kn)
        vb = v.astype(BF16)
        if cache:
            cks = split(ck_ref[:, hcol])
            cvb = cv_ref[:, hcol].astype(BF16)
        for qi in range(n_qblk):
            rows = slice(qi * BLK, (qi + 1) * BLK)
            q = qkv_ref[rows, hcol]
            qn = q * lax.rsqrt(_dot(q * q, seg) + EPS) * qg_ref[...]
            if rope:
                qn = _rope(qn, cos_ref[rows, :], sin_ref[rows, :])
            qb = (qn * scale).astype(BF16)
            a_self, a_cache = None, None
            for c in range(2):
                s = _dot_nt(qb, ks[c])
                m = jnp.max(s, axis=-1, keepdims=True)
                if cache:
                    sc = _dot_nt(qb, cks[c])
                    m = jnp.maximum(m, jnp.max(sc, axis=-1, keepdims=True))
                p = jnp.exp(s - m)
                den = jnp.sum(p, axis=-1, keepdims=True)
                if cache:
                    pc = jnp.exp(sc - m)
                    den = den + jnp.sum(pc, axis=-1, keepdims=True)
                w = (1.0 / den) if c == 0 else (-lam / den)
                a_self = p * w if c == 0 else a_self + p * w
                if cache:
                    a_cache = pc * w if c == 0 else a_cache + pc * w
            o = _dot(a_self.astype(BF16), vb)
            if cache:
                o = o + _dot(a_cache.astype(BF16), cvb)
            ms = jnp.mean(o * o, axis=-1, keepdims=True)
            o = (o * lax.rsqrt(ms + EPS) * sg_ref[...]) * (1.0 - lam_init)
            oa_ref[rows, hcol] = o.astype(BF16)

    heads_per_body = ATTN_HEADS_PER_BODY if n_qblk == 1 else HEADS_PER_STEP

    def group(hg, carry):
        for i in range(heads_per_body):
            one_head(hg * heads_per_body + i)
        return carry

    lax.fori_loop(0, HEADS // heads_per_body, group, 0)
    xo_ref[...] = x_ref[...] + gate_ref[...] * _dot(oa_ref[...], wout_ref[...])


def _attn_layer(x, n_seq, t, cond_of_seq, g, modr, w_qkv, qn_g, kn_g, lam_p, subln_g, w_out, layer_idx,
                cache_k_j, cache_v_j, kv_prev, slot):
    d = D_MODEL
    cache = cache_k_j is not None
    lam_init = 0.8 - 0.6 * math.exp(-0.3 * layer_idx)
    qg = jnp.tile(qn_g.reshape(1, QK_DIM), (1, 2))
    kg = jnp.tile(kn_g.reshape(1, QK_DIM), (1, 2))
    li = np.arange(LANES)
    seg = jnp.asarray((li[:, None] // QK_DIM == li[None, :] // QK_DIM).astype(np.float32) / QK_DIM)
    full = lambda shape, **kw: pl.BlockSpec(shape, lambda b, *_: (0,) * len(shape), **kw)
    in_specs = [
        pl.BlockSpec((t, d), lambda b: (b, 0)),
        full((1, d)),
        _mod_spec(cond_of_seq, 0), _mod_spec(cond_of_seq, 1), _mod_spec(cond_of_seq, 2),
        full((d, 3 * d), **_RESIDENT),
        full((1, LANES)), full((1, LANES)), full((4, QK_DIM)), full((1, HEAD_DIM)), full((LANES, LANES)),
    ]
    args = [x, g.reshape(1, d), modr, modr, modr, w_qkv, qg, kg, lam_p, subln_g.reshape(1, HEAD_DIM), seg]
    if cache:
        cos, sin = _rope_tables(t)
        past = cache_k_j.shape[1]
        in_specs += [full((t, LANES)), full((t, LANES)),
                     pl.BlockSpec((None, past, d), lambda b: (b, 0, 0)),
                     pl.BlockSpec((None, past, d), lambda b: (b, 0, 0))]
        args += [cos, sin, cache_k_j.reshape(n_seq, past, d), cache_v_j.reshape(n_seq, past, d)]
    in_specs.append(full((d, d), **_RESIDENT))
    args.append(w_out)
    out_specs = [pl.BlockSpec((t, d), lambda b: (b, 0))]
    out_shape = [jax.ShapeDtypeStruct((n_seq * t, d), F32)]
    aliases = {}
    n_carried = 0
    if not cache:
        if kv_prev is None:
            kv_spec = pl.BlockSpec((None, N_SLOTS, t, d), lambda b: (b, 0, 0, 0))
        else:
            kv_spec = pl.BlockSpec((None, None, t, d), lambda b: (b, slot, 0, 0))
            in_specs += [pl.BlockSpec(memory_space=pl.ANY)] * 2
            args += list(kv_prev)
            aliases = {len(args) - 2: 1, len(args) - 1: 2}
            n_carried = 2
        out_specs += [kv_spec, kv_spec]
        out_shape += [jax.ShapeDtypeStruct((n_seq, N_SLOTS, t, d), F32)] * 2

    def body(*refs):
        n_in = len(args)
        refs = refs[:n_in - n_carried] + refs[n_in:]
        _attn_kernel(*refs, n_qblk=t // BLK, rope=cache, cache=cache, emit_kv=not cache, lam_init=lam_init,
                     slot=slot, owns_all_slots=kv_prev is None)

    out = pl.pallas_call(
        body,
        grid=(n_seq,),
        in_specs=in_specs,
        out_specs=out_specs,
        out_shape=out_shape,
        scratch_shapes=[pltpu.VMEM((t, d), BF16), pltpu.VMEM((t, 3 * d), F32), pltpu.VMEM((t, d), BF16)],
        input_output_aliases=aliases,
        compiler_params=_cparams(("parallel",)),
        name="attn_layer_sample" if cache else "attn_layer_prompt",
    )(*args)
    return (out[0], None) if cache else (out[0], (out[1], out[2]))


ROUTE_ROWS = 32
INFO_GID, INFO_RANK = 8, 9


def _two_stream_specs(n_prompt_tiles, width):
    return [pl.BlockSpec((BLK, width), lambda i, *_: (jnp.minimum(i, n_prompt_tiles - 1), 0)),
            pl.BlockSpec((BLK, width), lambda i, *_: (jnp.maximum(i - n_prompt_tiles, 0), 0))]


def _route_kernel(xp_ref, xs_ref, g_ref, sh_ref, sc_ref, wr_ref, br_ref, tri_ref,
                  h_ref, il_ref, it_ref, cum_ref, cumhi_ref, tot_ref, carry_ref, *, n_prompt_tiles):
    i = pl.program_id(0)

    @pl.when(i == 0)
    def _():
        carry_ref[...] = jnp.zeros_like(carry_ref)

    @pl.when(i < n_prompt_tiles)
    def _():
        h_ref[...] = _norm_mod(xp_ref[...], g_ref[...], sc_ref[...], sh_ref[...]).astype(BF16)

    @pl.when(i >= n_prompt_tiles)
    def _():
        h_ref[...] = _norm_mod(xs_ref[...], g_ref[...], sc_ref[...], sh_ref[...]).astype(BF16)

    logit = _dot_nt(wr_ref[...], h_ref[...]) + br_ref[...]
    gl = [logit[g:g + 1, :] for g in range(MOE_GROUPS)]
    gmax = functools.reduce(jnp.maximum, gl)
    gz = functools.reduce(lambda a, b: a + b, [jnp.exp(x - gmax) for x in gl])
    g_w = 1.0 / gz
    gid = jnp.full_like(gmax, MOE_GROUPS - 1)
    for g in range(MOE_GROUPS - 2, -1, -1):
        gid = jnp.where(gl[g] == gmax, float(g), gid)
    el = []
    for j in range(MOE_EPG):
        e = logit[MOE_GROUPS + j:MOE_GROUPS + j + 1, :]
        for g in range(1, MOE_GROUPS):
            r = MOE_GROUPS + g * MOE_EPG + j
            e = jnp.where(gid == float(g), logit[r:r + 1, :], e)
        el.append(e)
    emax = functools.reduce(jnp.maximum, el)
    pe = [jnp.exp(e - emax) for e in el]
    idx1 = jnp.full_like(emax, MOE_EPG - 1)
    for j in range(MOE_EPG - 2, -1, -1):
        idx1 = jnp.where(el[j] == emax, float(j), idx1)
    el2 = [jnp.where(idx1 == float(j), -jnp.inf, el[j]) for j in range(MOE_EPG)]
    emax2 = functools.reduce(jnp.maximum, el2)
    idx2 = jnp.full_like(emax, MOE_EPG - 1)
    for j in range(MOE_EPG - 2, -1, -1):
        idx2 = jnp.where(el2[j] == emax2, float(j), idx2)
    sel = [(idx1 == float(j)) | (idx2 == float(j)) for j in range(MOE_EPG)]
    den = functools.reduce(lambda a, b: a + b, [jnp.where(sel[j], pe[j], 0.0) for j in range(MOE_EPG)])
    cw = [jnp.where(sel[j], pe[j] * (g_w / den), 0.0) for j in range(MOE_EPG)]

    row8 = lax.broadcasted_iota(jnp.int32, (8, BLK), 0)
    onehot = jnp.where(row8.astype(F32) == gid, 1.0, 0.0)
    within = _dot(onehot.astype(BF16), tri_ref[...])
    carry = carry_ref[...]
    rank = jnp.sum(onehot * (within + carry[:, 0:1]), axis=0, keepdims=True)
    cum_ref[...] = carry
    new_carry = carry + jnp.sum(onehot, axis=1, keepdims=True)
    carry_ref[...] = new_carry
    cumhi_ref[...] = new_carry
    tot_ref[...] = new_carry

    il_ref[...] = jnp.where(row8 == 0, gid, jnp.where(row8 == 1, rank, 0.0))
    rowl = lax.broadcasted_iota(jnp.int32, (LANES, BLK), 0)
    m = jnp.zeros((LANES, BLK), F32)
    for j in range(MOE_EPG):
        hi = cw[j].astype(BF16).astype(F32)
        m = jnp.where(rowl == j, hi, m)
        m = jnp.where(rowl == MOE_EPG + j, cw[j] - hi, m)
    m = jnp.where(rowl == INFO_GID, gid, m)
    m = jnp.where(rowl == INFO_RANK, rank, m)
    it_ref[...] = m.T


def _moe_route(xp, xs, g, modr, wr_t, br, cond_of_tile):
    d = D_MODEL
    npt = xp.shape[0] // BLK
    nt = npt + xs.shape[0] // BLK
    n = nt * BLK
    tri = jnp.asarray(np.triu(np.ones((BLK, BLK), np.float32), 1), BF16)
    full = lambda shape: pl.BlockSpec(shape, lambda i: (0,) * len(shape))
    return pl.pallas_call(
        functools.partial(_route_kernel, n_prompt_tiles=npt),
        grid=(nt,),
        in_specs=_two_stream_specs(npt, d) + [
            full((1, d)),
            _mod_spec(cond_of_tile, 3), _mod_spec(cond_of_tile, 4),
            full((ROUTE_ROWS, d)), full((ROUTE_ROWS, 1)), full((BLK, BLK)),
        ],
        out_specs=[
            pl.BlockSpec((BLK, d), lambda i: (i, 0)),
            pl.BlockSpec((8, BLK), lambda i: (0, i)),
            pl.BlockSpec((BLK, LANES), lambda i: (i, 0)),
            pl.BlockSpec((None, 8, LANES), lambda i: (i, 0, 0)),
            pl.BlockSpec((None, 8, LANES), lambda i: (i, 0, 0)),
            pl.BlockSpec((8, LANES), lambda i: (0, 0)),
        ],
        out_shape=[
            jax.ShapeDtypeStruct((n, d), BF16),
            jax.ShapeDtypeStruct((8, n), F32),
            jax.ShapeDtypeStruct((n, LANES), F32),
            jax.ShapeDtypeStruct((nt, 8, LANES), F32),
            jax.ShapeDtypeStruct((nt, 8, LANES), F32),
            jax.ShapeDtypeStruct((8, LANES), F32),
        ],
        scratch_shapes=[pltpu.VMEM((8, LANES), F32)],
        compiler_params=_cparams(("arbitrary",)),
        name="moe_route",
    )(xp, xs, g.reshape(1, d), modr, modr, wr_t, br, tri)


def _sorted_pos(gid, rank, rstart_ref):
    p = rank
    for g in range(MOE_GROUPS):
        p = p + jnp.where(gid == float(g), rstart_ref[g].astype(F32), 0.0)
    return p


def _gather_kernel(tbl_ref, h_ref, il_ref, it_ref, hs_ref, cws_ref, *, n_tiles):
    a = pl.program_id(0)
    win = GATHER_WIN * BLK
    code = tbl_ref[TBL_CODE * LANES + a]
    rstart_ref = tbl_ref

    @pl.when(code == CODE_ZERO)
    def _():
        hs_ref[...] = jnp.zeros_like(hs_ref)
        cws_ref[...] = jnp.zeros_like(cws_ref)

    @pl.when(code == CODE_GATHER)
    def _():
        dest = (lax.broadcasted_iota(jnp.int32, (BLK, 1), 0) + a * BLK).astype(F32)
        src_tile = lax.broadcasted_iota(jnp.int32, (1, win), 1) // BLK
        clo = tbl_ref[TBL_CLO * LANES + a]
        n_win = (tbl_ref[TBL_CHI * LANES + a] - clo + GATHER_WIN) // GATHER_WIN

        def window(w):
            first = clo + w * GATHER_WIN
            c0 = jnp.minimum(first, n_tiles - GATHER_WIN)
            rows = pl.ds(pl.multiple_of(c0 * BLK, BLK), win)
            info = il_ref[:, rows]
            p = _sorted_pos(info[0:1, :], info[1:2, :], rstart_ref)
            p = jnp.where(src_tile + c0 >= first, p, -1.0)
            onehot = jnp.where(dest == p, 1.0, 0.0).astype(BF16)
            r = _dot(onehot, it_ref[rows, :].astype(BF16))
            return _dot(onehot, h_ref[rows, :]), r + pltpu.roll(r, LANES - MOE_EPG, 1)

        dh, dc = window(0)
        hs_ref[...] = dh.astype(BF16)
        cws_ref[...] = dc

        def body(w, carry):
            dh, dc = window(w)
            hs_ref[...] = (hs_ref[...].astype(F32) + dh).astype(BF16)
            cws_ref[...] += dc
            return carry

        lax.fori_loop(1, n_win, body, 0)


def _moe_gather(h, info_lane, info_tok, tbl, n_dest_tiles):
    n, d = h.shape
    nt = n // BLK
    assert nt >= GATHER_WIN and n_dest_tiles <= LANES
    out_block = lambda a, tbl_r: (tbl_r[TBL_OBLK * LANES + a], 0)
    return pl.pallas_call(
        functools.partial(_gather_kernel, n_tiles=nt),
        grid_spec=pltpu.PrefetchScalarGridSpec(
            num_scalar_prefetch=1,
            grid=(n_dest_tiles,),
            in_specs=[
                pl.BlockSpec((n, d), lambda a, *_: (0, 0), **_RESIDENT),
                pl.BlockSpec((8, n), lambda a, *_: (0, 0), **_RESIDENT),
                pl.BlockSpec((n, LANES), lambda a, *_: (0, 0), **_RESIDENT),
            ],
            out_specs=[pl.BlockSpec((BLK, d), out_block), pl.BlockSpec((BLK, LANES), out_block)],
        ),
        out_shape=[
            jax.ShapeDtypeStruct((n_dest_tiles * BLK, d), BF16),
            jax.ShapeDtypeStruct((n_dest_tiles * BLK, LANES), F32),
        ],
        compiler_params=_cparams(("arbitrary",)),
        name="moe_gather",
    )(tbl, h, info_lane, info_tok)


def _moe_mlp_kernel(tbl_ref, hs_ref, cws_ref, wg_ref, wu_ref, wd_ref, ys_ref,
                    acc_ref, wgb_ref, wub_ref, wdb_ref):
    s = pl.program_id(0)
    k = pl.program_id(1)
    nt = tbl_ref[TBL_SNT * LANES + s]

    @pl.when(nt > 0)
    def _():
        wgb_ref[...] = wg_ref[...].astype(BF16)
        wub_ref[...] = wu_ref[...].astype(BF16)
        wdb_ref[...] = wd_ref[...].astype(BF16)

    @pl.when((k == 0) & (nt > 0))
    def _():
        acc_ref[...] = jnp.zeros_like(acc_ref)

    def block(j):
        rows = slice(j * EXPERT_TILES * BLK, (j + 1) * EXPERT_TILES * BLK)
        hsub = hs_ref[rows, :]
        cws = cws_ref[rows, :]
        cwk = jnp.zeros((EXPERT_TILES * BLK, 1), F32)
        for kk in range(MOE_EPG):
            cwk = jnp.where(k == kk, cws[:, kk:kk + 1], cwk)
        y = None
        for fh in range(MOE_D_FF // FF_PART):
            fc = slice(fh * FF_PART, (fh + 1) * FF_PART)
            gate = _dot(hsub, wgb_ref[:, fc])
            up = _dot(hsub, wub_ref[:, fc])
            act = ((_silu(gate) * up) * cwk).astype(BF16)
            part = _dot(act, wdb_ref[fc, :])
            y = part if y is None else y + part
        acc_ref[rows, :] += y

    n_blocks = SUPER // EXPERT_TILES
    n_straight = STRAIGHT_TILES // EXPERT_TILES

    @pl.when(nt >= STRAIGHT_TILES)
    def _():
        for j in range(n_straight):
            block(j)

    @pl.when(nt < STRAIGHT_TILES)
    def _():
        for j in range(n_straight):
            pl.when(j * EXPERT_TILES < nt)(functools.partial(block, j))

    for j in range(n_straight, n_blocks):
        pl.when(j * EXPERT_TILES < nt)(functools.partial(block, j))

    @pl.when((k == MOE_EPG - 1) & (nt > 0))
    def _():
        ys_ref[...] = acc_ref[...].astype(BF16)


def _moe_mlp(hs, cws, w_gate, w_up, w_down, layer, tbl, n_super):
    d = hs.shape[1]

    def widx(s, k, tbl_r):
        kk = jnp.where(tbl_r[TBL_SNT * LANES + s] > 0, k, MOE_EPG - 1)
        return (layer * MOE_EXPERTS + tbl_r[TBL_SGROUP * LANES + s] * MOE_EPG + kk, 0, 0)

    rows_idx = lambda s, k, tbl_r: (tbl_r[TBL_SBLK * LANES + s], 0)
    return pl.pallas_call(
        _moe_mlp_kernel,
        grid_spec=pltpu.PrefetchScalarGridSpec(
            num_scalar_prefetch=1,
            grid=(n_super, MOE_EPG),
            in_specs=[
                pl.BlockSpec((SUPER_ROWS, d), rows_idx),
                pl.BlockSpec((SUPER_ROWS, LANES), rows_idx),
                pl.BlockSpec((None, d, MOE_D_FF), widx),
                pl.BlockSpec((None, d, MOE_D_FF), widx),
                pl.BlockSpec((None, MOE_D_FF, d), widx),
            ],
            out_specs=pl.BlockSpec((SUPER_ROWS, d), rows_idx),
            scratch_shapes=[
                pltpu.VMEM((SUPER_ROWS, d), F32),
                pltpu.VMEM((d, MOE_D_FF), BF16),
                pltpu.VMEM((d, MOE_D_FF), BF16),
                pltpu.VMEM((MOE_D_FF, d), BF16),
            ],
        ),
        out_shape=jax.ShapeDtypeStruct((n_super * SUPER_ROWS, d), BF16),
        compiler_params=_cparams(("arbitrary", "arbitrary")),
        name="moe_experts",
    )(tbl, hs, cws, w_gate, w_up, w_down)


N_SRC = 2 * MOE_GROUPS


def _unsort_kernel(tbl_ref, xp_ref, xs_ref, it_ref, gate_ref, *rest, n_prompt_tiles):
    ys_refs, op_ref, os_ref, acc_ref = rest[:N_SRC], rest[N_SRC], rest[N_SRC + 1], rest[N_SRC + 2]
    t = pl.program_id(0)
    info = it_ref[...]
    p = _sorted_pos(info[:, INFO_GID:INFO_GID + 1], info[:, INFO_RANK:INFO_RANK + 1], tbl_ref)
    lane = lax.broadcasted_iota(jnp.int32, (1, BLK), 1).astype(F32)
    slot = lambda m: tbl_ref[(TBL_UT + t) * LANES + m]

    def take(m):
        a = slot(m)
        onehot = jnp.where(p - (a * BLK).astype(F32) == lane, 1.0, 0.0).astype(BF16)
        return _dot(onehot, ys_refs[m][...])

    acc_ref[...] = functools.reduce(lambda u, w: u + w, [take(m) for m in range(0, N_SRC, 2)])
    for m in range(1, N_SRC, 2):
        @pl.when(slot(m) >= 0)
        def _():
            acc_ref[...] += take(m)

    @pl.when(t < n_prompt_tiles)
    def _():
        op_ref[...] = xp_ref[...] + gate_ref[...] * acc_ref[...]

    @pl.when(t >= n_prompt_tiles)
    def _():
        os_ref[...] = xs_ref[...] + gate_ref[...] * acc_ref[...]


def _moe_unsort(xp, xs, info_tok, modr, ys, tbl, cond_of_tile):
    d = D_MODEL
    npt = xp.shape[0] // BLK
    nt = npt + xs.shape[0] // BLK

    def ys_spec(m):
        return pl.BlockSpec((BLK, d), lambda t, tbl_r: (jnp.maximum(tbl_r[(TBL_UT + t) * LANES + m], 0), 0))

    return pl.pallas_call(
        functools.partial(_unsort_kernel, n_prompt_tiles=npt),
        grid_spec=pltpu.PrefetchScalarGridSpec(
            num_scalar_prefetch=1,
            grid=(nt,),
            in_specs=_two_stream_specs(npt, d) + [
                pl.BlockSpec((BLK, LANES), lambda t, *_: (t, 0)),
                _mod_spec(cond_of_tile, 5),
            ] + [ys_spec(m) for m in range(N_SRC)],
            out_specs=_two_stream_specs(npt, d),
            scratch_shapes=[pltpu.VMEM((BLK, d), F32)],
        ),
        out_shape=[jax.ShapeDtypeStruct(xp.shape, F32), jax.ShapeDtypeStruct(xs.shape, F32)],
        compiler_params=_cparams(("arbitrary",)),
        name="moe_unsort_residual",
    )(tbl, xp, xs, info_tok, modr, *([ys] * N_SRC))


def _tables_kernel(lo_ref, hi_ref, tot_ref, tbl_ref, *, n_tiles, n_super):
    one = lambda cond: jnp.where(cond, 1.0, 0.0)
    groups = range(MOE_GROUPS)
    lane = lax.broadcasted_iota(jnp.int32, (1, LANES), 1).astype(F32)
    tot = [tot_ref[g:g + 1, :] for g in groups]
    pick = lambda vals, idx: sum(jnp.where(idx == float(g), vals[g], 0.0) for g in groups)

    n_sup = [sum(one(tot[g] > float(m * SUPER_ROWS)) for m in range(n_super)) for g in groups]
    sup_start, sup_end, run = [], [], 0.0
    for g in groups:
        sup_start.append(run + 0.0 * tot[g])
        run = run + n_sup[g]
        sup_end.append(run)
    n_used = sup_end[-1]
    rstart = [sup_start[g] * float(SUPER_ROWS) for g in groups]
    group_of = lambda s: jnp.minimum(sum(one(s >= sup_end[g]) for g in groups), float(MOE_GROUPS - 1))

    sg, snt, written = [], [], []
    for m in range(n_super):
        used = float(m) < n_used
        g_m = group_of(jnp.where(used, float(m), n_used - 1.0))
        rows_left = pick(tot, g_m) - (float(m) - pick(sup_start, g_m)) * float(SUPER_ROWS)
        nt_m = jnp.where(used, jnp.clip(jnp.floor((rows_left + float(BLK - 1)) * (1.0 / BLK)), 0.0, float(SUPER)), 0.0)
        sg.append(g_m)
        snt.append(nt_m)
        written.append(float(EXPERT_TILES) * jnp.floor((nt_m + float(EXPERT_TILES - 1)) * (1.0 / EXPERT_TILES)))
    sblk = [jnp.minimum(float(m), n_used - 1.0) for m in range(n_super)]
    by_super_lane = lambda vals: sum(jnp.where(lane == float(m), vals[m], 0.0) for m in range(n_super))
    last_written = sum(jnp.where(n_used - 1.0 == float(m), float(m * SUPER) + written[m] - 1.0, 0.0)
                       for m in range(n_super))

    sa = sum(one(lane >= float(m * SUPER)) for m in range(1, n_super))
    ja = lane - sa * float(SUPER)
    at_tile = lambda vals: sum(jnp.where(sa == float(m), vals[m], 0.0) for m in range(n_super))
    snt_a, ga, written_a = at_tile(snt), at_tile(sg), at_tile(written)
    code = jnp.where(ja < snt_a, float(CODE_GATHER), jnp.where(ja < written_a, float(CODE_ZERO), float(CODE_SKIP)))
    oblk = jnp.where(ja < written_a, lane,
                     jnp.where(written_a > 0.0, sa * float(SUPER) + written_a - 1.0, last_written))
    r0 = ((sa - pick(sup_start, ga)) * float(SUPER) + ja) * float(BLK)
    lo = [lo_ref[:, g, :] for g in groups]
    hi = [hi_ref[:, g, :] for g in groups]
    clo = jnp.sum(one(pick(hi, ga) <= r0), axis=0, keepdims=True)
    chi = jnp.sum(one(pick(lo, ga) < r0 + float(BLK)), axis=0, keepdims=True) - 1.0
    clo = jnp.clip(clo, 0.0, float(n_tiles - 1))
    chi = jnp.clip(chi, clo, float(n_tiles - 1))

    gm = one(lane >= 2.0) + one(lane >= 4.0) + one(lane >= 6.0)
    first = pick(rstart, gm) + pick(lo, gm)
    last = pick(rstart, gm) + pick(hi, gm) - 1.0
    t0 = jnp.floor(first * (1.0 / BLK))
    t1 = jnp.floor(last * (1.0 / BLK))
    has = pick(hi, gm) > pick(lo, gm)
    a0 = jnp.where(has, t0, -1.0)
    a1 = jnp.where(has, jnp.where(t1 != t0, t1, -1.0), -1.0)
    ut = jnp.where(lane >= float(N_SRC), -1.0, jnp.where(lane - 2.0 * gm == 0.0, a0, a1))

    def put(row, v):
        tbl_ref[row:row + 1, :] = v.astype(jnp.int32)

    put(TBL_RSTART, sum(jnp.where(lane == float(g), rstart[g], 0.0) for g in groups))
    put(TBL_SGROUP, by_super_lane(sg))
    put(TBL_SNT, by_super_lane(snt))
    put(TBL_SBLK, by_super_lane(sblk))
    put(TBL_CODE, code)
    put(TBL_CLO, clo)
    put(TBL_CHI, chi)
    put(TBL_OBLK, oblk)
    tbl_ref[TBL_UT:TBL_UT + n_tiles, :] = ut.astype(jnp.int32)


def _moe_tables(cum_lo, cum_hi, tot, n_tiles, n_super):
    assert n_super * SUPER <= LANES
    tbl = pl.pallas_call(
        functools.partial(_tables_kernel, n_tiles=n_tiles, n_super=n_super),
        out_shape=jax.ShapeDtypeStruct((TBL_UT + n_tiles, LANES), jnp.int32),
        name="moe_tables",
    )(cum_lo, cum_hi, tot)
    return tbl.reshape(-1)


def _moe_layer(xp, xs, g, modr, wr_t, br, w_gate, w_up, w_down, layer, cond_of_tile):
    nt = (xp.shape[0] + xs.shape[0]) // BLK
    n_super = (nt * BLK + SUPER_ROWS - 1) // SUPER_ROWS + MOE_GROUPS
    h, info_lane, info_tok, cum_lo, cum_hi, tot = _moe_route(xp, xs, g, modr, wr_t, br, cond_of_tile)
    tbl = _moe_tables(cum_lo, cum_hi, tot, nt, n_super)
    hs, cws = _moe_gather(h, info_lane, info_tok, tbl, n_super * SUPER)
    ys = _moe_mlp(hs, cws, w_gate, w_up, w_down, layer, tbl, n_super)
    return _moe_unsort(xp, xs, info_tok, modr, ys, tbl, cond_of_tile)


def kernel(x_prompt, x_sample, c, cache_k, cache_v, state_hgrn, c_ctx, norm_g, w_ada, b_ada, hgrn_w_in, hgrn_lb_logits, hgrn_onorm_g, hgrn_w_out, attn_w_qkv, attn_qn_g, attn_kn_g, attn_lambda, attn_subln_g, attn_w_out, moe_w_group, moe_b_group, moe_w_expert, moe_b_expert, moe_w_gate, moe_w_up, moe_w_down):
    n_prompt_seq, seq, d = x_prompt.shape
    dec_batch, dec_seq, _ = x_sample.shape
    n_prompt = n_prompt_seq * seq
    assert d == D_MODEL and seq == BLK and dec_seq % BLK == 0
    assert 1 + dec_batch <= N_COND

    xp = x_prompt.reshape(n_prompt, d)
    xs = x_sample.reshape(dec_batch * dec_seq, d)
    cond = jnp.zeros((N_COND, d), F32).at[0].set(c_ctx).at[1:1 + dec_batch].set(c)
    mod = _modulation(cond, w_ada, b_ada)

    lbs = jnp.cumsum(jax.nn.softmax(hgrn_lb_logits.astype(F32), axis=0), axis=0)
    lbs = lbs - lbs[0:1]

    cond_prompt = lambda b: 0
    cond_sample = lambda b: 1 + b
    npt, spt = n_prompt // BLK, dec_seq // BLK
    cond_tile = lambda i: jnp.where(i < npt, 0, 1 + (i - npt) // spt)

    w_in = hgrn_w_in.astype(BF16)
    w_hout = hgrn_w_out.astype(BF16)
    w_qkv = attn_w_qkv.astype(BF16)
    w_aout = attn_w_out.astype(BF16)
    wr_t = jnp.zeros((DEPTH, ROUTE_ROWS, d), F32)
    wr_t = wr_t.at[:, :MOE_GROUPS].set(moe_w_group.transpose(0, 2, 1))
    wr_t = wr_t.at[:, MOE_GROUPS:MOE_GROUPS + MOE_EXPERTS].set(moe_w_expert.transpose(0, 2, 1)).astype(BF16)
    br = jnp.zeros((DEPTH, ROUTE_ROWS, 1), F32)
    br = br.at[:, :MOE_GROUPS, 0].set(moe_b_group).at[:, MOE_GROUPS:MOE_GROUPS + MOE_EXPERTS, 0].set(moe_b_expert)
    w_gate = moe_w_gate.reshape(DEPTH * MOE_EXPERTS, d, MOE_D_FF)
    w_up = moe_w_up.reshape(DEPTH * MOE_EXPERTS, d, MOE_D_FF)
    w_down = moe_w_down.reshape(DEPTH * MOE_EXPERTS, MOE_D_FF, d)

    sfin, kv = None, None
    for i in range(DEPTH):
        j = i // 2
        modr = mod[i].reshape(N_COND * N_MOD, 1, d)
        if i % 2 == 0:
            common = (norm_g[i, 0], modr, w_in[j], lbs[j], hgrn_onorm_g[j], w_hout[j])
            xp, sfin = _hgrn_layer(xp, n_prompt_seq, seq, cond_prompt, *common, None, sfin, j)
            xs, _ = _hgrn_layer(xs, dec_batch, dec_seq, cond_sample, *common, state_hgrn[:, j], None, j)
        else:
            common = (norm_g[i, 0], modr, w_qkv[j], attn_qn_g[j], attn_kn_g[j], attn_lambda[j], attn_subln_g[j],
                      w_aout[j], i)
            xp, kv = _attn_layer(xp, n_prompt_seq, seq, cond_prompt, *common, None, None, kv, j)
            xs, _ = _attn_layer(xs, dec_batch, dec_seq, cond_sample, *common, cache_k[:, j], cache_v[:, j], None, j)
        xp, xs = _moe_layer(xp, xs, norm_g[i, 1], modr, wr_t[i], br[i], w_gate, w_up, w_down, i, cond_tile)

    new_k = kv[0].reshape(n_prompt_seq, DEPTH // 2, seq, HEADS, 2, QK_DIM)
    new_v = kv[1].reshape(n_prompt_seq, DEPTH // 2, seq, HEADS, HEAD_DIM)
    return (xp.reshape(n_prompt_seq, seq, d), xs.reshape(dec_batch, dec_seq, d), new_k, new_v, sfin)
```

```python
import functools
import math

import numpy as np
import jax
import jax.numpy as jnp
from jax import lax
from jax.experimental import pallas as pl
from jax.experimental.pallas import tpu as pltpu

F32 = jnp.float32
BF16 = jnp.bfloat16

D_MODEL = 1024
DEPTH = 4
GRID_W = 64
HEADS = 8
HEAD_DIM = 128
QK_DIM = 64
ROPE_THETA = 10000.0
MOE_GROUPS = 4
MOE_EPG = 4
MOE_EXPERTS = MOE_GROUPS * MOE_EPG
MOE_D_FF = 512
EPS = 1e-6
N_COND = 8
N_MOD = 6
HGRN_PARTS = 5

LANES = 128
BLK = 256
CHUNK = 32
N_CHUNK = BLK // CHUNK
HEADS_PER_STEP = 2
SCAN_HEADS_PER_BODY = 4
ATTN_HEADS_PER_BODY = 8
SUPER = 12
SUPER_ROWS = SUPER * BLK
STRAIGHT_TILES = 8
TBL_RSTART, TBL_SGROUP, TBL_SNT, TBL_SBLK, TBL_CODE, TBL_CLO, TBL_CHI, TBL_OBLK, TBL_UT = range(9)
CODE_SKIP, CODE_GATHER, CODE_ZERO = 0, 1, 2
EXPERT_TILES = 2
FF_PART = 256
GATHER_WIN = 6
EXP2_CLAMP = 115.0
VMEM_LIMIT = 56 * 1024 * 1024
N_SLOTS = DEPTH // 2
_RESIDENT = dict(pipeline_mode=pl.Buffered(1))


def _cparams(sem):
    return pltpu.CompilerParams(dimension_semantics=sem, vmem_limit_bytes=VMEM_LIMIT)


def _silu(x):
    return x * jax.nn.sigmoid(x)


def _dot(a, b):
    return jnp.dot(a, b, preferred_element_type=F32)


def _dot_nt(a, b):
    return lax.dot_general(a, b, (((1,), (1,)), ((), ())), preferred_element_type=F32)


def _dot_tn(a, b):
    return lax.dot_general(a, b, (((0,), (0,)), ((), ())), preferred_element_type=F32)


def _lane_block(i, width):
    return pl.ds(pl.multiple_of(i * width, width), width)


def _mod_kernel(c_ref, w_ref, b_ref, o_ref):
    o_ref[...] = _dot(_silu(c_ref[...]), w_ref[...]) + b_ref[...]


def _modulation(cond, w_ada, b_ada):
    tn = 1536
    nj = (N_MOD * D_MODEL) // tn
    return pl.pallas_call(
        _mod_kernel,
        grid=(DEPTH, nj),
        in_specs=[
            pl.BlockSpec((N_COND, D_MODEL), lambda l, j: (0, 0)),
            pl.BlockSpec((None, D_MODEL, tn), lambda l, j: (l, 0, j)),
            pl.BlockSpec((None, 1, tn), lambda l, j: (l, 0, j)),
        ],
        out_specs=pl.BlockSpec((None, N_COND, tn), lambda l, j: (l, 0, j)),
        out_shape=jax.ShapeDtypeStruct((DEPTH, N_COND, N_MOD * D_MODEL), F32),
        compiler_params=_cparams(("parallel", "parallel")),
        name="modulation",
    )(cond, w_ada, b_ada.reshape(DEPTH, 1, N_MOD * D_MODEL))


def _norm_mod(x, g, sc, sh):
    ms = jnp.mean(x * x, axis=-1, keepdims=True)
    return (x * lax.rsqrt(ms + EPS) * g) * (1.0 + sc) + sh


def _mod_spec(cond_of_step, which):
    return pl.BlockSpec((None, 1, D_MODEL), lambda i, *_: (cond_of_step(i) * N_MOD + which, 0, 0))


def _scan_constants():
    t = np.arange(BLK)
    out = []
    for rev in (False, True):
        u = (BLK - 1 - t) if rev else t
        ut, us = u[:, None], u[None, :]
        cums = (us <= ut).astype(np.float32)
        lev = np.where(us > ut, 0,
              np.where(ut // CHUNK == us // CHUNK, 1,
              np.where(ut // 64 == us // 64, 2,
              np.where(ut // 128 == us // 128, 3, 4)))).astype(np.int32)
        out += [jnp.asarray(cums, BF16), jnp.asarray(lev)]
    return out


def _rows_to_block(rows, rev):
    order = rows[::-1] if rev else rows
    return jnp.concatenate([jnp.broadcast_to(r, (CHUNK, LANES)) for r in order], axis=0)


def _scan_prep(q, z, lb, cums, rev, with_inter):
    sig = jax.nn.sigmoid(z)
    f = lb + (1.0 - lb) * sig
    logf = jnp.log2(f)
    k = (1.0 - lb) * (1.0 - sig)
    hi32 = lax.bitcast_convert_type(lax.bitcast_convert_type(logf, jnp.int32) & jnp.int32(-65536), F32)
    hi = hi32.astype(BF16)
    lo = (logf - hi32).astype(BF16)
    bb = _dot(cums, jnp.concatenate([hi, lo], axis=1))
    b = bb[:, :LANES] + bb[:, LANES:]

    e_row, m_row = (0, CHUNK // 2) if rev else (CHUNK - 1, CHUNK // 2 - 1)
    ends, mids = [], []
    for j in range(N_CHUNK):
        ends.append(b[j * CHUNK + e_row:j * CHUNK + e_row + 1, :])
        mids.append(b[j * CHUNK + m_row:j * CHUNK + m_row + 1, :])
    if rev:
        ends, mids = ends[::-1], mids[::-1]
    zero = jnp.zeros((1, LANES), F32)
    one = jnp.ones((1, LANES), F32)
    pres = [zero] + ends[:-1]
    b_pre = _rows_to_block(pres, rev)
    b_end = _rows_to_block(ends, rev)
    b_mid = _rows_to_block(mids, rev)

    qd = q * jnp.exp2(b - b_pre)
    ku = k * jnp.exp2(b_end - b)
    qm = q * jnp.exp2(jnp.clip(b - b_mid, -EXP2_CLAMP, EXP2_CLAMP))
    km = k * jnp.exp2(jnp.clip(b_mid - b, -EXP2_CLAMP, EXP2_CLAMP))

    levels = []
    for nc in (2, 4, 8):
        fq, fk = [], []
        for ju in range(N_CHUNK):
            r = (ju // nc) * nc + nc // 2 - 1
            if ju % nc >= nc // 2:
                fq.append(one if nc == 2 else jnp.exp2(pres[ju] - ends[r]))
                fk.append(zero)
            else:
                fq.append(zero)
                fk.append(one if nc == 2 else jnp.exp2(ends[r] - ends[ju]))
        levels.append(((qd * _rows_to_block(fq, rev)).astype(BF16), (ku * _rows_to_block(fk, rev)).astype(BF16)))

    last = ends[-1]
    qh = (qd * _rows_to_block([jnp.exp2(p) for p in pres], rev)).astype(BF16) if with_inter else None
    kh = (ku * _rows_to_block([jnp.exp2(last - e) for e in ends], rev)).astype(BF16)
    return (qm.astype(BF16), km.astype(BF16)), levels, qh, kh, last


def _scan_att(prep, lev):
    att = jnp.where(lev == 1, _dot_nt(*prep[0]), 0.0)
    for level, (ql, kl) in enumerate(prep[1], start=2):
        att = jnp.where(lev == level, _dot_nt(ql, kl), att)
    return att


def _scan_att_bidir(prep_f, prep_b, lev_f, lev_b):
    att = jnp.where(lev_f == 1, _dot_nt(*prep_f[0]), 0.0) + jnp.where(lev_b == 1, _dot_nt(*prep_b[0]), 0.0)
    lev = jnp.maximum(lev_f, lev_b)
    for level, ((qf, kf), (qb, kb)) in enumerate(zip(prep_f[1], prep_b[1]), start=2):
        both = _dot_nt(jnp.concatenate([qf, qb], axis=1), jnp.concatenate([kf, kb], axis=1))
        att = jnp.where(lev == level, both, att)
    return att


def _scan_state(prep, v, st_prev):
    ut = _dot_tn(v.astype(BF16), prep[3])
    return ut if st_prev is None else st_prev * jnp.exp2(prep[4]) + ut


def _scan_finish(o, g, on):
    ms = jnp.mean(o * o, axis=-1, keepdims=True)
    return ((o * lax.rsqrt(ms + EPS) * on) * _silu(g)).astype(BF16)


def _slot_view(ref, slot, owns_all_slots):
    if not owns_all_slots:
        return ref
    for s in range(ref.shape[0]):
        if s != slot:
            ref[s] = jnp.zeros(ref.shape[1:], ref.dtype)
    return ref.at[slot]


def _hgrn_kernel(*refs, n_blocks, has_state, slot, owns_all_slots, hps):
    it = iter(refs)
    x_ref, g_ref, sh_ref, sc_ref, gate_ref, win_ref, lb_ref, on_ref = (next(it) for _ in range(8))
    cf_ref, lf_ref, cb_ref, lvb_ref = (next(it) for _ in range(4))
    s0_ref = next(it) if has_state else None
    wout_ref, xo_ref = next(it), next(it)
    sfin_ref = None if has_state else next(it)
    h_ref, proj_ref, og_ref = next(it), next(it), next(it)
    oacc_ref, st_ref = (next(it), next(it)) if has_state else (None, None)

    h_ref[...] = _norm_mod(x_ref[...], g_ref[...], sc_ref[...], sh_ref[...]).astype(BF16)
    on = on_ref[...]
    if not has_state:
        sfin_ref = _slot_view(sfin_ref, slot, owns_all_slots)

    group_w = hps * HEAD_DIM

    def part(i, p, rows=slice(None)):
        return proj_ref[rows, p * group_w + i * HEAD_DIM:p * group_w + (i + 1) * HEAD_DIM]

    def pair(hp, carry):
        for p in range(HGRN_PARTS):
            cols = pl.ds(pl.multiple_of(p * D_MODEL + hp * group_w, group_w), group_w)
            proj_ref[:, p * group_w:(p + 1) * group_w] = _dot(h_ref[...], win_ref[:, cols])
        heads = [hp * hps + i for i in range(hps)]
        lbs = [lb_ref[:, _lane_block(hd, HEAD_DIM)] for hd in heads]
        if not has_state:
            for i, hd in enumerate(heads):
                q, v = part(i, 0), part(i, 1)
                prep_f = _scan_prep(q, part(i, 2), lbs[i][0:1, :], cf_ref[...], False, False)
                prep_b = _scan_prep(q, part(i, 3), lbs[i][1:2, :], cb_ref[...], True, False)
                att = _scan_att_bidir(prep_f, prep_b, lf_ref[...], lvb_ref[...])
                o = _dot(att.astype(BF16), v.astype(BF16))
                sfin_ref[0, hd] = _scan_state(prep_f, v, None).T
                sfin_ref[1, hd] = _scan_state(prep_b, v, None).T
                og_ref[:, _lane_block(hd, HEAD_DIM)] = _scan_finish(o, part(i, 4), on)
        else:
            for i, hd in enumerate(heads):
                st_ref[2 * i] = s0_ref[0, hd].T
                st_ref[2 * i + 1] = s0_ref[1, hd].T

            def body(tb, c2):
                rf = pl.ds(pl.multiple_of(tb * BLK, BLK), BLK)
                rb = pl.ds(pl.multiple_of((n_blocks - 1 - tb) * BLK, BLK), BLK)
                for i in range(hps):
                    cols = slice(i * HEAD_DIM, (i + 1) * HEAD_DIM)
                    v_f, v_b = part(i, 1, rf), part(i, 1, rb)
                    prep_f = _scan_prep(part(i, 0, rf), part(i, 2, rf), lbs[i][0:1, :], cf_ref[...], False, True)
                    prep_b = _scan_prep(part(i, 0, rb), part(i, 3, rb), lbs[i][1:2, :], cb_ref[...], True, True)
                    st_f, st_b = st_ref[2 * i], st_ref[2 * i + 1]
                    of = (_dot(_scan_att(prep_f, lf_ref[...]).astype(BF16), v_f.astype(BF16))
                          + _dot_nt(prep_f[2], st_f.astype(BF16)))
                    ob = (_dot(_scan_att(prep_b, lvb_ref[...]).astype(BF16), v_b.astype(BF16))
                          + _dot_nt(prep_b[2], st_b.astype(BF16)))
                    st_ref[2 * i] = _scan_state(prep_f, v_f, st_f)
                    st_ref[2 * i + 1] = _scan_state(prep_b, v_b, st_b)

                    @pl.when(2 * tb < n_blocks)
                    def _():
                        oacc_ref[rf, cols] = of
                        oacc_ref[rb, cols] = ob

                    @pl.when(2 * tb >= n_blocks)
                    def _():
                        oacc_ref[rf, cols] += of
                        oacc_ref[rb, cols] += ob

                return c2

            lax.fori_loop(0, n_blocks, body, 0)
            for i, hd in enumerate(heads):
                cols = slice(i * HEAD_DIM, (i + 1) * HEAD_DIM)
                og_ref[:, _lane_block(hd, HEAD_DIM)] = _scan_finish(oacc_ref[:, cols], part(i, 4), on)
        return carry

    lax.fori_loop(0, HEADS // hps, pair, 0)
    xo_ref[...] = x_ref[...] + gate_ref[...] * _dot(og_ref[...], wout_ref[...])


def _hgrn_layer(x, n_seq, t, cond_of_seq, g, modr, w_in, lbs_j, onorm_g, w_out, s0, sfin_prev, slot):
    d = D_MODEL
    has_state = s0 is not None
    n_blocks = t // BLK
    assert n_blocks == 1 or n_blocks % 2 == 0
    full = lambda shape, **kw: pl.BlockSpec(shape, lambda b, *_: (0,) * len(shape), **kw)
    in_specs = [
        pl.BlockSpec((t, d), lambda b: (b, 0)),
        full((1, d)),
        _mod_spec(cond_of_seq, 0), _mod_spec(cond_of_seq, 1), _mod_spec(cond_of_seq, 2),
        full((d, HGRN_PARTS * d), **_RESIDENT),
        full((2, d)),
        full((1, HEAD_DIM)),
    ] + [full((BLK, BLK))] * 4
    args = [x, g.reshape(1, d), modr, modr, modr, w_in, lbs_j, onorm_g.reshape(1, HEAD_DIM)] + _scan_constants()
    state_block = (None, 2, HEADS, HEAD_DIM, HEAD_DIM)
    if has_state:
        in_specs.append(pl.BlockSpec(state_block, lambda b: (b, 0, 0, 0, 0)))
        args.append(s0)
    in_specs.append(full((d, d), **_RESIDENT))
    args.append(w_out)
    out_specs = [pl.BlockSpec((t, d), lambda b: (b, 0))]
    out_shape = [jax.ShapeDtypeStruct((n_seq * t, d), F32)]
    hps = HEADS_PER_STEP if has_state else SCAN_HEADS_PER_BODY
    group_w = hps * HEAD_DIM
    scratch = [pltpu.VMEM((t, d), BF16), pltpu.VMEM((t, HGRN_PARTS * group_w), F32), pltpu.VMEM((t, d), BF16)]
    aliases = {}
    if has_state:
        scratch += [pltpu.VMEM((t, group_w), F32), pltpu.VMEM((2 * hps, HEAD_DIM, HEAD_DIM), F32)]
    else:
        state_dims = (2, HEADS, HEAD_DIM, HEAD_DIM)
        if sfin_prev is None:
            out_specs.append(pl.BlockSpec((None, N_SLOTS) + state_dims, lambda b: (b, 0, 0, 0, 0, 0)))
        else:
            out_specs.append(pl.BlockSpec((None, None) + state_dims, lambda b: (b, slot, 0, 0, 0, 0)))
            in_specs.append(pl.BlockSpec(memory_space=pl.ANY))
            args.append(sfin_prev)
            aliases = {len(args) - 1: 1}
        out_shape.append(jax.ShapeDtypeStruct((n_seq, N_SLOTS) + state_dims, F32))

    def body(*refs):
        if sfin_prev is not None:
            n_in = len(args)
            refs = refs[:n_in - 1] + refs[n_in:]
        _hgrn_kernel(*refs, n_blocks=n_blocks, has_state=has_state, slot=slot, owns_all_slots=sfin_prev is None,
                     hps=hps)

    out = pl.pallas_call(
        body,
        grid=(n_seq,),
        in_specs=in_specs,
        out_specs=out_specs,
        out_shape=out_shape,
        scratch_shapes=scratch,
        input_output_aliases=aliases,
        compiler_params=_cparams(("parallel",)),
        name="hgrn_layer_sample" if has_state else "hgrn_layer_prompt",
    )(*args)
    return out if not has_state else (out[0], None)


def _rope_tables(t_lat):
    rows = t_lat // GRID_W
    row = jnp.repeat(jnp.arange(rows), GRID_W).astype(F32)
    col = jnp.tile(jnp.arange(GRID_W), rows).astype(F32)
    half = QK_DIM // 2
    inv_freq = ROPE_THETA ** (-jnp.arange(0, half, 2, dtype=F32) / half)
    ang_row = row[:, None] * inv_freq
    ang_col = col[:, None] * inv_freq

    def part(ang):
        c, s = jnp.cos(ang), jnp.sin(ang)
        return jnp.concatenate([c, c], axis=1), jnp.concatenate([-s, s], axis=1)

    cr, sr = part(ang_row)
    cc, sc = part(ang_col)
    cos64 = jnp.concatenate([cr, cc], axis=1)
    sin64 = jnp.concatenate([sr, sc], axis=1)
    return jnp.concatenate([cos64, cos64], axis=1), jnp.concatenate([sin64, sin64], axis=1)


def _rope(x, cos, sin):
    lane = lax.broadcasted_iota(jnp.int32, x.shape, 1)
    first = (lane % (QK_DIM // 2)) < (QK_DIM // 4)
    swapped = jnp.where(first, pltpu.roll(x, LANES - QK_DIM // 4, 1), pltpu.roll(x, QK_DIM // 4, 1))
    return x * cos + swapped * sin


def _attn_kernel(*refs, n_qblk, rope, cache, emit_kv, lam_init, slot, owns_all_slots):
    it = iter(refs)
    x_ref, g_ref, sh_ref, sc_ref, gate_ref, wqkv_ref = (next(it) for _ in range(6))
    qg_ref, kg_ref, lam_ref, sg_ref, seg_ref = (next(it) for _ in range(5))
    cos_ref = sin_ref = ck_ref = cv_ref = nk_ref = nv_ref = None
    if rope:
        cos_ref, sin_ref = next(it), next(it)
    if cache:
        ck_ref, cv_ref = next(it), next(it)
    wout_ref, xo_ref = next(it), next(it)
    if emit_kv:
        nk_ref = _slot_view(next(it), slot, owns_all_slots)
        nv_ref = _slot_view(next(it), slot, owns_all_slots)
    h_ref, qkv_ref, oa_ref = next(it), next(it), next(it)

    d = D_MODEL
    h_ref[...] = _norm_mod(x_ref[...], g_ref[...], sc_ref[...], sh_ref[...]).astype(BF16)
    qkv_ref[...] = _dot(h_ref[...], wqkv_ref[...])
    seg = seg_ref[...]
    lp = lam_ref[...]
    lam = (jnp.exp(jnp.sum(lp[0:1, :] * lp[1:2, :], axis=-1, keepdims=True))
           - jnp.exp(jnp.sum(lp[2:3, :] * lp[3:4, :], axis=-1, keepdims=True)) + lam_init)
    lane = lax.broadcasted_iota(jnp.int32, (1, LANES), 1)
    comp0 = lane < QK_DIM
    scale = QK_DIM ** -0.5

    def split(a):
        return [jnp.where(comp0, a, 0.0).astype(BF16), jnp.where(comp0, 0.0, a).astype(BF16)]

    def one_head(hd):
        hcol = _lane_block(hd, HEAD_DIM)
        k = qkv_ref[:, pl.ds(pl.multiple_of(d + hd * HEAD_DIM, HEAD_DIM), HEAD_DIM)]
        v = qkv_ref[:, pl.ds(pl.multiple_of(2 * d + hd * HEAD_DIM, HEAD_DIM), HEAD_DIM)]
        kn = k * lax.rsqrt(_dot(k * k, seg) + EPS) * kg_ref[...]
        if emit_kv:
            nk_ref[:, hcol] = kn
            nv_ref[:, hcol] = v
        if rope:
            kn = _rope(kn, cos_ref[...], sin_ref[...])
        ks = split(kn)
        vb = v.astype(BF16)
        if cache:
            cks = split(ck_ref[:, hcol])
            cvb = cv_ref[:, hcol].astype(BF16)
        for qi in range(n_qblk):
            rows = slice(qi * BLK, (qi + 1) * BLK)
            q = qkv_ref[rows, hcol]
            qn = q * lax.rsqrt(_dot(q * q, seg) + EPS) * qg_ref[...]
            if rope:
                qn = _rope(qn, cos_ref[rows, :], sin_ref[rows, :])
            qb = (qn * scale).astype(BF16)
            a_self, a_cache = None, None
            for c in range(2):
                s = _dot_nt(qb, ks[c])
                m = jnp.max(s, axis=-1, keepdims=True)
                if cache:
                    sc = _dot_nt(qb, cks[c])
                    m = jnp.maximum(m, jnp.max(sc, axis=-1, keepdims=True))
                p = jnp.exp(s - m)
                den = jnp.sum(p, axis=-1, keepdims=True)
                if cache:
                    pc = jnp.exp(sc - m)
                    den = den + jnp.sum(pc, axis=-1, keepdims=True)
                w = (1.0 / den) if c == 0 else (-lam / den)
                a_self = p * w if c == 0 else a_self + p * w
                if cache:
                    a_cache = pc * w if c == 0 else a_cache + pc * w
            o = _dot(a_self.astype(BF16), vb)
            if cache:
                o = o + _dot(a_cache.astype(BF16), cvb)
            ms = jnp.mean(o * o, axis=-1, keepdims=True)
            o = (o * lax.rsqrt(ms + EPS) * sg_ref[...]) * (1.0 - lam_init)
            oa_ref[rows, hcol] = o.astype(BF16)

    heads_per_body = ATTN_HEADS_PER_BODY if n_qblk == 1 else HEADS_PER_STEP

    def group(hg, carry):
        for i in range(heads_per_body):
            one_head(hg * heads_per_body + i)
        return carry

    lax.fori_loop(0, HEADS // heads_per_body, group, 0)
    xo_ref[...] = x_ref[...] + gate_ref[...] * _dot(oa_ref[...], wout_ref[...])


def _attn_layer(x, n_seq, t, cond_of_seq, g, modr, w_qkv, qn_g, kn_g, lam_p, subln_g, w_out, layer_idx,
                cache_k_j, cache_v_j, kv_prev, slot):
    d = D_MODEL
    cache = cache_k_j is not None
    lam_init = 0.8 - 0.6 * math.exp(-0.3 * layer_idx)
    qg = jnp.tile(qn_g.reshape(1, QK_DIM), (1, 2))
    kg = jnp.tile(kn_g.reshape(1, QK_DIM), (1, 2))
    li = np.arange(LANES)
    seg = jnp.asarray((li[:, None] // QK_DIM == li[None, :] // QK_DIM).astype(np.float32) / QK_DIM)
    full = lambda shape, **kw: pl.BlockSpec(shape, lambda b, *_: (0,) * len(shape), **kw)
    in_specs = [
        pl.BlockSpec((t, d), lambda b: (b, 0)),
        full((1, d)),
        _mod_spec(cond_of_seq, 0), _mod_spec(cond_of_seq, 1), _mod_spec(cond_of_seq, 2),
        full((d, 3 * d), **_RESIDENT),
        full((1, LANES)), full((1, LANES)), full((4, QK_DIM)), full((1, HEAD_DIM)), full((LANES, LANES)),
    ]
    args = [x, g.reshape(1, d), modr, modr, modr, w_qkv, qg, kg, lam_p, subln_g.reshape(1, HEAD_DIM), seg]
    if cache:
        cos, sin = _rope_tables(t)
        past = cache_k_j.shape[1]
        in_specs += [full((t, LANES)), full((t, LANES)),
                     pl.BlockSpec((None, past, d), lambda b: (b, 0, 0)),
                     pl.BlockSpec((None, past, d), lambda b: (b, 0, 0))]
        args += [cos, sin, cache_k_j.reshape(n_seq, past, d), cache_v_j.reshape(n_seq, past, d)]
    in_specs.append(full((d, d), **_RESIDENT))
    args.append(w_out)
    out_specs = [pl.BlockSpec((t, d), lambda b: (b, 0))]
    out_shape = [jax.ShapeDtypeStruct((n_seq * t, d), F32)]
    aliases = {}
    n_carried = 0
    if not cache:
        if kv_prev is None:
            kv_spec = pl.BlockSpec((None, N_SLOTS, t, d), lambda b: (b, 0, 0, 0))
        else:
            kv_spec = pl.BlockSpec((None, None, t, d), lambda b: (b, slot, 0, 0))
            in_specs += [pl.BlockSpec(memory_space=pl.ANY)] * 2
            args += list(kv_prev)
            aliases = {len(args) - 2: 1, len(args) - 1: 2}
            n_carried = 2
        out_specs += [kv_spec, kv_spec]
        out_shape += [jax.ShapeDtypeStruct((n_seq, N_SLOTS, t, d), F32)] * 2

    def body(*refs):
        n_in = len(args)
        refs = refs[:n_in - n_carried] + refs[n_in:]
        _attn_kernel(*refs, n_qblk=t // BLK, rope=cache, cache=cache, emit_kv=not cache, lam_init=lam_init,
                     slot=slot, owns_all_slots=kv_prev is None)

    out = pl.pallas_call(
        body,
        grid=(n_seq,),
        in_specs=in_specs,
        out_specs=out_specs,
        out_shape=out_shape,
        scratch_shapes=[pltpu.VMEM((t, d), BF16), pltpu.VMEM((t, 3 * d), F32), pltpu.VMEM((t, d), BF16)],
        input_output_aliases=aliases,
        compiler_params=_cparams(("parallel",)),
        name="attn_layer_sample" if cache else "attn_layer_prompt",
    )(*args)
    return (out[0], None) if cache else (out[0], (out[1], out[2]))


ROUTE_ROWS = 32
INFO_GID, INFO_RANK = 8, 9


def _two_stream_specs(n_prompt_tiles, width):
    return [pl.BlockSpec((BLK, width), lambda i, *_: (jnp.minimum(i, n_prompt_tiles - 1), 0)),
            pl.BlockSpec((BLK, width), lambda i, *_: (jnp.maximum(i - n_prompt_tiles, 0), 0))]


def _route_kernel(xp_ref, xs_ref, g_ref, sh_ref, sc_ref, wr_ref, br_ref, tri_ref,
                  h_ref, il_ref, it_ref, cum_ref, cumhi_ref, tot_ref, carry_ref, *, n_prompt_tiles):
    i = pl.program_id(0)

    @pl.when(i == 0)
    def _():
        carry_ref[...] = jnp.zeros_like(carry_ref)

    @pl.when(i < n_prompt_tiles)
    def _():
        h_ref[...] = _norm_mod(xp_ref[...], g_ref[...], sc_ref[...], sh_ref[...]).astype(BF16)

    @pl.when(i >= n_prompt_tiles)
    def _():
        h_ref[...] = _norm_mod(xs_ref[...], g_ref[...], sc_ref[...], sh_ref[...]).astype(BF16)

    logit = _dot_nt(wr_ref[...], h_ref[...]) + br_ref[...]
    gl = [logit[g:g + 1, :] for g in range(MOE_GROUPS)]
    gmax = functools.reduce(jnp.maximum, gl)
    gz = functools.reduce(lambda a, b: a + b, [jnp.exp(x - gmax) for x in gl])
    g_w = 1.0 / gz
    gid = jnp.full_like(gmax, MOE_GROUPS - 1)
    for g in range(MOE_GROUPS - 2, -1, -1):
        gid = jnp.where(gl[g] == gmax, float(g), gid)
    el = []
    for j in range(MOE_EPG):
        e = logit[MOE_GROUPS + j:MOE_GROUPS + j + 1, :]
        for g in range(1, MOE_GROUPS):
            r = MOE_GROUPS + g * MOE_EPG + j
            e = jnp.where(gid == float(g), logit[r:r + 1, :], e)
        el.append(e)
    emax = functools.reduce(jnp.maximum, el)
    pe = [jnp.exp(e - emax) for e in el]
    idx1 = jnp.full_like(emax, MOE_EPG - 1)
    for j in range(MOE_EPG - 2, -1, -1):
        idx1 = jnp.where(el[j] == emax, float(j), idx1)
    el2 = [jnp.where(idx1 == float(j), -jnp.inf, el[j]) for j in range(MOE_EPG)]
    emax2 = functools.reduce(jnp.maximum, el2)
    idx2 = jnp.full_like(emax, MOE_EPG - 1)
    for j in range(MOE_EPG - 2, -1, -1):
        idx2 = jnp.where(el2[j] == emax2, float(j), idx2)
    sel = [(idx1 == float(j)) | (idx2 == float(j)) for j in range(MOE_EPG)]
    den = functools.reduce(lambda a, b: a + b, [jnp.where(sel[j], pe[j], 0.0) for j in range(MOE_EPG)])
    cw = [jnp.where(sel[j], pe[j] * (g_w / den), 0.0) for j in range(MOE_EPG)]

    row8 = lax.broadcasted_iota(jnp.int32, (8, BLK), 0)
    onehot = jnp.where(row8.astype(F32) == gid, 1.0, 0.0)
    within = _dot(onehot.astype(BF16), tri_ref[...])
    carry = carry_ref[...]
    rank = jnp.sum(onehot * (within + carry[:, 0:1]), axis=0, keepdims=True)
    cum_ref[...] = carry
    new_carry = carry + jnp.sum(onehot, axis=1, keepdims=True)
    carry_ref[...] = new_carry
    cumhi_ref[...] = new_carry
    tot_ref[...] = new_carry

    il_ref[...] = jnp.where(row8 == 0, gid, jnp.where(row8 == 1, rank, 0.0))
    rowl = lax.broadcasted_iota(jnp.int32, (LANES, BLK), 0)
    m = jnp.zeros((LANES, BLK), F32)
    for j in range(MOE_EPG):
        hi = cw[j].astype(BF16).astype(F32)
        m = jnp.where(rowl == j, hi, m)
        m = jnp.where(rowl == MOE_EPG + j, cw[j] - hi, m)
    m = jnp.where(rowl == INFO_GID, gid, m)
    m = jnp.where(rowl == INFO_RANK, rank, m)
    it_ref[...] = m.T


def _moe_route(xp, xs, g, modr, wr_t, br, cond_of_tile):
    d = D_MODEL
    npt = xp.shape[0] // BLK
    nt = npt + xs.shape[0] // BLK
    n = nt * BLK
    tri = jnp.asarray(np.triu(np.ones((BLK, BLK), np.float32), 1), BF16)
    full = lambda shape: pl.BlockSpec(shape, lambda i: (0,) * len(shape))
    return pl.pallas_call(
        functools.partial(_route_kernel, n_prompt_tiles=npt),
        grid=(nt,),
        in_specs=_two_stream_specs(npt, d) + [
            full((1, d)),
            _mod_spec(cond_of_tile, 3), _mod_spec(cond_of_tile, 4),
            full((ROUTE_ROWS, d)), full((ROUTE_ROWS, 1)), full((BLK, BLK)),
        ],
        out_specs=[
            pl.BlockSpec((BLK, d), lambda i: (i, 0)),
            pl.BlockSpec((8, BLK), lambda i: (0, i)),
            pl.BlockSpec((BLK, LANES), lambda i: (i, 0)),
            pl.BlockSpec((None, 8, LANES), lambda i: (i, 0, 0)),
            pl.BlockSpec((None, 8, LANES), lambda i: (i, 0, 0)),
            pl.BlockSpec((8, LANES), lambda i: (0, 0)),
        ],
        out_shape=[
            jax.ShapeDtypeStruct((n, d), BF16),
            jax.ShapeDtypeStruct((8, n), F32),
            jax.ShapeDtypeStruct((n, LANES), F32),
            jax.ShapeDtypeStruct((nt, 8, LANES), F32),
            jax.ShapeDtypeStruct((nt, 8, LANES), F32),
            jax.ShapeDtypeStruct((8, LANES), F32),
        ],
        scratch_shapes=[pltpu.VMEM((8, LANES), F32)],
        compiler_params=_cparams(("arbitrary",)),
        name="moe_route",
    )(xp, xs, g.reshape(1, d), modr, modr, wr_t, br, tri)


def _sorted_pos(gid, rank, rstart_ref):
    p = rank
    for g in range(MOE_GROUPS):
        p = p + jnp.where(gid == float(g), rstart_ref[g].astype(F32), 0.0)
    return p


def _gather_kernel(tbl_ref, h_ref, il_ref, it_ref, hs_ref, cws_ref, *, n_tiles):
    a = pl.program_id(0)
    win = GATHER_WIN * BLK
    code = tbl_ref[TBL_CODE * LANES + a]
    rstart_ref = tbl_ref

    @pl.when(code == CODE_ZERO)
    def _():
        hs_ref[...] = jnp.zeros_like(hs_ref)
        cws_ref[...] = jnp.zeros_like(cws_ref)

    @pl.when(code == CODE_GATHER)
    def _():
        dest = (lax.broadcasted_iota(jnp.int32, (BLK, 1), 0) + a * BLK).astype(F32)
        src_tile = lax.broadcasted_iota(jnp.int32, (1, win), 1) // BLK
        clo = tbl_ref[TBL_CLO * LANES + a]
        n_win = (tbl_ref[TBL_CHI * LANES + a] - clo + GATHER_WIN) // GATHER_WIN

        def window(w):
            first = clo + w * GATHER_WIN
            c0 = jnp.minimum(first, n_tiles - GATHER_WIN)
            rows = pl.ds(pl.multiple_of(c0 * BLK, BLK), win)
            info = il_ref[:, rows]
            p = _sorted_pos(info[0:1, :], info[1:2, :], rstart_ref)
            p = jnp.where(src_tile + c0 >= first, p, -1.0)
            onehot = jnp.where(dest == p, 1.0, 0.0).astype(BF16)
            r = _dot(onehot, it_ref[rows, :].astype(BF16))
            return _dot(onehot, h_ref[rows, :]), r + pltpu.roll(r, LANES - MOE_EPG, 1)

        dh, dc = window(0)
        hs_ref[...] = dh.astype(BF16)
        cws_ref[...] = dc

        def body(w, carry):
            dh, dc = window(w)
            hs_ref[...] = (hs_ref[...].astype(F32) + dh).astype(BF16)
            cws_ref[...] += dc
            return carry

        lax.fori_loop(1, n_win, body, 0)


def _moe_gather(h, info_lane, info_tok, tbl, n_dest_tiles):
    n, d = h.shape
    nt = n // BLK
    assert nt >= GATHER_WIN and n_dest_tiles <= LANES
    out_block = lambda a, tbl_r: (tbl_r[TBL_OBLK * LANES + a], 0)
    return pl.pallas_call(
        functools.partial(_gather_kernel, n_tiles=nt),
        grid_spec=pltpu.PrefetchScalarGridSpec(
            num_scalar_prefetch=1,
            grid=(n_dest_tiles,),
            in_specs=[
                pl.BlockSpec((n, d), lambda a, *_: (0, 0), **_RESIDENT),
                pl.BlockSpec((8, n), lambda a, *_: (0, 0), **_RESIDENT),
                pl.BlockSpec((n, LANES), lambda a, *_: (0, 0), **_RESIDENT),
            ],
            out_specs=[pl.BlockSpec((BLK, d), out_block), pl.BlockSpec((BLK, LANES), out_block)],
        ),
        out_shape=[
            jax.ShapeDtypeStruct((n_dest_tiles * BLK, d), BF16),
            jax.ShapeDtypeStruct((n_dest_tiles * BLK, LANES), F32),
        ],
        compiler_params=_cparams(("arbitrary",)),
        name="moe_gather",
    )(tbl, h, info_lane, info_tok)


def _moe_mlp_kernel(tbl_ref, hs_ref, cws_ref, wg_ref, wu_ref, wd_ref, ys_ref,
                    acc_ref, wgb_ref, wub_ref, wdb_ref):
    s = pl.program_id(0)
    k = pl.program_id(1)
    nt = tbl_ref[TBL_SNT * LANES + s]

    @pl.when(nt > 0)
    def _():
        wgb_ref[...] = wg_ref[...].astype(BF16)
        wub_ref[...] = wu_ref[...].astype(BF16)
        wdb_ref[...] = wd_ref[...].astype(BF16)

    @pl.when((s == 0) & (k == 0))
    def _():
        acc_ref[...] = jnp.zeros_like(acc_ref)

    def skipped(j):
        rows = slice(j * EXPERT_TILES * BLK, (j + 1) * EXPERT_TILES * BLK)
        ys_ref[rows, :] = jnp.zeros((EXPERT_TILES * BLK, D_MODEL), BF16)

    def block(j):
        rows = slice(j * EXPERT_TILES * BLK, (j + 1) * EXPERT_TILES * BLK)
        hsub = hs_ref[rows, :]
        cws = cws_ref[rows, :]
        cwk = jnp.zeros((EXPERT_TILES * BLK, 1), F32)
        for kk in range(MOE_EPG):
            cwk = jnp.where(k == kk, cws[:, kk:kk + 1], cwk)
        y = None
        for fh in range(MOE_D_FF // FF_PART):
            fc = slice(fh * FF_PART, (fh + 1) * FF_PART)
            gate = _dot(hsub, wgb_ref[:, fc])
            up = _dot(hsub, wub_ref[:, fc])
            act = ((_silu(gate) * up) * cwk).astype(BF16)
            part = _dot(act, wdb_ref[fc, :])
            y = part if y is None else y + part
        total = jnp.where(k == 0, 0.0, acc_ref[rows, :]) + y
        acc_ref[rows, :] = total
        ys_ref[rows, :] = total.astype(BF16)

    def maybe_block(j):
        pl.when(j * EXPERT_TILES < nt)(functools.partial(block, j))
        pl.when((j * EXPERT_TILES >= nt) & (nt > 0) & (k == 0))(functools.partial(skipped, j))

    n_blocks = SUPER // EXPERT_TILES
    n_straight = STRAIGHT_TILES // EXPERT_TILES

    @pl.when(nt >= STRAIGHT_TILES)
    def _():
        for j in range(n_straight):
            block(j)

    @pl.when(nt < STRAIGHT_TILES)
    def _():
        for j in range(n_straight):
            maybe_block(j)

    for j in range(n_straight, n_blocks):
        maybe_block(j)


def _moe_mlp(hs, cws, w_gate, w_up, w_down, layer, tbl, n_super):
    d = hs.shape[1]

    def widx(s, k, tbl_r):
        kk = jnp.where(tbl_r[TBL_SNT * LANES + s] > 0, k, MOE_EPG - 1)
        return (layer * MOE_EXPERTS + tbl_r[TBL_SGROUP * LANES + s] * MOE_EPG + kk, 0, 0)

    rows_idx = lambda s, k, tbl_r: (tbl_r[TBL_SBLK * LANES + s], 0)
    return pl.pallas_call(
        _moe_mlp_kernel,
        grid_spec=pltpu.PrefetchScalarGridSpec(
            num_scalar_prefetch=1,
            grid=(n_super, MOE_EPG),
            in_specs=[
                pl.BlockSpec((SUPER_ROWS, d), rows_idx),
                pl.BlockSpec((SUPER_ROWS, LANES), rows_idx),
                pl.BlockSpec((None, d, MOE_D_FF), widx),
                pl.BlockSpec((None, d, MOE_D_FF), widx),
                pl.BlockSpec((None, MOE_D_FF, d), widx),
            ],
            out_specs=pl.BlockSpec((SUPER_ROWS, d), rows_idx),
            scratch_shapes=[
                pltpu.VMEM((SUPER_ROWS, d), F32),
                pltpu.VMEM((d, MOE_D_FF), BF16),
                pltpu.VMEM((d, MOE_D_FF), BF16),
                pltpu.VMEM((MOE_D_FF, d), BF16),
            ],
        ),
        out_shape=jax.ShapeDtypeStruct((n_super * SUPER_ROWS, d), BF16),
        compiler_params=_cparams(("arbitrary", "arbitrary")),
        name="moe_experts",
    )(tbl, hs, cws, w_gate, w_up, w_down)


N_SRC = 2 * MOE_GROUPS


def _unsort_kernel(tbl_ref, xp_ref, xs_ref, it_ref, gate_ref, *rest, n_prompt_tiles):
    ys_refs, op_ref, os_ref = rest[:N_SRC], rest[N_SRC], rest[N_SRC + 1]
    t = pl.program_id(0)
    info = it_ref[...]
    p = _sorted_pos(info[:, INFO_GID:INFO_GID + 1], info[:, INFO_RANK:INFO_RANK + 1], tbl_ref)
    lane = lax.broadcasted_iota(jnp.int32, (1, BLK), 1).astype(F32)
    slot = lambda m: tbl_ref[(TBL_UT + t) * LANES + m]

    def take(m):
        a = slot(m)
        onehot = jnp.where(p - (a * BLK).astype(F32) == lane, 1.0, 0.0).astype(BF16)
        return _dot(onehot, ys_refs[m][...])

    def stream(x_ref, o_ref):
        first = functools.reduce(lambda u, w: u + w, [take(m) for m in range(0, N_SRC, 2)])
        o_ref[...] = x_ref[...] + gate_ref[...] * first
        for m in range(1, N_SRC, 2):
            @pl.when(slot(m) >= 0)
            def _():
                o_ref[...] += gate_ref[...] * take(m)

    pl.when(t < n_prompt_tiles)(functools.partial(stream, xp_ref, op_ref))
    pl.when(t >= n_prompt_tiles)(functools.partial(stream, xs_ref, os_ref))


def _moe_unsort(xp, xs, info_tok, modr, ys, tbl, cond_of_tile):
    d = D_MODEL
    npt = xp.shape[0] // BLK
    nt = npt + xs.shape[0] // BLK

    def ys_spec(m):
        return pl.BlockSpec((BLK, d), lambda t, tbl_r: (jnp.maximum(tbl_r[(TBL_UT + t) * LANES + m], 0), 0))

    return pl.pallas_call(
        functools.partial(_unsort_kernel, n_prompt_tiles=npt),
        grid_spec=pltpu.PrefetchScalarGridSpec(
            num_scalar_prefetch=1,
            grid=(nt,),
            in_specs=_two_stream_specs(npt, d) + [
                pl.BlockSpec((BLK, LANES), lambda t, *_: (t, 0)),
                _mod_spec(cond_of_tile, 5),
            ] + [ys_spec(m) for m in range(N_SRC)],
            out_specs=_two_stream_specs(npt, d),
        ),
        out_shape=[jax.ShapeDtypeStruct(xp.shape, F32), jax.ShapeDtypeStruct(xs.shape, F32)],
        compiler_params=_cparams(("arbitrary",)),
        name="moe_unsort_residual",
    )(tbl, xp, xs, info_tok, modr, *([ys] * N_SRC))


def _tables_kernel(lo_ref, hi_ref, tot_ref, tbl_ref, *, n_tiles, n_super):
    one = lambda cond: jnp.where(cond, 1.0, 0.0)
    groups = range(MOE_GROUPS)
    lane = lax.broadcasted_iota(jnp.int32, (1, LANES), 1).astype(F32)
    tot = [tot_ref[g:g + 1, :] for g in groups]
    pick = lambda vals, idx: sum(jnp.where(idx == float(g), vals[g], 0.0) for g in groups)

    n_sup = [sum(one(tot[g] > float(m * SUPER_ROWS)) for m in range(n_super)) for g in groups]
    sup_start, sup_end, run = [], [], 0.0
    for g in groups:
        sup_start.append(run + 0.0 * tot[g])
        run = run + n_sup[g]
        sup_end.append(run)
    n_used = sup_end[-1]
    rstart = [sup_start[g] * float(SUPER_ROWS) for g in groups]
    group_of = lambda s: jnp.minimum(sum(one(s >= sup_end[g]) for g in groups), float(MOE_GROUPS - 1))

    sg, snt, written = [], [], []
    for m in range(n_super):
        used = float(m) < n_used
        g_m = group_of(jnp.where(used, float(m), n_used - 1.0))
        rows_left = pick(tot, g_m) - (float(m) - pick(sup_start, g_m)) * float(SUPER_ROWS)
        nt_m = jnp.where(used, jnp.clip(jnp.floor((rows_left + float(BLK - 1)) * (1.0 / BLK)), 0.0, float(SUPER)), 0.0)
        sg.append(g_m)
        snt.append(nt_m)
        written.append(float(EXPERT_TILES) * jnp.floor((nt_m + float(EXPERT_TILES - 1)) * (1.0 / EXPERT_TILES)))
    sblk = [jnp.minimum(float(m), n_used - 1.0) for m in range(n_super)]
    by_super_lane = lambda vals: sum(jnp.where(lane == float(m), vals[m], 0.0) for m in range(n_super))
    last_written = sum(jnp.where(n_used - 1.0 == float(m), float(m * SUPER) + written[m] - 1.0, 0.0)
                       for m in range(n_super))

    sa = sum(one(lane >= float(m * SUPER)) for m in range(1, n_super))
    ja = lane - sa * float(SUPER)
    at_tile = lambda vals: sum(jnp.where(sa == float(m), vals[m], 0.0) for m in range(n_super))
    snt_a, ga, written_a = at_tile(snt), at_tile(sg), at_tile(written)
    code = jnp.where(ja < snt_a, float(CODE_GATHER), jnp.where(ja < written_a, float(CODE_ZERO), float(CODE_SKIP)))
    oblk = jnp.where(ja < written_a, lane,
                     jnp.where(written_a > 0.0, sa * float(SUPER) + written_a - 1.0, last_written))
    r0 = ((sa - pick(sup_start, ga)) * float(SUPER) + ja) * float(BLK)
    lo = [lo_ref[:, g, :] for g in groups]
    hi = [hi_ref[:, g, :] for g in groups]
    clo = jnp.sum(one(pick(hi, ga) <= r0), axis=0, keepdims=True)
    chi = jnp.sum(one(pick(lo, ga) < r0 + float(BLK)), axis=0, keepdims=True) - 1.0
    clo = jnp.clip(clo, 0.0, float(n_tiles - 1))
    chi = jnp.clip(chi, clo, float(n_tiles - 1))

    gm = one(lane >= 2.0) + one(lane >= 4.0) + one(lane >= 6.0)
    first = pick(rstart, gm) + pick(lo, gm)
    last = pick(rstart, gm) + pick(hi, gm) - 1.0
    t0 = jnp.floor(first * (1.0 / BLK))
    t1 = jnp.floor(last * (1.0 / BLK))
    has = pick(hi, gm) > pick(lo, gm)
    a0 = jnp.where(has, t0, -1.0)
    a1 = jnp.where(has, jnp.where(t1 != t0, t1, -1.0), -1.0)
    ut = jnp.where(lane >= float(N_SRC), -1.0, jnp.where(lane - 2.0 * gm == 0.0, a0, a1))

    def put(row, v):
        tbl_ref[row:row + 1, :] = v.astype(jnp.int32)

    put(TBL_RSTART, sum(jnp.where(lane == float(g), rstart[g], 0.0) for g in groups))
    put(TBL_SGROUP, by_super_lane(sg))
    put(TBL_SNT, by_super_lane(snt))
    put(TBL_SBLK, by_super_lane(sblk))
    put(TBL_CODE, code)
    put(TBL_CLO, clo)
    put(TBL_CHI, chi)
    put(TBL_OBLK, oblk)
    tbl_ref[TBL_UT:TBL_UT + n_tiles, :] = ut.astype(jnp.int32)


def _moe_tables(cum_lo, cum_hi, tot, n_tiles, n_super):
    assert n_super * SUPER <= LANES
    tbl = pl.pallas_call(
        functools.partial(_tables_kernel, n_tiles=n_tiles, n_super=n_super),
        out_shape=jax.ShapeDtypeStruct((TBL_UT + n_tiles, LANES), jnp.int32),
        name="moe_tables",
    )(cum_lo, cum_hi, tot)
    return tbl.reshape(-1)


def _moe_layer(xp, xs, g, modr, wr_t, br, w_gate, w_up, w_down, layer, cond_of_tile):
    nt = (xp.shape[0] + xs.shape[0]) // BLK
    n_super = (nt * BLK + SUPER_ROWS - 1) // SUPER_ROWS + MOE_GROUPS
    h, info_lane, info_tok, cum_lo, cum_hi, tot = _moe_route(xp, xs, g, modr, wr_t, br, cond_of_tile)
    tbl = _moe_tables(cum_lo, cum_hi, tot, nt, n_super)
    hs, cws = _moe_gather(h, info_lane, info_tok, tbl, n_super * SUPER)
    ys = _moe_mlp(hs, cws, w_gate, w_up, w_down, layer, tbl, n_super)
    return _moe_unsort(xp, xs, info_tok, modr, ys, tbl, cond_of_tile)


def kernel(x_prompt, x_sample, c, cache_k, cache_v, state_hgrn, c_ctx, norm_g, w_ada, b_ada, hgrn_w_in, hgrn_lb_logits, hgrn_onorm_g, hgrn_w_out, attn_w_qkv, attn_qn_g, attn_kn_g, attn_lambda, attn_subln_g, attn_w_out, moe_w_group, moe_b_group, moe_w_expert, moe_b_expert, moe_w_gate, moe_w_up, moe_w_down):
    n_prompt_seq, seq, d = x_prompt.shape
    dec_batch, dec_seq, _ = x_sample.shape
    n_prompt = n_prompt_seq * seq
    assert d == D_MODEL and seq == BLK and dec_seq % BLK == 0
    assert 1 + dec_batch <= N_COND

    xp = x_prompt.reshape(n_prompt, d)
    xs = x_sample.reshape(dec_batch * dec_seq, d)
    cond = jnp.zeros((N_COND, d), F32).at[0].set(c_ctx).at[1:1 + dec_batch].set(c)
    mod = _modulation(cond, w_ada, b_ada)

    lbs = jnp.cumsum(jax.nn.softmax(hgrn_lb_logits.astype(F32), axis=0), axis=0)
    lbs = lbs - lbs[0:1]

    cond_prompt = lambda b: 0
    cond_sample = lambda b: 1 + b
    npt, spt = n_prompt // BLK, dec_seq // BLK
    cond_tile = lambda i: jnp.where(i < npt, 0, 1 + (i - npt) // spt)

    w_in = hgrn_w_in.astype(BF16)
    w_hout = hgrn_w_out.astype(BF16)
    w_qkv = attn_w_qkv.astype(BF16)
    w_aout = attn_w_out.astype(BF16)
    wr_t = jnp.zeros((DEPTH, ROUTE_ROWS, d), F32)
    wr_t = wr_t.at[:, :MOE_GROUPS].set(moe_w_group.transpose(0, 2, 1))
    wr_t = wr_t.at[:, MOE_GROUPS:MOE_GROUPS + MOE_EXPERTS].set(moe_w_expert.transpose(0, 2, 1)).astype(BF16)
    br = jnp.zeros((DEPTH, ROUTE_ROWS, 1), F32)
    br = br.at[:, :MOE_GROUPS, 0].set(moe_b_group).at[:, MOE_GROUPS:MOE_GROUPS + MOE_EXPERTS, 0].set(moe_b_expert)
    w_gate = moe_w_gate.reshape(DEPTH * MOE_EXPERTS, d, MOE_D_FF)
    w_up = moe_w_up.reshape(DEPTH * MOE_EXPERTS, d, MOE_D_FF)
    w_down = moe_w_down.reshape(DEPTH * MOE_EXPERTS, MOE_D_FF, d)

    sfin, kv = None, None
    for i in range(DEPTH):
        j = i // 2
        modr = mod[i].reshape(N_COND * N_MOD, 1, d)
        if i % 2 == 0:
            common = (norm_g[i, 0], modr, w_in[j], lbs[j], hgrn_onorm_g[j], w_hout[j])
            xp, sfin = _hgrn_layer(xp, n_prompt_seq, seq, cond_prompt, *common, None, sfin, j)
            xs, _ = _hgrn_layer(xs, dec_batch, dec_seq, cond_sample, *common, state_hgrn[:, j], None, j)
        else:
            common = (norm_g[i, 0], modr, w_qkv[j], attn_qn_g[j], attn_kn_g[j], attn_lambda[j], attn_subln_g[j],
                      w_aout[j], i)
            xp, kv = _attn_layer(xp, n_prompt_seq, seq, cond_prompt, *common, None, None, kv, j)
            xs, _ = _attn_layer(xs, dec_batch, dec_seq, cond_sample, *common, cache_k[:, j], cache_v[:, j], None, j)
        xp, xs = _moe_layer(xp, xs, norm_g[i, 1], modr, wr_t[i], br[i], w_gate, w_up, w_down, i, cond_tile)

    new_k = kv[0].reshape(n_prompt_seq, DEPTH // 2, seq, HEADS, 2, QK_DIM)
    new_v = kv[1].reshape(n_prompt_seq, DEPTH // 2, seq, HEADS, HEAD_DIM)
    return (xp.reshape(n_prompt_seq, seq, d), xs.reshape(dec_batch, dec_seq, d), new_k, new_v, sfin)
```

```python
import functools
import math

import numpy as np
import jax
import jax.numpy as jnp
from jax import lax
from jax.experimental import pallas as pl
from jax.experimental.pallas import tpu as pltpu

F32 = jnp.float32
BF16 = jnp.bfloat16

D_MODEL = 1024
DEPTH = 4
GRID_W = 64
HEADS = 8
HEAD_DIM = 128
QK_DIM = 64
ROPE_THETA = 10000.0
MOE_GROUPS = 4
MOE_EPG = 4
MOE_EXPERTS = MOE_GROUPS * MOE_EPG
MOE_D_FF = 512
EPS = 1e-6
N_COND = 8
N_MOD = 6
HGRN_PARTS = 5

LANES = 128
BLK = 256
CHUNK = 32
N_CHUNK = BLK // CHUNK
HEADS_PER_STEP = 2
SCAN_HEADS_PER_BODY = 8
ATTN_HEADS_PER_BODY = 8
SUPER = 12
SUPER_ROWS = SUPER * BLK
STRAIGHT_TILES = 8
TBL_RSTART, TBL_SGROUP, TBL_SNT, TBL_SBLK, TBL_CODE, TBL_CLO, TBL_CHI, TBL_OBLK, TBL_UT = range(9)
CODE_SKIP, CODE_GATHER, CODE_ZERO = 0, 1, 2
EXPERT_TILES = 2
FF_PART = 256
GATHER_WIN = 6
EXP2_CLAMP = 115.0
VMEM_LIMIT = 56 * 1024 * 1024
N_SLOTS = DEPTH // 2
_RESIDENT = dict(pipeline_mode=pl.Buffered(1))


def _cparams(sem):
    return pltpu.CompilerParams(dimension_semantics=sem, vmem_limit_bytes=VMEM_LIMIT)


def _silu(x):
    return x * jax.nn.sigmoid(x)


def _dot(a, b):
    return jnp.dot(a, b, preferred_element_type=F32)


def _dot_nt(a, b):
    return lax.dot_general(a, b, (((1,), (1,)), ((), ())), preferred_element_type=F32)


def _dot_tn(a, b):
    return lax.dot_general(a, b, (((0,), (0,)), ((), ())), preferred_element_type=F32)


def _lane_block(i, width):
    return pl.ds(pl.multiple_of(i * width, width), width)


def _mod_kernel(c_ref, w_ref, b_ref, o_ref):
    o_ref[...] = _dot(_silu(c_ref[...]), w_ref[...]) + b_ref[...]


def _modulation(cond, w_ada, b_ada):
    tn = 1536
    nj = (N_MOD * D_MODEL) // tn
    return pl.pallas_call(
        _mod_kernel,
        grid=(DEPTH, nj),
        in_specs=[
            pl.BlockSpec((N_COND, D_MODEL), lambda l, j: (0, 0)),
            pl.BlockSpec((None, D_MODEL, tn), lambda l, j: (l, 0, j)),
            pl.BlockSpec((None, 1, tn), lambda l, j: (l, 0, j)),
        ],
        out_specs=pl.BlockSpec((None, N_COND, tn), lambda l, j: (l, 0, j)),
        out_shape=jax.ShapeDtypeStruct((DEPTH, N_COND, N_MOD * D_MODEL), F32),
        compiler_params=_cparams(("parallel", "parallel")),
        name="modulation",
    )(cond, w_ada, b_ada.reshape(DEPTH, 1, N_MOD * D_MODEL))


def _norm_mod(x, g, sc, sh):
    ms = jnp.mean(x * x, axis=-1, keepdims=True)
    return (x * lax.rsqrt(ms + EPS) * g) * (1.0 + sc) + sh


def _mod_spec(cond_of_step, which):
    return pl.BlockSpec((None, 1, D_MODEL), lambda i, *_: (cond_of_step(i) * N_MOD + which, 0, 0))


def _scan_constants():
    t = np.arange(BLK)
    out = []
    for rev in (False, True):
        u = (BLK - 1 - t) if rev else t
        ut, us = u[:, None], u[None, :]
        cums = (us <= ut).astype(np.float32)
        lev = np.where(us > ut, 0,
              np.where(ut // CHUNK == us // CHUNK, 1,
              np.where(ut // 64 == us // 64, 2,
              np.where(ut // 128 == us // 128, 3, 4)))).astype(np.int32)
        out += [jnp.asarray(cums, BF16), jnp.asarray(lev)]
    return out


def _rows_to_block(rows, rev):
    order = rows[::-1] if rev else rows
    return jnp.concatenate([jnp.broadcast_to(r, (CHUNK, LANES)) for r in order], axis=0)


def _scan_prep(q, z, lb, cums, rev, with_inter):
    sig = jax.nn.sigmoid(z)
    f = lb + (1.0 - lb) * sig
    logf = jnp.log2(f)
    k = (1.0 - lb) * (1.0 - sig)
    hi32 = lax.bitcast_convert_type(lax.bitcast_convert_type(logf, jnp.int32) & jnp.int32(-65536), F32)
    hi = hi32.astype(BF16)
    lo = (logf - hi32).astype(BF16)
    bb = _dot(cums, jnp.concatenate([hi, lo], axis=1))
    b = bb[:, :LANES] + bb[:, LANES:]

    e_row, m_row = (0, CHUNK // 2) if rev else (CHUNK - 1, CHUNK // 2 - 1)
    ends, mids = [], []
    for j in range(N_CHUNK):
        ends.append(b[j * CHUNK + e_row:j * CHUNK + e_row + 1, :])
        mids.append(b[j * CHUNK + m_row:j * CHUNK + m_row + 1, :])
    if rev:
        ends, mids = ends[::-1], mids[::-1]
    zero = jnp.zeros((1, LANES), F32)
    one = jnp.ones((1, LANES), F32)
    pres = [zero] + ends[:-1]
    b_pre = _rows_to_block(pres, rev)
    b_end = _rows_to_block(ends, rev)
    b_mid = _rows_to_block(mids, rev)

    qd = q * jnp.exp2(b - b_pre)
    ku = k * jnp.exp2(b_end - b)
    qm = q * jnp.exp2(jnp.clip(b - b_mid, -EXP2_CLAMP, EXP2_CLAMP))
    km = k * jnp.exp2(jnp.clip(b_mid - b, -EXP2_CLAMP, EXP2_CLAMP))

    levels = []
    for nc in (2, 4, 8):
        fq, fk = [], []
        for ju in range(N_CHUNK):
            r = (ju // nc) * nc + nc // 2 - 1
            if ju % nc >= nc // 2:
                fq.append(one if nc == 2 else jnp.exp2(pres[ju] - ends[r]))
                fk.append(zero)
            else:
                fq.append(zero)
                fk.append(one if nc == 2 else jnp.exp2(ends[r] - ends[ju]))
        levels.append(((qd * _rows_to_block(fq, rev)).astype(BF16), (ku * _rows_to_block(fk, rev)).astype(BF16)))

    last = ends[-1]
    qh = (qd * _rows_to_block([jnp.exp2(p) for p in pres], rev)).astype(BF16) if with_inter else None
    kh = (ku * _rows_to_block([jnp.exp2(last - e) for e in ends], rev)).astype(BF16)
    return (qm.astype(BF16), km.astype(BF16)), levels, qh, kh, last


def _scan_att(prep, lev):
    att = jnp.where(lev == 1, _dot_nt(*prep[0]), 0.0)
    for level, (ql, kl) in enumerate(prep[1], start=2):
        att = jnp.where(lev == level, _dot_nt(ql, kl), att)
    return att


def _scan_att_bidir(prep_f, prep_b, lev_f, lev_b):
    att = jnp.where(lev_f == 1, _dot_nt(*prep_f[0]), 0.0) + jnp.where(lev_b == 1, _dot_nt(*prep_b[0]), 0.0)
    lev = jnp.maximum(lev_f, lev_b)
    for level, ((qf, kf), (qb, kb)) in enumerate(zip(prep_f[1], prep_b[1]), start=2):
        both = _dot_nt(jnp.concatenate([qf, qb], axis=1), jnp.concatenate([kf, kb], axis=1))
        att = jnp.where(lev == level, both, att)
    return att


def _scan_state(prep, v, st_prev):
    ut = _dot_tn(v.astype(BF16), prep[3])
    return ut if st_prev is None else st_prev * jnp.exp2(prep[4]) + ut


def _scan_finish(o, g, on):
    ms = jnp.mean(o * o, axis=-1, keepdims=True)
    return ((o * lax.rsqrt(ms + EPS) * on) * _silu(g)).astype(BF16)


def _slot_view(ref, slot, owns_all_slots):
    if not owns_all_slots:
        return ref
    for s in range(ref.shape[0]):
        if s != slot:
            ref[s] = jnp.zeros(ref.shape[1:], ref.dtype)
    return ref.at[slot]


def _hgrn_kernel(*refs, n_blocks, has_state, slot, owns_all_slots, hps):
    it = iter(refs)
    x_ref, g_ref, sh_ref, sc_ref, gate_ref, win_ref, lb_ref, on_ref = (next(it) for _ in range(8))
    cf_ref, lf_ref, cb_ref, lvb_ref = (next(it) for _ in range(4))
    s0_ref = next(it) if has_state else None
    wout_ref, xo_ref = next(it), next(it)
    sfin_ref = None if has_state else next(it)
    h_ref, proj_ref, og_ref = next(it), next(it), next(it)
    oacc_ref, st_ref = (next(it), next(it)) if has_state else (None, None)

    h_ref[...] = _norm_mod(x_ref[...], g_ref[...], sc_ref[...], sh_ref[...]).astype(BF16)
    on = on_ref[...]
    if not has_state:
        sfin_ref = _slot_view(sfin_ref, slot, owns_all_slots)

    group_w = hps * HEAD_DIM

    def part(i, p, rows=slice(None)):
        return proj_ref[rows, p * group_w + i * HEAD_DIM:p * group_w + (i + 1) * HEAD_DIM]

    def pair(hp, carry):
        for p in range(HGRN_PARTS):
            cols = pl.ds(pl.multiple_of(p * D_MODEL + hp * group_w, group_w), group_w)
            proj_ref[:, p * group_w:(p + 1) * group_w] = _dot(h_ref[...], win_ref[:, cols])
        heads = [hp * hps + i for i in range(hps)]
        lbs = [lb_ref[:, _lane_block(hd, HEAD_DIM)] for hd in heads]
        if not has_state:
            for i, hd in enumerate(heads):
                q, v = part(i, 0), part(i, 1)
                prep_f = _scan_prep(q, part(i, 2), lbs[i][0:1, :], cf_ref[...], False, False)
                prep_b = _scan_prep(q, part(i, 3), lbs[i][1:2, :], cb_ref[...], True, False)
                att = _scan_att_bidir(prep_f, prep_b, lf_ref[...], lvb_ref[...])
                o = _dot(att.astype(BF16), v.astype(BF16))
                sfin_ref[0, hd] = _scan_state(prep_f, v, None).T
                sfin_ref[1, hd] = _scan_state(prep_b, v, None).T
                og_ref[:, _lane_block(hd, HEAD_DIM)] = _scan_finish(o, part(i, 4), on)
        else:
            for i, hd in enumerate(heads):
                st_ref[2 * i] = s0_ref[0, hd].T
                st_ref[2 * i + 1] = s0_ref[1, hd].T

            def body(tb, c2):
                rf = pl.ds(pl.multiple_of(tb * BLK, BLK), BLK)
                rb = pl.ds(pl.multiple_of((n_blocks - 1 - tb) * BLK, BLK), BLK)
                for i in range(hps):
                    cols = slice(i * HEAD_DIM, (i + 1) * HEAD_DIM)
                    v_f, v_b = part(i, 1, rf), part(i, 1, rb)
                    prep_f = _scan_prep(part(i, 0, rf), part(i, 2, rf), lbs[i][0:1, :], cf_ref[...], False, True)
                    prep_b = _scan_prep(part(i, 0, rb), part(i, 3, rb), lbs[i][1:2, :], cb_ref[...], True, True)
                    st_f, st_b = st_ref[2 * i], st_ref[2 * i + 1]
                    of = (_dot(_scan_att(prep_f, lf_ref[...]).astype(BF16), v_f.astype(BF16))
                          + _dot_nt(prep_f[2], st_f.astype(BF16)))
                    ob = (_dot(_scan_att(prep_b, lvb_ref[...]).astype(BF16), v_b.astype(BF16))
                          + _dot_nt(prep_b[2], st_b.astype(BF16)))
                    st_ref[2 * i] = _scan_state(prep_f, v_f, st_f)
                    st_ref[2 * i + 1] = _scan_state(prep_b, v_b, st_b)

                    @pl.when(2 * tb < n_blocks)
                    def _():
                        oacc_ref[rf, cols] = of
                        oacc_ref[rb, cols] = ob

                    @pl.when(2 * tb >= n_blocks)
                    def _():
                        oacc_ref[rf, cols] += of
                        oacc_ref[rb, cols] += ob

                return c2

            lax.fori_loop(0, n_blocks, body, 0)
            for i, hd in enumerate(heads):
                cols = slice(i * HEAD_DIM, (i + 1) * HEAD_DIM)
                og_ref[:, _lane_block(hd, HEAD_DIM)] = _scan_finish(oacc_ref[:, cols], part(i, 4), on)
        return carry

    lax.fori_loop(0, HEADS // hps, pair, 0)
    xo_ref[...] = x_ref[...] + gate_ref[...] * _dot(og_ref[...], wout_ref[...])


def _hgrn_layer(x, n_seq, t, cond_of_seq, g, modr, w_in, lbs_j, onorm_g, w_out, s0, sfin_prev, slot):
    d = D_MODEL
    has_state = s0 is not None
    n_blocks = t // BLK
    assert n_blocks == 1 or n_blocks % 2 == 0
    full = lambda shape, **kw: pl.BlockSpec(shape, lambda b, *_: (0,) * len(shape), **kw)
    in_specs = [
        pl.BlockSpec((t, d), lambda b: (b, 0)),
        full((1, d)),
        _mod_spec(cond_of_seq, 0), _mod_spec(cond_of_seq, 1), _mod_spec(cond_of_seq, 2),
        full((d, HGRN_PARTS * d), **_RESIDENT),
        full((2, d)),
        full((1, HEAD_DIM)),
    ] + [full((BLK, BLK))] * 4
    args = [x, g.reshape(1, d), modr, modr, modr, w_in, lbs_j, onorm_g.reshape(1, HEAD_DIM)] + _scan_constants()
    state_block = (None, 2, HEADS, HEAD_DIM, HEAD_DIM)
    if has_state:
        in_specs.append(pl.BlockSpec(state_block, lambda b: (b, 0, 0, 0, 0)))
        args.append(s0)
    in_specs.append(full((d, d), **_RESIDENT))
    args.append(w_out)
    out_specs = [pl.BlockSpec((t, d), lambda b: (b, 0))]
    out_shape = [jax.ShapeDtypeStruct((n_seq * t, d), F32)]
    hps = HEADS_PER_STEP if has_state else SCAN_HEADS_PER_BODY
    group_w = hps * HEAD_DIM
    scratch = [pltpu.VMEM((t, d), BF16), pltpu.VMEM((t, HGRN_PARTS * group_w), F32), pltpu.VMEM((t, d), BF16)]
    aliases = {}
    if has_state:
        scratch += [pltpu.VMEM((t, group_w), F32), pltpu.VMEM((2 * hps, HEAD_DIM, HEAD_DIM), F32)]
    else:
        state_dims = (2, HEADS, HEAD_DIM, HEAD_DIM)
        if sfin_prev is None:
            out_specs.append(pl.BlockSpec((None, N_SLOTS) + state_dims, lambda b: (b, 0, 0, 0, 0, 0)))
        else:
            out_specs.append(pl.BlockSpec((None, None) + state_dims, lambda b: (b, slot, 0, 0, 0, 0)))
            in_specs.append(pl.BlockSpec(memory_space=pl.ANY))
            args.append(sfin_prev)
            aliases = {len(args) - 1: 1}
        out_shape.append(jax.ShapeDtypeStruct((n_seq, N_SLOTS) + state_dims, F32))

    def body(*refs):
        if sfin_prev is not None:
            n_in = len(args)
            refs = refs[:n_in - 1] + refs[n_in:]
        _hgrn_kernel(*refs, n_blocks=n_blocks, has_state=has_state, slot=slot, owns_all_slots=sfin_prev is None,
                     hps=hps)

    out = pl.pallas_call(
        body,
        grid=(n_seq,),
        in_specs=in_specs,
        out_specs=out_specs,
        out_shape=out_shape,
        scratch_shapes=scratch,
        input_output_aliases=aliases,
        compiler_params=_cparams(("parallel",)),
        name="hgrn_layer_sample" if has_state else "hgrn_layer_prompt",
    )(*args)
    return out if not has_state else (out[0], None)


def _rope_tables(t_lat):
    rows = t_lat // GRID_W
    row = jnp.repeat(jnp.arange(rows), GRID_W).astype(F32)
    col = jnp.tile(jnp.arange(GRID_W), rows).astype(F32)
    half = QK_DIM // 2
    inv_freq = ROPE_THETA ** (-jnp.arange(0, half, 2, dtype=F32) / half)
    ang_row = row[:, None] * inv_freq
    ang_col = col[:, None] * inv_freq

    def part(ang):
        c, s = jnp.cos(ang), jnp.sin(ang)
        return jnp.concatenate([c, c], axis=1), jnp.concatenate([-s, s], axis=1)

    cr, sr = part(ang_row)
    cc, sc = part(ang_col)
    cos64 = jnp.concatenate([cr, cc], axis=1)
    sin64 = jnp.concatenate([sr, sc], axis=1)
    return jnp.concatenate([cos64, cos64], axis=1), jnp.concatenate([sin64, sin64], axis=1)


def _rope(x, cos, sin):
    lane = lax.broadcasted_iota(jnp.int32, x.shape, 1)
    first = (lane % (QK_DIM // 2)) < (QK_DIM // 4)
    swapped = jnp.where(first, pltpu.roll(x, LANES - QK_DIM // 4, 1), pltpu.roll(x, QK_DIM // 4, 1))
    return x * cos + swapped * sin


def _attn_kernel(*refs, n_qblk, rope, cache, emit_kv, lam_init, slot, owns_all_slots):
    it = iter(refs)
    x_ref, g_ref, sh_ref, sc_ref, gate_ref, wqkv_ref = (next(it) for _ in range(6))
    qg_ref, kg_ref, lam_ref, sg_ref, seg_ref = (next(it) for _ in range(5))
    cos_ref = sin_ref = ck_ref = cv_ref = nk_ref = nv_ref = None
    if rope:
        cos_ref, sin_ref = next(it), next(it)
    if cache:
        ck_ref, cv_ref = next(it), next(it)
    wout_ref, xo_ref = next(it), next(it)
    if emit_kv:
        nk_ref = _slot_view(next(it), slot, owns_all_slots)
        nv_ref = _slot_view(next(it), slot, owns_all_slots)
    h_ref, qkv_ref, oa_ref = next(it), next(it), next(it)

    d = D_MODEL
    h_ref[...] = _norm_mod(x_ref[...], g_ref[...], sc_ref[...], sh_ref[...]).astype(BF16)
    qkv_ref[...] = _dot(h_ref[...], wqkv_ref[...])
    seg = seg_ref[...]
    lp = lam_ref[...]
    lam = (jnp.exp(jnp.sum(lp[0:1, :] * lp[1:2, :], axis=-1, keepdims=True))
           - jnp.exp(jnp.sum(lp[2:3, :] * lp[3:4, :], axis=-1, keepdims=True)) + lam_init)
    lane = lax.broadcasted_iota(jnp.int32, (1, LANES), 1)
    comp0 = lane < QK_DIM
    scale = QK_DIM ** -0.5

    def split(a):
        return [jnp.where(comp0, a, 0.0).astype(BF16), jnp.where(comp0, 0.0, a).astype(BF16)]

    def one_head(hd):
        hcol = _lane_block(hd, HEAD_DIM)
        k = qkv_ref[:, pl.ds(pl.multiple_of(d + hd * HEAD_DIM, HEAD_DIM), HEAD_DIM)]
        v = qkv_ref[:, pl.ds(pl.multiple_of(2 * d + hd * HEAD_DIM, HEAD_DIM), HEAD_DIM)]
        kn = k * lax.rsqrt(_dot(k * k, seg) + EPS) * kg_ref[...]
        if emit_kv:
            nk_ref[:, hcol] = kn
            nv_ref[:, hcol] = v
        if rope:
            kn = _rope(kn, cos_ref[...], sin_ref[...])
        ks = split(kn)
        vb = v.astype(BF16)
        if cache:
            cks = split(ck_ref[:, hcol])
            cvb = cv_ref[:, hcol].astype(BF16)
        for qi in range(n_qblk):
            rows = slice(qi * BLK, (qi + 1) * BLK)
            q = qkv_ref[rows, hcol]
            qn = q * lax.rsqrt(_dot(q * q, seg) + EPS) * qg_ref[...]
            if rope:
                qn = _rope(qn, cos_ref[rows, :], sin_ref[rows, :])
            qb = (qn * scale).astype(BF16)
            a_self, a_cache = None, None
            for c in range(2):
                s = _dot_nt(qb, ks[c])
                m = jnp.max(s, axis=-1, keepdims=True)
                if cache:
                    sc = _dot_nt(qb, cks[c])
                    m = jnp.maximum(m, jnp.max(sc, axis=-1, keepdims=True))
                p = jnp.exp(s - m)
                den = jnp.sum(p, axis=-1, keepdims=True)
                if cache:
                    pc = jnp.exp(sc - m)
                    den = den + jnp.sum(pc, axis=-1, keepdims=True)
                w = (1.0 / den) if c == 0 else (-lam / den)
                a_self = p * w if c == 0 else a_self + p * w
                if cache:
                    a_cache = pc * w if c == 0 else a_cache + pc * w
            o = _dot(a_self.astype(BF16), vb)
            if cache:
                o = o + _dot(a_cache.astype(BF16), cvb)
            ms = jnp.mean(o * o, axis=-1, keepdims=True)
            o = (o * lax.rsqrt(ms + EPS) * sg_ref[...]) * (1.0 - lam_init)
            oa_ref[rows, hcol] = o.astype(BF16)

    heads_per_body = ATTN_HEADS_PER_BODY if n_qblk == 1 else HEADS_PER_STEP

    def group(hg, carry):
        for i in range(heads_per_body):
            one_head(hg * heads_per_body + i)
        return carry

    lax.fori_loop(0, HEADS // heads_per_body, group, 0)
    xo_ref[...] = x_ref[...] + gate_ref[...] * _dot(oa_ref[...], wout_ref[...])


def _attn_layer(x, n_seq, t, cond_of_seq, g, modr, w_qkv, qn_g, kn_g, lam_p, subln_g, w_out, layer_idx,
                cache_k_j, cache_v_j, kv_prev, slot):
    d = D_MODEL
    cache = cache_k_j is not None
    lam_init = 0.8 - 0.6 * math.exp(-0.3 * layer_idx)
    qg = jnp.tile(qn_g.reshape(1, QK_DIM), (1, 2))
    kg = jnp.tile(kn_g.reshape(1, QK_DIM), (1, 2))
    li = np.arange(LANES)
    seg = jnp.asarray((li[:, None] // QK_DIM == li[None, :] // QK_DIM).astype(np.float32) / QK_DIM)
    full = lambda shape, **kw: pl.BlockSpec(shape, lambda b, *_: (0,) * len(shape), **kw)
    in_specs = [
        pl.BlockSpec((t, d), lambda b: (b, 0)),
        full((1, d)),
        _mod_spec(cond_of_seq, 0), _mod_spec(cond_of_seq, 1), _mod_spec(cond_of_seq, 2),
        full((d, 3 * d), **_RESIDENT),
        full((1, LANES)), full((1, LANES)), full((4, QK_DIM)), full((1, HEAD_DIM)), full((LANES, LANES)),
    ]
    args = [x, g.reshape(1, d), modr, modr, modr, w_qkv, qg, kg, lam_p, subln_g.reshape(1, HEAD_DIM), seg]
    if cache:
        cos, sin = _rope_tables(t)
        past = cache_k_j.shape[1]
        in_specs += [full((t, LANES)), full((t, LANES)),
                     pl.BlockSpec((None, past, d), lambda b: (b, 0, 0)),
                     pl.BlockSpec((None, past, d), lambda b: (b, 0, 0))]
        args += [cos, sin, cache_k_j.reshape(n_seq, past, d), cache_v_j.reshape(n_seq, past, d)]
    in_specs.append(full((d, d), **_RESIDENT))
    args.append(w_out)
    out_specs = [pl.BlockSpec((t, d), lambda b: (b, 0))]
    out_shape = [jax.ShapeDtypeStruct((n_seq * t, d), F32)]
    aliases = {}
    n_carried = 0
    if not cache:
        if kv_prev is None:
            kv_spec = pl.BlockSpec((None, N_SLOTS, t, d), lambda b: (b, 0, 0, 0))
        else:
            kv_spec = pl.BlockSpec((None, None, t, d), lambda b: (b, slot, 0, 0))
            in_specs += [pl.BlockSpec(memory_space=pl.ANY)] * 2
            args += list(kv_prev)
            aliases = {len(args) - 2: 1, len(args) - 1: 2}
            n_carried = 2
        out_specs += [kv_spec, kv_spec]
        out_shape += [jax.ShapeDtypeStruct((n_seq, N_SLOTS, t, d), F32)] * 2

    def body(*refs):
        n_in = len(args)
        refs = refs[:n_in - n_carried] + refs[n_in:]
        _attn_kernel(*refs, n_qblk=t // BLK, rope=cache, cache=cache, emit_kv=not cache, lam_init=lam_init,
                     slot=slot, owns_all_slots=kv_prev is None)

    out = pl.pallas_call(
        body,
        grid=(n_seq,),
        in_specs=in_specs,
        out_specs=out_specs,
        out_shape=out_shape,
        scratch_shapes=[pltpu.VMEM((t, d), BF16), pltpu.VMEM((t, 3 * d), F32), pltpu.VMEM((t, d), BF16)],
        input_output_aliases=aliases,
        compiler_params=_cparams(("parallel",)),
        name="attn_layer_sample" if cache else "attn_layer_prompt",
    )(*args)
    return (out[0], None) if cache else (out[0], (out[1], out[2]))


ROUTE_ROWS = 32
INFO_GID, INFO_RANK = 8, 9


def _two_stream_specs(n_prompt_tiles, width):
    return [pl.BlockSpec((BLK, width), lambda i, *_: (jnp.minimum(i, n_prompt_tiles - 1), 0)),
            pl.BlockSpec((BLK, width), lambda i, *_: (jnp.maximum(i - n_prompt_tiles, 0), 0))]


def _route_kernel(xp_ref, xs_ref, g_ref, sh_ref, sc_ref, wr_ref, br_ref, tri_ref,
                  h_ref, il_ref, it_ref, cum_ref, cumhi_ref, tot_ref, carry_ref, *, n_prompt_tiles):
    i = pl.program_id(0)

    @pl.when(i == 0)
    def _():
        carry_ref[...] = jnp.zeros_like(carry_ref)

    @pl.when(i < n_prompt_tiles)
    def _():
        h_ref[...] = _norm_mod(xp_ref[...], g_ref[...], sc_ref[...], sh_ref[...]).astype(BF16)

    @pl.when(i >= n_prompt_tiles)
    def _():
        h_ref[...] = _norm_mod(xs_ref[...], g_ref[...], sc_ref[...], sh_ref[...]).astype(BF16)

    logit = _dot_nt(wr_ref[...], h_ref[...]) + br_ref[...]
    gl = [logit[g:g + 1, :] for g in range(MOE_GROUPS)]
    gmax = functools.reduce(jnp.maximum, gl)
    gz = functools.reduce(lambda a, b: a + b, [jnp.exp(x - gmax) for x in gl])
    g_w = 1.0 / gz
    gid = jnp.full_like(gmax, MOE_GROUPS - 1)
    for g in range(MOE_GROUPS - 2, -1, -1):
        gid = jnp.where(gl[g] == gmax, float(g), gid)
    el = []
    for j in range(MOE_EPG):
        e = logit[MOE_GROUPS + j:MOE_GROUPS + j + 1, :]
        for g in range(1, MOE_GROUPS):
            r = MOE_GROUPS + g * MOE_EPG + j
            e = jnp.where(gid == float(g), logit[r:r + 1, :], e)
        el.append(e)
    emax = functools.reduce(jnp.maximum, el)
    pe = [jnp.exp(e - emax) for e in el]
    idx1 = jnp.full_like(emax, MOE_EPG - 1)
    for j in range(MOE_EPG - 2, -1, -1):
        idx1 = jnp.where(el[j] == emax, float(j), idx1)
    el2 = [jnp.where(idx1 == float(j), -jnp.inf, el[j]) for j in range(MOE_EPG)]
    emax2 = functools.reduce(jnp.maximum, el2)
    idx2 = jnp.full_like(emax, MOE_EPG - 1)
    for j in range(MOE_EPG - 2, -1, -1):
        idx2 = jnp.where(el2[j] == emax2, float(j), idx2)
    sel = [(idx1 == float(j)) | (idx2 == float(j)) for j in range(MOE_EPG)]
    den = functools.reduce(lambda a, b: a + b, [jnp.where(sel[j], pe[j], 0.0) for j in range(MOE_EPG)])
    cw = [jnp.where(sel[j], pe[j] * (g_w / den), 0.0) for j in range(MOE_EPG)]

    row8 = lax.broadcasted_iota(jnp.int32, (8, BLK), 0)
    onehot = jnp.where(row8.astype(F32) == gid, 1.0, 0.0)
    within = _dot(onehot.astype(BF16), tri_ref[...])
    carry = carry_ref[...]
    rank = jnp.sum(onehot * (within + carry[:, 0:1]), axis=0, keepdims=True)
    cum_ref[...] = carry
    new_carry = carry + jnp.sum(onehot, axis=1, keepdims=True)
    carry_ref[...] = new_carry
    cumhi_ref[...] = new_carry
    tot_ref[...] = new_carry

    il_ref[...] = jnp.where(row8 == 0, gid, jnp.where(row8 == 1, rank, 0.0))
    rowl = lax.broadcasted_iota(jnp.int32, (LANES, BLK), 0)
    m = jnp.zeros((LANES, BLK), F32)
    for j in range(MOE_EPG):
        hi = cw[j].astype(BF16).astype(F32)
        m = jnp.where(rowl == j, hi, m)
        m = jnp.where(rowl == MOE_EPG + j, cw[j] - hi, m)
    m = jnp.where(rowl == INFO_GID, gid, m)
    m = jnp.where(rowl == INFO_RANK, rank, m)
    it_ref[...] = m.T


def _moe_route(xp, xs, g, modr, wr_t, br, cond_of_tile):
    d = D_MODEL
    npt = xp.shape[0] // BLK
    nt = npt + xs.shape[0] // BLK
    n = nt * BLK
    tri = jnp.asarray(np.triu(np.ones((BLK, BLK), np.float32), 1), BF16)
    full = lambda shape: pl.BlockSpec(shape, lambda i: (0,) * len(shape))
    return pl.pallas_call(
        functools.partial(_route_kernel, n_prompt_tiles=npt),
        grid=(nt,),
        in_specs=_two_stream_specs(npt, d) + [
            full((1, d)),
            _mod_spec(cond_of_tile, 3), _mod_spec(cond_of_tile, 4),
            full((ROUTE_ROWS, d)), full((ROUTE_ROWS, 1)), full((BLK, BLK)),
        ],
        out_specs=[
            pl.BlockSpec((BLK, d), lambda i: (i, 0)),
            pl.BlockSpec((8, BLK), lambda i: (0, i)),
            pl.BlockSpec((BLK, LANES), lambda i: (i, 0)),
            pl.BlockSpec((None, 8, LANES), lambda i: (i, 0, 0)),
            pl.BlockSpec((None, 8, LANES), lambda i: (i, 0, 0)),
            pl.BlockSpec((8, LANES), lambda i: (0, 0)),
        ],
        out_shape=[
            jax.ShapeDtypeStruct((n, d), BF16),
            jax.ShapeDtypeStruct((8, n), F32),
            jax.ShapeDtypeStruct((n, LANES), F32),
            jax.ShapeDtypeStruct((nt, 8, LANES), F32),
            jax.ShapeDtypeStruct((nt, 8, LANES), F32),
            jax.ShapeDtypeStruct((8, LANES), F32),
        ],
        scratch_shapes=[pltpu.VMEM((8, LANES), F32)],
        compiler_params=_cparams(("arbitrary",)),
        name="moe_route",
    )(xp, xs, g.reshape(1, d), modr, modr, wr_t, br, tri)


def _sorted_pos(gid, rank, rstart_ref):
    p = rank
    for g in range(MOE_GROUPS):
        p = p + jnp.where(gid == float(g), rstart_ref[g].astype(F32), 0.0)
    return p


def _gather_kernel(tbl_ref, h_ref, il_ref, it_ref, hs_ref, cws_ref, *, n_tiles):
    a = pl.program_id(0)
    win = GATHER_WIN * BLK
    code = tbl_ref[TBL_CODE * LANES + a]
    rstart_ref = tbl_ref

    @pl.when(code == CODE_ZERO)
    def _():
        hs_ref[...] = jnp.zeros_like(hs_ref)
        cws_ref[...] = jnp.zeros_like(cws_ref)

    @pl.when(code == CODE_GATHER)
    def _():
        dest = (lax.broadcasted_iota(jnp.int32, (BLK, 1), 0) + a * BLK).astype(F32)
        src_tile = lax.broadcasted_iota(jnp.int32, (1, win), 1) // BLK
        clo = tbl_ref[TBL_CLO * LANES + a]
        n_win = (tbl_ref[TBL_CHI * LANES + a] - clo + GATHER_WIN) // GATHER_WIN

        def window(w):
            first = clo + w * GATHER_WIN
            c0 = jnp.minimum(first, n_tiles - GATHER_WIN)
            rows = pl.ds(pl.multiple_of(c0 * BLK, BLK), win)
            info = il_ref[:, rows]
            p = _sorted_pos(info[0:1, :], info[1:2, :], rstart_ref)
            p = jnp.where(src_tile + c0 >= first, p, -1.0)
            onehot = jnp.where(dest == p, 1.0, 0.0).astype(BF16)
            r = _dot(onehot, it_ref[rows, :].astype(BF16))
            return _dot(onehot, h_ref[rows, :]), r + pltpu.roll(r, LANES - MOE_EPG, 1)

        dh, dc = window(0)
        hs_ref[...] = dh.astype(BF16)
        cws_ref[...] = dc

        def body(w, carry):
            dh, dc = window(w)
            hs_ref[...] = (hs_ref[...].astype(F32) + dh).astype(BF16)
            cws_ref[...] += dc
            return carry

        lax.fori_loop(1, n_win, body, 0)


def _moe_gather(h, info_lane, info_tok, tbl, hs_buf, cws_buf):
    n, d = h.shape
    nt = n // BLK
    n_dest_tiles = hs_buf.shape[0] // BLK
    assert nt >= GATHER_WIN and n_dest_tiles <= LANES
    out_block = lambda a, tbl_r: (tbl_r[TBL_OBLK * LANES + a], 0)
    carried = pl.BlockSpec(memory_space=pl.ANY)

    def body(tbl_ref, h_ref, il_ref, it_ref, hs_old, cws_old, hs_ref, cws_ref):
        _gather_kernel(tbl_ref, h_ref, il_ref, it_ref, hs_ref, cws_ref, n_tiles=nt)

    return pl.pallas_call(
        body,
        grid_spec=pltpu.PrefetchScalarGridSpec(
            num_scalar_prefetch=1,
            grid=(n_dest_tiles,),
            in_specs=[
                pl.BlockSpec((n, d), lambda a, *_: (0, 0), **_RESIDENT),
                pl.BlockSpec((8, n), lambda a, *_: (0, 0), **_RESIDENT),
                pl.BlockSpec((n, LANES), lambda a, *_: (0, 0), **_RESIDENT),
                carried, carried,
            ],
            out_specs=[pl.BlockSpec((BLK, d), out_block), pl.BlockSpec((BLK, LANES), out_block)],
        ),
        out_shape=[jax.ShapeDtypeStruct(hs_buf.shape, BF16), jax.ShapeDtypeStruct(cws_buf.shape, F32)],
        input_output_aliases={4: 0, 5: 1},
        compiler_params=_cparams(("arbitrary",)),
        name="moe_gather",
    )(tbl, h, info_lane, info_tok, hs_buf, cws_buf)


def _moe_mlp_kernel(tbl_ref, hs_ref, cws_ref, wg_ref, wu_ref, wd_ref, ys_ref,
                    acc_ref, wgb_ref, wub_ref, wdb_ref):
    s = pl.program_id(0)
    k = pl.program_id(1)
    nt = tbl_ref[TBL_SNT * LANES + s]

    @pl.when(nt > 0)
    def _():
        wgb_ref[...] = wg_ref[...].astype(BF16)
        wub_ref[...] = wu_ref[...].astype(BF16)
        wdb_ref[...] = wd_ref[...].astype(BF16)

    @pl.when((s == 0) & (k == 0))
    def _():
        acc_ref[...] = jnp.zeros_like(acc_ref)

    def skipped(j):
        rows = slice(j * EXPERT_TILES * BLK, (j + 1) * EXPERT_TILES * BLK)
        ys_ref[rows, :] = jnp.zeros((EXPERT_TILES * BLK, D_MODEL), BF16)

    def block(j):
        rows = slice(j * EXPERT_TILES * BLK, (j + 1) * EXPERT_TILES * BLK)
        hsub = hs_ref[rows, :]
        cws = cws_ref[rows, :]
        cwk = jnp.zeros((EXPERT_TILES * BLK, 1), F32)
        for kk in range(MOE_EPG):
            cwk = jnp.where(k == kk, cws[:, kk:kk + 1], cwk)
        y = None
        for fh in range(MOE_D_FF // FF_PART):
            fc = slice(fh * FF_PART, (fh + 1) * FF_PART)
            gate = _dot(hsub, wgb_ref[:, fc])
            up = _dot(hsub, wub_ref[:, fc])
            act = ((_silu(gate) * up) * cwk).astype(BF16)
            part = _dot(act, wdb_ref[fc, :])
            y = part if y is None else y + part
        total = jnp.where(k == 0, 0.0, acc_ref[rows, :]) + y
        acc_ref[rows, :] = total
        ys_ref[rows, :] = total.astype(BF16)

    def maybe_block(j):
        pl.when(j * EXPERT_TILES < nt)(functools.partial(block, j))
        pl.when((j * EXPERT_TILES >= nt) & (nt > 0) & (k == 0))(functools.partial(skipped, j))

    n_blocks = SUPER // EXPERT_TILES
    n_straight = STRAIGHT_TILES // EXPERT_TILES

    @pl.when(nt >= STRAIGHT_TILES)
    def _():
        for j in range(n_straight):
            block(j)

    @pl.when(nt < STRAIGHT_TILES)
    def _():
        for j in range(n_straight):
            maybe_block(j)

    for j in range(n_straight, n_blocks):
        maybe_block(j)


def _moe_mlp(hs, cws, w_gate, w_up, w_down, layer, tbl, ys_buf):
    d = hs.shape[1]
    n_super = hs.shape[0] // SUPER_ROWS

    def body(tbl_ref, hs_ref, cws_ref, wg_ref, wu_ref, wd_ref, ys_old, ys_ref, *scratch):
        _moe_mlp_kernel(tbl_ref, hs_ref, cws_ref, wg_ref, wu_ref, wd_ref, ys_ref, *scratch)

    def widx(s, k, tbl_r):
        kk = jnp.where(tbl_r[TBL_SNT * LANES + s] > 0, k, MOE_EPG - 1)
        return (layer * MOE_EXPERTS + tbl_r[TBL_SGROUP * LANES + s] * MOE_EPG + kk, 0, 0)

    rows_idx = lambda s, k, tbl_r: (tbl_r[TBL_SBLK * LANES + s], 0)
    return pl.pallas_call(
        body,
        grid_spec=pltpu.PrefetchScalarGridSpec(
            num_scalar_prefetch=1,
            grid=(n_super, MOE_EPG),
            in_specs=[
                pl.BlockSpec((SUPER_ROWS, d), rows_idx),
                pl.BlockSpec((SUPER_ROWS, LANES), rows_idx),
                pl.BlockSpec((None, d, MOE_D_FF), widx),
                pl.BlockSpec((None, d, MOE_D_FF), widx),
                pl.BlockSpec((None, MOE_D_FF, d), widx),
                pl.BlockSpec(memory_space=pl.ANY),
            ],
            out_specs=pl.BlockSpec((SUPER_ROWS, d), rows_idx),
            scratch_shapes=[
                pltpu.VMEM((SUPER_ROWS, d), F32),
                pltpu.VMEM((d, MOE_D_FF), BF16),
                pltpu.VMEM((d, MOE_D_FF), BF16),
                pltpu.VMEM((MOE_D_FF, d), BF16),
            ],
        ),
        out_shape=jax.ShapeDtypeStruct(ys_buf.shape, BF16),
        input_output_aliases={6: 0},
        compiler_params=_cparams(("arbitrary", "arbitrary")),
        name="moe_experts",
    )(tbl, hs, cws, w_gate, w_up, w_down, ys_buf)


N_SRC = 2 * MOE_GROUPS


def _unsort_kernel(tbl_ref, xp_ref, xs_ref, it_ref, gate_ref, *rest, n_prompt_tiles):
    ys_refs, op_ref, os_ref = rest[:N_SRC], rest[N_SRC], rest[N_SRC + 1]
    t = pl.program_id(0)
    info = it_ref[...]
    p = _sorted_pos(info[:, INFO_GID:INFO_GID + 1], info[:, INFO_RANK:INFO_RANK + 1], tbl_ref)
    lane = lax.broadcasted_iota(jnp.int32, (1, BLK), 1).astype(F32)
    slot = lambda m: tbl_ref[(TBL_UT + t) * LANES + m]

    def take(m):
        a = slot(m)
        onehot = jnp.where(p - (a * BLK).astype(F32) == lane, 1.0, 0.0).astype(BF16)
        return _dot(onehot, ys_refs[m][...])

    def stream(x_ref, o_ref):
        first = functools.reduce(lambda u, w: u + w, [take(m) for m in range(0, N_SRC, 2)])
        o_ref[...] = x_ref[...] + gate_ref[...] * first
        for m in range(1, N_SRC, 2):
            @pl.when(slot(m) >= 0)
            def _():
                o_ref[...] += gate_ref[...] * take(m)

    pl.when(t < n_prompt_tiles)(functools.partial(stream, xp_ref, op_ref))
    pl.when(t >= n_prompt_tiles)(functools.partial(stream, xs_ref, os_ref))


def _moe_unsort(xp, xs, info_tok, modr, ys, tbl, cond_of_tile):
    d = D_MODEL
    npt = xp.shape[0] // BLK
    nt = npt + xs.shape[0] // BLK

    def ys_spec(m):
        return pl.BlockSpec((BLK, d), lambda t, tbl_r: (jnp.maximum(tbl_r[(TBL_UT + t) * LANES + m], 0), 0))

    return pl.pallas_call(
        functools.partial(_unsort_kernel, n_prompt_tiles=npt),
        grid_spec=pltpu.PrefetchScalarGridSpec(
            num_scalar_prefetch=1,
            grid=(nt,),
            in_specs=_two_stream_specs(npt, d) + [
                pl.BlockSpec((BLK, LANES), lambda t, *_: (t, 0)),
                _mod_spec(cond_of_tile, 5),
            ] + [ys_spec(m) for m in range(N_SRC)],
            out_specs=_two_stream_specs(npt, d),
        ),
        out_shape=[jax.ShapeDtypeStruct(xp.shape, F32), jax.ShapeDtypeStruct(xs.shape, F32)],
        compiler_params=_cparams(("arbitrary",)),
        name="moe_unsort_residual",
    )(tbl, xp, xs, info_tok, modr, *([ys] * N_SRC))


def _tables_kernel(lo_ref, hi_ref, tot_ref, tbl_ref, *, n_tiles, n_super):
    one = lambda cond: jnp.where(cond, 1.0, 0.0)
    groups = range(MOE_GROUPS)
    lane = lax.broadcasted_iota(jnp.int32, (1, LANES), 1).astype(F32)
    tot = [tot_ref[g:g + 1, :] for g in groups]
    pick = lambda vals, idx: sum(jnp.where(idx == float(g), vals[g], 0.0) for g in groups)

    n_sup = [sum(one(tot[g] > float(m * SUPER_ROWS)) for m in range(n_super)) for g in groups]
    sup_start, sup_end, run = [], [], 0.0
    for g in groups:
        sup_start.append(run + 0.0 * tot[g])
        run = run + n_sup[g]
        sup_end.append(run)
    n_used = sup_end[-1]
    rstart = [sup_start[g] * float(SUPER_ROWS) for g in groups]
    group_of = lambda s: jnp.minimum(sum(one(s >= sup_end[g]) for g in groups), float(MOE_GROUPS - 1))

    sg, snt, written = [], [], []
    for m in range(n_super):
        used = float(m) < n_used
        g_m = group_of(jnp.where(used, float(m), n_used - 1.0))
        rows_left = pick(tot, g_m) - (float(m) - pick(sup_start, g_m)) * float(SUPER_ROWS)
        nt_m = jnp.where(used, jnp.clip(jnp.floor((rows_left + float(BLK - 1)) * (1.0 / BLK)), 0.0, float(SUPER)), 0.0)
        sg.append(g_m)
        snt.append(nt_m)
        written.append(float(EXPERT_TILES) * jnp.floor((nt_m + float(EXPERT_TILES - 1)) * (1.0 / EXPERT_TILES)))
    sblk = [jnp.minimum(float(m), n_used - 1.0) for m in range(n_super)]
    by_super_lane = lambda vals: sum(jnp.where(lane == float(m), vals[m], 0.0) for m in range(n_super))
    last_written = sum(jnp.where(n_used - 1.0 == float(m), float(m * SUPER) + written[m] - 1.0, 0.0)
                       for m in range(n_super))

    sa = sum(one(lane >= float(m * SUPER)) for m in range(1, n_super))
    ja = lane - sa * float(SUPER)
    at_tile = lambda vals: sum(jnp.where(sa == float(m), vals[m], 0.0) for m in range(n_super))
    snt_a, ga, written_a = at_tile(snt), at_tile(sg), at_tile(written)
    code = jnp.where(ja < snt_a, float(CODE_GATHER), jnp.where(ja < written_a, float(CODE_ZERO), float(CODE_SKIP)))
    oblk = jnp.where(ja < written_a, lane,
                     jnp.where(written_a > 0.0, sa * float(SUPER) + written_a - 1.0, last_written))
    r0 = ((sa - pick(sup_start, ga)) * float(SUPER) + ja) * float(BLK)
    lo = [lo_ref[:, g, :] for g in groups]
    hi = [hi_ref[:, g, :] for g in groups]
    clo = jnp.sum(one(pick(hi, ga) <= r0), axis=0, keepdims=True)
    chi = jnp.sum(one(pick(lo, ga) < r0 + float(BLK)), axis=0, keepdims=True) - 1.0
    clo = jnp.clip(clo, 0.0, float(n_tiles - 1))
    chi = jnp.clip(chi, clo, float(n_tiles - 1))

    gm = one(lane >= 2.0) + one(lane >= 4.0) + one(lane >= 6.0)
    first = pick(rstart, gm) + pick(lo, gm)
    last = pick(rstart, gm) + pick(hi, gm) - 1.0
    t0 = jnp.floor(first * (1.0 / BLK))
    t1 = jnp.floor(last * (1.0 / BLK))
    has = pick(hi, gm) > pick(lo, gm)
    a0 = jnp.where(has, t0, -1.0)
    a1 = jnp.where(has, jnp.where(t1 != t0, t1, -1.0), -1.0)
    ut = jnp.where(lane >= float(N_SRC), -1.0, jnp.where(lane - 2.0 * gm == 0.0, a0, a1))

    def put(row, v):
        tbl_ref[row:row + 1, :] = v.astype(jnp.int32)

    put(TBL_RSTART, sum(jnp.where(lane == float(g), rstart[g], 0.0) for g in groups))
    put(TBL_SGROUP, by_super_lane(sg))
    put(TBL_SNT, by_super_lane(snt))
    put(TBL_SBLK, by_super_lane(sblk))
    put(TBL_CODE, code)
    put(TBL_CLO, clo)
    put(TBL_CHI, chi)
    put(TBL_OBLK, oblk)
    tbl_ref[TBL_UT:TBL_UT + n_tiles, :] = ut.astype(jnp.int32)


def _moe_tables(cum_lo, cum_hi, tot, n_tiles, n_super):
    assert n_super * SUPER <= LANES
    tbl = pl.pallas_call(
        functools.partial(_tables_kernel, n_tiles=n_tiles, n_super=n_super),
        out_shape=jax.ShapeDtypeStruct((TBL_UT + n_tiles, LANES), jnp.int32),
        name="moe_tables",
    )(cum_lo, cum_hi, tot)
    return tbl.reshape(-1)


def _moe_buffers(n_tokens):
    n_super = (n_tokens + SUPER_ROWS - 1) // SUPER_ROWS + MOE_GROUPS
    rows = n_super * SUPER_ROWS
    return (jnp.zeros((rows, D_MODEL), BF16), jnp.zeros((rows, LANES), F32), jnp.zeros((rows, D_MODEL), BF16))


def _moe_layer(xp, xs, bufs, g, modr, wr_t, br, w_gate, w_up, w_down, layer, cond_of_tile):
    hs, cws, ys = bufs
    nt = (xp.shape[0] + xs.shape[0]) // BLK
    h, info_lane, info_tok, cum_lo, cum_hi, tot = _moe_route(xp, xs, g, modr, wr_t, br, cond_of_tile)
    tbl = _moe_tables(cum_lo, cum_hi, tot, nt, hs.shape[0] // SUPER_ROWS)
    hs, cws = _moe_gather(h, info_lane, info_tok, tbl, hs, cws)
    ys = _moe_mlp(hs, cws, w_gate, w_up, w_down, layer, tbl, ys)
    xp, xs = _moe_unsort(xp, xs, info_tok, modr, ys, tbl, cond_of_tile)
    return xp, xs, (hs, cws, ys)


def kernel(x_prompt, x_sample, c, cache_k, cache_v, state_hgrn, c_ctx, norm_g, w_ada, b_ada, hgrn_w_in, hgrn_lb_logits, hgrn_onorm_g, hgrn_w_out, attn_w_qkv, attn_qn_g, attn_kn_g, attn_lambda, attn_subln_g, attn_w_out, moe_w_group, moe_b_group, moe_w_expert, moe_b_expert, moe_w_gate, moe_w_up, moe_w_down):
    n_prompt_seq, seq, d = x_prompt.shape
    dec_batch, dec_seq, _ = x_sample.shape
    n_prompt = n_prompt_seq * seq
    assert d == D_MODEL and seq == BLK and dec_seq % BLK == 0
    assert 1 + dec_batch <= N_COND

    xp = x_prompt.reshape(n_prompt, d)
    xs = x_sample.reshape(dec_batch * dec_seq, d)
    cond = jnp.zeros((N_COND, d), F32).at[0].set(c_ctx).at[1:1 + dec_batch].set(c)
    mod = _modulation(cond, w_ada, b_ada)

    lbs = jnp.cumsum(jax.nn.softmax(hgrn_lb_logits.astype(F32), axis=0), axis=0)
    lbs = lbs - lbs[0:1]

    cond_prompt = lambda b: 0
    cond_sample = lambda b: 1 + b
    npt, spt = n_prompt // BLK, dec_seq // BLK
    cond_tile = lambda i: jnp.where(i < npt, 0, 1 + (i - npt) // spt)

    w_in = hgrn_w_in.astype(BF16)
    w_hout = hgrn_w_out.astype(BF16)
    w_qkv = attn_w_qkv.astype(BF16)
    w_aout = attn_w_out.astype(BF16)
    wr_t = jnp.zeros((DEPTH, ROUTE_ROWS, d), F32)
    wr_t = wr_t.at[:, :MOE_GROUPS].set(moe_w_group.transpose(0, 2, 1))
    wr_t = wr_t.at[:, MOE_GROUPS:MOE_GROUPS + MOE_EXPERTS].set(moe_w_expert.transpose(0, 2, 1)).astype(BF16)
    br = jnp.zeros((DEPTH, ROUTE_ROWS, 1), F32)
    br = br.at[:, :MOE_GROUPS, 0].set(moe_b_group).at[:, MOE_GROUPS:MOE_GROUPS + MOE_EXPERTS, 0].set(moe_b_expert)
    w_gate = moe_w_gate.reshape(DEPTH * MOE_EXPERTS, d, MOE_D_FF)
    w_up = moe_w_up.reshape(DEPTH * MOE_EXPERTS, d, MOE_D_FF)
    w_down = moe_w_down.reshape(DEPTH * MOE_EXPERTS, MOE_D_FF, d)

    sfin, kv = None, None
    bufs = _moe_buffers(xp.shape[0] + xs.shape[0])
    for i in range(DEPTH):
        j = i // 2
        modr = mod[i].reshape(N_COND * N_MOD, 1, d)
        if i % 2 == 0:
            common = (norm_g[i, 0], modr, w_in[j], lbs[j], hgrn_onorm_g[j], w_hout[j])
            xp, sfin = _hgrn_layer(xp, n_prompt_seq, seq, cond_prompt, *common, None, sfin, j)
            xs, _ = _hgrn_layer(xs, dec_batch, dec_seq, cond_sample, *common, state_hgrn[:, j], None, j)
        else:
            common = (norm_g[i, 0], modr, w_qkv[j], attn_qn_g[j], attn_kn_g[j], attn_lambda[j], attn_subln_g[j],
                      w_aout[j], i)
            xp, kv = _attn_layer(xp, n_prompt_seq, seq, cond_prompt, *common, None, None, kv, j)
            xs, _ = _attn_layer(xs, dec_batch, dec_seq, cond_sample, *common, cache_k[:, j], cache_v[:, j], None, j)
        xp, xs, bufs = _moe_layer(xp, xs, bufs, norm_g[i, 1], modr, wr_t[i], br[i], w_gate, w_up, w_down, i, cond_tile)

    new_k = kv[0].reshape(n_prompt_seq, DEPTH // 2, seq, HEADS, 2, QK_DIM)
    new_v = kv[1].reshape(n_prompt_seq, DEPTH // 2, seq, HEADS, HEAD_DIM)
    return (xp.reshape(n_prompt_seq, seq, d), xs.reshape(dec_batch, dec_seq, d), new_k, new_v, sfin)
```

```python
import functools
import math

import numpy as np
import jax
import jax.numpy as jnp
from jax import lax
from jax.experimental import pallas as pl
from jax.experimental.pallas import tpu as pltpu

F32 = jnp.float32
BF16 = jnp.bfloat16

D_MODEL = 1024
DEPTH = 4
GRID_W = 64
HEADS = 8
HEAD_DIM = 128
QK_DIM = 64
ROPE_THETA = 10000.0
MOE_GROUPS = 4
MOE_EPG = 4
MOE_EXPERTS = MOE_GROUPS * MOE_EPG
MOE_D_FF = 512
EPS = 1e-6
N_COND = 8
N_MOD = 6
HGRN_PARTS = 5

LANES = 128
BLK = 256
CHUNK = 32
N_CHUNK = BLK // CHUNK
HEADS_PER_STEP = 2
SCAN_HEADS_PER_BODY = 8
ATTN_HEADS_PER_BODY = 8
SUPER = 12
SUPER_ROWS = SUPER * BLK
STRAIGHT_TILES = 8
TBL_RSTART, TBL_SGROUP, TBL_SNT, TBL_SBLK, TBL_CODE, TBL_CLO, TBL_CHI, TBL_OBLK, TBL_UT = range(9)
CODE_SKIP, CODE_GATHER, CODE_ZERO = 0, 1, 2
EXPERT_TILES = 2
FF_PART = 256
GATHER_WIN = 6
EXP2_CLAMP = 115.0
SCAN_GUARD = 100.0
VMEM_LIMIT = 56 * 1024 * 1024
N_SLOTS = DEPTH // 2
_RESIDENT = dict(pipeline_mode=pl.Buffered(1))


def _cparams(sem):
    return pltpu.CompilerParams(dimension_semantics=sem, vmem_limit_bytes=VMEM_LIMIT)


def _silu(x):
    return x * jax.nn.sigmoid(x)


def _dot(a, b):
    return jnp.dot(a, b, preferred_element_type=F32)


def _dot_nt(a, b):
    return lax.dot_general(a, b, (((1,), (1,)), ((), ())), preferred_element_type=F32)


def _dot_tn(a, b):
    return lax.dot_general(a, b, (((0,), (0,)), ((), ())), preferred_element_type=F32)


def _lane_block(i, width):
    return pl.ds(pl.multiple_of(i * width, width), width)


def _mod_kernel(c_ref, w_ref, b_ref, o_ref):
    o_ref[...] = _dot(_silu(c_ref[...]), w_ref[...]) + b_ref[...]


def _modulation(cond, w_ada, b_ada):
    tn = 1536
    nj = (N_MOD * D_MODEL) // tn
    return pl.pallas_call(
        _mod_kernel,
        grid=(DEPTH, nj),
        in_specs=[
            pl.BlockSpec((N_COND, D_MODEL), lambda l, j: (0, 0)),
            pl.BlockSpec((None, D_MODEL, tn), lambda l, j: (l, 0, j)),
            pl.BlockSpec((None, 1, tn), lambda l, j: (l, 0, j)),
        ],
        out_specs=pl.BlockSpec((None, N_COND, tn), lambda l, j: (l, 0, j)),
        out_shape=jax.ShapeDtypeStruct((DEPTH, N_COND, N_MOD * D_MODEL), F32),
        compiler_params=_cparams(("parallel", "parallel")),
        name="modulation",
    )(cond, w_ada, b_ada.reshape(DEPTH, 1, N_MOD * D_MODEL))


def _norm_mod(x, g, sc, sh):
    ms = jnp.mean(x * x, axis=-1, keepdims=True)
    return (x * lax.rsqrt(ms + EPS) * g) * (1.0 + sc) + sh


def _mod_spec(cond_of_step, which):
    return pl.BlockSpec((None, 1, D_MODEL), lambda i, *_: (cond_of_step(i) * N_MOD + which, 0, 0))


def _scan_constants():
    t = np.arange(BLK)
    out = []
    for rev in (False, True):
        u = (BLK - 1 - t) if rev else t
        ut, us = u[:, None], u[None, :]
        cums = (us <= ut).astype(np.float32)
        lev = np.where(us > ut, 0,
              np.where(ut // CHUNK == us // CHUNK, 1,
              np.where(ut // 64 == us // 64, 2,
              np.where(ut // 128 == us // 128, 3, 4)))).astype(np.int32)
        out += [jnp.asarray(cums, BF16), jnp.asarray(lev)]
    return out


def _rows_to_block(rows, rev):
    order = rows[::-1] if rev else rows
    return jnp.concatenate([jnp.broadcast_to(r, (CHUNK, LANES)) for r in order], axis=0)


def _trunc_bf16(x):
    return lax.bitcast_convert_type(lax.bitcast_convert_type(x, jnp.int32) & jnp.int32(-65536), F32)


def _scan_gates(z, lb, cums):
    sig = jax.nn.sigmoid(z)
    f = lb + (1.0 - lb) * sig
    logf = jnp.log2(f)
    k = (1.0 - lb) * (1.0 - sig)
    hi32 = _trunc_bf16(logf)
    bb = _dot(cums, jnp.concatenate([hi32.astype(BF16), (logf - hi32).astype(BF16)], axis=1))
    return k, bb[:, :LANES] + bb[:, LANES:]


def _scan_att_exact(q, k, b, rev):
    ti = lax.broadcasted_iota(jnp.int32, (BLK, BLK), 0)
    si = lax.broadcasted_iota(jnp.int32, (BLK, BLK), 1)
    ri = lax.broadcasted_iota(jnp.int32, (BLK, 1), 0)
    if rev:
        ti, si, ri = BLK - 1 - ti, BLK - 1 - si, BLK - 1 - ri
    h1 = _trunc_bf16(b)
    h2 = _trunc_bf16(b - h1)
    pieces = jnp.concatenate([h1.astype(BF16), h2.astype(BF16), (b - h1 - h2).astype(BF16)], axis=1)
    att = jnp.where(ti == si, jnp.sum(q * k, axis=-1, keepdims=True), 0.0)
    g = 2
    while g <= BLK:
        h = g // 2
        at_mid = jnp.where(((ti & -g) + (h - 1)) == si, 1.0, 0.0).astype(BF16)
        bb = _dot(at_mid, pieces)
        b_r = bb[:, :LANES] + bb[:, LANES:2 * LANES] + bb[:, 2 * LANES:]
        after = (ri & h) != 0
        qg = jnp.where(after, q * jnp.exp2(jnp.minimum(b - b_r, 0.0)), 0.0).astype(BF16)
        kg = jnp.where(after, 0.0, k * jnp.exp2(jnp.minimum(b_r - b, 0.0))).astype(BF16)
        prod = _dot_nt(qg, kg)
        att = jnp.where((ti & -g) == (si & -g),
                        jnp.where((ti & h) != 0, jnp.where((si & h) == 0, prod, att), att), att)
        g *= 2
    return att


def _scan_prep(q, z, lb, cums, rev, with_inter):
    k, b = _scan_gates(z, lb, cums)

    e_row, m_row = (0, CHUNK // 2) if rev else (CHUNK - 1, CHUNK // 2 - 1)
    ends, mids = [], []
    for j in range(N_CHUNK):
        ends.append(b[j * CHUNK + e_row:j * CHUNK + e_row + 1, :])
        mids.append(b[j * CHUNK + m_row:j * CHUNK + m_row + 1, :])
    if rev:
        ends, mids = ends[::-1], mids[::-1]
    zero = jnp.zeros((1, LANES), F32)
    one = jnp.ones((1, LANES), F32)
    pres = [zero] + ends[:-1]
    b_pre = _rows_to_block(pres, rev)
    b_end = _rows_to_block(ends, rev)
    b_mid = _rows_to_block(mids, rev)

    qd = q * jnp.exp2(b - b_pre)
    ku = k * jnp.exp2(b_end - b)
    qm = q * jnp.exp2(jnp.clip(b - b_mid, -EXP2_CLAMP, EXP2_CLAMP))
    km = k * jnp.exp2(jnp.clip(b_mid - b, -EXP2_CLAMP, EXP2_CLAMP))

    levels = []
    for nc in (2, 4, 8):
        fq, fk = [], []
        for ju in range(N_CHUNK):
            r = (ju // nc) * nc + nc // 2 - 1
            if ju % nc >= nc // 2:
                fq.append(one if nc == 2 else jnp.exp2(pres[ju] - ends[r]))
                fk.append(zero)
            else:
                fq.append(zero)
                fk.append(one if nc == 2 else jnp.exp2(ends[r] - ends[ju]))
        levels.append(((qd * _rows_to_block(fq, rev)).astype(BF16), (ku * _rows_to_block(fk, rev)).astype(BF16)))

    last = ends[-1]
    qh = (qd * _rows_to_block([jnp.exp2(p) for p in pres], rev)).astype(BF16) if with_inter else None
    kh = (ku * _rows_to_block([jnp.exp2(last - e) for e in ends], rev)).astype(BF16)
    risk = functools.reduce(jnp.maximum, [jnp.maximum(p - m, m - e) for p, m, e in zip(pres, mids, ends)])
    return (qm.astype(BF16), km.astype(BF16)), levels, qh, kh, last, risk


def _scan_att(prep, lev):
    att = jnp.where(lev == 1, _dot_nt(*prep[0]), 0.0)
    for level, (ql, kl) in enumerate(prep[1], start=2):
        att = jnp.where(lev == level, _dot_nt(ql, kl), att)
    return att


def _scan_att_bidir(prep_f, prep_b, lev_f, lev_b):
    att = jnp.where(lev_f == 1, _dot_nt(*prep_f[0]), 0.0) + jnp.where(lev_b == 1, _dot_nt(*prep_b[0]), 0.0)
    lev = jnp.maximum(lev_f, lev_b)
    for level, ((qf, kf), (qb, kb)) in enumerate(zip(prep_f[1], prep_b[1]), start=2):
        both = _dot_nt(jnp.concatenate([qf, qb], axis=1), jnp.concatenate([kf, kb], axis=1))
        att = jnp.where(lev == level, both, att)
    return att


def _scan_state(prep, v, st_prev):
    ut = _dot_tn(v.astype(BF16), prep[3])
    return ut if st_prev is None else st_prev * jnp.exp2(prep[4]) + ut


def _scan_finish(o, g, on):
    ms = jnp.mean(o * o, axis=-1, keepdims=True)
    return ((o * lax.rsqrt(ms + EPS) * on) * _silu(g)).astype(BF16)


def _slot_view(ref, slot, owns_all_slots):
    if not owns_all_slots:
        return ref
    for s in range(ref.shape[0]):
        if s != slot:
            ref[s] = jnp.zeros(ref.shape[1:], ref.dtype)
    return ref.at[slot]


def _hgrn_kernel(*refs, n_blocks, has_state, slot, owns_all_slots, hps):
    it = iter(refs)
    x_ref, g_ref, sh_ref, sc_ref, gate_ref, win_ref, lb_ref, on_ref = (next(it) for _ in range(8))
    cf_ref, lf_ref, cb_ref, lvb_ref = (next(it) for _ in range(4))
    s0_ref = next(it) if has_state else None
    wout_ref, xo_ref = next(it), next(it)
    sfin_ref = None if has_state else next(it)
    h_ref, proj_ref, og_ref = next(it), next(it), next(it)
    oacc_ref, st_ref = (next(it), next(it)) if has_state else (None, None)

    h_ref[...] = _norm_mod(x_ref[...], g_ref[...], sc_ref[...], sh_ref[...]).astype(BF16)
    on = on_ref[...]
    if not has_state:
        sfin_ref = _slot_view(sfin_ref, slot, owns_all_slots)

    group_w = hps * HEAD_DIM

    def part(i, p, rows=slice(None)):
        return proj_ref[rows, p * group_w + i * HEAD_DIM:p * group_w + (i + 1) * HEAD_DIM]

    def pair(hp, carry):
        for p in range(HGRN_PARTS):
            cols = pl.ds(pl.multiple_of(p * D_MODEL + hp * group_w, group_w), group_w)
            proj_ref[:, p * group_w:(p + 1) * group_w] = _dot(h_ref[...], win_ref[:, cols])
        heads = [hp * hps + i for i in range(hps)]
        lbs = [lb_ref[:, _lane_block(hd, HEAD_DIM)] for hd in heads]

        def dyn_part(i, p, rows=slice(None)):
            return proj_ref[rows, pl.ds(pl.multiple_of(p * group_w + i * HEAD_DIM, HEAD_DIM), HEAD_DIM)]

        if not has_state:
            risk = jnp.zeros((1, LANES), F32)
            for i, hd in enumerate(heads):
                q, v = part(i, 0), part(i, 1)
                prep_f = _scan_prep(q, part(i, 2), lbs[i][0:1, :], cf_ref[...], False, False)
                prep_b = _scan_prep(q, part(i, 3), lbs[i][1:2, :], cb_ref[...], True, False)
                risk = jnp.maximum(risk, jnp.maximum(prep_f[5], prep_b[5]))
                att = _scan_att_bidir(prep_f, prep_b, lf_ref[...], lvb_ref[...])
                o = _dot(att.astype(BF16), v.astype(BF16))
                sfin_ref[0, hd] = _scan_state(prep_f, v, None).T
                sfin_ref[1, hd] = _scan_state(prep_b, v, None).T
                og_ref[:, _lane_block(hd, HEAD_DIM)] = _scan_finish(o, part(i, 4), on)

            @pl.when(jnp.max(risk) > SCAN_GUARD)
            def _():
                def exact_head(i, c2):
                    hd = hp * hps + i
                    lb = lb_ref[:, _lane_block(hd, HEAD_DIM)]
                    q = dyn_part(i, 0)
                    k_f, b_f = _scan_gates(dyn_part(i, 2), lb[0:1, :], cf_ref[...])
                    k_b, b_b = _scan_gates(dyn_part(i, 3), lb[1:2, :], cb_ref[...])
                    att = _scan_att_exact(q, k_f, b_f, False) + _scan_att_exact(q, k_b, b_b, True)
                    o = _dot(att.astype(BF16), dyn_part(i, 1).astype(BF16))
                    og_ref[:, _lane_block(hd, HEAD_DIM)] = _scan_finish(o, dyn_part(i, 4), on)
                    return c2

                lax.fori_loop(0, hps, exact_head, 0)
        else:
            def one_head(i, hd, lb, tb, exact, get):
                rf = pl.ds(pl.multiple_of(tb * BLK, BLK), BLK)
                rb = pl.ds(pl.multiple_of((n_blocks - 1 - tb) * BLK, BLK), BLK)
                cols = (slice(i * HEAD_DIM, (i + 1) * HEAD_DIM) if isinstance(i, int)
                        else pl.ds(pl.multiple_of(i * HEAD_DIM, HEAD_DIM), HEAD_DIM))
                q_f, q_b, v_f, v_b = get(i, 0, rf), get(i, 0, rb), get(i, 1, rf), get(i, 1, rb)
                prep_f = _scan_prep(q_f, get(i, 2, rf), lb[0:1, :], cf_ref[...], False, True)
                prep_b = _scan_prep(q_b, get(i, 3, rb), lb[1:2, :], cb_ref[...], True, True)
                if exact:
                    k_f, b_f = _scan_gates(get(i, 2, rf), lb[0:1, :], cf_ref[...])
                    k_b, b_b = _scan_gates(get(i, 3, rb), lb[1:2, :], cb_ref[...])
                    att_f = _scan_att_exact(q_f, k_f, b_f, False)
                    att_b = _scan_att_exact(q_b, k_b, b_b, True)
                else:
                    att_f, att_b = _scan_att(prep_f, lf_ref[...]), _scan_att(prep_b, lvb_ref[...])
                st_f, st_b = st_ref[2 * i], st_ref[2 * i + 1]
                of = _dot(att_f.astype(BF16), v_f.astype(BF16)) + _dot_nt(prep_f[2], st_f.astype(BF16))
                ob = _dot(att_b.astype(BF16), v_b.astype(BF16)) + _dot_nt(prep_b[2], st_b.astype(BF16))
                st_ref[2 * i] = _scan_state(prep_f, v_f, st_f)
                st_ref[2 * i + 1] = _scan_state(prep_b, v_b, st_b)

                @pl.when(2 * tb < n_blocks)
                def _():
                    oacc_ref[rf, cols] = of
                    oacc_ref[rb, cols] = ob

                @pl.when(2 * tb >= n_blocks)
                def _():
                    oacc_ref[rf, cols] += of
                    oacc_ref[rb, cols] += ob

                return jnp.maximum(prep_f[5], prep_b[5])

            for i, hd in enumerate(heads):
                st_ref[2 * i] = s0_ref[0, hd].T
                st_ref[2 * i + 1] = s0_ref[1, hd].T

            def body(tb, risk):
                for i, hd in enumerate(heads):
                    risk = jnp.maximum(risk, one_head(i, hd, lbs[i], tb, False, part))
                return risk

            risk = lax.fori_loop(0, n_blocks, body, jnp.zeros((1, LANES), F32))

            @pl.when(jnp.max(risk) > SCAN_GUARD)
            def _():
                def exact_head(i, c2):
                    hd = hp * hps + i
                    st_ref[2 * i] = s0_ref[0, hd].T
                    st_ref[2 * i + 1] = s0_ref[1, hd].T
                    lb = lb_ref[:, _lane_block(hd, HEAD_DIM)]

                    def exact_body(tb, c3):
                        one_head(i, hd, lb, tb, True, dyn_part)
                        return c3

                    lax.fori_loop(0, n_blocks, exact_body, 0)
                    return c2

                lax.fori_loop(0, hps, exact_head, 0)

            for i, hd in enumerate(heads):
                cols = slice(i * HEAD_DIM, (i + 1) * HEAD_DIM)
                og_ref[:, _lane_block(hd, HEAD_DIM)] = _scan_finish(oacc_ref[:, cols], part(i, 4), on)
        return carry

    lax.fori_loop(0, HEADS // hps, pair, 0)
    xo_ref[...] = x_ref[...] + gate_ref[...] * _dot(og_ref[...], wout_ref[...])


def _hgrn_layer(x, n_seq, t, cond_of_seq, g, modr, w_in, lbs_j, onorm_g, w_out, s0, sfin_prev, slot):
    d = D_MODEL
    has_state = s0 is not None
    n_blocks = t // BLK
    assert n_blocks == 1 or n_blocks % 2 == 0
    full = lambda shape, **kw: pl.BlockSpec(shape, lambda b, *_: (0,) * len(shape), **kw)
    in_specs = [
        pl.BlockSpec((t, d), lambda b: (b, 0)),
        full((1, d)),
        _mod_spec(cond_of_seq, 0), _mod_spec(cond_of_seq, 1), _mod_spec(cond_of_seq, 2),
        full((d, HGRN_PARTS * d), **_RESIDENT),
        full((2, d)),
        full((1, HEAD_DIM)),
    ] + [full((BLK, BLK))] * 4
    args = [x, g.reshape(1, d), modr, modr, modr, w_in, lbs_j, onorm_g.reshape(1, HEAD_DIM)] + _scan_constants()
    state_block = (None, 2, HEADS, HEAD_DIM, HEAD_DIM)
    if has_state:
        in_specs.append(pl.BlockSpec(state_block, lambda b: (b, 0, 0, 0, 0)))
        args.append(s0)
    in_specs.append(full((d, d), **_RESIDENT))
    args.append(w_out)
    out_specs = [pl.BlockSpec((t, d), lambda b: (b, 0))]
    out_shape = [jax.ShapeDtypeStruct((n_seq * t, d), F32)]
    hps = HEADS_PER_STEP if has_state else SCAN_HEADS_PER_BODY
    group_w = hps * HEAD_DIM
    scratch = [pltpu.VMEM((t, d), BF16), pltpu.VMEM((t, HGRN_PARTS * group_w), F32), pltpu.VMEM((t, d), BF16)]
    aliases = {}
    if has_state:
        scratch += [pltpu.VMEM((t, group_w), F32), pltpu.VMEM((2 * hps, HEAD_DIM, HEAD_DIM), F32)]
    else:
        state_dims = (2, HEADS, HEAD_DIM, HEAD_DIM)
        if sfin_prev is None:
            out_specs.append(pl.BlockSpec((None, N_SLOTS) + state_dims, lambda b: (b, 0, 0, 0, 0, 0)))
        else:
            out_specs.append(pl.BlockSpec((None, None) + state_dims, lambda b: (b, slot, 0, 0, 0, 0)))
            in_specs.append(pl.BlockSpec(memory_space=pl.ANY))
            args.append(sfin_prev)
            aliases = {len(args) - 1: 1}
        out_shape.append(jax.ShapeDtypeStruct((n_seq, N_SLOTS) + state_dims, F32))

    def body(*refs):
        if sfin_prev is not None:
            n_in = len(args)
            refs = refs[:n_in - 1] + refs[n_in:]
        _hgrn_kernel(*refs, n_blocks=n_blocks, has_state=has_state, slot=slot, owns_all_slots=sfin_prev is None,
                     hps=hps)

    out = pl.pallas_call(
        body,
        grid=(n_seq,),
        in_specs=in_specs,
        out_specs=out_specs,
        out_shape=out_shape,
        scratch_shapes=scratch,
        input_output_aliases=aliases,
        compiler_params=_cparams(("parallel",)),
        name="hgrn_layer_sample" if has_state else "hgrn_layer_prompt",
    )(*args)
    return out if not has_state else (out[0], None)


def _rope_tables(t_lat):
    rows = t_lat // GRID_W
    row = jnp.repeat(jnp.arange(rows), GRID_W).astype(F32)
    col = jnp.tile(jnp.arange(GRID_W), rows).astype(F32)
    half = QK_DIM // 2
    inv_freq = ROPE_THETA ** (-jnp.arange(0, half, 2, dtype=F32) / half)
    ang_row = row[:, None] * inv_freq
    ang_col = col[:, None] * inv_freq

    def part(ang):
        c, s = jnp.cos(ang), jnp.sin(ang)
        return jnp.concatenate([c, c], axis=1), jnp.concatenate([-s, s], axis=1)

    cr, sr = part(ang_row)
    cc, sc = part(ang_col)
    cos64 = jnp.concatenate([cr, cc], axis=1)
    sin64 = jnp.concatenate([sr, sc], axis=1)
    return jnp.concatenate([cos64, cos64], axis=1), jnp.concatenate([sin64, sin64], axis=1)


def _rope(x, cos, sin):
    lane = lax.broadcasted_iota(jnp.int32, x.shape, 1)
    first = (lane % (QK_DIM // 2)) < (QK_DIM // 4)
    swapped = jnp.where(first, pltpu.roll(x, LANES - QK_DIM // 4, 1), pltpu.roll(x, QK_DIM // 4, 1))
    return x * cos + swapped * sin


def _attn_kernel(*refs, n_qblk, rope, cache, emit_kv, lam_init, slot, owns_all_slots):
    it = iter(refs)
    x_ref, g_ref, sh_ref, sc_ref, gate_ref, wqkv_ref = (next(it) for _ in range(6))
    qg_ref, kg_ref, lam_ref, sg_ref, seg_ref = (next(it) for _ in range(5))
    cos_ref = sin_ref = ck_ref = cv_ref = nk_ref = nv_ref = None
    if rope:
        cos_ref, sin_ref = next(it), next(it)
    if cache:
        ck_ref, cv_ref = next(it), next(it)
    wout_ref, xo_ref = next(it), next(it)
    if emit_kv:
        nk_ref = _slot_view(next(it), slot, owns_all_slots)
        nv_ref = _slot_view(next(it), slot, owns_all_slots)
    h_ref, qkv_ref, oa_ref = next(it), next(it), next(it)

    d = D_MODEL
    h_ref[...] = _norm_mod(x_ref[...], g_ref[...], sc_ref[...], sh_ref[...]).astype(BF16)
    qkv_ref[...] = _dot(h_ref[...], wqkv_ref[...])
    seg = seg_ref[...]
    lp = lam_ref[...]
    lam = (jnp.exp(jnp.sum(lp[0:1, :] * lp[1:2, :], axis=-1, keepdims=True))
           - jnp.exp(jnp.sum(lp[2:3, :] * lp[3:4, :], axis=-1, keepdims=True)) + lam_init)
    lane = lax.broadcasted_iota(jnp.int32, (1, LANES), 1)
    comp0 = lane < QK_DIM
    scale = QK_DIM ** -0.5

    def split(a):
        return [jnp.where(comp0, a, 0.0).astype(BF16), jnp.where(comp0, 0.0, a).astype(BF16)]

    def one_head(hd):
        hcol = _lane_block(hd, HEAD_DIM)
        k = qkv_ref[:, pl.ds(pl.multiple_of(d + hd * HEAD_DIM, HEAD_DIM), HEAD_DIM)]
        v = qkv_ref[:, pl.ds(pl.multiple_of(2 * d + hd * HEAD_DIM, HEAD_DIM), HEAD_DIM)]
        kn = k * lax.rsqrt(_dot(k * k, seg) + EPS) * kg_ref[...]
        if emit_kv:
            nk_ref[:, hcol] = kn
            nv_ref[:, hcol] = v
        if rope:
            kn = _rope(kn, cos_ref[...], sin_ref[...])
        ks = split(kn)
        vb = v.astype(BF16)
        if cache:
            cks = split(ck_ref[:, hcol])
            cvb = cv_ref[:, hcol].astype(BF16)
        for qi in range(n_qblk):
            rows = slice(qi * BLK, (qi + 1) * BLK)
            q = qkv_ref[rows, hcol]
            qn = q * lax.rsqrt(_dot(q * q, seg) + EPS) * qg_ref[...]
            if rope:
                qn = _rope(qn, cos_ref[rows, :], sin_ref[rows, :])
            qb = (qn * scale).astype(BF16)
            a_self, a_cache = None, None
            for c in range(2):
                s = _dot_nt(qb, ks[c])
                m = jnp.max(s, axis=-1, keepdims=True)
                if cache:
                    sc = _dot_nt(qb, cks[c])
                    m = jnp.maximum(m, jnp.max(sc, axis=-1, keepdims=True))
                p = jnp.exp(s - m)
                den = jnp.sum(p, axis=-1, keepdims=True)
                if cache:
                    pc = jnp.exp(sc - m)
                    den = den + jnp.sum(pc, axis=-1, keepdims=True)
                w = (1.0 / den) if c == 0 else (-lam / den)
                a_self = p * w if c == 0 else a_self + p * w
                if cache:
                    a_cache = pc * w if c == 0 else a_cache + pc * w
            o = _dot(a_self.astype(BF16), vb)
            if cache:
                o = o + _dot(a_cache.astype(BF16), cvb)
            ms = jnp.mean(o * o, axis=-1, keepdims=True)
            o = (o * lax.rsqrt(ms + EPS) * sg_ref[...]) * (1.0 - lam_init)
            oa_ref[rows, hcol] = o.astype(BF16)

    heads_per_body = ATTN_HEADS_PER_BODY if n_qblk == 1 else HEADS_PER_STEP

    def group(hg, carry):
        for i in range(heads_per_body):
            one_head(hg * heads_per_body + i)
        return carry

    lax.fori_loop(0, HEADS // heads_per_body, group, 0)
    xo_ref[...] = x_ref[...] + gate_ref[...] * _dot(oa_ref[...], wout_ref[...])


def _attn_layer(x, n_seq, t, cond_of_seq, g, modr, w_qkv, qn_g, kn_g, lam_p, subln_g, w_out, layer_idx,
                cache_k_j, cache_v_j, kv_prev, slot):
    d = D_MODEL
    cache = cache_k_j is not None
    lam_init = 0.8 - 0.6 * math.exp(-0.3 * layer_idx)
    qg = jnp.tile(qn_g.reshape(1, QK_DIM), (1, 2))
    kg = jnp.tile(kn_g.reshape(1, QK_DIM), (1, 2))
    li = np.arange(LANES)
    seg = jnp.asarray((li[:, None] // QK_DIM == li[None, :] // QK_DIM).astype(np.float32) / QK_DIM)
    full = lambda shape, **kw: pl.BlockSpec(shape, lambda b, *_: (0,) * len(shape), **kw)
    in_specs = [
        pl.BlockSpec((t, d), lambda b: (b, 0)),
        full((1, d)),
        _mod_spec(cond_of_seq, 0), _mod_spec(cond_of_seq, 1), _mod_spec(cond_of_seq, 2),
        full((d, 3 * d), **_RESIDENT),
        full((1, LANES)), full((1, LANES)), full((4, QK_DIM)), full((1, HEAD_DIM)), full((LANES, LANES)),
    ]
    args = [x, g.reshape(1, d), modr, modr, modr, w_qkv, qg, kg, lam_p, subln_g.reshape(1, HEAD_DIM), seg]
    if cache:
        cos, sin = _rope_tables(t)
        past = cache_k_j.shape[1]
        in_specs += [full((t, LANES)), full((t, LANES)),
                     pl.BlockSpec((None, past, d), lambda b: (b, 0, 0)),
                     pl.BlockSpec((None, past, d), lambda b: (b, 0, 0))]
        args += [cos, sin, cache_k_j.reshape(n_seq, past, d), cache_v_j.reshape(n_seq, past, d)]
    in_specs.append(full((d, d), **_RESIDENT))
    args.append(w_out)
    out_specs = [pl.BlockSpec((t, d), lambda b: (b, 0))]
    out_shape = [jax.ShapeDtypeStruct((n_seq * t, d), F32)]
    aliases = {}
    n_carried = 0
    if not cache:
        if kv_prev is None:
            kv_spec = pl.BlockSpec((None, N_SLOTS, t, d), lambda b: (b, 0, 0, 0))
        else:
            kv_spec = pl.BlockSpec((None, None, t, d), lambda b: (b, slot, 0, 0))
            in_specs += [pl.BlockSpec(memory_space=pl.ANY)] * 2
            args += list(kv_prev)
            aliases = {len(args) - 2: 1, len(args) - 1: 2}
            n_carried = 2
        out_specs += [kv_spec, kv_spec]
        out_shape += [jax.ShapeDtypeStruct((n_seq, N_SLOTS, t, d), F32)] * 2

    def body(*refs):
        n_in = len(args)
        refs = refs[:n_in - n_carried] + refs[n_in:]
        _attn_kernel(*refs, n_qblk=t // BLK, rope=cache, cache=cache, emit_kv=not cache, lam_init=lam_init,
                     slot=slot, owns_all_slots=kv_prev is None)

    out = pl.pallas_call(
        body,
        grid=(n_seq,),
        in_specs=in_specs,
        out_specs=out_specs,
        out_shape=out_shape,
        scratch_shapes=[pltpu.VMEM((t, d), BF16), pltpu.VMEM((t, 3 * d), F32), pltpu.VMEM((t, d), BF16)],
        input_output_aliases=aliases,
        compiler_params=_cparams(("parallel",)),
        name="attn_layer_sample" if cache else "attn_layer_prompt",
    )(*args)
    return (out[0], None) if cache else (out[0], (out[1], out[2]))


ROUTE_ROWS = 32
INFO_GID, INFO_RANK = 8, 9


def _two_stream_specs(n_prompt_tiles, width):
    return [pl.BlockSpec((BLK, width), lambda i, *_: (jnp.minimum(i, n_prompt_tiles - 1), 0)),
            pl.BlockSpec((BLK, width), lambda i, *_: (jnp.maximum(i - n_prompt_tiles, 0), 0))]


def _route_kernel(xp_ref, xs_ref, g_ref, sh_ref, sc_ref, wr_ref, br_ref, tri_ref,
                  h_ref, il_ref, it_ref, cum_ref, cumhi_ref, tot_ref, carry_ref, *, n_prompt_tiles):
    i = pl.program_id(0)

    @pl.when(i == 0)
    def _():
        carry_ref[...] = jnp.zeros_like(carry_ref)

    @pl.when(i < n_prompt_tiles)
    def _():
        h_ref[...] = _norm_mod(xp_ref[...], g_ref[...], sc_ref[...], sh_ref[...]).astype(BF16)

    @pl.when(i >= n_prompt_tiles)
    def _():
        h_ref[...] = _norm_mod(xs_ref[...], g_ref[...], sc_ref[...], sh_ref[...]).astype(BF16)

    logit = _dot_nt(wr_ref[...], h_ref[...]) + br_ref[...]
    gl = [logit[g:g + 1, :] for g in range(MOE_GROUPS)]
    gmax = functools.reduce(jnp.maximum, gl)
    gz = functools.reduce(lambda a, b: a + b, [jnp.exp(x - gmax) for x in gl])
    g_w = 1.0 / gz
    gid = jnp.full_like(gmax, MOE_GROUPS - 1)
    for g in range(MOE_GROUPS - 2, -1, -1):
        gid = jnp.where(gl[g] == gmax, float(g), gid)
    el = []
    for j in range(MOE_EPG):
        e = logit[MOE_GROUPS + j:MOE_GROUPS + j + 1, :]
        for g in range(1, MOE_GROUPS):
            r = MOE_GROUPS + g * MOE_EPG + j
            e = jnp.where(gid == float(g), logit[r:r + 1, :], e)
        el.append(e)
    emax = functools.reduce(jnp.maximum, el)
    pe = [jnp.exp(e - emax) for e in el]
    idx1 = jnp.full_like(emax, MOE_EPG - 1)
    for j in range(MOE_EPG - 2, -1, -1):
        idx1 = jnp.where(el[j] == emax, float(j), idx1)
    el2 = [jnp.where(idx1 == float(j), -jnp.inf, el[j]) for j in range(MOE_EPG)]
    emax2 = functools.reduce(jnp.maximum, el2)
    idx2 = jnp.full_like(emax, MOE_EPG - 1)
    for j in range(MOE_EPG - 2, -1, -1):
        idx2 = jnp.where(el2[j] == emax2, float(j), idx2)
    sel = [(idx1 == float(j)) | (idx2 == float(j)) for j in range(MOE_EPG)]
    den = functools.reduce(lambda a, b: a + b, [jnp.where(sel[j], pe[j], 0.0) for j in range(MOE_EPG)])
    cw = [jnp.where(sel[j], pe[j] * (g_w / den), 0.0) for j in range(MOE_EPG)]

    row8 = lax.broadcasted_iota(jnp.int32, (8, BLK), 0)
    onehot = jnp.where(row8.astype(F32) == gid, 1.0, 0.0)
    within = _dot(onehot.astype(BF16), tri_ref[...])
    carry = carry_ref[...]
    rank = jnp.sum(onehot * (within + carry[:, 0:1]), axis=0, keepdims=True)
    cum_ref[...] = carry
    new_carry = carry + jnp.sum(onehot, axis=1, keepdims=True)
    carry_ref[...] = new_carry
    cumhi_ref[...] = new_carry
    tot_ref[...] = new_carry

    il_ref[...] = jnp.where(row8 == 0, gid, jnp.where(row8 == 1, rank, 0.0))
    rowl = lax.broadcasted_iota(jnp.int32, (LANES, BLK), 0)
    m = jnp.zeros((LANES, BLK), F32)
    for j in range(MOE_EPG):
        hi = cw[j].astype(BF16).astype(F32)
        m = jnp.where(rowl == j, hi, m)
        m = jnp.where(rowl == MOE_EPG + j, cw[j] - hi, m)
    m = jnp.where(rowl == INFO_GID, gid, m)
    m = jnp.where(rowl == INFO_RANK, rank, m)
    it_ref[...] = m.T


def _moe_route(xp, xs, g, modr, wr_t, br, cond_of_tile):
    d = D_MODEL
    npt = xp.shape[0] // BLK
    nt = npt + xs.shape[0] // BLK
    n = nt * BLK
    tri = jnp.asarray(np.triu(np.ones((BLK, BLK), np.float32), 1), BF16)
    full = lambda shape: pl.BlockSpec(shape, lambda i: (0,) * len(shape))
    return pl.pallas_call(
        functools.partial(_route_kernel, n_prompt_tiles=npt),
        grid=(nt,),
        in_specs=_two_stream_specs(npt, d) + [
            full((1, d)),
            _mod_spec(cond_of_tile, 3), _mod_spec(cond_of_tile, 4),
            full((ROUTE_ROWS, d)), full((ROUTE_ROWS, 1)), full((BLK, BLK)),
        ],
        out_specs=[
            pl.BlockSpec((BLK, d), lambda i: (i, 0)),
            pl.BlockSpec((8, BLK), lambda i: (0, i)),
            pl.BlockSpec((BLK, LANES), lambda i: (i, 0)),
            pl.BlockSpec((None, 8, LANES), lambda i: (i, 0, 0)),
            pl.BlockSpec((None, 8, LANES), lambda i: (i, 0, 0)),
            pl.BlockSpec((8, LANES), lambda i: (0, 0)),
        ],
        out_shape=[
            jax.ShapeDtypeStruct((n, d), BF16),
            jax.ShapeDtypeStruct((8, n), F32),
            jax.ShapeDtypeStruct((n, LANES), F32),
            jax.ShapeDtypeStruct((nt, 8, LANES), F32),
            jax.ShapeDtypeStruct((nt, 8, LANES), F32),
            jax.ShapeDtypeStruct((8, LANES), F32),
        ],
        scratch_shapes=[pltpu.VMEM((8, LANES), F32)],
        compiler_params=_cparams(("arbitrary",)),
        name="moe_route",
    )(xp, xs, g.reshape(1, d), modr, modr, wr_t, br, tri)


def _sorted_pos(gid, rank, rstart_ref):
    p = rank
    for g in range(MOE_GROUPS):
        p = p + jnp.where(gid == float(g), rstart_ref[g].astype(F32), 0.0)
    return p


def _gather_kernel(tbl_ref, h_ref, il_ref, it_ref, hs_ref, cws_ref, *, n_tiles):
    a = pl.program_id(0)
    win = GATHER_WIN * BLK
    code = tbl_ref[TBL_CODE * LANES + a]
    rstart_ref = tbl_ref

    @pl.when(code == CODE_ZERO)
    def _():
        hs_ref[...] = jnp.zeros_like(hs_ref)
        cws_ref[...] = jnp.zeros_like(cws_ref)

    @pl.when(code == CODE_GATHER)
    def _():
        dest = (lax.broadcasted_iota(jnp.int32, (BLK, 1), 0) + a * BLK).astype(F32)
        src_tile = lax.broadcasted_iota(jnp.int32, (1, win), 1) // BLK
        clo = tbl_ref[TBL_CLO * LANES + a]
        n_win = (tbl_ref[TBL_CHI * LANES + a] - clo + GATHER_WIN) // GATHER_WIN

        def window(w):
            first = clo + w * GATHER_WIN
            c0 = jnp.minimum(first, n_tiles - GATHER_WIN)
            rows = pl.ds(pl.multiple_of(c0 * BLK, BLK), win)
            info = il_ref[:, rows]
            p = _sorted_pos(info[0:1, :], info[1:2, :], rstart_ref)
            p = jnp.where(src_tile + c0 >= first, p, -1.0)
            onehot = jnp.where(dest == p, 1.0, 0.0).astype(BF16)
            r = _dot(onehot, it_ref[rows, :].astype(BF16))
            return _dot(onehot, h_ref[rows, :]), r + pltpu.roll(r, LANES - MOE_EPG, 1)

        dh, dc = window(0)
        hs_ref[...] = dh.astype(BF16)
        cws_ref[...] = dc

        def body(w, carry):
            dh, dc = window(w)
            hs_ref[...] = (hs_ref[...].astype(F32) + dh).astype(BF16)
            cws_ref[...] += dc
            return carry

        lax.fori_loop(1, n_win, body, 0)


def _moe_gather(h, info_lane, info_tok, tbl, hs_buf, cws_buf):
    n, d = h.shape
    nt = n // BLK
    n_dest_tiles = hs_buf.shape[0] // BLK
    assert nt >= GATHER_WIN and n_dest_tiles <= LANES
    out_block = lambda a, tbl_r: (tbl_r[TBL_OBLK * LANES + a], 0)
    carried = pl.BlockSpec(memory_space=pl.ANY)

    def body(tbl_ref, h_ref, il_ref, it_ref, hs_old, cws_old, hs_ref, cws_ref):
        _gather_kernel(tbl_ref, h_ref, il_ref, it_ref, hs_ref, cws_ref, n_tiles=nt)

    return pl.pallas_call(
        body,
        grid_spec=pltpu.PrefetchScalarGridSpec(
            num_scalar_prefetch=1,
            grid=(n_dest_tiles,),
            in_specs=[
                pl.BlockSpec((n, d), lambda a, *_: (0, 0), **_RESIDENT),
                pl.BlockSpec((8, n), lambda a, *_: (0, 0), **_RESIDENT),
                pl.BlockSpec((n, LANES), lambda a, *_: (0, 0), **_RESIDENT),
                carried, carried,
            ],
            out_specs=[pl.BlockSpec((BLK, d), out_block), pl.BlockSpec((BLK, LANES), out_block)],
        ),
        out_shape=[jax.ShapeDtypeStruct(hs_buf.shape, BF16), jax.ShapeDtypeStruct(cws_buf.shape, F32)],
        input_output_aliases={4: 0, 5: 1},
        compiler_params=_cparams(("arbitrary",)),
        name="moe_gather",
    )(tbl, h, info_lane, info_tok, hs_buf, cws_buf)


def _moe_mlp_kernel(tbl_ref, hs_ref, cws_ref, wg_ref, wu_ref, wd_ref, ys_ref,
                    acc_ref, wgb_ref, wub_ref, wdb_ref):
    s = pl.program_id(0)
    k = pl.program_id(1)
    nt = tbl_ref[TBL_SNT * LANES + s]

    @pl.when(nt > 0)
    def _():
        wgb_ref[...] = wg_ref[...].astype(BF16)
        wub_ref[...] = wu_ref[...].astype(BF16)
        wdb_ref[...] = wd_ref[...].astype(BF16)

    @pl.when((s == 0) & (k == 0))
    def _():
        acc_ref[...] = jnp.zeros_like(acc_ref)

    def skipped(j):
        rows = slice(j * EXPERT_TILES * BLK, (j + 1) * EXPERT_TILES * BLK)
        ys_ref[rows, :] = jnp.zeros((EXPERT_TILES * BLK, D_MODEL), BF16)

    def block(j):
        rows = slice(j * EXPERT_TILES * BLK, (j + 1) * EXPERT_TILES * BLK)
        hsub = hs_ref[rows, :]
        cws = cws_ref[rows, :]
        cwk = jnp.zeros((EXPERT_TILES * BLK, 1), F32)
        for kk in range(MOE_EPG):
            cwk = jnp.where(k == kk, cws[:, kk:kk + 1], cwk)
        y = None
        for fh in range(MOE_D_FF // FF_PART):
            fc = slice(fh * FF_PART, (fh + 1) * FF_PART)
            gate = _dot(hsub, wgb_ref[:, fc])
            up = _dot(hsub, wub_ref[:, fc])
            act = ((_silu(gate) * up) * cwk).astype(BF16)
            part = _dot(act, wdb_ref[fc, :])
            y = part if y is None else y + part
        total = jnp.where(k == 0, 0.0, acc_ref[rows, :]) + y
        acc_ref[rows, :] = total
        ys_ref[rows, :] = total.astype(BF16)

    def maybe_block(j):
        pl.when(j * EXPERT_TILES < nt)(functools.partial(block, j))
        pl.when((j * EXPERT_TILES >= nt) & (nt > 0) & (k == 0))(functools.partial(skipped, j))

    n_blocks = SUPER // EXPERT_TILES
    n_straight = STRAIGHT_TILES // EXPERT_TILES

    @pl.when(nt >= STRAIGHT_TILES)
    def _():
        for j in range(n_straight):
            block(j)

    @pl.when(nt < STRAIGHT_TILES)
    def _():
        for j in range(n_straight):
            maybe_block(j)

    for j in range(n_straight, n_blocks):
        maybe_block(j)


def _moe_mlp(hs, cws, w_gate, w_up, w_down, layer, tbl, ys_buf):
    d = hs.shape[1]
    n_super = hs.shape[0] // SUPER_ROWS

    def body(tbl_ref, hs_ref, cws_ref, wg_ref, wu_ref, wd_ref, ys_old, ys_ref, *scratch):
        _moe_mlp_kernel(tbl_ref, hs_ref, cws_ref, wg_ref, wu_ref, wd_ref, ys_ref, *scratch)

    def widx(s, k, tbl_r):
        kk = jnp.where(tbl_r[TBL_SNT * LANES + s] > 0, k, MOE_EPG - 1)
        return (layer * MOE_EXPERTS + tbl_r[TBL_SGROUP * LANES + s] * MOE_EPG + kk, 0, 0)

    rows_idx = lambda s, k, tbl_r: (tbl_r[TBL_SBLK * LANES + s], 0)
    return pl.pallas_call(
        body,
        grid_spec=pltpu.PrefetchScalarGridSpec(
            num_scalar_prefetch=1,
            grid=(n_super, MOE_EPG),
            in_specs=[
                pl.BlockSpec((SUPER_ROWS, d), rows_idx),
                pl.BlockSpec((SUPER_ROWS, LANES), rows_idx),
                pl.BlockSpec((None, d, MOE_D_FF), widx),
                pl.BlockSpec((None, d, MOE_D_FF), widx),
                pl.BlockSpec((None, MOE_D_FF, d), widx),
                pl.BlockSpec(memory_space=pl.ANY),
            ],
            out_specs=pl.BlockSpec((SUPER_ROWS, d), rows_idx),
            scratch_shapes=[
                pltpu.VMEM((SUPER_ROWS, d), F32),
                pltpu.VMEM((d, MOE_D_FF), BF16),
                pltpu.VMEM((d, MOE_D_FF), BF16),
                pltpu.VMEM((MOE_D_FF, d), BF16),
            ],
        ),
        out_shape=jax.ShapeDtypeStruct(ys_buf.shape, BF16),
        input_output_aliases={6: 0},
        compiler_params=_cparams(("arbitrary", "arbitrary")),
        name="moe_experts",
    )(tbl, hs, cws, w_gate, w_up, w_down, ys_buf)


N_SRC = 2 * MOE_GROUPS


def _unsort_kernel(tbl_ref, xp_ref, xs_ref, it_ref, gate_ref, *rest, n_prompt_tiles):
    ys_refs, op_ref, os_ref = rest[:N_SRC], rest[N_SRC], rest[N_SRC + 1]
    t = pl.program_id(0)
    info = it_ref[...]
    p = _sorted_pos(info[:, INFO_GID:INFO_GID + 1], info[:, INFO_RANK:INFO_RANK + 1], tbl_ref)
    lane = lax.broadcasted_iota(jnp.int32, (1, BLK), 1).astype(F32)
    slot = lambda m: tbl_ref[(TBL_UT + t) * LANES + m]

    def take(m):
        a = slot(m)
        onehot = jnp.where(p - (a * BLK).astype(F32) == lane, 1.0, 0.0).astype(BF16)
        return _dot(onehot, ys_refs[m][...])

    def stream(x_ref, o_ref):
        first = functools.reduce(lambda u, w: u + w, [take(m) for m in range(0, N_SRC, 2)])
        o_ref[...] = x_ref[...] + gate_ref[...] * first
        for m in range(1, N_SRC, 2):
            @pl.when(slot(m) >= 0)
            def _():
                o_ref[...] += gate_ref[...] * take(m)

    pl.when(t < n_prompt_tiles)(functools.partial(stream, xp_ref, op_ref))
    pl.when(t >= n_prompt_tiles)(functools.partial(stream, xs_ref, os_ref))


def _moe_unsort(xp, xs, info_tok, modr, ys, tbl, cond_of_tile):
    d = D_MODEL
    npt = xp.shape[0] // BLK
    nt = npt + xs.shape[0] // BLK

    def ys_spec(m):
        return pl.BlockSpec((BLK, d), lambda t, tbl_r: (jnp.maximum(tbl_r[(TBL_UT + t) * LANES + m], 0), 0))

    return pl.pallas_call(
        functools.partial(_unsort_kernel, n_prompt_tiles=npt),
        grid_spec=pltpu.PrefetchScalarGridSpec(
            num_scalar_prefetch=1,
            grid=(nt,),
            in_specs=_two_stream_specs(npt, d) + [
                pl.BlockSpec((BLK, LANES), lambda t, *_: (t, 0)),
                _mod_spec(cond_of_tile, 5),
            ] + [ys_spec(m) for m in range(N_SRC)],
            out_specs=_two_stream_specs(npt, d),
        ),
        out_shape=[jax.ShapeDtypeStruct(xp.shape, F32), jax.ShapeDtypeStruct(xs.shape, F32)],
        compiler_params=_cparams(("arbitrary",)),
        name="moe_unsort_residual",
    )(tbl, xp, xs, info_tok, modr, *([ys] * N_SRC))


def _tables_kernel(lo_ref, hi_ref, tot_ref, tbl_ref, *, n_tiles, n_super):
    one = lambda cond: jnp.where(cond, 1.0, 0.0)
    groups = range(MOE_GROUPS)
    lane = lax.broadcasted_iota(jnp.int32, (1, LANES), 1).astype(F32)
    tot = [tot_ref[g:g + 1, :] for g in groups]
    pick = lambda vals, idx: sum(jnp.where(idx == float(g), vals[g], 0.0) for g in groups)

    n_sup = [sum(one(tot[g] > float(m * SUPER_ROWS)) for m in range(n_super)) for g in groups]
    sup_start, sup_end, run = [], [], 0.0
    for g in groups:
        sup_start.append(run + 0.0 * tot[g])
        run = run + n_sup[g]
        sup_end.append(run)
    n_used = sup_end[-1]
    rstart = [sup_start[g] * float(SUPER_ROWS) for g in groups]
    group_of = lambda s: jnp.minimum(sum(one(s >= sup_end[g]) for g in groups), float(MOE_GROUPS - 1))

    sg, snt, written = [], [], []
    for m in range(n_super):
        used = float(m) < n_used
        g_m = group_of(jnp.where(used, float(m), n_used - 1.0))
        rows_left = pick(tot, g_m) - (float(m) - pick(sup_start, g_m)) * float(SUPER_ROWS)
        nt_m = jnp.where(used, jnp.clip(jnp.floor((rows_left + float(BLK - 1)) * (1.0 / BLK)), 0.0, float(SUPER)), 0.0)
        sg.append(g_m)
        snt.append(nt_m)
        written.append(float(EXPERT_TILES) * jnp.floor((nt_m + float(EXPERT_TILES - 1)) * (1.0 / EXPERT_TILES)))
    sblk = [jnp.minimum(float(m), n_used - 1.0) for m in range(n_super)]
    by_super_lane = lambda vals: sum(jnp.where(lane == float(m), vals[m], 0.0) for m in range(n_super))
    last_written = sum(jnp.where(n_used - 1.0 == float(m), float(m * SUPER) + written[m] - 1.0, 0.0)
                       for m in range(n_super))

    sa = sum(one(lane >= float(m * SUPER)) for m in range(1, n_super))
    ja = lane - sa * float(SUPER)
    at_tile = lambda vals: sum(jnp.where(sa == float(m), vals[m], 0.0) for m in range(n_super))
    snt_a, ga, written_a = at_tile(snt), at_tile(sg), at_tile(written)
    code = jnp.where(ja < snt_a, float(CODE_GATHER), jnp.where(ja < written_a, float(CODE_ZERO), float(CODE_SKIP)))
    oblk = jnp.where(ja < written_a, lane,
                     jnp.where(written_a > 0.0, sa * float(SUPER) + written_a - 1.0, last_written))
    r0 = ((sa - pick(sup_start, ga)) * float(SUPER) + ja) * float(BLK)
    lo = [lo_ref[:, g, :] for g in groups]
    hi = [hi_ref[:, g, :] for g in groups]
    clo = jnp.sum(one(pick(hi, ga) <= r0), axis=0, keepdims=True)
    chi = jnp.sum(one(pick(lo, ga) < r0 + float(BLK)), axis=0, keepdims=True) - 1.0
    clo = jnp.clip(clo, 0.0, float(n_tiles - 1))
    chi = jnp.clip(chi, clo, float(n_tiles - 1))

    gm = one(lane >= 2.0) + one(lane >= 4.0) + one(lane >= 6.0)
    first = pick(rstart, gm) + pick(lo, gm)
    last = pick(rstart, gm) + pick(hi, gm) - 1.0
    t0 = jnp.floor(first * (1.0 / BLK))
    t1 = jnp.floor(last * (1.0 / BLK))
    has = pick(hi, gm) > pick(lo, gm)
    a0 = jnp.where(has, t0, -1.0)
    a1 = jnp.where(has, jnp.where(t1 != t0, t1, -1.0), -1.0)
    ut = jnp.where(lane >= float(N_SRC), -1.0, jnp.where(lane - 2.0 * gm == 0.0, a0, a1))

    def put(row, v):
        tbl_ref[row:row + 1, :] = v.astype(jnp.int32)

    put(TBL_RSTART, sum(jnp.where(lane == float(g), rstart[g], 0.0) for g in groups))
    put(TBL_SGROUP, by_super_lane(sg))
    put(TBL_SNT, by_super_lane(snt))
    put(TBL_SBLK, by_super_lane(sblk))
    put(TBL_CODE, code)
    put(TBL_CLO, clo)
    put(TBL_CHI, chi)
    put(TBL_OBLK, oblk)
    tbl_ref[TBL_UT:TBL_UT + n_tiles, :] = ut.astype(jnp.int32)


def _moe_tables(cum_lo, cum_hi, tot, n_tiles, n_super):
    assert n_super * SUPER <= LANES
    tbl = pl.pallas_call(
        functools.partial(_tables_kernel, n_tiles=n_tiles, n_super=n_super),
        out_shape=jax.ShapeDtypeStruct((TBL_UT + n_tiles, LANES), jnp.int32),
        name="moe_tables",
    )(cum_lo, cum_hi, tot)
    return tbl.reshape(-1)


def _moe_buffers(n_tokens):
    n_super = (n_tokens + SUPER_ROWS - 1) // SUPER_ROWS + MOE_GROUPS
    rows = n_super * SUPER_ROWS
    return (jnp.zeros((rows, D_MODEL), BF16), jnp.zeros((rows, LANES), F32), jnp.zeros((rows, D_MODEL), BF16))


def _moe_layer(xp, xs, bufs, g, modr, wr_t, br, w_gate, w_up, w_down, layer, cond_of_tile):
    hs, cws, ys = bufs
    nt = (xp.shape[0] + xs.shape[0]) // BLK
    h, info_lane, info_tok, cum_lo, cum_hi, tot = _moe_route(xp, xs, g, modr, wr_t, br, cond_of_tile)
    tbl = _moe_tables(cum_lo, cum_hi, tot, nt, hs.shape[0] // SUPER_ROWS)
    hs, cws = _moe_gather(h, info_lane, info_tok, tbl, hs, cws)
    ys = _moe_mlp(hs, cws, w_gate, w_up, w_down, layer, tbl, ys)
    xp, xs = _moe_unsort(xp, xs, info_tok, modr, ys, tbl, cond_of_tile)
    return xp, xs, (hs, cws, ys)


def kernel(x_prompt, x_sample, c, cache_k, cache_v, state_hgrn, c_ctx, norm_g, w_ada, b_ada, hgrn_w_in, hgrn_lb_logits, hgrn_onorm_g, hgrn_w_out, attn_w_qkv, attn_qn_g, attn_kn_g, attn_lambda, attn_subln_g, attn_w_out, moe_w_group, moe_b_group, moe_w_expert, moe_b_expert, moe_w_gate, moe_w_up, moe_w_down):
    n_prompt_seq, seq, d = x_prompt.shape
    dec_batch, dec_seq, _ = x_sample.shape
    n_prompt = n_prompt_seq * seq
    assert d == D_MODEL and seq == BLK and dec_seq % BLK == 0
    assert 1 + dec_batch <= N_COND

    xp = x_prompt.reshape(n_prompt, d)
    xs = x_sample.reshape(dec_batch * dec_seq, d)
    cond = jnp.zeros((N_COND, d), F32).at[0].set(c_ctx).at[1:1 + dec_batch].set(c)
    mod = _modulation(cond, w_ada, b_ada)

    lbs = jnp.cumsum(jax.nn.softmax(hgrn_lb_logits.astype(F32), axis=0), axis=0)
    lbs = lbs - lbs[0:1]

    cond_prompt = lambda b: 0
    cond_sample = lambda b: 1 + b
    npt, spt = n_prompt // BLK, dec_seq // BLK
    cond_tile = lambda i: jnp.where(i < npt, 0, 1 + (i - npt) // spt)

    w_in = hgrn_w_in.astype(BF16)
    w_hout = hgrn_w_out.astype(BF16)
    w_qkv = attn_w_qkv.astype(BF16)
    w_aout = attn_w_out.astype(BF16)
    wr_t = jnp.zeros((DEPTH, ROUTE_ROWS, d), F32)
    wr_t = wr_t.at[:, :MOE_GROUPS].set(moe_w_group.transpose(0, 2, 1))
    wr_t = wr_t.at[:, MOE_GROUPS:MOE_GROUPS + MOE_EXPERTS].set(moe_w_expert.transpose(0, 2, 1)).astype(BF16)
    br = jnp.zeros((DEPTH, ROUTE_ROWS, 1), F32)
    br = br.at[:, :MOE_GROUPS, 0].set(moe_b_group).at[:, MOE_GROUPS:MOE_GROUPS + MOE_EXPERTS, 0].set(moe_b_expert)
    w_gate = moe_w_gate.reshape(DEPTH * MOE_EXPERTS, d, MOE_D_FF)
    w_up = moe_w_up.reshape(DEPTH * MOE_EXPERTS, d, MOE_D_FF)
    w_down = moe_w_down.reshape(DEPTH * MOE_EXPERTS, MOE_D_FF, d)

    sfin, kv = None, None
    bufs = _moe_buffers(xp.shape[0] + xs.shape[0])
    for i in range(DEPTH):
        j = i // 2
        modr = mod[i].reshape(N_COND * N_MOD, 1, d)
        if i % 2 == 0:
            common = (norm_g[i, 0], modr, w_in[j], lbs[j], hgrn_onorm_g[j], w_hout[j])
            xp, sfin = _hgrn_layer(xp, n_prompt_seq, seq, cond_prompt, *common, None, sfin, j)
            xs, _ = _hgrn_layer(xs, dec_batch, dec_seq, cond_sample, *common, state_hgrn[:, j], None, j)
        else:
            common = (norm_g[i, 0], modr, w_qkv[j], attn_qn_g[j], attn_kn_g[j], attn_lambda[j], attn_subln_g[j],
                      w_aout[j], i)
            xp, kv = _attn_layer(xp, n_prompt_seq, seq, cond_prompt, *common, None, None, kv, j)
            xs, _ = _attn_layer(xs, dec_batch, dec_seq, cond_sample, *common, cache_k[:, j], cache_v[:, j], None, j)
        xp, xs, bufs = _moe_layer(xp, xs, bufs, norm_g[i, 1], modr, wr_t[i], br[i], w_gate, w_up, w_down, i, cond_tile)

    new_k = kv[0].reshape(n_prompt_seq, DEPTH // 2, seq, HEADS, 2, QK_DIM)
    new_v = kv[1].reshape(n_prompt_seq, DEPTH // 2, seq, HEADS, HEAD_DIM)
    return (xp.reshape(n_prompt_seq, seq, d), xs.reshape(dec_batch, dec_seq, d), new_k, new_v, sfin)
```

```python
import functools
import math

import numpy as np
import jax
import jax.numpy as jnp
from jax import lax
from jax.experimental import pallas as pl
from jax.experimental.pallas import tpu as pltpu

F32 = jnp.float32
BF16 = jnp.bfloat16

D_MODEL = 1024
DEPTH = 4
GRID_W = 64
HEADS = 8
HEAD_DIM = 128
QK_DIM = 64
ROPE_THETA = 10000.0
MOE_GROUPS = 4
MOE_EPG = 4
MOE_EXPERTS = MOE_GROUPS * MOE_EPG
MOE_D_FF = 512
EPS = 1e-6
N_COND = 8
N_MOD = 6
HGRN_PARTS = 5

LANES = 128
BLK = 256
CHUNK = 32
N_CHUNK = BLK // CHUNK
HEADS_PER_STEP = 2
SCAN_HEADS_PER_BODY = 8
ATTN_HEADS_PER_BODY = 8
SUPER = 12
SUPER_ROWS = SUPER * BLK
STRAIGHT_TILES = 8
TBL_RSTART, TBL_SGROUP, TBL_SNT, TBL_SBLK, TBL_CODE, TBL_CLO, TBL_CHI, TBL_OBLK, TBL_UT = range(9)
CODE_SKIP, CODE_GATHER, CODE_ZERO = 0, 1, 2
EXPERT_TILES = 2
FF_PART = 256
GATHER_WIN = 6
EXP2_CLAMP = 115.0
SCAN_GUARD = 100.0
VMEM_LIMIT = 56 * 1024 * 1024
N_SLOTS = DEPTH // 2
_RESIDENT = dict(pipeline_mode=pl.Buffered(1))


def _cparams(sem):
    return pltpu.CompilerParams(dimension_semantics=sem, vmem_limit_bytes=VMEM_LIMIT)


def _silu(x):
    return x * jax.nn.sigmoid(x)


def _dot(a, b):
    return jnp.dot(a, b, preferred_element_type=F32)


def _dot_nt(a, b):
    return lax.dot_general(a, b, (((1,), (1,)), ((), ())), preferred_element_type=F32)


def _dot_tn(a, b):
    return lax.dot_general(a, b, (((0,), (0,)), ((), ())), preferred_element_type=F32)


def _lane_block(i, width):
    return pl.ds(pl.multiple_of(i * width, width), width)


def _mod_kernel(c_ref, w_ref, b_ref, o_ref):
    o_ref[...] = _dot(_silu(c_ref[...]), w_ref[...]) + b_ref[...]


def _modulation(cond, w_ada, b_ada):
    tn = 1536
    nj = (N_MOD * D_MODEL) // tn
    return pl.pallas_call(
        _mod_kernel,
        grid=(DEPTH, nj),
        in_specs=[
            pl.BlockSpec((N_COND, D_MODEL), lambda l, j: (0, 0)),
            pl.BlockSpec((None, D_MODEL, tn), lambda l, j: (l, 0, j)),
            pl.BlockSpec((None, 1, tn), lambda l, j: (l, 0, j)),
        ],
        out_specs=pl.BlockSpec((None, N_COND, tn), lambda l, j: (l, 0, j)),
        out_shape=jax.ShapeDtypeStruct((DEPTH, N_COND, N_MOD * D_MODEL), F32),
        compiler_params=_cparams(("parallel", "parallel")),
        name="modulation",
    )(cond, w_ada, b_ada.reshape(DEPTH, 1, N_MOD * D_MODEL))


def _norm_mod(x, g, sc, sh):
    ms = jnp.mean(x * x, axis=-1, keepdims=True)
    return (x * lax.rsqrt(ms + EPS) * g) * (1.0 + sc) + sh


def _mod_spec(cond_of_step, which):
    return pl.BlockSpec((None, 1, D_MODEL), lambda i, *_: (cond_of_step(i) * N_MOD + which, 0, 0))


def _scan_constants():
    t = np.arange(BLK)
    out = []
    for rev in (False, True):
        u = (BLK - 1 - t) if rev else t
        ut, us = u[:, None], u[None, :]
        cums = (us <= ut).astype(np.float32)
        lev = np.where(us > ut, 0,
              np.where(ut // CHUNK == us // CHUNK, 1,
              np.where(ut // 64 == us // 64, 2,
              np.where(ut // 128 == us // 128, 3, 4)))).astype(np.int32)
        out += [jnp.asarray(cums, BF16), jnp.asarray(lev)]
    return out


def _rows_to_block(rows, rev):
    order = rows[::-1] if rev else rows
    return jnp.concatenate([jnp.broadcast_to(r, (CHUNK, LANES)) for r in order], axis=0)


def _trunc_bf16(x):
    return lax.bitcast_convert_type(lax.bitcast_convert_type(x, jnp.int32) & jnp.int32(-65536), F32)


def _scan_gates(z, lb, cums):
    sig = jax.nn.sigmoid(z)
    f = lb + (1.0 - lb) * sig
    logf = jnp.log2(f)
    k = (1.0 - lb) * (1.0 - sig)
    hi32 = _trunc_bf16(logf)
    bb = _dot(cums, jnp.concatenate([hi32.astype(BF16), (logf - hi32).astype(BF16)], axis=1))
    return k, bb[:, :LANES] + bb[:, LANES:]


def _scan_att_exact(q, k, b, rev):
    ti = lax.broadcasted_iota(jnp.int32, (BLK, BLK), 0)
    si = lax.broadcasted_iota(jnp.int32, (BLK, BLK), 1)
    ri = lax.broadcasted_iota(jnp.int32, (BLK, 1), 0)
    if rev:
        ti, si, ri = BLK - 1 - ti, BLK - 1 - si, BLK - 1 - ri
    h1 = _trunc_bf16(b)
    h2 = _trunc_bf16(b - h1)
    pieces = jnp.concatenate([h1.astype(BF16), h2.astype(BF16), (b - h1 - h2).astype(BF16)], axis=1)
    att = jnp.where(ti == si, jnp.sum(q * k, axis=-1, keepdims=True), 0.0)
    g = 2
    while g <= BLK:
        h = g // 2
        at_mid = jnp.where(((ti & -g) + (h - 1)) == si, 1.0, 0.0).astype(BF16)
        bb = _dot(at_mid, pieces)
        b_r = bb[:, :LANES] + bb[:, LANES:2 * LANES] + bb[:, 2 * LANES:]
        after = (ri & h) != 0
        qg = jnp.where(after, q * jnp.exp2(jnp.minimum(b - b_r, 0.0)), 0.0).astype(BF16)
        kg = jnp.where(after, 0.0, k * jnp.exp2(jnp.minimum(b_r - b, 0.0))).astype(BF16)
        prod = _dot_nt(qg, kg)
        att = jnp.where((ti & -g) == (si & -g),
                        jnp.where((ti & h) != 0, jnp.where((si & h) == 0, prod, att), att), att)
        g *= 2
    return att


def _scan_prep(q, z, lb, cums, rev, with_inter):
    k, b = _scan_gates(z, lb, cums)

    e_row, m_row = (0, CHUNK // 2) if rev else (CHUNK - 1, CHUNK // 2 - 1)
    ends, mids = [], []
    for j in range(N_CHUNK):
        ends.append(b[j * CHUNK + e_row:j * CHUNK + e_row + 1, :])
        mids.append(b[j * CHUNK + m_row:j * CHUNK + m_row + 1, :])
    if rev:
        ends, mids = ends[::-1], mids[::-1]
    zero = jnp.zeros((1, LANES), F32)
    one = jnp.ones((1, LANES), F32)
    pres = [zero] + ends[:-1]
    b_pre = _rows_to_block(pres, rev)
    b_end = _rows_to_block(ends, rev)
    b_mid = _rows_to_block(mids, rev)

    qd = q * jnp.exp2(b - b_pre)
    ku = k * jnp.exp2(b_end - b)
    qm = q * jnp.exp2(jnp.clip(b - b_mid, -EXP2_CLAMP, EXP2_CLAMP))
    km = k * jnp.exp2(jnp.clip(b_mid - b, -EXP2_CLAMP, EXP2_CLAMP))

    levels = []
    for nc in (2, 4, 8):
        fq, fk = [], []
        for ju in range(N_CHUNK):
            r = (ju // nc) * nc + nc // 2 - 1
            if ju % nc >= nc // 2:
                fq.append(one if nc == 2 else jnp.exp2(pres[ju] - ends[r]))
                fk.append(zero)
            else:
                fq.append(zero)
                fk.append(one if nc == 2 else jnp.exp2(ends[r] - ends[ju]))
        levels.append(((qd * _rows_to_block(fq, rev)).astype(BF16), (ku * _rows_to_block(fk, rev)).astype(BF16)))

    last = ends[-1]
    qh = (qd * _rows_to_block([jnp.exp2(p) for p in pres], rev)).astype(BF16) if with_inter else None
    kh = (ku * _rows_to_block([jnp.exp2(last - e) for e in ends], rev)).astype(BF16)
    risk = functools.reduce(jnp.maximum, [jnp.maximum(p - m, m - e) for p, m, e in zip(pres, mids, ends)])
    return (qm.astype(BF16), km.astype(BF16)), levels, qh, kh, last, risk


def _scan_att(prep, lev):
    att = jnp.where(lev == 1, _dot_nt(*prep[0]), 0.0)
    for level, (ql, kl) in enumerate(prep[1], start=2):
        att = jnp.where(lev == level, _dot_nt(ql, kl), att)
    return att


def _scan_att_bidir(prep_f, prep_b, lev_f, lev_b):
    att = jnp.where(lev_f == 1, _dot_nt(*prep_f[0]), 0.0) + jnp.where(lev_b == 1, _dot_nt(*prep_b[0]), 0.0)
    lev = jnp.maximum(lev_f, lev_b)
    for level, ((qf, kf), (qb, kb)) in enumerate(zip(prep_f[1], prep_b[1]), start=2):
        both = _dot_nt(jnp.concatenate([qf, qb], axis=1), jnp.concatenate([kf, kb], axis=1))
        att = jnp.where(lev == level, both, att)
    return att


def _scan_state(prep, v, st_prev):
    ut = _dot_tn(v.astype(BF16), prep[3])
    return ut if st_prev is None else st_prev * jnp.exp2(prep[4]) + ut


def _scan_finish(o, g, on):
    ms = jnp.mean(o * o, axis=-1, keepdims=True)
    return ((o * lax.rsqrt(ms + EPS) * on) * _silu(g)).astype(BF16)


def _slot_view(ref, slot, owns_all_slots):
    if not owns_all_slots:
        return ref
    for s in range(ref.shape[0]):
        if s != slot:
            ref[s] = jnp.zeros(ref.shape[1:], ref.dtype)
    return ref.at[slot]


def _hgrn_kernel(*refs, n_blocks, has_state, slot, owns_all_slots, hps):
    it = iter(refs)
    x_ref, g_ref, sh_ref, sc_ref, gate_ref, win_ref, lb_ref, on_ref = (next(it) for _ in range(8))
    cf_ref, lf_ref, cb_ref, lvb_ref = (next(it) for _ in range(4))
    s0_ref = next(it) if has_state else None
    wout_ref, xo_ref = next(it), next(it)
    sfin_ref = None if has_state else next(it)
    h_ref, proj_ref, og_ref = next(it), next(it), next(it)
    oacc_ref, st_ref = (next(it), next(it)) if has_state else (None, None)

    h_ref[...] = _norm_mod(x_ref[...], g_ref[...], sc_ref[...], sh_ref[...]).astype(BF16)
    on = on_ref[...]
    if not has_state:
        sfin_ref = _slot_view(sfin_ref, slot, owns_all_slots)

    group_w = hps * HEAD_DIM

    def part(i, p, rows=slice(None)):
        return proj_ref[rows, p * group_w + i * HEAD_DIM:p * group_w + (i + 1) * HEAD_DIM]

    def pair(hp, carry):
        for p in range(HGRN_PARTS):
            cols = pl.ds(pl.multiple_of(p * D_MODEL + hp * group_w, group_w), group_w)
            proj_ref[:, p * group_w:(p + 1) * group_w] = _dot(h_ref[...], win_ref[:, cols])
        heads = [hp * hps + i for i in range(hps)]
        lbs = [lb_ref[:, _lane_block(hd, HEAD_DIM)] for hd in heads]

        def dyn_part(i, p, rows=slice(None)):
            return proj_ref[rows, pl.ds(pl.multiple_of(p * group_w + i * HEAD_DIM, HEAD_DIM), HEAD_DIM)]

        if not has_state:
            risk = jnp.zeros((1, LANES), F32)
            for i, hd in enumerate(heads):
                q, v = part(i, 0), part(i, 1)
                prep_f = _scan_prep(q, part(i, 2), lbs[i][0:1, :], cf_ref[...], False, False)
                prep_b = _scan_prep(q, part(i, 3), lbs[i][1:2, :], cb_ref[...], True, False)
                risk = jnp.maximum(risk, jnp.maximum(prep_f[5], prep_b[5]))
                att = _scan_att_bidir(prep_f, prep_b, lf_ref[...], lvb_ref[...])
                o = _dot(att.astype(BF16), v.astype(BF16))
                sfin_ref[0, hd] = _scan_state(prep_f, v, None).T
                sfin_ref[1, hd] = _scan_state(prep_b, v, None).T
                og_ref[:, _lane_block(hd, HEAD_DIM)] = _scan_finish(o, part(i, 4), on)

            @pl.when(jnp.max(risk) > SCAN_GUARD)
            def _():
                def exact_head(i, c2):
                    hd = hp * hps + i
                    lb = lb_ref[:, _lane_block(hd, HEAD_DIM)]
                    q = dyn_part(i, 0)
                    k_f, b_f = _scan_gates(dyn_part(i, 2), lb[0:1, :], cf_ref[...])
                    k_b, b_b = _scan_gates(dyn_part(i, 3), lb[1:2, :], cb_ref[...])
                    att = _scan_att_exact(q, k_f, b_f, False) + _scan_att_exact(q, k_b, b_b, True)
                    o = _dot(att.astype(BF16), dyn_part(i, 1).astype(BF16))
                    og_ref[:, _lane_block(hd, HEAD_DIM)] = _scan_finish(o, dyn_part(i, 4), on)
                    return c2

                lax.fori_loop(0, hps, exact_head, 0)
        else:
            def one_head(i, hd, lb, tb, exact, get):
                rf = pl.ds(pl.multiple_of(tb * BLK, BLK), BLK)
                rb = pl.ds(pl.multiple_of((n_blocks - 1 - tb) * BLK, BLK), BLK)
                cols = (slice(i * HEAD_DIM, (i + 1) * HEAD_DIM) if isinstance(i, int)
                        else pl.ds(pl.multiple_of(i * HEAD_DIM, HEAD_DIM), HEAD_DIM))
                q_f, q_b, v_f, v_b = get(i, 0, rf), get(i, 0, rb), get(i, 1, rf), get(i, 1, rb)
                prep_f = _scan_prep(q_f, get(i, 2, rf), lb[0:1, :], cf_ref[...], False, True)
                prep_b = _scan_prep(q_b, get(i, 3, rb), lb[1:2, :], cb_ref[...], True, True)
                if exact:
                    k_f, b_f = _scan_gates(get(i, 2, rf), lb[0:1, :], cf_ref[...])
                    k_b, b_b = _scan_gates(get(i, 3, rb), lb[1:2, :], cb_ref[...])
                    att_f = _scan_att_exact(q_f, k_f, b_f, False)
                    att_b = _scan_att_exact(q_b, k_b, b_b, True)
                else:
                    att_f, att_b = _scan_att(prep_f, lf_ref[...]), _scan_att(prep_b, lvb_ref[...])
                st_f, st_b = st_ref[2 * i], st_ref[2 * i + 1]
                of = _dot(att_f.astype(BF16), v_f.astype(BF16)) + _dot_nt(prep_f[2], st_f.astype(BF16))
                ob = _dot(att_b.astype(BF16), v_b.astype(BF16)) + _dot_nt(prep_b[2], st_b.astype(BF16))
                st_ref[2 * i] = _scan_state(prep_f, v_f, st_f)
                st_ref[2 * i + 1] = _scan_state(prep_b, v_b, st_b)

                @pl.when(2 * tb < n_blocks)
                def _():
                    oacc_ref[rf, cols] = of
                    oacc_ref[rb, cols] = ob

                @pl.when(2 * tb >= n_blocks)
                def _():
                    oacc_ref[rf, cols] += of
                    oacc_ref[rb, cols] += ob

                return jnp.maximum(prep_f[5], prep_b[5])

            for i, hd in enumerate(heads):
                st_ref[2 * i] = s0_ref[0, hd].T
                st_ref[2 * i + 1] = s0_ref[1, hd].T

            def body(tb, risk):
                for i, hd in enumerate(heads):
                    risk = jnp.maximum(risk, one_head(i, hd, lbs[i], tb, False, part))
                return risk

            risk = lax.fori_loop(0, n_blocks, body, jnp.zeros((1, LANES), F32))

            @pl.when(jnp.max(risk) > SCAN_GUARD)
            def _():
                def exact_head(i, c2):
                    hd = hp * hps + i
                    st_ref[2 * i] = s0_ref[0, hd].T
                    st_ref[2 * i + 1] = s0_ref[1, hd].T
                    lb = lb_ref[:, _lane_block(hd, HEAD_DIM)]

                    def exact_body(tb, c3):
                        one_head(i, hd, lb, tb, True, dyn_part)
                        return c3

                    lax.fori_loop(0, n_blocks, exact_body, 0)
                    return c2

                lax.fori_loop(0, hps, exact_head, 0)

            for i, hd in enumerate(heads):
                cols = slice(i * HEAD_DIM, (i + 1) * HEAD_DIM)
                og_ref[:, _lane_block(hd, HEAD_DIM)] = _scan_finish(oacc_ref[:, cols], part(i, 4), on)
        return carry

    lax.fori_loop(0, HEADS // hps, pair, 0)
    xo_ref[...] = x_ref[...] + gate_ref[...] * _dot(og_ref[...], wout_ref[...])


def _hgrn_layer(x, n_seq, t, cond_of_seq, g, modr, w_in, lbs_j, onorm_g, w_out, s0, sfin_prev, slot):
    d = D_MODEL
    has_state = s0 is not None
    n_blocks = t // BLK
    assert n_blocks == 1 or n_blocks % 2 == 0
    full = lambda shape, **kw: pl.BlockSpec(shape, lambda b, *_: (0,) * len(shape), **kw)
    in_specs = [
        pl.BlockSpec((t, d), lambda b: (b, 0)),
        full((1, d)),
        _mod_spec(cond_of_seq, 0), _mod_spec(cond_of_seq, 1), _mod_spec(cond_of_seq, 2),
        full((d, HGRN_PARTS * d), **_RESIDENT),
        full((2, d)),
        full((1, HEAD_DIM)),
    ] + [full((BLK, BLK))] * 4
    args = [x, g.reshape(1, d), modr, modr, modr, w_in, lbs_j, onorm_g.reshape(1, HEAD_DIM)] + _scan_constants()
    state_block = (None, 2, HEADS, HEAD_DIM, HEAD_DIM)
    if has_state:
        in_specs.append(pl.BlockSpec(state_block, lambda b: (b, 0, 0, 0, 0)))
        args.append(s0)
    in_specs.append(full((d, d), **_RESIDENT))
    args.append(w_out)
    out_specs = [pl.BlockSpec((t, d), lambda b: (b, 0))]
    out_shape = [jax.ShapeDtypeStruct((n_seq * t, d), F32)]
    hps = HEADS_PER_STEP if has_state else SCAN_HEADS_PER_BODY
    group_w = hps * HEAD_DIM
    scratch = [pltpu.VMEM((t, d), BF16), pltpu.VMEM((t, HGRN_PARTS * group_w), F32), pltpu.VMEM((t, d), BF16)]
    aliases = {}
    if has_state:
        scratch += [pltpu.VMEM((t, group_w), F32), pltpu.VMEM((2 * hps, HEAD_DIM, HEAD_DIM), F32)]
    else:
        state_dims = (2, HEADS, HEAD_DIM, HEAD_DIM)
        if sfin_prev is None:
            out_specs.append(pl.BlockSpec((None, N_SLOTS) + state_dims, lambda b: (b, 0, 0, 0, 0, 0)))
        else:
            out_specs.append(pl.BlockSpec((None, None) + state_dims, lambda b: (b, slot, 0, 0, 0, 0)))
            in_specs.append(pl.BlockSpec(memory_space=pl.ANY))
            args.append(sfin_prev)
            aliases = {len(args) - 1: 1}
        out_shape.append(jax.ShapeDtypeStruct((n_seq, N_SLOTS) + state_dims, F32))

    def body(*refs):
        if sfin_prev is not None:
            n_in = len(args)
            refs = refs[:n_in - 1] + refs[n_in:]
        _hgrn_kernel(*refs, n_blocks=n_blocks, has_state=has_state, slot=slot, owns_all_slots=sfin_prev is None,
                     hps=hps)

    out = pl.pallas_call(
        body,
        grid=(n_seq,),
        in_specs=in_specs,
        out_specs=out_specs,
        out_shape=out_shape,
        scratch_shapes=scratch,
        input_output_aliases=aliases,
        compiler_params=_cparams(("parallel",)),
        name="hgrn_layer_sample" if has_state else "hgrn_layer_prompt",
    )(*args)
    return out if not has_state else (out[0], None)


def _rope_tables(t_lat):
    rows = t_lat // GRID_W
    row = jnp.repeat(jnp.arange(rows), GRID_W).astype(F32)
    col = jnp.tile(jnp.arange(GRID_W), rows).astype(F32)
    half = QK_DIM // 2
    inv_freq = ROPE_THETA ** (-jnp.arange(0, half, 2, dtype=F32) / half)
    ang_row = row[:, None] * inv_freq
    ang_col = col[:, None] * inv_freq

    def part(ang):
        c, s = jnp.cos(ang), jnp.sin(ang)
        return jnp.concatenate([c, c], axis=1), jnp.concatenate([-s, s], axis=1)

    cr, sr = part(ang_row)
    cc, sc = part(ang_col)
    cos64 = jnp.concatenate([cr, cc], axis=1)
    sin64 = jnp.concatenate([sr, sc], axis=1)
    return jnp.concatenate([cos64, cos64], axis=1), jnp.concatenate([sin64, sin64], axis=1)


def _rope(x, cos, sin):
    lane = lax.broadcasted_iota(jnp.int32, x.shape, 1)
    first = (lane % (QK_DIM // 2)) < (QK_DIM // 4)
    swapped = jnp.where(first, pltpu.roll(x, LANES - QK_DIM // 4, 1), pltpu.roll(x, QK_DIM // 4, 1))
    return x * cos + swapped * sin


def _attn_kernel(*refs, n_qblk, rope, cache, emit_kv, lam_init, slot, owns_all_slots):
    it = iter(refs)
    x_ref, g_ref, sh_ref, sc_ref, gate_ref, wqkv_ref = (next(it) for _ in range(6))
    qg_ref, kg_ref, lam_ref, sg_ref, seg_ref = (next(it) for _ in range(5))
    cos_ref = sin_ref = ck_ref = cv_ref = nk_ref = nv_ref = None
    if rope:
        cos_ref, sin_ref = next(it), next(it)
    if cache:
        ck_ref, cv_ref = next(it), next(it)
    wout_ref, xo_ref = next(it), next(it)
    if emit_kv:
        nk_ref = _slot_view(next(it), slot, owns_all_slots)
        nv_ref = _slot_view(next(it), slot, owns_all_slots)
    h_ref, qkv_ref, oa_ref = next(it), next(it), next(it)

    d = D_MODEL
    h_ref[...] = _norm_mod(x_ref[...], g_ref[...], sc_ref[...], sh_ref[...]).astype(BF16)
    qkv_ref[...] = _dot(h_ref[...], wqkv_ref[...])
    seg = seg_ref[...]
    lp = lam_ref[...]
    lam = (jnp.exp(jnp.sum(lp[0:1, :] * lp[1:2, :], axis=-1, keepdims=True))
           - jnp.exp(jnp.sum(lp[2:3, :] * lp[3:4, :], axis=-1, keepdims=True)) + lam_init)
    lane = lax.broadcasted_iota(jnp.int32, (1, LANES), 1)
    comp0 = lane < QK_DIM
    scale = QK_DIM ** -0.5

    def split(a):
        return [jnp.where(comp0, a, 0.0).astype(BF16), jnp.where(comp0, 0.0, a).astype(BF16)]

    def one_head(hd):
        hcol = _lane_block(hd, HEAD_DIM)
        k = qkv_ref[:, pl.ds(pl.multiple_of(d + hd * HEAD_DIM, HEAD_DIM), HEAD_DIM)]
        v = qkv_ref[:, pl.ds(pl.multiple_of(2 * d + hd * HEAD_DIM, HEAD_DIM), HEAD_DIM)]
        kn = k * lax.rsqrt(_dot(k * k, seg) + EPS) * kg_ref[...]
        if emit_kv:
            nk_ref[:, hcol] = kn
            nv_ref[:, hcol] = v
        if rope:
            kn = _rope(kn, cos_ref[...], sin_ref[...])
        ks = split(kn)
        vb = v.astype(BF16)
        if cache:
            cks = split(ck_ref[:, hcol])
            cvb = cv_ref[:, hcol].astype(BF16)
        for qi in range(n_qblk):
            rows = slice(qi * BLK, (qi + 1) * BLK)
            q = qkv_ref[rows, hcol]
            qn = q * lax.rsqrt(_dot(q * q, seg) + EPS) * qg_ref[...]
            if rope:
                qn = _rope(qn, cos_ref[rows, :], sin_ref[rows, :])
            qb = (qn * scale).astype(BF16)
            a_self, a_cache = None, None
            for c in range(2):
                s = _dot_nt(qb, ks[c])
                m = jnp.max(s, axis=-1, keepdims=True)
                if cache:
                    sc = _dot_nt(qb, cks[c])
                    m = jnp.maximum(m, jnp.max(sc, axis=-1, keepdims=True))
                p = jnp.exp(s - m)
                den = jnp.sum(p, axis=-1, keepdims=True)
                if cache:
                    pc = jnp.exp(sc - m)
                    den = den + jnp.sum(pc, axis=-1, keepdims=True)
                w = (1.0 / den) if c == 0 else (-lam / den)
                a_self = p * w if c == 0 else a_self + p * w
                if cache:
                    a_cache = pc * w if c == 0 else a_cache + pc * w
            o = _dot(a_self.astype(BF16), vb)
            if cache:
                o = o + _dot(a_cache.astype(BF16), cvb)
            ms = jnp.mean(o * o, axis=-1, keepdims=True)
            o = (o * lax.rsqrt(ms + EPS) * sg_ref[...]) * (1.0 - lam_init)
            oa_ref[rows, hcol] = o.astype(BF16)

    heads_per_body = ATTN_HEADS_PER_BODY if n_qblk == 1 else HEADS_PER_STEP

    def group(hg, carry):
        for i in range(heads_per_body):
            one_head(hg * heads_per_body + i)
        return carry

    lax.fori_loop(0, HEADS // heads_per_body, group, 0)
    xo_ref[...] = x_ref[...] + gate_ref[...] * _dot(oa_ref[...], wout_ref[...])


def _attn_layer(x, n_seq, t, cond_of_seq, g, modr, w_qkv, qn_g, kn_g, lam_p, subln_g, w_out, layer_idx,
                cache_k_j, cache_v_j, kv_prev, slot):
    d = D_MODEL
    cache = cache_k_j is not None
    lam_init = 0.8 - 0.6 * math.exp(-0.3 * layer_idx)
    qg = jnp.tile(qn_g.reshape(1, QK_DIM), (1, 2))
    kg = jnp.tile(kn_g.reshape(1, QK_DIM), (1, 2))
    li = np.arange(LANES)
    seg = jnp.asarray((li[:, None] // QK_DIM == li[None, :] // QK_DIM).astype(np.float32) / QK_DIM)
    full = lambda shape, **kw: pl.BlockSpec(shape, lambda b, *_: (0,) * len(shape), **kw)
    in_specs = [
        pl.BlockSpec((t, d), lambda b: (b, 0)),
        full((1, d)),
        _mod_spec(cond_of_seq, 0), _mod_spec(cond_of_seq, 1), _mod_spec(cond_of_seq, 2),
        full((d, 3 * d), **_RESIDENT),
        full((1, LANES)), full((1, LANES)), full((4, QK_DIM)), full((1, HEAD_DIM)), full((LANES, LANES)),
    ]
    args = [x, g.reshape(1, d), modr, modr, modr, w_qkv, qg, kg, lam_p, subln_g.reshape(1, HEAD_DIM), seg]
    if cache:
        cos, sin = _rope_tables(t)
        past = cache_k_j.shape[1]
        in_specs += [full((t, LANES)), full((t, LANES)),
                     pl.BlockSpec((None, past, d), lambda b: (b, 0, 0)),
                     pl.BlockSpec((None, past, d), lambda b: (b, 0, 0))]
        args += [cos, sin, cache_k_j.reshape(n_seq, past, d), cache_v_j.reshape(n_seq, past, d)]
    in_specs.append(full((d, d), **_RESIDENT))
    args.append(w_out)
    out_specs = [pl.BlockSpec((t, d), lambda b: (b, 0))]
    out_shape = [jax.ShapeDtypeStruct((n_seq * t, d), F32)]
    aliases = {}
    n_carried = 0
    if not cache:
        if kv_prev is None:
            kv_spec = pl.BlockSpec((None, N_SLOTS, t, d), lambda b: (b, 0, 0, 0))
        else:
            kv_spec = pl.BlockSpec((None, None, t, d), lambda b: (b, slot, 0, 0))
            in_specs += [pl.BlockSpec(memory_space=pl.ANY)] * 2
            args += list(kv_prev)
            aliases = {len(args) - 2: 1, len(args) - 1: 2}
            n_carried = 2
        out_specs += [kv_spec, kv_spec]
        out_shape += [jax.ShapeDtypeStruct((n_seq, N_SLOTS, t, d), F32)] * 2

    def body(*refs):
        n_in = len(args)
        refs = refs[:n_in - n_carried] + refs[n_in:]
        _attn_kernel(*refs, n_qblk=t // BLK, rope=cache, cache=cache, emit_kv=not cache, lam_init=lam_init,
                     slot=slot, owns_all_slots=kv_prev is None)

    out = pl.pallas_call(
        body,
        grid=(n_seq,),
        in_specs=in_specs,
        out_specs=out_specs,
        out_shape=out_shape,
        scratch_shapes=[pltpu.VMEM((t, d), BF16), pltpu.VMEM((t, 3 * d), F32), pltpu.VMEM((t, d), BF16)],
        input_output_aliases=aliases,
        compiler_params=_cparams(("parallel",)),
        name="attn_layer_sample" if cache else "attn_layer_prompt",
    )(*args)
    return (out[0], None) if cache else (out[0], (out[1], out[2]))


ROUTE_ROWS = 32
INFO_GID, INFO_RANK = 8, 9


def _two_stream_specs(n_prompt_tiles, width):
    return [pl.BlockSpec((BLK, width), lambda i, *_: (jnp.minimum(i, n_prompt_tiles - 1), 0)),
            pl.BlockSpec((BLK, width), lambda i, *_: (jnp.maximum(i - n_prompt_tiles, 0), 0))]


def _route_kernel(xp_ref, xs_ref, g_ref, sh_ref, sc_ref, wr_ref, br_ref, tri_ref,
                  h_ref, il_ref, it_ref, cum_ref, cumhi_ref, tot_ref, carry_ref, *, n_prompt_tiles):
    i = pl.program_id(0)

    @pl.when(i == 0)
    def _():
        carry_ref[...] = jnp.zeros_like(carry_ref)

    @pl.when(i < n_prompt_tiles)
    def _():
        h_ref[...] = _norm_mod(xp_ref[...], g_ref[...], sc_ref[...], sh_ref[...]).astype(BF16)

    @pl.when(i >= n_prompt_tiles)
    def _():
        h_ref[...] = _norm_mod(xs_ref[...], g_ref[...], sc_ref[...], sh_ref[...]).astype(BF16)

    logit = _dot_nt(wr_ref[...], h_ref[...]) + br_ref[...]
    gl = [logit[g:g + 1, :] for g in range(MOE_GROUPS)]
    gmax = functools.reduce(jnp.maximum, gl)
    gz = functools.reduce(lambda a, b: a + b, [jnp.exp(x - gmax) for x in gl])
    g_w = 1.0 / gz
    gid = jnp.full_like(gmax, MOE_GROUPS - 1)
    for g in range(MOE_GROUPS - 2, -1, -1):
        gid = jnp.where(gl[g] == gmax, float(g), gid)
    el = []
    for j in range(MOE_EPG):
        e = logit[MOE_GROUPS + j:MOE_GROUPS + j + 1, :]
        for g in range(1, MOE_GROUPS):
            r = MOE_GROUPS + g * MOE_EPG + j
            e = jnp.where(gid == float(g), logit[r:r + 1, :], e)
        el.append(e)
    emax = functools.reduce(jnp.maximum, el)
    pe = [jnp.exp(e - emax) for e in el]
    idx1 = jnp.full_like(emax, MOE_EPG - 1)
    for j in range(MOE_EPG - 2, -1, -1):
        idx1 = jnp.where(el[j] == emax, float(j), idx1)
    el2 = [jnp.where(idx1 == float(j), -jnp.inf, el[j]) for j in range(MOE_EPG)]
    emax2 = functools.reduce(jnp.maximum, el2)
    idx2 = jnp.full_like(emax, MOE_EPG - 1)
    for j in range(MOE_EPG - 2, -1, -1):
        idx2 = jnp.where(el2[j] == emax2, float(j), idx2)
    sel = [(idx1 == float(j)) | (idx2 == float(j)) for j in range(MOE_EPG)]
    den = functools.reduce(lambda a, b: a + b, [jnp.where(sel[j], pe[j], 0.0) for j in range(MOE_EPG)])
    cw = [jnp.where(sel[j], pe[j] * (g_w / den), 0.0) for j in range(MOE_EPG)]

    row8 = lax.broadcasted_iota(jnp.int32, (8, BLK), 0)
    onehot = jnp.where(row8.astype(F32) == gid, 1.0, 0.0)
    within = _dot(onehot.astype(BF16), tri_ref[...])
    carry = carry_ref[...]
    rank = jnp.sum(onehot * (within + carry[:, 0:1]), axis=0, keepdims=True)
    cum_ref[...] = carry
    new_carry = carry + jnp.sum(onehot, axis=1, keepdims=True)
    carry_ref[...] = new_carry
    cumhi_ref[...] = new_carry
    tot_ref[...] = new_carry

    il_ref[...] = jnp.where(row8 == 0, gid, jnp.where(row8 == 1, rank, 0.0))
    rowl = lax.broadcasted_iota(jnp.int32, (LANES, BLK), 0)
    m = jnp.zeros((LANES, BLK), F32)
    for j in range(MOE_EPG):
        hi = cw[j].astype(BF16).astype(F32)
        m = jnp.where(rowl == j, hi, m)
        m = jnp.where(rowl == MOE_EPG + j, cw[j] - hi, m)
    m = jnp.where(rowl == INFO_GID, gid, m)
    m = jnp.where(rowl == INFO_RANK, rank, m)
    it_ref[...] = m.T


def _moe_route(xp, xs, g, modr, wr_t, br, cond_of_tile):
    d = D_MODEL
    npt = xp.shape[0] // BLK
    nt = npt + xs.shape[0] // BLK
    n = nt * BLK
    tri = jnp.asarray(np.triu(np.ones((BLK, BLK), np.float32), 1), BF16)
    full = lambda shape: pl.BlockSpec(shape, lambda i: (0,) * len(shape))
    return pl.pallas_call(
        functools.partial(_route_kernel, n_prompt_tiles=npt),
        grid=(nt,),
        in_specs=_two_stream_specs(npt, d) + [
            full((1, d)),
            _mod_spec(cond_of_tile, 3), _mod_spec(cond_of_tile, 4),
            full((ROUTE_ROWS, d)), full((ROUTE_ROWS, 1)), full((BLK, BLK)),
        ],
        out_specs=[
            pl.BlockSpec((BLK, d), lambda i: (i, 0)),
            pl.BlockSpec((8, BLK), lambda i: (0, i)),
            pl.BlockSpec((BLK, LANES), lambda i: (i, 0)),
            pl.BlockSpec((None, 8, LANES), lambda i: (i, 0, 0)),
            pl.BlockSpec((None, 8, LANES), lambda i: (i, 0, 0)),
            pl.BlockSpec((8, LANES), lambda i: (0, 0)),
        ],
        out_shape=[
            jax.ShapeDtypeStruct((n, d), BF16),
            jax.ShapeDtypeStruct((8, n), F32),
            jax.ShapeDtypeStruct((n, LANES), F32),
            jax.ShapeDtypeStruct((nt, 8, LANES), F32),
            jax.ShapeDtypeStruct((nt, 8, LANES), F32),
            jax.ShapeDtypeStruct((8, LANES), F32),
        ],
        scratch_shapes=[pltpu.VMEM((8, LANES), F32)],
        compiler_params=_cparams(("arbitrary",)),
        name="moe_route",
    )(xp, xs, g.reshape(1, d), modr, modr, wr_t, br, tri)


def _sorted_pos(gid, rank, rstart_ref):
    p = rank
    for g in range(MOE_GROUPS):
        p = p + jnp.where(gid == float(g), rstart_ref[g].astype(F32), 0.0)
    return p


def _gather_tile(tbl_ref, h_ref, il_ref, it_ref, hs_ref, cws_ref, a, used, *, n_tiles):
    win = GATHER_WIN * BLK
    rstart_ref = tbl_ref

    @pl.when(jnp.logical_not(used))
    def _():
        hs_ref[...] = jnp.zeros(hs_ref.shape, hs_ref.dtype)
        cws_ref[...] = jnp.zeros(cws_ref.shape, cws_ref.dtype)

    @pl.when(used)
    def _():
        dest = (lax.broadcasted_iota(jnp.int32, (BLK, 1), 0) + a * BLK).astype(F32)
        src_tile = lax.broadcasted_iota(jnp.int32, (1, win), 1) // BLK
        clo = tbl_ref[TBL_CLO * LANES + a]
        n_win = (tbl_ref[TBL_CHI * LANES + a] - clo + GATHER_WIN) // GATHER_WIN

        def window(w):
            first = clo + w * GATHER_WIN
            c0 = jnp.minimum(first, n_tiles - GATHER_WIN)
            rows = pl.ds(pl.multiple_of(c0 * BLK, BLK), win)
            info = il_ref[:, rows]
            p = _sorted_pos(info[0:1, :], info[1:2, :], rstart_ref)
            p = jnp.where(src_tile + c0 >= first, p, -1.0)
            onehot = jnp.where(dest == p, 1.0, 0.0).astype(BF16)
            r = _dot(onehot, it_ref[rows, :].astype(BF16))
            return _dot(onehot, h_ref[rows, :]), r + pltpu.roll(r, LANES - MOE_EPG, 1)

        dh, dc = window(0)
        hs_ref[...] = dh.astype(BF16)
        cws_ref[...] = dc

        def body(w, carry):
            dh, dc = window(w)
            hs_ref[...] = (hs_ref[...].astype(F32) + dh).astype(BF16)
            cws_ref[...] += dc
            return carry

        lax.fori_loop(1, n_win, body, 0)


def _gather_kernel(tbl_ref, h_ref, il_ref, it_ref, hs_ref, cws_ref, *, n_tiles):
    s = pl.program_id(0)
    nt = tbl_ref[TBL_SNT * LANES + s]

    def tile(j, carry):
        rows = pl.ds(pl.multiple_of(j * BLK, BLK), BLK)
        _gather_tile(tbl_ref, h_ref, il_ref, it_ref, hs_ref.at[rows, :], cws_ref.at[rows, :],
                     s * SUPER + j, j < nt, n_tiles=n_tiles)
        return carry

    lax.fori_loop(0, jnp.where(nt > 0, SUPER, 0), tile, 0)


def _moe_gather(h, info_lane, info_tok, tbl, hs_buf, cws_buf):
    n, d = h.shape
    nt = n // BLK
    n_super = hs_buf.shape[0] // SUPER_ROWS
    assert nt >= GATHER_WIN and n_super * SUPER <= LANES
    out_block = lambda s, tbl_r: (tbl_r[TBL_SBLK * LANES + s], 0)
    carried = pl.BlockSpec(memory_space=pl.ANY)

    def body(tbl_ref, h_ref, il_ref, it_ref, hs_old, cws_old, hs_ref, cws_ref):
        _gather_kernel(tbl_ref, h_ref, il_ref, it_ref, hs_ref, cws_ref, n_tiles=nt)

    return pl.pallas_call(
        body,
        grid_spec=pltpu.PrefetchScalarGridSpec(
            num_scalar_prefetch=1,
            grid=(n_super,),
            in_specs=[
                pl.BlockSpec((n, d), lambda a, *_: (0, 0), **_RESIDENT),
                pl.BlockSpec((8, n), lambda a, *_: (0, 0), **_RESIDENT),
                pl.BlockSpec((n, LANES), lambda a, *_: (0, 0), **_RESIDENT),
                carried, carried,
            ],
            out_specs=[pl.BlockSpec((SUPER_ROWS, d), out_block), pl.BlockSpec((SUPER_ROWS, LANES), out_block)],
        ),
        out_shape=[jax.ShapeDtypeStruct(hs_buf.shape, BF16), jax.ShapeDtypeStruct(cws_buf.shape, F32)],
        input_output_aliases={4: 0, 5: 1},
        compiler_params=_cparams(("arbitrary",)),
        name="moe_gather",
    )(tbl, h, info_lane, info_tok, hs_buf, cws_buf)


def _moe_mlp_kernel(tbl_ref, hs_ref, cws_ref, wg_ref, wu_ref, wd_ref, ys_ref,
                    acc_ref, wgb_ref, wub_ref, wdb_ref):
    s = pl.program_id(0)
    k = pl.program_id(1)
    nt = tbl_ref[TBL_SNT * LANES + s]

    @pl.when(nt > 0)
    def _():
        wgb_ref[...] = wg_ref[...].astype(BF16)
        wub_ref[...] = wu_ref[...].astype(BF16)
        wdb_ref[...] = wd_ref[...].astype(BF16)

    @pl.when((s == 0) & (k == 0))
    def _():
        acc_ref[...] = jnp.zeros_like(acc_ref)

    def skipped(j):
        rows = slice(j * EXPERT_TILES * BLK, (j + 1) * EXPERT_TILES * BLK)
        ys_ref[rows, :] = jnp.zeros((EXPERT_TILES * BLK, D_MODEL), BF16)

    def block(j):
        rows = slice(j * EXPERT_TILES * BLK, (j + 1) * EXPERT_TILES * BLK)
        hsub = hs_ref[rows, :]
        cws = cws_ref[rows, :]
        cwk = jnp.zeros((EXPERT_TILES * BLK, 1), F32)
        for kk in range(MOE_EPG):
            cwk = jnp.where(k == kk, cws[:, kk:kk + 1], cwk)
        y = None
        for fh in range(MOE_D_FF // FF_PART):
            fc = slice(fh * FF_PART, (fh + 1) * FF_PART)
            gate = _dot(hsub, wgb_ref[:, fc])
            up = _dot(hsub, wub_ref[:, fc])
            act = ((_silu(gate) * up) * cwk).astype(BF16)
            part = _dot(act, wdb_ref[fc, :])
            y = part if y is None else y + part
        total = jnp.where(k == 0, 0.0, acc_ref[rows, :]) + y
        acc_ref[rows, :] = total
        ys_ref[rows, :] = total.astype(BF16)

    def maybe_block(j):
        pl.when(j * EXPERT_TILES < nt)(functools.partial(block, j))
        pl.when((j * EXPERT_TILES >= nt) & (nt > 0) & (k == 0))(functools.partial(skipped, j))

    n_blocks = SUPER // EXPERT_TILES
    n_straight = STRAIGHT_TILES // EXPERT_TILES

    @pl.when(nt >= STRAIGHT_TILES)
    def _():
        for j in range(n_straight):
            block(j)

    @pl.when(nt < STRAIGHT_TILES)
    def _():
        for j in range(n_straight):
            maybe_block(j)

    for j in range(n_straight, n_blocks):
        maybe_block(j)


def _moe_mlp(hs, cws, w_gate, w_up, w_down, layer, tbl, ys_buf):
    d = hs.shape[1]
    n_super = hs.shape[0] // SUPER_ROWS

    def body(tbl_ref, hs_ref, cws_ref, wg_ref, wu_ref, wd_ref, ys_old, ys_ref, *scratch):
        _moe_mlp_kernel(tbl_ref, hs_ref, cws_ref, wg_ref, wu_ref, wd_ref, ys_ref, *scratch)

    def widx(s, k, tbl_r):
        kk = jnp.where(tbl_r[TBL_SNT * LANES + s] > 0, k, MOE_EPG - 1)
        return (layer * MOE_EXPERTS + tbl_r[TBL_SGROUP * LANES + s] * MOE_EPG + kk, 0, 0)

    rows_idx = lambda s, k, tbl_r: (tbl_r[TBL_SBLK * LANES + s], 0)
    return pl.pallas_call(
        body,
        grid_spec=pltpu.PrefetchScalarGridSpec(
            num_scalar_prefetch=1,
            grid=(n_super, MOE_EPG),
            in_specs=[
                pl.BlockSpec((SUPER_ROWS, d), rows_idx),
                pl.BlockSpec((SUPER_ROWS, LANES), rows_idx),
                pl.BlockSpec((None, d, MOE_D_FF), widx),
                pl.BlockSpec((None, d, MOE_D_FF), widx),
                pl.BlockSpec((None, MOE_D_FF, d), widx),
                pl.BlockSpec(memory_space=pl.ANY),
            ],
            out_specs=pl.BlockSpec((SUPER_ROWS, d), rows_idx),
            scratch_shapes=[
                pltpu.VMEM((SUPER_ROWS, d), F32),
                pltpu.VMEM((d, MOE_D_FF), BF16),
                pltpu.VMEM((d, MOE_D_FF), BF16),
                pltpu.VMEM((MOE_D_FF, d), BF16),
            ],
        ),
        out_shape=jax.ShapeDtypeStruct(ys_buf.shape, BF16),
        input_output_aliases={6: 0},
        compiler_params=_cparams(("arbitrary", "arbitrary")),
        name="moe_experts",
    )(tbl, hs, cws, w_gate, w_up, w_down, ys_buf)


N_SRC = 2 * MOE_GROUPS


def _unsort_kernel(tbl_ref, xp_ref, xs_ref, it_ref, gate_ref, *rest, n_prompt_tiles):
    ys_refs, op_ref, os_ref = rest[:N_SRC], rest[N_SRC], rest[N_SRC + 1]
    t = pl.program_id(0)
    info = it_ref[...]
    p = _sorted_pos(info[:, INFO_GID:INFO_GID + 1], info[:, INFO_RANK:INFO_RANK + 1], tbl_ref)
    lane = lax.broadcasted_iota(jnp.int32, (1, BLK), 1).astype(F32)
    slot = lambda m: tbl_ref[(TBL_UT + t) * LANES + m]

    def take(m):
        a = slot(m)
        onehot = jnp.where(p - (a * BLK).astype(F32) == lane, 1.0, 0.0).astype(BF16)
        return _dot(onehot, ys_refs[m][...])

    def stream(x_ref, o_ref):
        first = functools.reduce(lambda u, w: u + w, [take(m) for m in range(0, N_SRC, 2)])
        o_ref[...] = x_ref[...] + gate_ref[...] * first
        for m in range(1, N_SRC, 2):
            @pl.when(slot(m) >= 0)
            def _():
                o_ref[...] += gate_ref[...] * take(m)

    pl.when(t < n_prompt_tiles)(functools.partial(stream, xp_ref, op_ref))
    pl.when(t >= n_prompt_tiles)(functools.partial(stream, xs_ref, os_ref))


def _moe_unsort(xp, xs, info_tok, modr, ys, tbl, cond_of_tile):
    d = D_MODEL
    npt = xp.shape[0] // BLK
    nt = npt + xs.shape[0] // BLK

    def ys_spec(m):
        return pl.BlockSpec((BLK, d), lambda t, tbl_r: (jnp.maximum(tbl_r[(TBL_UT + t) * LANES + m], 0), 0))

    return pl.pallas_call(
        functools.partial(_unsort_kernel, n_prompt_tiles=npt),
        grid_spec=pltpu.PrefetchScalarGridSpec(
            num_scalar_prefetch=1,
            grid=(nt,),
            in_specs=_two_stream_specs(npt, d) + [
                pl.BlockSpec((BLK, LANES), lambda t, *_: (t, 0)),
                _mod_spec(cond_of_tile, 5),
            ] + [ys_spec(m) for m in range(N_SRC)],
            out_specs=_two_stream_specs(npt, d),
        ),
        out_shape=[jax.ShapeDtypeStruct(xp.shape, F32), jax.ShapeDtypeStruct(xs.shape, F32)],
        compiler_params=_cparams(("arbitrary",)),
        name="moe_unsort_residual",
    )(tbl, xp, xs, info_tok, modr, *([ys] * N_SRC))


def _tables_kernel(lo_ref, hi_ref, tot_ref, tbl_ref, *, n_tiles, n_super):
    one = lambda cond: jnp.where(cond, 1.0, 0.0)
    groups = range(MOE_GROUPS)
    lane = lax.broadcasted_iota(jnp.int32, (1, LANES), 1).astype(F32)
    tot = [tot_ref[g:g + 1, :] for g in groups]
    pick = lambda vals, idx: sum(jnp.where(idx == float(g), vals[g], 0.0) for g in groups)

    n_sup = [sum(one(tot[g] > float(m * SUPER_ROWS)) for m in range(n_super)) for g in groups]
    sup_start, sup_end, run = [], [], 0.0
    for g in groups:
        sup_start.append(run + 0.0 * tot[g])
        run = run + n_sup[g]
        sup_end.append(run)
    n_used = sup_end[-1]
    rstart = [sup_start[g] * float(SUPER_ROWS) for g in groups]
    group_of = lambda s: jnp.minimum(sum(one(s >= sup_end[g]) for g in groups), float(MOE_GROUPS - 1))

    sg, snt, written = [], [], []
    for m in range(n_super):
        used = float(m) < n_used
        g_m = group_of(jnp.where(used, float(m), n_used - 1.0))
        rows_left = pick(tot, g_m) - (float(m) - pick(sup_start, g_m)) * float(SUPER_ROWS)
        nt_m = jnp.where(used, jnp.clip(jnp.floor((rows_left + float(BLK - 1)) * (1.0 / BLK)), 0.0, float(SUPER)), 0.0)
        sg.append(g_m)
        snt.append(nt_m)
        written.append(float(EXPERT_TILES) * jnp.floor((nt_m + float(EXPERT_TILES - 1)) * (1.0 / EXPERT_TILES)))
    sblk = [jnp.minimum(float(m), n_used - 1.0) for m in range(n_super)]
    by_super_lane = lambda vals: sum(jnp.where(lane == float(m), vals[m], 0.0) for m in range(n_super))
    last_written = sum(jnp.where(n_used - 1.0 == float(m), float(m * SUPER) + written[m] - 1.0, 0.0)
                       for m in range(n_super))

    sa = sum(one(lane >= float(m * SUPER)) for m in range(1, n_super))
    ja = lane - sa * float(SUPER)
    at_tile = lambda vals: sum(jnp.where(sa == float(m), vals[m], 0.0) for m in range(n_super))
    snt_a, ga, written_a = at_tile(snt), at_tile(sg), at_tile(written)
    code = jnp.where(ja < snt_a, float(CODE_GATHER), jnp.where(ja < written_a, float(CODE_ZERO), float(CODE_SKIP)))
    oblk = jnp.where(ja < written_a, lane,
                     jnp.where(written_a > 0.0, sa * float(SUPER) + written_a - 1.0, last_written))
    r0 = ((sa - pick(sup_start, ga)) * float(SUPER) + ja) * float(BLK)
    lo = [lo_ref[:, g, :] for g in groups]
    hi = [hi_ref[:, g, :] for g in groups]
    clo = jnp.sum(one(pick(hi, ga) <= r0), axis=0, keepdims=True)
    chi = jnp.sum(one(pick(lo, ga) < r0 + float(BLK)), axis=0, keepdims=True) - 1.0
    clo = jnp.clip(clo, 0.0, float(n_tiles - 1))
    chi = jnp.clip(chi, clo, float(n_tiles - 1))

    gm = one(lane >= 2.0) + one(lane >= 4.0) + one(lane >= 6.0)
    first = pick(rstart, gm) + pick(lo, gm)
    last = pick(rstart, gm) + pick(hi, gm) - 1.0
    t0 = jnp.floor(first * (1.0 / BLK))
    t1 = jnp.floor(last * (1.0 / BLK))
    has = pick(hi, gm) > pick(lo, gm)
    a0 = jnp.where(has, t0, -1.0)
    a1 = jnp.where(has, jnp.where(t1 != t0, t1, -1.0), -1.0)
    ut = jnp.where(lane >= float(N_SRC), -1.0, jnp.where(lane - 2.0 * gm == 0.0, a0, a1))

    def put(row, v):
        tbl_ref[row:row + 1, :] = v.astype(jnp.int32)

    put(TBL_RSTART, sum(jnp.where(lane == float(g), rstart[g], 0.0) for g in groups))
    put(TBL_SGROUP, by_super_lane(sg))
    put(TBL_SNT, by_super_lane(snt))
    put(TBL_SBLK, by_super_lane(sblk))
    put(TBL_CODE, code)
    put(TBL_CLO, clo)
    put(TBL_CHI, chi)
    put(TBL_OBLK, oblk)
    tbl_ref[TBL_UT:TBL_UT + n_tiles, :] = ut.astype(jnp.int32)


def _moe_tables(cum_lo, cum_hi, tot, n_tiles, n_super):
    assert n_super * SUPER <= LANES
    tbl = pl.pallas_call(
        functools.partial(_tables_kernel, n_tiles=n_tiles, n_super=n_super),
        out_shape=jax.ShapeDtypeStruct((TBL_UT + n_tiles, LANES), jnp.int32),
        name="moe_tables",
    )(cum_lo, cum_hi, tot)
    return tbl.reshape(-1)


def _moe_buffers(n_tokens):
    n_super = (n_tokens + SUPER_ROWS - 1) // SUPER_ROWS + MOE_GROUPS
    rows = n_super * SUPER_ROWS
    return (jnp.zeros((rows, D_MODEL), BF16), jnp.zeros((rows, LANES), F32), jnp.zeros((rows, D_MODEL), BF16))


def _moe_layer(xp, xs, bufs, g, modr, wr_t, br, w_gate, w_up, w_down, layer, cond_of_tile):
    hs, cws, ys = bufs
    nt = (xp.shape[0] + xs.shape[0]) // BLK
    h, info_lane, info_tok, cum_lo, cum_hi, tot = _moe_route(xp, xs, g, modr, wr_t, br, cond_of_tile)
    tbl = _moe_tables(cum_lo, cum_hi, tot, nt, hs.shape[0] // SUPER_ROWS)
    hs, cws = _moe_gather(h, info_lane, info_tok, tbl, hs, cws)
    ys = _moe_mlp(hs, cws, w_gate, w_up, w_down, layer, tbl, ys)
    xp, xs = _moe_unsort(xp, xs, info_tok, modr, ys, tbl, cond_of_tile)
    return xp, xs, (hs, cws, ys)


def kernel(x_prompt, x_sample, c, cache_k, cache_v, state_hgrn, c_ctx, norm_g, w_ada, b_ada, hgrn_w_in, hgrn_lb_logits, hgrn_onorm_g, hgrn_w_out, attn_w_qkv, attn_qn_g, attn_kn_g, attn_lambda, attn_subln_g, attn_w_out, moe_w_group, moe_b_group, moe_w_expert, moe_b_expert, moe_w_gate, moe_w_up, moe_w_down):
    n_prompt_seq, seq, d = x_prompt.shape
    dec_batch, dec_seq, _ = x_sample.shape
    n_prompt = n_prompt_seq * seq
    assert d == D_MODEL and seq == BLK and dec_seq % BLK == 0
    assert 1 + dec_batch <= N_COND

    xp = x_prompt.reshape(n_prompt, d)
    xs = x_sample.reshape(dec_batch * dec_seq, d)
    cond = jnp.zeros((N_COND, d), F32).at[0].set(c_ctx).at[1:1 + dec_batch].set(c)
    mod = _modulation(cond, w_ada, b_ada)

    lbs = jnp.cumsum(jax.nn.softmax(hgrn_lb_logits.astype(F32), axis=0), axis=0)
    lbs = lbs - lbs[0:1]

    cond_prompt = lambda b: 0
    cond_sample = lambda b: 1 + b
    npt, spt = n_prompt // BLK, dec_seq // BLK
    cond_tile = lambda i: jnp.where(i < npt, 0, 1 + (i - npt) // spt)

    w_in = hgrn_w_in.astype(BF16)
    w_hout = hgrn_w_out.astype(BF16)
    w_qkv = attn_w_qkv.astype(BF16)
    w_aout = attn_w_out.astype(BF16)
    wr_t = jnp.zeros((DEPTH, ROUTE_ROWS, d), F32)
    wr_t = wr_t.at[:, :MOE_GROUPS].set(moe_w_group.transpose(0, 2, 1))
    wr_t = wr_t.at[:, MOE_GROUPS:MOE_GROUPS + MOE_EXPERTS].set(moe_w_expert.transpose(0, 2, 1)).astype(BF16)
    br = jnp.zeros((DEPTH, ROUTE_ROWS, 1), F32)
    br = br.at[:, :MOE_GROUPS, 0].set(moe_b_group).at[:, MOE_GROUPS:MOE_GROUPS + MOE_EXPERTS, 0].set(moe_b_expert)
    w_gate = moe_w_gate.reshape(DEPTH * MOE_EXPERTS, d, MOE_D_FF)
    w_up = moe_w_up.reshape(DEPTH * MOE_EXPERTS, d, MOE_D_FF)
    w_down = moe_w_down.reshape(DEPTH * MOE_EXPERTS, MOE_D_FF, d)

    sfin, kv = None, None
    bufs = _moe_buffers(xp.shape[0] + xs.shape[0])
    for i in range(DEPTH):
        j = i // 2
        modr = mod[i].reshape(N_COND * N_MOD, 1, d)
        if i % 2 == 0:
            common = (norm_g[i, 0], modr, w_in[j], lbs[j], hgrn_onorm_g[j], w_hout[j])
            xp, sfin = _hgrn_layer(xp, n_prompt_seq, seq, cond_prompt, *common, None, sfin, j)
            xs, _ = _hgrn_layer(xs, dec_batch, dec_seq, cond_sample, *common, state_hgrn[:, j], None, j)
        else:
            common = (norm_g[i, 0], modr, w_qkv[j], attn_qn_g[j], attn_kn_g[j], attn_lambda[j], attn_subln_g[j],
                      w_aout[j], i)
            xp, kv = _attn_layer(xp, n_prompt_seq, seq, cond_prompt, *common, None, None, kv, j)
            xs, _ = _attn_layer(xs, dec_batch, dec_seq, cond_sample, *common, cache_k[:, j], cache_v[:, j], None, j)
        xp, xs, bufs = _moe_layer(xp, xs, bufs, norm_g[i, 1], modr, wr_t[i], br[i], w_gate, w_up, w_down, i, cond_tile)

    new_k = kv[0].reshape(n_prompt_seq, DEPTH // 2, seq, HEADS, 2, QK_DIM)
    new_v = kv[1].reshape(n_prompt_seq, DEPTH // 2, seq, HEADS, HEAD_DIM)
    return (xp.reshape(n_prompt_seq, seq, d), xs.reshape(dec_batch, dec_seq, d), new_k, new_v, sfin)
```

```python
import functools
import math

import numpy as np
import jax
import jax.numpy as jnp
from jax import lax
from jax.experimental import pallas as pl
from jax.experimental.pallas import tpu as pltpu

F32 = jnp.float32
BF16 = jnp.bfloat16

D_MODEL = 1024
DEPTH = 4
GRID_W = 64
HEADS = 8
HEAD_DIM = 128
QK_DIM = 64
ROPE_THETA = 10000.0
MOE_GROUPS = 4
MOE_EPG = 4
MOE_EXPERTS = MOE_GROUPS * MOE_EPG
MOE_D_FF = 512
EPS = 1e-6
N_COND = 8
N_MOD = 6
HGRN_PARTS = 5

LANES = 128
BLK = 256
CHUNK = 32
N_CHUNK = BLK // CHUNK
HEADS_PER_STEP = 2
SCAN_HEADS_PER_BODY = 8
ATTN_HEADS_PER_BODY = 8
ATTN_SEQS_PER_STEP = 2
SUPER = 12
SUPER_ROWS = SUPER * BLK
STRAIGHT_TILES = 8
TBL_RSTART, TBL_SGROUP, TBL_SNT, TBL_SBLK, TBL_CODE, TBL_CLO, TBL_CHI, TBL_OBLK, TBL_UT = range(9)
CODE_SKIP, CODE_GATHER, CODE_ZERO = 0, 1, 2
EXPERT_TILES = 2
FF_PART = 256
GATHER_WIN = 6
EXP2_CLAMP = 115.0
SCAN_GUARD = 100.0
VMEM_LIMIT = 56 * 1024 * 1024
N_SLOTS = DEPTH // 2
_RESIDENT = dict(pipeline_mode=pl.Buffered(1))


def _cparams(sem):
    return pltpu.CompilerParams(dimension_semantics=sem, vmem_limit_bytes=VMEM_LIMIT)


def _silu(x):
    return x * jax.nn.sigmoid(x)


def _dot(a, b):
    return jnp.dot(a, b, preferred_element_type=F32)


def _dot_nt(a, b):
    return lax.dot_general(a, b, (((1,), (1,)), ((), ())), preferred_element_type=F32)


def _dot_tn(a, b):
    return lax.dot_general(a, b, (((0,), (0,)), ((), ())), preferred_element_type=F32)


def _lane_block(i, width):
    return _block_at(i * width, width)


def _block_at(start, width):
    if isinstance(start, int):
        return slice(start, start + width)
    return pl.ds(pl.multiple_of(start, width), width)


def _for_each_group(n_groups, body):
    if n_groups == 1:
        body(0, 0)
    else:
        lax.fori_loop(0, n_groups, body, 0)


def _mod_kernel(c_ref, w_ref, b_ref, o_ref):
    o_ref[...] = _dot(_silu(c_ref[...]), w_ref[...]) + b_ref[...]


def _modulation(cond, w_ada, b_ada):
    tn = 1536
    nj = (N_MOD * D_MODEL) // tn
    return pl.pallas_call(
        _mod_kernel,
        grid=(DEPTH, nj),
        in_specs=[
            pl.BlockSpec((N_COND, D_MODEL), lambda l, j: (0, 0)),
            pl.BlockSpec((None, D_MODEL, tn), lambda l, j: (l, 0, j)),
            pl.BlockSpec((None, 1, tn), lambda l, j: (l, 0, j)),
        ],
        out_specs=pl.BlockSpec((None, N_COND, tn), lambda l, j: (l, 0, j)),
        out_shape=jax.ShapeDtypeStruct((DEPTH, N_COND, N_MOD * D_MODEL), F32),
        compiler_params=_cparams(("parallel", "parallel")),
        name="modulation",
    )(cond, w_ada, b_ada.reshape(DEPTH, 1, N_MOD * D_MODEL))


def _norm_mod(x, g, sc, sh):
    ms = jnp.mean(x * x, axis=-1, keepdims=True)
    return (x * lax.rsqrt(ms + EPS) * g) * (1.0 + sc) + sh


def _mod_spec(cond_of_step, which):
    return pl.BlockSpec((None, 1, D_MODEL), lambda i, *_: (cond_of_step(i) * N_MOD + which, 0, 0))


def _scan_constants():
    t = np.arange(BLK)
    out = []
    for rev in (False, True):
        u = (BLK - 1 - t) if rev else t
        ut, us = u[:, None], u[None, :]
        cums = (us <= ut).astype(np.float32)
        lev = np.where(us > ut, 0,
              np.where(ut // CHUNK == us // CHUNK, 1,
              np.where(ut // 64 == us // 64, 2,
              np.where(ut // 128 == us // 128, 3, 4)))).astype(np.int32)
        out += [jnp.asarray(cums, BF16), jnp.asarray(lev)]
    return out


def _rows_to_block(rows, rev):
    order = rows[::-1] if rev else rows
    return jnp.concatenate([jnp.broadcast_to(r, (CHUNK, LANES)) for r in order], axis=0)


def _trunc_bf16(x):
    return lax.bitcast_convert_type(lax.bitcast_convert_type(x, jnp.int32) & jnp.int32(-65536), F32)


def _scan_gates(z, lb, cums):
    sig = jax.nn.sigmoid(z)
    f = lb + (1.0 - lb) * sig
    logf = jnp.log2(f)
    k = (1.0 - lb) * (1.0 - sig)
    hi32 = _trunc_bf16(logf)
    bb = _dot(cums, jnp.concatenate([hi32.astype(BF16), (logf - hi32).astype(BF16)], axis=1))
    return k, bb[:, :LANES] + bb[:, LANES:]


def _scan_att_exact(q, k, b, rev):
    ti = lax.broadcasted_iota(jnp.int32, (BLK, BLK), 0)
    si = lax.broadcasted_iota(jnp.int32, (BLK, BLK), 1)
    ri = lax.broadcasted_iota(jnp.int32, (BLK, 1), 0)
    if rev:
        ti, si, ri = BLK - 1 - ti, BLK - 1 - si, BLK - 1 - ri
    h1 = _trunc_bf16(b)
    h2 = _trunc_bf16(b - h1)
    pieces = jnp.concatenate([h1.astype(BF16), h2.astype(BF16), (b - h1 - h2).astype(BF16)], axis=1)
    att = jnp.where(ti == si, jnp.sum(q * k, axis=-1, keepdims=True), 0.0)
    g = 2
    while g <= BLK:
        h = g // 2
        at_mid = jnp.where(((ti & -g) + (h - 1)) == si, 1.0, 0.0).astype(BF16)
        bb = _dot(at_mid, pieces)
        b_r = bb[:, :LANES] + bb[:, LANES:2 * LANES] + bb[:, 2 * LANES:]
        after = (ri & h) != 0
        qg = jnp.where(after, q * jnp.exp2(jnp.minimum(b - b_r, 0.0)), 0.0).astype(BF16)
        kg = jnp.where(after, 0.0, k * jnp.exp2(jnp.minimum(b_r - b, 0.0))).astype(BF16)
        prod = _dot_nt(qg, kg)
        att = jnp.where((ti & -g) == (si & -g),
                        jnp.where((ti & h) != 0, jnp.where((si & h) == 0, prod, att), att), att)
        g *= 2
    return att


def _scan_prep(q, z, lb, cums, rev, with_inter):
    k, b = _scan_gates(z, lb, cums)

    e_row, m_row = (0, CHUNK // 2) if rev else (CHUNK - 1, CHUNK // 2 - 1)
    ends, mids = [], []
    for j in range(N_CHUNK):
        ends.append(b[j * CHUNK + e_row:j * CHUNK + e_row + 1, :])
        mids.append(b[j * CHUNK + m_row:j * CHUNK + m_row + 1, :])
    if rev:
        ends, mids = ends[::-1], mids[::-1]
    zero = jnp.zeros((1, LANES), F32)
    one = jnp.ones((1, LANES), F32)
    pres = [zero] + ends[:-1]
    b_pre = _rows_to_block(pres, rev)
    b_end = _rows_to_block(ends, rev)
    b_mid = _rows_to_block(mids, rev)

    qd = q * jnp.exp2(b - b_pre)
    ku = k * jnp.exp2(b_end - b)
    qm = q * jnp.exp2(jnp.clip(b - b_mid, -EXP2_CLAMP, EXP2_CLAMP))
    km = k * jnp.exp2(jnp.clip(b_mid - b, -EXP2_CLAMP, EXP2_CLAMP))

    levels = []
    for nc in (2, 4, 8):
        fq, fk = [], []
        for ju in range(N_CHUNK):
            r = (ju // nc) * nc + nc // 2 - 1
            if ju % nc >= nc // 2:
                fq.append(one if nc == 2 else jnp.exp2(pres[ju] - ends[r]))
                fk.append(zero)
            else:
                fq.append(zero)
                fk.append(one if nc == 2 else jnp.exp2(ends[r] - ends[ju]))
        levels.append(((qd * _rows_to_block(fq, rev)).astype(BF16), (ku * _rows_to_block(fk, rev)).astype(BF16)))

    last = ends[-1]
    qh = (qd * _rows_to_block([jnp.exp2(p) for p in pres], rev)).astype(BF16) if with_inter else None
    kh = (ku * _rows_to_block([jnp.exp2(last - e) for e in ends], rev)).astype(BF16)
    risk = functools.reduce(jnp.maximum, [jnp.maximum(p - m, m - e) for p, m, e in zip(pres, mids, ends)])
    return (qm.astype(BF16), km.astype(BF16)), levels, qh, kh, last, risk


def _scan_att(prep, lev):
    att = jnp.where(lev == 1, _dot_nt(*prep[0]), 0.0)
    for level, (ql, kl) in enumerate(prep[1], start=2):
        att = jnp.where(lev == level, _dot_nt(ql, kl), att)
    return att


def _scan_att_bidir(prep_f, prep_b, lev_f, lev_b):
    att = jnp.where(lev_f == 1, _dot_nt(*prep_f[0]), 0.0) + jnp.where(lev_b == 1, _dot_nt(*prep_b[0]), 0.0)
    lev = jnp.maximum(lev_f, lev_b)
    for level, ((qf, kf), (qb, kb)) in enumerate(zip(prep_f[1], prep_b[1]), start=2):
        both = _dot_nt(jnp.concatenate([qf, qb], axis=1), jnp.concatenate([kf, kb], axis=1))
        att = jnp.where(lev == level, both, att)
    return att


def _scan_state(prep, v, st_prev):
    ut = _dot_tn(v.astype(BF16), prep[3])
    return ut if st_prev is None else st_prev * jnp.exp2(prep[4]) + ut


def _scan_finish(o, g, on):
    ms = jnp.mean(o * o, axis=-1, keepdims=True)
    return ((o * lax.rsqrt(ms + EPS) * on) * _silu(g)).astype(BF16)


def _slot_view(ref, slot, owns_all_slots):
    if not owns_all_slots:
        return ref
    for s in range(ref.shape[0]):
        if s != slot:
            ref[s] = jnp.zeros(ref.shape[1:], ref.dtype)
    return ref.at[slot]


def _hgrn_kernel(*refs, n_blocks, has_state, slot, owns_all_slots, hps):
    it = iter(refs)
    x_ref, g_ref, sh_ref, sc_ref, gate_ref, win_ref, lb_ref, on_ref = (next(it) for _ in range(8))
    cf_ref, lf_ref, cb_ref, lvb_ref = (next(it) for _ in range(4))
    s0_ref = next(it) if has_state else None
    wout_ref, xo_ref = next(it), next(it)
    sfin_ref = None if has_state else next(it)
    h_ref, proj_ref, og_ref = next(it), next(it), next(it)
    oacc_ref, st_ref = (next(it), next(it)) if has_state else (None, None)

    h_ref[...] = _norm_mod(x_ref[...], g_ref[...], sc_ref[...], sh_ref[...]).astype(BF16)
    on = on_ref[...]
    if not has_state:
        sfin_ref = _slot_view(sfin_ref, slot, owns_all_slots)

    group_w = hps * HEAD_DIM

    def part(i, p, rows=slice(None)):
        return proj_ref[rows, p * group_w + i * HEAD_DIM:p * group_w + (i + 1) * HEAD_DIM]

    def pair(hp, carry):
        for p in range(HGRN_PARTS):
            cols = _block_at(p * D_MODEL + hp * group_w, group_w)
            proj_ref[:, p * group_w:(p + 1) * group_w] = _dot(h_ref[...], win_ref[:, cols])
        heads = [hp * hps + i for i in range(hps)]
        lbs = [lb_ref[:, _lane_block(hd, HEAD_DIM)] for hd in heads]

        def dyn_part(i, p, rows=slice(None)):
            return proj_ref[rows, pl.ds(pl.multiple_of(p * group_w + i * HEAD_DIM, HEAD_DIM), HEAD_DIM)]

        if not has_state:
            risk = jnp.zeros((1, LANES), F32)
            for i, hd in enumerate(heads):
                q, v = part(i, 0), part(i, 1)
                prep_f = _scan_prep(q, part(i, 2), lbs[i][0:1, :], cf_ref[...], False, False)
                prep_b = _scan_prep(q, part(i, 3), lbs[i][1:2, :], cb_ref[...], True, False)
                risk = jnp.maximum(risk, jnp.maximum(prep_f[5], prep_b[5]))
                att = _scan_att_bidir(prep_f, prep_b, lf_ref[...], lvb_ref[...])
                o = _dot(att.astype(BF16), v.astype(BF16))
                sfin_ref[0, hd] = _scan_state(prep_f, v, None).T
                sfin_ref[1, hd] = _scan_state(prep_b, v, None).T
                og_ref[:, _lane_block(hd, HEAD_DIM)] = _scan_finish(o, part(i, 4), on)

            @pl.when(jnp.max(risk) > SCAN_GUARD)
            def _():
                def exact_head(i, c2):
                    hd = hp * hps + i
                    lb = lb_ref[:, _lane_block(hd, HEAD_DIM)]
                    q = dyn_part(i, 0)
                    k_f, b_f = _scan_gates(dyn_part(i, 2), lb[0:1, :], cf_ref[...])
                    k_b, b_b = _scan_gates(dyn_part(i, 3), lb[1:2, :], cb_ref[...])
                    att = _scan_att_exact(q, k_f, b_f, False) + _scan_att_exact(q, k_b, b_b, True)
                    o = _dot(att.astype(BF16), dyn_part(i, 1).astype(BF16))
                    og_ref[:, _lane_block(hd, HEAD_DIM)] = _scan_finish(o, dyn_part(i, 4), on)
                    return c2

                lax.fori_loop(0, hps, exact_head, 0)
        else:
            def one_head(i, hd, lb, tb, exact, get):
                rf = pl.ds(pl.multiple_of(tb * BLK, BLK), BLK)
                rb = pl.ds(pl.multiple_of((n_blocks - 1 - tb) * BLK, BLK), BLK)
                cols = (slice(i * HEAD_DIM, (i + 1) * HEAD_DIM) if isinstance(i, int)
                        else pl.ds(pl.multiple_of(i * HEAD_DIM, HEAD_DIM), HEAD_DIM))
                q_f, q_b, v_f, v_b = get(i, 0, rf), get(i, 0, rb), get(i, 1, rf), get(i, 1, rb)
                prep_f = _scan_prep(q_f, get(i, 2, rf), lb[0:1, :], cf_ref[...], False, True)
                prep_b = _scan_prep(q_b, get(i, 3, rb), lb[1:2, :], cb_ref[...], True, True)
                if exact:
                    k_f, b_f = _scan_gates(get(i, 2, rf), lb[0:1, :], cf_ref[...])
                    k_b, b_b = _scan_gates(get(i, 3, rb), lb[1:2, :], cb_ref[...])
                    att_f = _scan_att_exact(q_f, k_f, b_f, False)
                    att_b = _scan_att_exact(q_b, k_b, b_b, True)
                else:
                    att_f, att_b = _scan_att(prep_f, lf_ref[...]), _scan_att(prep_b, lvb_ref[...])
                st_f, st_b = st_ref[2 * i], st_ref[2 * i + 1]
                of = _dot(att_f.astype(BF16), v_f.astype(BF16)) + _dot_nt(prep_f[2], st_f.astype(BF16))
                ob = _dot(att_b.astype(BF16), v_b.astype(BF16)) + _dot_nt(prep_b[2], st_b.astype(BF16))
                st_ref[2 * i] = _scan_state(prep_f, v_f, st_f)
                st_ref[2 * i + 1] = _scan_state(prep_b, v_b, st_b)

                @pl.when(2 * tb < n_blocks)
                def _():
                    oacc_ref[rf, cols] = of
                    oacc_ref[rb, cols] = ob

                @pl.when(2 * tb >= n_blocks)
                def _():
                    oacc_ref[rf, cols] += of
                    oacc_ref[rb, cols] += ob

                return jnp.maximum(prep_f[5], prep_b[5])

            for i, hd in enumerate(heads):
                st_ref[2 * i] = s0_ref[0, hd].T
                st_ref[2 * i + 1] = s0_ref[1, hd].T

            def body(tb, risk):
                for i, hd in enumerate(heads):
                    risk = jnp.maximum(risk, one_head(i, hd, lbs[i], tb, False, part))
                return risk

            risk = lax.fori_loop(0, n_blocks, body, jnp.zeros((1, LANES), F32))

            @pl.when(jnp.max(risk) > SCAN_GUARD)
            def _():
                def exact_head(i, c2):
                    hd = hp * hps + i
                    st_ref[2 * i] = s0_ref[0, hd].T
                    st_ref[2 * i + 1] = s0_ref[1, hd].T
                    lb = lb_ref[:, _lane_block(hd, HEAD_DIM)]

                    def exact_body(tb, c3):
                        one_head(i, hd, lb, tb, True, dyn_part)
                        return c3

                    lax.fori_loop(0, n_blocks, exact_body, 0)
                    return c2

                lax.fori_loop(0, hps, exact_head, 0)

            for i, hd in enumerate(heads):
                cols = slice(i * HEAD_DIM, (i + 1) * HEAD_DIM)
                og_ref[:, _lane_block(hd, HEAD_DIM)] = _scan_finish(oacc_ref[:, cols], part(i, 4), on)
        return carry

    _for_each_group(HEADS // hps, pair)
    xo_ref[...] = x_ref[...] + gate_ref[...] * _dot(og_ref[...], wout_ref[...])


def _hgrn_layer(x, n_seq, t, cond_of_seq, g, modr, w_in, lbs_j, onorm_g, w_out, s0, sfin_prev, slot):
    d = D_MODEL
    has_state = s0 is not None
    n_blocks = t // BLK
    assert n_blocks == 1 or n_blocks % 2 == 0
    full = lambda shape, **kw: pl.BlockSpec(shape, lambda b, *_: (0,) * len(shape), **kw)
    in_specs = [
        pl.BlockSpec((t, d), lambda b: (b, 0)),
        full((1, d)),
        _mod_spec(cond_of_seq, 0), _mod_spec(cond_of_seq, 1), _mod_spec(cond_of_seq, 2),
        full((d, HGRN_PARTS * d), **_RESIDENT),
        full((2, d)),
        full((1, HEAD_DIM)),
    ] + [full((BLK, BLK))] * 4
    args = [x, g.reshape(1, d), modr, modr, modr, w_in, lbs_j, onorm_g.reshape(1, HEAD_DIM)] + _scan_constants()
    state_block = (None, 2, HEADS, HEAD_DIM, HEAD_DIM)
    if has_state:
        in_specs.append(pl.BlockSpec(state_block, lambda b: (b, 0, 0, 0, 0)))
        args.append(s0)
    in_specs.append(full((d, d), **_RESIDENT))
    args.append(w_out)
    out_specs = [pl.BlockSpec((t, d), lambda b: (b, 0))]
    out_shape = [jax.ShapeDtypeStruct((n_seq * t, d), F32)]
    hps = HEADS_PER_STEP if has_state else SCAN_HEADS_PER_BODY
    group_w = hps * HEAD_DIM
    scratch = [pltpu.VMEM((t, d), BF16), pltpu.VMEM((t, HGRN_PARTS * group_w), F32), pltpu.VMEM((t, d), BF16)]
    aliases = {}
    if has_state:
        scratch += [pltpu.VMEM((t, group_w), F32), pltpu.VMEM((2 * hps, HEAD_DIM, HEAD_DIM), F32)]
    else:
        state_dims = (2, HEADS, HEAD_DIM, HEAD_DIM)
        if sfin_prev is None:
            out_specs.append(pl.BlockSpec((None, N_SLOTS) + state_dims, lambda b: (b, 0, 0, 0, 0, 0)))
        else:
            out_specs.append(pl.BlockSpec((None, None) + state_dims, lambda b: (b, slot, 0, 0, 0, 0)))
            in_specs.append(pl.BlockSpec(memory_space=pl.ANY))
            args.append(sfin_prev)
            aliases = {len(args) - 1: 1}
        out_shape.append(jax.ShapeDtypeStruct((n_seq, N_SLOTS) + state_dims, F32))

    def body(*refs):
        if sfin_prev is not None:
            n_in = len(args)
            refs = refs[:n_in - 1] + refs[n_in:]
        _hgrn_kernel(*refs, n_blocks=n_blocks, has_state=has_state, slot=slot, owns_all_slots=sfin_prev is None,
                     hps=hps)

    out = pl.pallas_call(
        body,
        grid=(n_seq,),
        in_specs=in_specs,
        out_specs=out_specs,
        out_shape=out_shape,
        scratch_shapes=scratch,
        input_output_aliases=aliases,
        compiler_params=_cparams(("parallel",)),
        name="hgrn_layer_sample" if has_state else "hgrn_layer_prompt",
    )(*args)
    return out if not has_state else (out[0], None)


def _rope_tables(t_lat):
    rows = t_lat // GRID_W
    row = jnp.repeat(jnp.arange(rows), GRID_W).astype(F32)
    col = jnp.tile(jnp.arange(GRID_W), rows).astype(F32)
    half = QK_DIM // 2
    inv_freq = ROPE_THETA ** (-jnp.arange(0, half, 2, dtype=F32) / half)
    ang_row = row[:, None] * inv_freq
    ang_col = col[:, None] * inv_freq

    def part(ang):
        c, s = jnp.cos(ang), jnp.sin(ang)
        return jnp.concatenate([c, c], axis=1), jnp.concatenate([-s, s], axis=1)

    cr, sr = part(ang_row)
    cc, sc = part(ang_col)
    cos64 = jnp.concatenate([cr, cc], axis=1)
    sin64 = jnp.concatenate([sr, sc], axis=1)
    return jnp.concatenate([cos64, cos64], axis=1), jnp.concatenate([sin64, sin64], axis=1)


def _rope(x, cos, sin):
    lane = lax.broadcasted_iota(jnp.int32, x.shape, 1)
    first = (lane % (QK_DIM // 2)) < (QK_DIM // 4)
    swapped = jnp.where(first, pltpu.roll(x, LANES - QK_DIM // 4, 1), pltpu.roll(x, QK_DIM // 4, 1))
    return x * cos + swapped * sin


def _attn_kernel(*refs, n_qblk, n_sub, rope, cache, emit_kv, lam_init, slot, owns_all_slots):
    it = iter(refs)
    x_ref, g_ref, sh_ref, sc_ref, gate_ref, wqkv_ref = (next(it) for _ in range(6))
    qg_ref, kg_ref, lam_ref, sg_ref, seg_ref = (next(it) for _ in range(5))
    cos_ref = sin_ref = ck_ref = cv_ref = nk_ref = nv_ref = None
    if rope:
        cos_ref, sin_ref = next(it), next(it)
    if cache:
        ck_ref, cv_ref = next(it), next(it)
    wout_ref, xo_ref = next(it), next(it)
    if emit_kv:
        nk_all, nv_all = next(it), next(it)
        nk_refs = [_slot_view(nk_all.at[u], slot, owns_all_slots) for u in range(n_sub)]
        nv_refs = [_slot_view(nv_all.at[u], slot, owns_all_slots) for u in range(n_sub)]
    h_ref, qkv_ref, oa_ref = next(it), next(it), next(it)
    t = n_qblk * BLK

    d = D_MODEL
    for u in range(n_sub):
        seq = slice(u * t, (u + 1) * t)
        h_ref[seq, :] = _norm_mod(x_ref[seq, :], g_ref[...], sc_ref[...], sh_ref[...]).astype(BF16)
        qkv_ref[seq, :] = _dot(h_ref[seq, :], wqkv_ref[...])
    seg = seg_ref[...]
    lp = lam_ref[...]
    lam = (jnp.exp(jnp.sum(lp[0:1, :] * lp[1:2, :], axis=-1, keepdims=True))
           - jnp.exp(jnp.sum(lp[2:3, :] * lp[3:4, :], axis=-1, keepdims=True)) + lam_init)
    lane = lax.broadcasted_iota(jnp.int32, (1, LANES), 1)
    comp0 = lane < QK_DIM
    scale = QK_DIM ** -0.5

    def split(a):
        return [jnp.where(comp0, a, 0.0).astype(BF16), jnp.where(comp0, 0.0, a).astype(BF16)]

    def one_head(hd, u):
        hcol = _lane_block(hd, HEAD_DIM)
        seq = slice(u * t, (u + 1) * t)
        k = qkv_ref[seq, _block_at(d + hd * HEAD_DIM, HEAD_DIM)]
        v = qkv_ref[seq, _block_at(2 * d + hd * HEAD_DIM, HEAD_DIM)]
        kn = k * lax.rsqrt(_dot(k * k, seg) + EPS) * kg_ref[...]
        if emit_kv:
            nk_refs[u][:, hcol] = kn
            nv_refs[u][:, hcol] = v
        if rope:
            kn = _rope(kn, cos_ref[...], sin_ref[...])
        ks = split(kn)
        vb = v.astype(BF16)
        if cache:
            cks = split(ck_ref[:, hcol])
            cvb = cv_ref[:, hcol].astype(BF16)
        for qi in range(n_qblk):
            pos = slice(qi * BLK, (qi + 1) * BLK)
            rows = slice(u * t + qi * BLK, u * t + (qi + 1) * BLK)
            q = qkv_ref[rows, hcol]
            qn = q * lax.rsqrt(_dot(q * q, seg) + EPS) * qg_ref[...]
            if rope:
                qn = _rope(qn, cos_ref[pos, :], sin_ref[pos, :])
            qb = (qn * scale).astype(BF16)
            a_self, a_cache = None, None
            for c in range(2):
                s = _dot_nt(qb, ks[c])
                m = jnp.max(s, axis=-1, keepdims=True)
                if cache:
                    sc = _dot_nt(qb, cks[c])
                    m = jnp.maximum(m, jnp.max(sc, axis=-1, keepdims=True))
                p = jnp.exp(s - m)
                den = jnp.sum(p, axis=-1, keepdims=True)
                if cache:
                    pc = jnp.exp(sc - m)
                    den = den + jnp.sum(pc, axis=-1, keepdims=True)
                w = (1.0 / den) if c == 0 else (-lam / den)
                a_self = p * w if c == 0 else a_self + p * w
                if cache:
                    a_cache = pc * w if c == 0 else a_cache + pc * w
            o = _dot(a_self.astype(BF16), vb)
            if cache:
                o = o + _dot(a_cache.astype(BF16), cvb)
            ms = jnp.mean(o * o, axis=-1, keepdims=True)
            o = (o * lax.rsqrt(ms + EPS) * sg_ref[...]) * (1.0 - lam_init)
            oa_ref[rows, hcol] = o.astype(BF16)

    heads_per_body = ATTN_HEADS_PER_BODY if n_qblk == 1 else HEADS_PER_STEP

    def group(hg, carry):
        for u in range(n_sub):
            for i in range(heads_per_body):
                one_head(hg * heads_per_body + i, u)
        return carry

    _for_each_group(HEADS // heads_per_body, group)
    for u in range(n_sub):
        seq = slice(u * t, (u + 1) * t)
        xo_ref[seq, :] = x_ref[seq, :] + gate_ref[...] * _dot(oa_ref[seq, :], wout_ref[...])


def _attn_layer(x, n_seq, t, cond_of_seq, g, modr, w_qkv, qn_g, kn_g, lam_p, subln_g, w_out, layer_idx,
                cache_k_j, cache_v_j, kv_prev, slot):
    d = D_MODEL
    cache = cache_k_j is not None
    lam_init = 0.8 - 0.6 * math.exp(-0.3 * layer_idx)
    qg = jnp.tile(qn_g.reshape(1, QK_DIM), (1, 2))
    kg = jnp.tile(kn_g.reshape(1, QK_DIM), (1, 2))
    li = np.arange(LANES)
    seg = jnp.asarray((li[:, None] // QK_DIM == li[None, :] // QK_DIM).astype(np.float32) / QK_DIM)
    full = lambda shape, **kw: pl.BlockSpec(shape, lambda b, *_: (0,) * len(shape), **kw)
    n_sub = 1 if cache else ATTN_SEQS_PER_STEP
    assert n_seq % n_sub == 0
    rows = n_sub * t
    in_specs = [
        pl.BlockSpec((rows, d), lambda b: (b, 0)),
        full((1, d)),
        _mod_spec(cond_of_seq, 0), _mod_spec(cond_of_seq, 1), _mod_spec(cond_of_seq, 2),
        full((d, 3 * d), **_RESIDENT),
        full((1, LANES)), full((1, LANES)), full((4, QK_DIM)), full((1, HEAD_DIM)), full((LANES, LANES)),
    ]
    args = [x, g.reshape(1, d), modr, modr, modr, w_qkv, qg, kg, lam_p, subln_g.reshape(1, HEAD_DIM), seg]
    if cache:
        cos, sin = _rope_tables(t)
        past = cache_k_j.shape[1]
        in_specs += [full((t, LANES)), full((t, LANES)),
                     pl.BlockSpec((None, past, d), lambda b: (b, 0, 0)),
                     pl.BlockSpec((None, past, d), lambda b: (b, 0, 0))]
        args += [cos, sin, cache_k_j.reshape(n_seq, past, d), cache_v_j.reshape(n_seq, past, d)]
    in_specs.append(full((d, d), **_RESIDENT))
    args.append(w_out)
    out_specs = [pl.BlockSpec((rows, d), lambda b: (b, 0))]
    out_shape = [jax.ShapeDtypeStruct((n_seq * t, d), F32)]
    aliases = {}
    n_carried = 0
    if not cache:
        if kv_prev is None:
            kv_spec = pl.BlockSpec((n_sub, N_SLOTS, t, d), lambda b: (b, 0, 0, 0))
        else:
            kv_spec = pl.BlockSpec((n_sub, None, t, d), lambda b: (b, slot, 0, 0))
            in_specs += [pl.BlockSpec(memory_space=pl.ANY)] * 2
            args += list(kv_prev)
            aliases = {len(args) - 2: 1, len(args) - 1: 2}
            n_carried = 2
        out_specs += [kv_spec, kv_spec]
        out_shape += [jax.ShapeDtypeStruct((n_seq, N_SLOTS, t, d), F32)] * 2

    def body(*refs):
        n_in = len(args)
        refs = refs[:n_in - n_carried] + refs[n_in:]
        _attn_kernel(*refs, n_qblk=t // BLK, n_sub=n_sub, rope=cache, cache=cache, emit_kv=not cache,
                     lam_init=lam_init, slot=slot, owns_all_slots=kv_prev is None)

    out = pl.pallas_call(
        body,
        grid=(n_seq // n_sub,),
        in_specs=in_specs,
        out_specs=out_specs,
        out_shape=out_shape,
        scratch_shapes=[pltpu.VMEM((rows, d), BF16), pltpu.VMEM((rows, 3 * d), F32), pltpu.VMEM((rows, d), BF16)],
        input_output_aliases=aliases,
        compiler_params=_cparams(("parallel",)),
        name="attn_layer_sample" if cache else "attn_layer_prompt",
    )(*args)
    return (out[0], None) if cache else (out[0], (out[1], out[2]))


ROUTE_ROWS = 32
INFO_GID, INFO_RANK = 8, 9


def _two_stream_specs(n_prompt_tiles, width):
    return [pl.BlockSpec((BLK, width), lambda i, *_: (jnp.minimum(i, n_prompt_tiles - 1), 0)),
            pl.BlockSpec((BLK, width), lambda i, *_: (jnp.maximum(i - n_prompt_tiles, 0), 0))]


def _route_kernel(xp_ref, xs_ref, g_ref, sh_ref, sc_ref, wr_ref, br_ref, tri_ref,
                  h_ref, il_ref, it_ref, cum_ref, cumhi_ref, tot_ref, carry_ref, *, n_prompt_tiles):
    i = pl.program_id(0)

    @pl.when(i == 0)
    def _():
        carry_ref[...] = jnp.zeros_like(carry_ref)

    @pl.when(i < n_prompt_tiles)
    def _():
        h_ref[...] = _norm_mod(xp_ref[...], g_ref[...], sc_ref[...], sh_ref[...]).astype(BF16)

    @pl.when(i >= n_prompt_tiles)
    def _():
        h_ref[...] = _norm_mod(xs_ref[...], g_ref[...], sc_ref[...], sh_ref[...]).astype(BF16)

    logit = _dot_nt(wr_ref[...], h_ref[...]) + br_ref[...]
    gl = [logit[g:g + 1, :] for g in range(MOE_GROUPS)]
    gmax = functools.reduce(jnp.maximum, gl)
    gz = functools.reduce(lambda a, b: a + b, [jnp.exp(x - gmax) for x in gl])
    g_w = 1.0 / gz
    gid = jnp.full_like(gmax, MOE_GROUPS - 1)
    for g in range(MOE_GROUPS - 2, -1, -1):
        gid = jnp.where(gl[g] == gmax, float(g), gid)
    el = []
    for j in range(MOE_EPG):
        e = logit[MOE_GROUPS + j:MOE_GROUPS + j + 1, :]
        for g in range(1, MOE_GROUPS):
            r = MOE_GROUPS + g * MOE_EPG + j
            e = jnp.where(gid == float(g), logit[r:r + 1, :], e)
        el.append(e)
    emax = functools.reduce(jnp.maximum, el)
    pe = [jnp.exp(e - emax) for e in el]
    idx1 = jnp.full_like(emax, MOE_EPG - 1)
    for j in range(MOE_EPG - 2, -1, -1):
        idx1 = jnp.where(el[j] == emax, float(j), idx1)
    el2 = [jnp.where(idx1 == float(j), -jnp.inf, el[j]) for j in range(MOE_EPG)]
    emax2 = functools.reduce(jnp.maximum, el2)
    idx2 = jnp.full_like(emax, MOE_EPG - 1)
    for j in range(MOE_EPG - 2, -1, -1):
        idx2 = jnp.where(el2[j] == emax2, float(j), idx2)
    sel = [(idx1 == float(j)) | (idx2 == float(j)) for j in range(MOE_EPG)]
    den = functools.reduce(lambda a, b: a + b, [jnp.where(sel[j], pe[j], 0.0) for j in range(MOE_EPG)])
    cw = [jnp.where(sel[j], pe[j] * (g_w / den), 0.0) for j in range(MOE_EPG)]

    row8 = lax.broadcasted_iota(jnp.int32, (8, BLK), 0)
    onehot = jnp.where(row8.astype(F32) == gid, 1.0, 0.0)
    within = _dot(onehot.astype(BF16), tri_ref[...])
    carry = carry_ref[...]
    rank = jnp.sum(onehot * (within + carry[:, 0:1]), axis=0, keepdims=True)
    cum_ref[...] = carry
    new_carry = carry + jnp.sum(onehot, axis=1, keepdims=True)
    carry_ref[...] = new_carry
    cumhi_ref[...] = new_carry
    tot_ref[...] = new_carry

    il_ref[...] = jnp.where(row8 == 0, gid, jnp.where(row8 == 1, rank, 0.0))
    rowl = lax.broadcasted_iota(jnp.int32, (LANES, BLK), 0)
    m = jnp.zeros((LANES, BLK), F32)
    for j in range(MOE_EPG):
        hi = cw[j].astype(BF16).astype(F32)
        m = jnp.where(rowl == j, hi, m)
        m = jnp.where(rowl == MOE_EPG + j, cw[j] - hi, m)
    m = jnp.where(rowl == INFO_GID, gid, m)
    m = jnp.where(rowl == INFO_RANK, rank, m)
    it_ref[...] = m.T


def _moe_route(xp, xs, g, modr, wr_t, br, cond_of_tile):
    d = D_MODEL
    npt = xp.shape[0] // BLK
    nt = npt + xs.shape[0] // BLK
    n = nt * BLK
    tri = jnp.asarray(np.triu(np.ones((BLK, BLK), np.float32), 1), BF16)
    full = lambda shape: pl.BlockSpec(shape, lambda i: (0,) * len(shape))
    return pl.pallas_call(
        functools.partial(_route_kernel, n_prompt_tiles=npt),
        grid=(nt,),
        in_specs=_two_stream_specs(npt, d) + [
            full((1, d)),
            _mod_spec(cond_of_tile, 3), _mod_spec(cond_of_tile, 4),
            full((ROUTE_ROWS, d)), full((ROUTE_ROWS, 1)), full((BLK, BLK)),
        ],
        out_specs=[
            pl.BlockSpec((BLK, d), lambda i: (i, 0)),
            pl.BlockSpec((8, BLK), lambda i: (0, i)),
            pl.BlockSpec((BLK, LANES), lambda i: (i, 0)),
            pl.BlockSpec((None, 8, LANES), lambda i: (i, 0, 0)),
            pl.BlockSpec((None, 8, LANES), lambda i: (i, 0, 0)),
            pl.BlockSpec((8, LANES), lambda i: (0, 0)),
        ],
        out_shape=[
            jax.ShapeDtypeStruct((n, d), BF16),
            jax.ShapeDtypeStruct((8, n), F32),
            jax.ShapeDtypeStruct((n, LANES), F32),
            jax.ShapeDtypeStruct((nt, 8, LANES), F32),
            jax.ShapeDtypeStruct((nt, 8, LANES), F32),
            jax.ShapeDtypeStruct((8, LANES), F32),
        ],
        scratch_shapes=[pltpu.VMEM((8, LANES), F32)],
        compiler_params=_cparams(("arbitrary",)),
        name="moe_route",
    )(xp, xs, g.reshape(1, d), modr, modr, wr_t, br, tri)


def _sorted_pos(gid, rank, rstart_ref):
    p = rank
    for g in range(MOE_GROUPS):
        p = p + jnp.where(gid == float(g), rstart_ref[g].astype(F32), 0.0)
    return p


def _gather_tile(tbl_ref, h_ref, il_ref, it_ref, hs_ref, cws_ref, a, used, *, n_tiles):
    win = GATHER_WIN * BLK
    rstart_ref = tbl_ref

    @pl.when(jnp.logical_not(used))
    def _():
        hs_ref[...] = jnp.zeros(hs_ref.shape, hs_ref.dtype)
        cws_ref[...] = jnp.zeros(cws_ref.shape, cws_ref.dtype)

    @pl.when(used)
    def _():
        dest = (lax.broadcasted_iota(jnp.int32, (BLK, 1), 0) + a * BLK).astype(F32)
        src_tile = lax.broadcasted_iota(jnp.int32, (1, win), 1) // BLK
        clo = tbl_ref[TBL_CLO * LANES + a]
        n_win = (tbl_ref[TBL_CHI * LANES + a] - clo + GATHER_WIN) // GATHER_WIN

        def window(w):
            first = clo + w * GATHER_WIN
            c0 = jnp.minimum(first, n_tiles - GATHER_WIN)
            rows = pl.ds(pl.multiple_of(c0 * BLK, BLK), win)
            info = il_ref[:, rows]
            p = _sorted_pos(info[0:1, :], info[1:2, :], rstart_ref)
            p = jnp.where(src_tile + c0 >= first, p, -1.0)
            onehot = jnp.where(dest == p, 1.0, 0.0).astype(BF16)
            r = _dot(onehot, it_ref[rows, :].astype(BF16))
            return _dot(onehot, h_ref[rows, :]), r + pltpu.roll(r, LANES - MOE_EPG, 1)

        dh, dc = window(0)
        hs_ref[...] = dh.astype(BF16)
        cws_ref[...] = dc

        def body(w, carry):
            dh, dc = window(w)
            hs_ref[...] = (hs_ref[...].astype(F32) + dh).astype(BF16)
            cws_ref[...] += dc
            return carry

        lax.fori_loop(1, n_win, body, 0)


def _gather_kernel(tbl_ref, h_ref, il_ref, it_ref, hs_ref, cws_ref, *, n_tiles):
    s = pl.program_id(0)
    nt = tbl_ref[TBL_SNT * LANES + s]

    def tile(j, carry):
        rows = pl.ds(pl.multiple_of(j * BLK, BLK), BLK)
        _gather_tile(tbl_ref, h_ref, il_ref, it_ref, hs_ref.at[rows, :], cws_ref.at[rows, :],
                     s * SUPER + j, j < nt, n_tiles=n_tiles)
        return carry

    lax.fori_loop(0, jnp.where(nt > 0, SUPER, 0), tile, 0)


def _moe_gather(h, info_lane, info_tok, tbl, hs_buf, cws_buf):
    n, d = h.shape
    nt = n // BLK
    n_super = hs_buf.shape[0] // SUPER_ROWS
    assert nt >= GATHER_WIN and n_super * SUPER <= LANES
    out_block = lambda s, tbl_r: (tbl_r[TBL_SBLK * LANES + s], 0)
    carried = pl.BlockSpec(memory_space=pl.ANY)

    def body(tbl_ref, h_ref, il_ref, it_ref, hs_old, cws_old, hs_ref, cws_ref):
        _gather_kernel(tbl_ref, h_ref, il_ref, it_ref, hs_ref, cws_ref, n_tiles=nt)

    return pl.pallas_call(
        body,
        grid_spec=pltpu.PrefetchScalarGridSpec(
            num_scalar_prefetch=1,
            grid=(n_super,),
            in_specs=[
                pl.BlockSpec((n, d), lambda a, *_: (0, 0), **_RESIDENT),
                pl.BlockSpec((8, n), lambda a, *_: (0, 0), **_RESIDENT),
                pl.BlockSpec((n, LANES), lambda a, *_: (0, 0), **_RESIDENT),
                carried, carried,
            ],
            out_specs=[pl.BlockSpec((SUPER_ROWS, d), out_block), pl.BlockSpec((SUPER_ROWS, LANES), out_block)],
        ),
        out_shape=[jax.ShapeDtypeStruct(hs_buf.shape, BF16), jax.ShapeDtypeStruct(cws_buf.shape, F32)],
        input_output_aliases={4: 0, 5: 1},
        compiler_params=_cparams(("arbitrary",)),
        name="moe_gather",
    )(tbl, h, info_lane, info_tok, hs_buf, cws_buf)


def _moe_mlp_kernel(tbl_ref, hs_ref, cws_ref, wg_ref, wu_ref, wd_ref, ys_ref,
                    acc_ref, wgb_ref, wub_ref, wdb_ref):
    s = pl.program_id(0)
    k = pl.program_id(1)
    nt = tbl_ref[TBL_SNT * LANES + s]

    @pl.when(nt > 0)
    def _():
        wgb_ref[...] = wg_ref[...].astype(BF16)
        wub_ref[...] = wu_ref[...].astype(BF16)
        wdb_ref[...] = wd_ref[...].astype(BF16)

    @pl.when((s == 0) & (k == 0))
    def _():
        acc_ref[...] = jnp.zeros_like(acc_ref)

    def skipped(j):
        rows = slice(j * EXPERT_TILES * BLK, (j + 1) * EXPERT_TILES * BLK)
        ys_ref[rows, :] = jnp.zeros((EXPERT_TILES * BLK, D_MODEL), BF16)

    def block(j):
        rows = slice(j * EXPERT_TILES * BLK, (j + 1) * EXPERT_TILES * BLK)
        hsub = hs_ref[rows, :]
        cws = cws_ref[rows, :]
        cwk = jnp.zeros((EXPERT_TILES * BLK, 1), F32)
        for kk in range(MOE_EPG):
            cwk = jnp.where(k == kk, cws[:, kk:kk + 1], cwk)
        y = None
        for fh in range(MOE_D_FF // FF_PART):
            fc = slice(fh * FF_PART, (fh + 1) * FF_PART)
            gate = _dot(hsub, wgb_ref[:, fc])
            up = _dot(hsub, wub_ref[:, fc])
            act = ((_silu(gate) * up) * cwk).astype(BF16)
            part = _dot(act, wdb_ref[fc, :])
            y = part if y is None else y + part
        total = jnp.where(k == 0, 0.0, acc_ref[rows, :]) + y
        acc_ref[rows, :] = total
        ys_ref[rows, :] = total.astype(BF16)

    def maybe_block(j):
        pl.when(j * EXPERT_TILES < nt)(functools.partial(block, j))
        pl.when((j * EXPERT_TILES >= nt) & (nt > 0) & (k == 0))(functools.partial(skipped, j))

    n_blocks = SUPER // EXPERT_TILES
    n_straight = STRAIGHT_TILES // EXPERT_TILES

    @pl.when(nt >= STRAIGHT_TILES)
    def _():
        for j in range(n_straight):
            block(j)

    @pl.when(nt < STRAIGHT_TILES)
    def _():
        for j in range(n_straight):
            maybe_block(j)

    for j in range(n_straight, n_blocks):
        maybe_block(j)


def _moe_mlp(hs, cws, w_gate, w_up, w_down, layer, tbl, ys_buf):
    d = hs.shape[1]
    n_super = hs.shape[0] // SUPER_ROWS

    def body(tbl_ref, hs_ref, cws_ref, wg_ref, wu_ref, wd_ref, ys_old, ys_ref, *scratch):
        _moe_mlp_kernel(tbl_ref, hs_ref, cws_ref, wg_ref, wu_ref, wd_ref, ys_ref, *scratch)

    def widx(s, k, tbl_r):
        kk = jnp.where(tbl_r[TBL_SNT * LANES + s] > 0, k, MOE_EPG - 1)
        return (layer * MOE_EXPERTS + tbl_r[TBL_SGROUP * LANES + s] * MOE_EPG + kk, 0, 0)

    rows_idx = lambda s, k, tbl_r: (tbl_r[TBL_SBLK * LANES + s], 0)
    return pl.pallas_call(
        body,
        grid_spec=pltpu.PrefetchScalarGridSpec(
            num_scalar_prefetch=1,
            grid=(n_super, MOE_EPG),
            in_specs=[
                pl.BlockSpec((SUPER_ROWS, d), rows_idx),
                pl.BlockSpec((SUPER_ROWS, LANES), rows_idx),
                pl.BlockSpec((None, d, MOE_D_FF), widx),
                pl.BlockSpec((None, d, MOE_D_FF), widx),
                pl.BlockSpec((None, MOE_D_FF, d), widx),
                pl.BlockSpec(memory_space=pl.ANY),
            ],
            out_specs=pl.BlockSpec((SUPER_ROWS, d), rows_idx),
            scratch_shapes=[
                pltpu.VMEM((SUPER_ROWS, d), F32),
                pltpu.VMEM((d, MOE_D_FF), BF16),
                pltpu.VMEM((d, MOE_D_FF), BF16),
                pltpu.VMEM((MOE_D_FF, d), BF16),
            ],
        ),
        out_shape=jax.ShapeDtypeStruct(ys_buf.shape, BF16),
        input_output_aliases={6: 0},
        compiler_params=_cparams(("arbitrary", "arbitrary")),
        name="moe_experts",
    )(tbl, hs, cws, w_gate, w_up, w_down, ys_buf)


N_SRC = 2 * MOE_GROUPS


def _unsort_kernel(tbl_ref, xp_ref, xs_ref, it_ref, gate_ref, *rest, n_prompt_tiles):
    ys_refs, op_ref, os_ref = rest[:N_SRC], rest[N_SRC], rest[N_SRC + 1]
    t = pl.program_id(0)
    info = it_ref[...]
    p = _sorted_pos(info[:, INFO_GID:INFO_GID + 1], info[:, INFO_RANK:INFO_RANK + 1], tbl_ref)
    lane = lax.broadcasted_iota(jnp.int32, (1, BLK), 1).astype(F32)
    slot = lambda m: tbl_ref[(TBL_UT + t) * LANES + m]

    def take(m):
        a = slot(m)
        onehot = jnp.where(p - (a * BLK).astype(F32) == lane, 1.0, 0.0).astype(BF16)
        return _dot(onehot, ys_refs[m][...])

    def stream(x_ref, o_ref):
        first = functools.reduce(lambda u, w: u + w, [take(m) for m in range(0, N_SRC, 2)])
        o_ref[...] = x_ref[...] + gate_ref[...] * first
        for m in range(1, N_SRC, 2):
            @pl.when(slot(m) >= 0)
            def _():
                o_ref[...] += gate_ref[...] * take(m)

    pl.when(t < n_prompt_tiles)(functools.partial(stream, xp_ref, op_ref))
    pl.when(t >= n_prompt_tiles)(functools.partial(stream, xs_ref, os_ref))


def _moe_unsort(xp, xs, info_tok, modr, ys, tbl, cond_of_tile):
    d = D_MODEL
    npt = xp.shape[0] // BLK
    nt = npt + xs.shape[0] // BLK

    def ys_spec(m):
        return pl.BlockSpec((BLK, d), lambda t, tbl_r: (jnp.maximum(tbl_r[(TBL_UT + t) * LANES + m], 0), 0))

    return pl.pallas_call(
        functools.partial(_unsort_kernel, n_prompt_tiles=npt),
        grid_spec=pltpu.PrefetchScalarGridSpec(
            num_scalar_prefetch=1,
            grid=(nt,),
            in_specs=_two_stream_specs(npt, d) + [
                pl.BlockSpec((BLK, LANES), lambda t, *_: (t, 0)),
                _mod_spec(cond_of_tile, 5),
            ] + [ys_spec(m) for m in range(N_SRC)],
            out_specs=_two_stream_specs(npt, d),
        ),
        out_shape=[jax.ShapeDtypeStruct(xp.shape, F32), jax.ShapeDtypeStruct(xs.shape, F32)],
        compiler_params=_cparams(("arbitrary",)),
        name="moe_unsort_residual",
    )(tbl, xp, xs, info_tok, modr, *([ys] * N_SRC))


def _tables_kernel(lo_ref, hi_ref, tot_ref, tbl_ref, *, n_tiles, n_super):
    one = lambda cond: jnp.where(cond, 1.0, 0.0)
    groups = range(MOE_GROUPS)
    lane = lax.broadcasted_iota(jnp.int32, (1, LANES), 1).astype(F32)
    tot = [tot_ref[g:g + 1, :] for g in groups]
    pick = lambda vals, idx: sum(jnp.where(idx == float(g), vals[g], 0.0) for g in groups)

    n_sup = [sum(one(tot[g] > float(m * SUPER_ROWS)) for m in range(n_super)) for g in groups]
    sup_start, sup_end, run = [], [], 0.0
    for g in groups:
        sup_start.append(run + 0.0 * tot[g])
        run = run + n_sup[g]
        sup_end.append(run)
    n_used = sup_end[-1]
    rstart = [sup_start[g] * float(SUPER_ROWS) for g in groups]
    group_of = lambda s: jnp.minimum(sum(one(s >= sup_end[g]) for g in groups), float(MOE_GROUPS - 1))

    sg, snt, written = [], [], []
    for m in range(n_super):
        used = float(m) < n_used
        g_m = group_of(jnp.where(used, float(m), n_used - 1.0))
        rows_left = pick(tot, g_m) - (float(m) - pick(sup_start, g_m)) * float(SUPER_ROWS)
        nt_m = jnp.where(used, jnp.clip(jnp.floor((rows_left + float(BLK - 1)) * (1.0 / BLK)), 0.0, float(SUPER)), 0.0)
        sg.append(g_m)
        snt.append(nt_m)
        written.append(float(EXPERT_TILES) * jnp.floor((nt_m + float(EXPERT_TILES - 1)) * (1.0 / EXPERT_TILES)))
    sblk = [jnp.minimum(float(m), n_used - 1.0) for m in range(n_super)]
    by_super_lane = lambda vals: sum(jnp.where(lane == float(m), vals[m], 0.0) for m in range(n_super))
    last_written = sum(jnp.where(n_used - 1.0 == float(m), float(m * SUPER) + written[m] - 1.0, 0.0)
                       for m in range(n_super))

    sa = sum(one(lane >= float(m * SUPER)) for m in range(1, n_super))
    ja = lane - sa * float(SUPER)
    at_tile = lambda vals: sum(jnp.where(sa == float(m), vals[m], 0.0) for m in range(n_super))
    snt_a, ga, written_a = at_tile(snt), at_tile(sg), at_tile(written)
    code = jnp.where(ja < snt_a, float(CODE_GATHER), jnp.where(ja < written_a, float(CODE_ZERO), float(CODE_SKIP)))
    oblk = jnp.where(ja < written_a, lane,
                     jnp.where(written_a > 0.0, sa * float(SUPER) + written_a - 1.0, last_written))
    r0 = ((sa - pick(sup_start, ga)) * float(SUPER) + ja) * float(BLK)
    lo = [lo_ref[:, g, :] for g in groups]
    hi = [hi_ref[:, g, :] for g in groups]
    clo = jnp.sum(one(pick(hi, ga) <= r0), axis=0, keepdims=True)
    chi = jnp.sum(one(pick(lo, ga) < r0 + float(BLK)), axis=0, keepdims=True) - 1.0
    clo = jnp.clip(clo, 0.0, float(n_tiles - 1))
    chi = jnp.clip(chi, clo, float(n_tiles - 1))

    gm = one(lane >= 2.0) + one(lane >= 4.0) + one(lane >= 6.0)
    first = pick(rstart, gm) + pick(lo, gm)
    last = pick(rstart, gm) + pick(hi, gm) - 1.0
    t0 = jnp.floor(first * (1.0 / BLK))
    t1 = jnp.floor(last * (1.0 / BLK))
    has = pick(hi, gm) > pick(lo, gm)
    a0 = jnp.where(has, t0, -1.0)
    a1 = jnp.where(has, jnp.where(t1 != t0, t1, -1.0), -1.0)
    ut = jnp.where(lane >= float(N_SRC), -1.0, jnp.where(lane - 2.0 * gm == 0.0, a0, a1))

    def put(row, v):
        tbl_ref[row:row + 1, :] = v.astype(jnp.int32)

    put(TBL_RSTART, sum(jnp.where(lane == float(g), rstart[g], 0.0) for g in groups))
    put(TBL_SGROUP, by_super_lane(sg))
    put(TBL_SNT, by_super_lane(snt))
    put(TBL_SBLK, by_super_lane(sblk))
    put(TBL_CODE, code)
    put(TBL_CLO, clo)
    put(TBL_CHI, chi)
    put(TBL_OBLK, oblk)
    tbl_ref[TBL_UT:TBL_UT + n_tiles, :] = ut.astype(jnp.int32)


def _moe_tables(cum_lo, cum_hi, tot, n_tiles, n_super):
    assert n_super * SUPER <= LANES
    tbl = pl.pallas_call(
        functools.partial(_tables_kernel, n_tiles=n_tiles, n_super=n_super),
        out_shape=jax.ShapeDtypeStruct((TBL_UT + n_tiles, LANES), jnp.int32),
        name="moe_tables",
    )(cum_lo, cum_hi, tot)
    return tbl.reshape(-1)


def _moe_buffers(n_tokens):
    n_super = (n_tokens - 1) // SUPER_ROWS + MOE_GROUPS
    rows = n_super * SUPER_ROWS
    return (jnp.zeros((rows, D_MODEL), BF16), jnp.zeros((rows, LANES), F32), jnp.zeros((rows, D_MODEL), BF16))


def _moe_layer(xp, xs, bufs, g, modr, wr_t, br, w_gate, w_up, w_down, layer, cond_of_tile):
    hs, cws, ys = bufs
    nt = (xp.shape[0] + xs.shape[0]) // BLK
    h, info_lane, info_tok, cum_lo, cum_hi, tot = _moe_route(xp, xs, g, modr, wr_t, br, cond_of_tile)
    tbl = _moe_tables(cum_lo, cum_hi, tot, nt, hs.shape[0] // SUPER_ROWS)
    hs, cws = _moe_gather(h, info_lane, info_tok, tbl, hs, cws)
    ys = _moe_mlp(hs, cws, w_gate, w_up, w_down, layer, tbl, ys)
    xp, xs = _moe_unsort(xp, xs, info_tok, modr, ys, tbl, cond_of_tile)
    return xp, xs, (hs, cws, ys)


def kernel(x_prompt, x_sample, c, cache_k, cache_v, state_hgrn, c_ctx, norm_g, w_ada, b_ada, hgrn_w_in, hgrn_lb_logits, hgrn_onorm_g, hgrn_w_out, attn_w_qkv, attn_qn_g, attn_kn_g, attn_lambda, attn_subln_g, attn_w_out, moe_w_group, moe_b_group, moe_w_expert, moe_b_expert, moe_w_gate, moe_w_up, moe_w_down):
    n_prompt_seq, seq, d = x_prompt.shape
    dec_batch, dec_seq, _ = x_sample.shape
    n_prompt = n_prompt_seq * seq
    assert d == D_MODEL and seq == BLK and dec_seq % BLK == 0
    assert 1 + dec_batch <= N_COND

    xp = x_prompt.reshape(n_prompt, d)
    xs = x_sample.reshape(dec_batch * dec_seq, d)
    cond = jnp.zeros((N_COND, d), F32).at[0].set(c_ctx).at[1:1 + dec_batch].set(c)
    mod = _modulation(cond, w_ada, b_ada)

    lbs = jnp.cumsum(jax.nn.softmax(hgrn_lb_logits.astype(F32), axis=0), axis=0)
    lbs = lbs - lbs[0:1]

    cond_prompt = lambda b: 0
    cond_sample = lambda b: 1 + b
    npt, spt = n_prompt // BLK, dec_seq // BLK
    cond_tile = lambda i: jnp.where(i < npt, 0, 1 + (i - npt) // spt)

    w_in = hgrn_w_in.astype(BF16)
    w_hout = hgrn_w_out.astype(BF16)
    w_qkv = attn_w_qkv.astype(BF16)
    w_aout = attn_w_out.astype(BF16)
    wr_t = jnp.zeros((DEPTH, ROUTE_ROWS, d), F32)
    wr_t = wr_t.at[:, :MOE_GROUPS].set(moe_w_group.transpose(0, 2, 1))
    wr_t = wr_t.at[:, MOE_GROUPS:MOE_GROUPS + MOE_EXPERTS].set(moe_w_expert.transpose(0, 2, 1)).astype(BF16)
    br = jnp.zeros((DEPTH, ROUTE_ROWS, 1), F32)
    br = br.at[:, :MOE_GROUPS, 0].set(moe_b_group).at[:, MOE_GROUPS:MOE_GROUPS + MOE_EXPERTS, 0].set(moe_b_expert)
    w_gate = moe_w_gate.reshape(DEPTH * MOE_EXPERTS, d, MOE_D_FF)
    w_up = moe_w_up.reshape(DEPTH * MOE_EXPERTS, d, MOE_D_FF)
    w_down = moe_w_down.reshape(DEPTH * MOE_EXPERTS, MOE_D_FF, d)

    sfin, kv = None, None
    bufs = _moe_buffers(xp.shape[0] + xs.shape[0])
    for i in range(DEPTH):
        j = i // 2
        modr = mod[i].reshape(N_COND * N_MOD, 1, d)
        if i % 2 == 0:
            common = (norm_g[i, 0], modr, w_in[j], lbs[j], hgrn_onorm_g[j], w_hout[j])
            xp, sfin = _hgrn_layer(xp, n_prompt_seq, seq, cond_prompt, *common, None, sfin, j)
            xs, _ = _hgrn_layer(xs, dec_batch, dec_seq, cond_sample, *common, state_hgrn[:, j], None, j)
        else:
            common = (norm_g[i, 0], modr, w_qkv[j], attn_qn_g[j], attn_kn_g[j], attn_lambda[j], attn_subln_g[j],
                      w_aout[j], i)
            xp, kv = _attn_layer(xp, n_prompt_seq, seq, cond_prompt, *common, None, None, kv, j)
            xs, _ = _attn_layer(xs, dec_batch, dec_seq, cond_sample, *common, cache_k[:, j], cache_v[:, j], None, j)
        xp, xs, bufs = _moe_layer(xp, xs, bufs, norm_g[i, 1], modr, wr_t[i], br[i], w_gate, w_up, w_down, i, cond_tile)

    new_k = kv[0].reshape(n_prompt_seq, DEPTH // 2, seq, HEADS, 2, QK_DIM)
    new_v = kv[1].reshape(n_prompt_seq, DEPTH // 2, seq, HEADS, HEAD_DIM)
    return (xp.reshape(n_prompt_seq, seq, d), xs.reshape(dec_batch, dec_seq, d), new_k, new_v, sfin)
```

```python
import functools
import math

import numpy as np
import jax
import jax.numpy as jnp
from jax import lax
from jax.experimental import pallas as pl
from jax.experimental.pallas import tpu as pltpu

F32 = jnp.float32
BF16 = jnp.bfloat16

D_MODEL = 1024
DEPTH = 4
GRID_W = 64
HEADS = 8
HEAD_DIM = 128
QK_DIM = 64
ROPE_THETA = 10000.0
MOE_GROUPS = 4
MOE_EPG = 4
MOE_EXPERTS = MOE_GROUPS * MOE_EPG
MOE_D_FF = 512
EPS = 1e-6
N_COND = 8
N_MOD = 6
HGRN_PARTS = 5

LANES = 128
BLK = 256
CHUNK = 32
N_CHUNK = BLK // CHUNK
HEADS_PER_STEP = 2
SCAN_HEADS_PER_STEP = 4
SCAN_HEADS_PER_BODY = 8
ATTN_HEADS_PER_BODY = 8
ATTN_SEQS_PER_STEP = 2
SUPER = 12
SUPER_ROWS = SUPER * BLK
STRAIGHT_TILES = 8
TBL_RSTART, TBL_SGROUP, TBL_SNT, TBL_SBLK, TBL_CODE, TBL_CLO, TBL_CHI, TBL_OBLK, TBL_UT = range(9)
CODE_SKIP, CODE_GATHER, CODE_ZERO = 0, 1, 2
EXPERT_TILES = 2
FF_PART = 256
GATHER_WIN = 6
EXP2_CLAMP = 115.0
SCAN_GUARD = 100.0
VMEM_LIMIT = 56 * 1024 * 1024
N_SLOTS = DEPTH // 2
_RESIDENT = dict(pipeline_mode=pl.Buffered(1))


def _cparams(sem):
    return pltpu.CompilerParams(dimension_semantics=sem, vmem_limit_bytes=VMEM_LIMIT)


def _silu(x):
    return x * jax.nn.sigmoid(x)


def _dot(a, b):
    return jnp.dot(a, b, preferred_element_type=F32)


def _dot_nt(a, b):
    return lax.dot_general(a, b, (((1,), (1,)), ((), ())), preferred_element_type=F32)


def _dot_tn(a, b):
    return lax.dot_general(a, b, (((0,), (0,)), ((), ())), preferred_element_type=F32)


def _lane_block(i, width):
    return _block_at(i * width, width)


def _block_at(start, width):
    if isinstance(start, int):
        return slice(start, start + width)
    return pl.ds(pl.multiple_of(start, width), width)


def _for_each_group(n_groups, body):
    if n_groups == 1:
        body(0, 0)
    else:
        lax.fori_loop(0, n_groups, body, 0)


def _mod_kernel(c_ref, w_ref, b_ref, o_ref):
    o_ref[...] = _dot(_silu(c_ref[...]), w_ref[...]) + b_ref[...]


def _modulation(cond, w_ada, b_ada):
    tn = 1536
    nj = (N_MOD * D_MODEL) // tn
    return pl.pallas_call(
        _mod_kernel,
        grid=(DEPTH, nj),
        in_specs=[
            pl.BlockSpec((N_COND, D_MODEL), lambda l, j: (0, 0)),
            pl.BlockSpec((None, D_MODEL, tn), lambda l, j: (l, 0, j)),
            pl.BlockSpec((None, 1, tn), lambda l, j: (l, 0, j)),
        ],
        out_specs=pl.BlockSpec((None, N_COND, tn), lambda l, j: (l, 0, j)),
        out_shape=jax.ShapeDtypeStruct((DEPTH, N_COND, N_MOD * D_MODEL), F32),
        compiler_params=_cparams(("parallel", "parallel")),
        name="modulation",
    )(cond, w_ada, b_ada.reshape(DEPTH, 1, N_MOD * D_MODEL))


def _norm_mod(x, g, sc, sh):
    ms = jnp.mean(x * x, axis=-1, keepdims=True)
    return (x * lax.rsqrt(ms + EPS) * g) * (1.0 + sc) + sh


def _mod_spec(cond_of_step, which):
    return pl.BlockSpec((None, 1, D_MODEL), lambda i, *_: (cond_of_step(i) * N_MOD + which, 0, 0))


def _scan_constants():
    t = np.arange(BLK)
    out = []
    for rev in (False, True):
        u = (BLK - 1 - t) if rev else t
        ut, us = u[:, None], u[None, :]
        cums = (us <= ut).astype(np.float32)
        lev = np.where(us > ut, 0,
              np.where(ut // CHUNK == us // CHUNK, 1,
              np.where(ut // 64 == us // 64, 2,
              np.where(ut // 128 == us // 128, 3, 4)))).astype(np.int32)
        out += [jnp.asarray(cums, BF16), jnp.asarray(lev)]
    return out


def _rows_to_block(rows, rev):
    order = rows[::-1] if rev else rows
    return jnp.concatenate([jnp.broadcast_to(r, (CHUNK, LANES)) for r in order], axis=0)


def _trunc_bf16(x):
    return lax.bitcast_convert_type(lax.bitcast_convert_type(x, jnp.int32) & jnp.int32(-65536), F32)


def _scan_gates(z, lb, cums):
    sig = jax.nn.sigmoid(z)
    f = lb + (1.0 - lb) * sig
    logf = jnp.log2(f)
    k = (1.0 - lb) * (1.0 - sig)
    hi32 = _trunc_bf16(logf)
    bb = _dot(cums, jnp.concatenate([hi32.astype(BF16), (logf - hi32).astype(BF16)], axis=1))
    return k, bb[:, :LANES] + bb[:, LANES:]


def _scan_att_exact(q, k, b, rev):
    ti = lax.broadcasted_iota(jnp.int32, (BLK, BLK), 0)
    si = lax.broadcasted_iota(jnp.int32, (BLK, BLK), 1)
    ri = lax.broadcasted_iota(jnp.int32, (BLK, 1), 0)
    if rev:
        ti, si, ri = BLK - 1 - ti, BLK - 1 - si, BLK - 1 - ri
    h1 = _trunc_bf16(b)
    h2 = _trunc_bf16(b - h1)
    pieces = jnp.concatenate([h1.astype(BF16), h2.astype(BF16), (b - h1 - h2).astype(BF16)], axis=1)
    att = jnp.where(ti == si, jnp.sum(q * k, axis=-1, keepdims=True), 0.0)
    g = 2
    while g <= BLK:
        h = g // 2
        at_mid = jnp.where(((ti & -g) + (h - 1)) == si, 1.0, 0.0).astype(BF16)
        bb = _dot(at_mid, pieces)
        b_r = bb[:, :LANES] + bb[:, LANES:2 * LANES] + bb[:, 2 * LANES:]
        after = (ri & h) != 0
        qg = jnp.where(after, q * jnp.exp2(jnp.minimum(b - b_r, 0.0)), 0.0).astype(BF16)
        kg = jnp.where(after, 0.0, k * jnp.exp2(jnp.minimum(b_r - b, 0.0))).astype(BF16)
        prod = _dot_nt(qg, kg)
        att = jnp.where((ti & -g) == (si & -g),
                        jnp.where((ti & h) != 0, jnp.where((si & h) == 0, prod, att), att), att)
        g *= 2
    return att


def _scan_prep(q, z, lb, cums, rev, with_inter):
    k, b = _scan_gates(z, lb, cums)

    e_row, m_row = (0, CHUNK // 2) if rev else (CHUNK - 1, CHUNK // 2 - 1)
    ends, mids = [], []
    for j in range(N_CHUNK):
        ends.append(b[j * CHUNK + e_row:j * CHUNK + e_row + 1, :])
        mids.append(b[j * CHUNK + m_row:j * CHUNK + m_row + 1, :])
    if rev:
        ends, mids = ends[::-1], mids[::-1]
    zero = jnp.zeros((1, LANES), F32)
    one = jnp.ones((1, LANES), F32)
    pres = [zero] + ends[:-1]
    b_pre = _rows_to_block(pres, rev)
    b_end = _rows_to_block(ends, rev)
    b_mid = _rows_to_block(mids, rev)

    qd = q * jnp.exp2(b - b_pre)
    ku = k * jnp.exp2(b_end - b)
    qm = q * jnp.exp2(jnp.clip(b - b_mid, -EXP2_CLAMP, EXP2_CLAMP))
    km = k * jnp.exp2(jnp.clip(b_mid - b, -EXP2_CLAMP, EXP2_CLAMP))

    levels = []
    for nc in (2, 4, 8):
        fq, fk = [], []
        for ju in range(N_CHUNK):
            r = (ju // nc) * nc + nc // 2 - 1
            if ju % nc >= nc // 2:
                fq.append(one if nc == 2 else jnp.exp2(pres[ju] - ends[r]))
                fk.append(zero)
            else:
                fq.append(zero)
                fk.append(one if nc == 2 else jnp.exp2(ends[r] - ends[ju]))
        levels.append(((qd * _rows_to_block(fq, rev)).astype(BF16), (ku * _rows_to_block(fk, rev)).astype(BF16)))

    last = ends[-1]
    qh = (qd * _rows_to_block([jnp.exp2(p) for p in pres], rev)).astype(BF16) if with_inter else None
    kh = (ku * _rows_to_block([jnp.exp2(last - e) for e in ends], rev)).astype(BF16)
    risk = functools.reduce(jnp.maximum, [jnp.maximum(p - m, m - e) for p, m, e in zip(pres, mids, ends)])
    return (qm.astype(BF16), km.astype(BF16)), levels, qh, kh, last, risk


def _scan_att(prep, lev):
    att = jnp.where(lev == 1, _dot_nt(*prep[0]), 0.0)
    for level, (ql, kl) in enumerate(prep[1], start=2):
        att = jnp.where(lev == level, _dot_nt(ql, kl), att)
    return att


def _scan_att_bidir(prep_f, prep_b, lev_f, lev_b):
    att = jnp.where(lev_f == 1, _dot_nt(*prep_f[0]), 0.0) + jnp.where(lev_b == 1, _dot_nt(*prep_b[0]), 0.0)
    lev = jnp.maximum(lev_f, lev_b)
    for level, ((qf, kf), (qb, kb)) in enumerate(zip(prep_f[1], prep_b[1]), start=2):
        both = _dot_nt(jnp.concatenate([qf, qb], axis=1), jnp.concatenate([kf, kb], axis=1))
        att = jnp.where(lev == level, both, att)
    return att


def _scan_state(prep, v, st_prev):
    ut = _dot_tn(v.astype(BF16), prep[3])
    return ut if st_prev is None else st_prev * jnp.exp2(prep[4]) + ut


def _scan_finish(o, g, on):
    ms = jnp.mean(o * o, axis=-1, keepdims=True)
    return ((o * lax.rsqrt(ms + EPS) * on) * _silu(g)).astype(BF16)


def _slot_view(ref, slot, owns_all_slots):
    if not owns_all_slots:
        return ref
    for s in range(ref.shape[0]):
        if s != slot:
            ref[s] = jnp.zeros(ref.shape[1:], ref.dtype)
    return ref.at[slot]


def _hgrn_kernel(*refs, n_blocks, has_state, slot, owns_all_slots, hps):
    it = iter(refs)
    x_ref, g_ref, sh_ref, sc_ref, gate_ref, win_ref, lb_ref, on_ref = (next(it) for _ in range(8))
    cf_ref, lf_ref, cb_ref, lvb_ref = (next(it) for _ in range(4))
    s0_ref = next(it) if has_state else None
    wout_ref, xo_ref = next(it), next(it)
    sfin_ref = None if has_state else next(it)
    h_ref, proj_ref, og_ref = next(it), next(it), next(it)
    oacc_ref, st_ref = (next(it), next(it)) if has_state else (None, None)

    h_ref[...] = _norm_mod(x_ref[...], g_ref[...], sc_ref[...], sh_ref[...]).astype(BF16)
    on = on_ref[...]
    if not has_state:
        sfin_ref = _slot_view(sfin_ref, slot, owns_all_slots)

    group_w = hps * HEAD_DIM

    def part(i, p, rows=slice(None)):
        return proj_ref[rows, p * group_w + i * HEAD_DIM:p * group_w + (i + 1) * HEAD_DIM]

    def pair(hp, carry):
        for p in range(HGRN_PARTS):
            cols = _block_at(p * D_MODEL + hp * group_w, group_w)
            proj_ref[:, p * group_w:(p + 1) * group_w] = _dot(h_ref[...], win_ref[:, cols])
        heads = [hp * hps + i for i in range(hps)]
        lbs = [lb_ref[:, _lane_block(hd, HEAD_DIM)] for hd in heads]

        def dyn_part(i, p, rows=slice(None)):
            return proj_ref[rows, pl.ds(pl.multiple_of(p * group_w + i * HEAD_DIM, HEAD_DIM), HEAD_DIM)]

        if not has_state:
            risk = jnp.zeros((1, LANES), F32)
            for i, hd in enumerate(heads):
                q, v = part(i, 0), part(i, 1)
                prep_f = _scan_prep(q, part(i, 2), lbs[i][0:1, :], cf_ref[...], False, False)
                prep_b = _scan_prep(q, part(i, 3), lbs[i][1:2, :], cb_ref[...], True, False)
                risk = jnp.maximum(risk, jnp.maximum(prep_f[5], prep_b[5]))
                att = _scan_att_bidir(prep_f, prep_b, lf_ref[...], lvb_ref[...])
                o = _dot(att.astype(BF16), v.astype(BF16))
                sfin_ref[0, hd] = _scan_state(prep_f, v, None).T
                sfin_ref[1, hd] = _scan_state(prep_b, v, None).T
                og_ref[:, _lane_block(hd, HEAD_DIM)] = _scan_finish(o, part(i, 4), on)

            @pl.when(jnp.max(risk) > SCAN_GUARD)
            def _():
                def exact_head(i, c2):
                    hd = hp * hps + i
                    lb = lb_ref[:, _lane_block(hd, HEAD_DIM)]
                    q = dyn_part(i, 0)
                    k_f, b_f = _scan_gates(dyn_part(i, 2), lb[0:1, :], cf_ref[...])
                    k_b, b_b = _scan_gates(dyn_part(i, 3), lb[1:2, :], cb_ref[...])
                    att = _scan_att_exact(q, k_f, b_f, False) + _scan_att_exact(q, k_b, b_b, True)
                    o = _dot(att.astype(BF16), dyn_part(i, 1).astype(BF16))
                    og_ref[:, _lane_block(hd, HEAD_DIM)] = _scan_finish(o, dyn_part(i, 4), on)
                    return c2

                lax.fori_loop(0, hps, exact_head, 0)
        else:
            def one_head(i, hd, lb, tb, exact, get):
                rf = pl.ds(pl.multiple_of(tb * BLK, BLK), BLK)
                rb = pl.ds(pl.multiple_of((n_blocks - 1 - tb) * BLK, BLK), BLK)
                cols = (slice(i * HEAD_DIM, (i + 1) * HEAD_DIM) if isinstance(i, int)
                        else pl.ds(pl.multiple_of(i * HEAD_DIM, HEAD_DIM), HEAD_DIM))
                q_f, q_b, v_f, v_b = get(i, 0, rf), get(i, 0, rb), get(i, 1, rf), get(i, 1, rb)
                prep_f = _scan_prep(q_f, get(i, 2, rf), lb[0:1, :], cf_ref[...], False, True)
                prep_b = _scan_prep(q_b, get(i, 3, rb), lb[1:2, :], cb_ref[...], True, True)
                if exact:
                    k_f, b_f = _scan_gates(get(i, 2, rf), lb[0:1, :], cf_ref[...])
                    k_b, b_b = _scan_gates(get(i, 3, rb), lb[1:2, :], cb_ref[...])
                    att_f = _scan_att_exact(q_f, k_f, b_f, False)
                    att_b = _scan_att_exact(q_b, k_b, b_b, True)
                else:
                    att_f, att_b = _scan_att(prep_f, lf_ref[...]), _scan_att(prep_b, lvb_ref[...])
                st_f, st_b = st_ref[2 * i], st_ref[2 * i + 1]
                of = _dot(att_f.astype(BF16), v_f.astype(BF16)) + _dot_nt(prep_f[2], st_f.astype(BF16))
                ob = _dot(att_b.astype(BF16), v_b.astype(BF16)) + _dot_nt(prep_b[2], st_b.astype(BF16))
                st_ref[2 * i] = _scan_state(prep_f, v_f, st_f)
                st_ref[2 * i + 1] = _scan_state(prep_b, v_b, st_b)

                @pl.when(2 * tb < n_blocks)
                def _():
                    oacc_ref[rf, cols] = of
                    oacc_ref[rb, cols] = ob

                @pl.when(2 * tb >= n_blocks)
                def _():
                    oacc_ref[rf, cols] += of
                    oacc_ref[rb, cols] += ob

                return jnp.maximum(prep_f[5], prep_b[5])

            for i, hd in enumerate(heads):
                st_ref[2 * i] = s0_ref[0, hd].T
                st_ref[2 * i + 1] = s0_ref[1, hd].T

            def body(tb, risk):
                for i, hd in enumerate(heads):
                    risk = jnp.maximum(risk, one_head(i, hd, lbs[i], tb, False, part))
                return risk

            risk = lax.fori_loop(0, n_blocks, body, jnp.zeros((1, LANES), F32))

            @pl.when(jnp.max(risk) > SCAN_GUARD)
            def _():
                def exact_head(i, c2):
                    hd = hp * hps + i
                    st_ref[2 * i] = s0_ref[0, hd].T
                    st_ref[2 * i + 1] = s0_ref[1, hd].T
                    lb = lb_ref[:, _lane_block(hd, HEAD_DIM)]

                    def exact_body(tb, c3):
                        one_head(i, hd, lb, tb, True, dyn_part)
                        return c3

                    lax.fori_loop(0, n_blocks, exact_body, 0)
                    return c2

                lax.fori_loop(0, hps, exact_head, 0)

            for i, hd in enumerate(heads):
                cols = slice(i * HEAD_DIM, (i + 1) * HEAD_DIM)
                og_ref[:, _lane_block(hd, HEAD_DIM)] = _scan_finish(oacc_ref[:, cols], part(i, 4), on)
        return carry

    _for_each_group(HEADS // hps, pair)
    xo_ref[...] = x_ref[...] + gate_ref[...] * _dot(og_ref[...], wout_ref[...])


def _hgrn_layer(x, n_seq, t, cond_of_seq, g, modr, w_in, lbs_j, onorm_g, w_out, s0, sfin_prev, slot):
    d = D_MODEL
    has_state = s0 is not None
    n_blocks = t // BLK
    assert n_blocks == 1 or n_blocks % 2 == 0
    full = lambda shape, **kw: pl.BlockSpec(shape, lambda b, *_: (0,) * len(shape), **kw)
    in_specs = [
        pl.BlockSpec((t, d), lambda b: (b, 0)),
        full((1, d)),
        _mod_spec(cond_of_seq, 0), _mod_spec(cond_of_seq, 1), _mod_spec(cond_of_seq, 2),
        full((d, HGRN_PARTS * d), **_RESIDENT),
        full((2, d)),
        full((1, HEAD_DIM)),
    ] + [full((BLK, BLK))] * 4
    args = [x, g.reshape(1, d), modr, modr, modr, w_in, lbs_j, onorm_g.reshape(1, HEAD_DIM)] + _scan_constants()
    state_block = (None, 2, HEADS, HEAD_DIM, HEAD_DIM)
    if has_state:
        in_specs.append(pl.BlockSpec(state_block, lambda b: (b, 0, 0, 0, 0)))
        args.append(s0)
    in_specs.append(full((d, d), **_RESIDENT))
    args.append(w_out)
    out_specs = [pl.BlockSpec((t, d), lambda b: (b, 0))]
    out_shape = [jax.ShapeDtypeStruct((n_seq * t, d), F32)]
    hps = SCAN_HEADS_PER_STEP if has_state else SCAN_HEADS_PER_BODY
    group_w = hps * HEAD_DIM
    scratch = [pltpu.VMEM((t, d), BF16), pltpu.VMEM((t, HGRN_PARTS * group_w), F32), pltpu.VMEM((t, d), BF16)]
    aliases = {}
    if has_state:
        scratch += [pltpu.VMEM((t, group_w), F32), pltpu.VMEM((2 * hps, HEAD_DIM, HEAD_DIM), F32)]
    else:
        state_dims = (2, HEADS, HEAD_DIM, HEAD_DIM)
        if sfin_prev is None:
            out_specs.append(pl.BlockSpec((None, N_SLOTS) + state_dims, lambda b: (b, 0, 0, 0, 0, 0)))
        else:
            out_specs.append(pl.BlockSpec((None, None) + state_dims, lambda b: (b, slot, 0, 0, 0, 0)))
            in_specs.append(pl.BlockSpec(memory_space=pl.ANY))
            args.append(sfin_prev)
            aliases = {len(args) - 1: 1}
        out_shape.append(jax.ShapeDtypeStruct((n_seq, N_SLOTS) + state_dims, F32))

    def body(*refs):
        if sfin_prev is not None:
            n_in = len(args)
            refs = refs[:n_in - 1] + refs[n_in:]
        _hgrn_kernel(*refs, n_blocks=n_blocks, has_state=has_state, slot=slot, owns_all_slots=sfin_prev is None,
                     hps=hps)

    out = pl.pallas_call(
        body,
        grid=(n_seq,),
        in_specs=in_specs,
        out_specs=out_specs,
        out_shape=out_shape,
        scratch_shapes=scratch,
        input_output_aliases=aliases,
        compiler_params=_cparams(("parallel",)),
        name="hgrn_layer_sample" if has_state else "hgrn_layer_prompt",
    )(*args)
    return out if not has_state else (out[0], None)


def _rope_tables(t_lat):
    rows = t_lat // GRID_W
    row = jnp.repeat(jnp.arange(rows), GRID_W).astype(F32)
    col = jnp.tile(jnp.arange(GRID_W), rows).astype(F32)
    half = QK_DIM // 2
    inv_freq = ROPE_THETA ** (-jnp.arange(0, half, 2, dtype=F32) / half)
    ang_row = row[:, None] * inv_freq
    ang_col = col[:, None] * inv_freq

    def part(ang):
        c, s = jnp.cos(ang), jnp.sin(ang)
        return jnp.concatenate([c, c], axis=1), jnp.concatenate([-s, s], axis=1)

    cr, sr = part(ang_row)
    cc, sc = part(ang_col)
    cos64 = jnp.concatenate([cr, cc], axis=1)
    sin64 = jnp.concatenate([sr, sc], axis=1)
    return jnp.concatenate([cos64, cos64], axis=1), jnp.concatenate([sin64, sin64], axis=1)


def _rope(x, cos, sin):
    lane = lax.broadcasted_iota(jnp.int32, x.shape, 1)
    first = (lane % (QK_DIM // 2)) < (QK_DIM // 4)
    swapped = jnp.where(first, pltpu.roll(x, LANES - QK_DIM // 4, 1), pltpu.roll(x, QK_DIM // 4, 1))
    return x * cos + swapped * sin


def _attn_kernel(*refs, n_qblk, n_sub, rope, cache, emit_kv, lam_init, slot, owns_all_slots):
    it = iter(refs)
    x_ref, g_ref, sh_ref, sc_ref, gate_ref, wqkv_ref = (next(it) for _ in range(6))
    qg_ref, kg_ref, lam_ref, sg_ref, seg_ref = (next(it) for _ in range(5))
    cos_ref = sin_ref = ck_ref = cv_ref = nk_ref = nv_ref = None
    if rope:
        cos_ref, sin_ref = next(it), next(it)
    if cache:
        ck_ref, cv_ref = next(it), next(it)
    wout_ref, xo_ref = next(it), next(it)
    if emit_kv:
        nk_all, nv_all = next(it), next(it)
        nk_refs = [_slot_view(nk_all.at[u], slot, owns_all_slots) for u in range(n_sub)]
        nv_refs = [_slot_view(nv_all.at[u], slot, owns_all_slots) for u in range(n_sub)]
    h_ref, qkv_ref, oa_ref = next(it), next(it), next(it)
    t = n_qblk * BLK

    d = D_MODEL
    for u in range(n_sub):
        seq = slice(u * t, (u + 1) * t)
        h_ref[seq, :] = _norm_mod(x_ref[seq, :], g_ref[...], sc_ref[...], sh_ref[...]).astype(BF16)
        qkv_ref[seq, :] = _dot(h_ref[seq, :], wqkv_ref[...])
    seg = seg_ref[...]
    lp = lam_ref[...]
    lam = (jnp.exp(jnp.sum(lp[0:1, :] * lp[1:2, :], axis=-1, keepdims=True))
           - jnp.exp(jnp.sum(lp[2:3, :] * lp[3:4, :], axis=-1, keepdims=True)) + lam_init)
    lane = lax.broadcasted_iota(jnp.int32, (1, LANES), 1)
    comp0 = lane < QK_DIM
    scale = QK_DIM ** -0.5

    def split(a):
        return [jnp.where(comp0, a, 0.0).astype(BF16), jnp.where(comp0, 0.0, a).astype(BF16)]

    def one_head(hd, u):
        hcol = _lane_block(hd, HEAD_DIM)
        seq = slice(u * t, (u + 1) * t)
        k = qkv_ref[seq, _block_at(d + hd * HEAD_DIM, HEAD_DIM)]
        v = qkv_ref[seq, _block_at(2 * d + hd * HEAD_DIM, HEAD_DIM)]
        kn = k * lax.rsqrt(_dot(k * k, seg) + EPS) * kg_ref[...]
        if emit_kv:
            nk_refs[u][:, hcol] = kn
            nv_refs[u][:, hcol] = v
        if rope:
            kn = _rope(kn, cos_ref[...], sin_ref[...])
        ks = split(kn)
        vb = v.astype(BF16)
        if cache:
            cks = split(ck_ref[:, hcol])
            cvb = cv_ref[:, hcol].astype(BF16)
        for qi in range(n_qblk):
            pos = slice(qi * BLK, (qi + 1) * BLK)
            rows = slice(u * t + qi * BLK, u * t + (qi + 1) * BLK)
            q = qkv_ref[rows, hcol]
            qn = q * lax.rsqrt(_dot(q * q, seg) + EPS) * qg_ref[...]
            if rope:
                qn = _rope(qn, cos_ref[pos, :], sin_ref[pos, :])
            qb = (qn * scale).astype(BF16)
            a_self, a_cache = None, None
            for c in range(2):
                s = _dot_nt(qb, ks[c])
                m = jnp.max(s, axis=-1, keepdims=True)
                if cache:
                    sc = _dot_nt(qb, cks[c])
                    m = jnp.maximum(m, jnp.max(sc, axis=-1, keepdims=True))
                p = jnp.exp(s - m)
                den = jnp.sum(p, axis=-1, keepdims=True)
                if cache:
                    pc = jnp.exp(sc - m)
                    den = den + jnp.sum(pc, axis=-1, keepdims=True)
                w = (1.0 / den) if c == 0 else (-lam / den)
                a_self = p * w if c == 0 else a_self + p * w
                if cache:
                    a_cache = pc * w if c == 0 else a_cache + pc * w
            o = _dot(a_self.astype(BF16), vb)
            if cache:
                o = o + _dot(a_cache.astype(BF16), cvb)
            ms = jnp.mean(o * o, axis=-1, keepdims=True)
            o = (o * lax.rsqrt(ms + EPS) * sg_ref[...]) * (1.0 - lam_init)
            oa_ref[rows, hcol] = o.astype(BF16)

    heads_per_body = ATTN_HEADS_PER_BODY if n_qblk == 1 else HEADS_PER_STEP

    def group(hg, carry):
        for u in range(n_sub):
            for i in range(heads_per_body):
                one_head(hg * heads_per_body + i, u)
        return carry

    _for_each_group(HEADS // heads_per_body, group)
    for u in range(n_sub):
        seq = slice(u * t, (u + 1) * t)
        xo_ref[seq, :] = x_ref[seq, :] + gate_ref[...] * _dot(oa_ref[seq, :], wout_ref[...])


def _attn_layer(x, n_seq, t, cond_of_seq, g, modr, w_qkv, qn_g, kn_g, lam_p, subln_g, w_out, layer_idx,
                cache_k_j, cache_v_j, kv_prev, slot):
    d = D_MODEL
    cache = cache_k_j is not None
    lam_init = 0.8 - 0.6 * math.exp(-0.3 * layer_idx)
    qg = jnp.tile(qn_g.reshape(1, QK_DIM), (1, 2))
    kg = jnp.tile(kn_g.reshape(1, QK_DIM), (1, 2))
    li = np.arange(LANES)
    seg = jnp.asarray((li[:, None] // QK_DIM == li[None, :] // QK_DIM).astype(np.float32) / QK_DIM)
    full = lambda shape, **kw: pl.BlockSpec(shape, lambda b, *_: (0,) * len(shape), **kw)
    n_sub = 1 if cache else ATTN_SEQS_PER_STEP
    assert n_seq % n_sub == 0
    rows = n_sub * t
    in_specs = [
        pl.BlockSpec((rows, d), lambda b: (b, 0)),
        full((1, d)),
        _mod_spec(cond_of_seq, 0), _mod_spec(cond_of_seq, 1), _mod_spec(cond_of_seq, 2),
        full((d, 3 * d), **_RESIDENT),
        full((1, LANES)), full((1, LANES)), full((4, QK_DIM)), full((1, HEAD_DIM)), full((LANES, LANES)),
    ]
    args = [x, g.reshape(1, d), modr, modr, modr, w_qkv, qg, kg, lam_p, subln_g.reshape(1, HEAD_DIM), seg]
    if cache:
        cos, sin = _rope_tables(t)
        past = cache_k_j.shape[1]
        in_specs += [full((t, LANES)), full((t, LANES)),
                     pl.BlockSpec((None, past, d), lambda b: (b, 0, 0)),
                     pl.BlockSpec((None, past, d), lambda b: (b, 0, 0))]
        args += [cos, sin, cache_k_j.reshape(n_seq, past, d), cache_v_j.reshape(n_seq, past, d)]
    in_specs.append(full((d, d), **_RESIDENT))
    args.append(w_out)
    out_specs = [pl.BlockSpec((rows, d), lambda b: (b, 0))]
    out_shape = [jax.ShapeDtypeStruct((n_seq * t, d), F32)]
    aliases = {}
    n_carried = 0
    if not cache:
        if kv_prev is None:
            kv_spec = pl.BlockSpec((n_sub, N_SLOTS, t, d), lambda b: (b, 0, 0, 0))
        else:
            kv_spec = pl.BlockSpec((n_sub, None, t, d), lambda b: (b, slot, 0, 0))
            in_specs += [pl.BlockSpec(memory_space=pl.ANY)] * 2
            args += list(kv_prev)
            aliases = {len(args) - 2: 1, len(args) - 1: 2}
            n_carried = 2
        out_specs += [kv_spec, kv_spec]
        out_shape += [jax.ShapeDtypeStruct((n_seq, N_SLOTS, t, d), F32)] * 2

    def body(*refs):
        n_in = len(args)
        refs = refs[:n_in - n_carried] + refs[n_in:]
        _attn_kernel(*refs, n_qblk=t // BLK, n_sub=n_sub, rope=cache, cache=cache, emit_kv=not cache,
                     lam_init=lam_init, slot=slot, owns_all_slots=kv_prev is None)

    out = pl.pallas_call(
        body,
        grid=(n_seq // n_sub,),
        in_specs=in_specs,
        out_specs=out_specs,
        out_shape=out_shape,
        scratch_shapes=[pltpu.VMEM((rows, d), BF16), pltpu.VMEM((rows, 3 * d), F32), pltpu.VMEM((rows, d), BF16)],
        input_output_aliases=aliases,
        compiler_params=_cparams(("parallel",)),
        name="attn_layer_sample" if cache else "attn_layer_prompt",
    )(*args)
    return (out[0], None) if cache else (out[0], (out[1], out[2]))


ROUTE_ROWS = 32
INFO_GID, INFO_RANK = 8, 9


def _two_stream_specs(n_prompt_tiles, width):
    return [pl.BlockSpec((BLK, width), lambda i, *_: (jnp.minimum(i, n_prompt_tiles - 1), 0)),
            pl.BlockSpec((BLK, width), lambda i, *_: (jnp.maximum(i - n_prompt_tiles, 0), 0))]


def _route_kernel(xp_ref, xs_ref, g_ref, sh_ref, sc_ref, wr_ref, br_ref, tri_ref,
                  h_ref, il_ref, it_ref, cum_ref, cumhi_ref, tot_ref, carry_ref, *, n_prompt_tiles):
    i = pl.program_id(0)

    @pl.when(i == 0)
    def _():
        carry_ref[...] = jnp.zeros_like(carry_ref)

    @pl.when(i < n_prompt_tiles)
    def _():
        h_ref[...] = _norm_mod(xp_ref[...], g_ref[...], sc_ref[...], sh_ref[...]).astype(BF16)

    @pl.when(i >= n_prompt_tiles)
    def _():
        h_ref[...] = _norm_mod(xs_ref[...], g_ref[...], sc_ref[...], sh_ref[...]).astype(BF16)

    logit = _dot_nt(wr_ref[...], h_ref[...]) + br_ref[...]
    gl = [logit[g:g + 1, :] for g in range(MOE_GROUPS)]
    gmax = functools.reduce(jnp.maximum, gl)
    gz = functools.reduce(lambda a, b: a + b, [jnp.exp(x - gmax) for x in gl])
    g_w = 1.0 / gz
    gid = jnp.full_like(gmax, MOE_GROUPS - 1)
    for g in range(MOE_GROUPS - 2, -1, -1):
        gid = jnp.where(gl[g] == gmax, float(g), gid)
    el = []
    for j in range(MOE_EPG):
        e = logit[MOE_GROUPS + j:MOE_GROUPS + j + 1, :]
        for g in range(1, MOE_GROUPS):
            r = MOE_GROUPS + g * MOE_EPG + j
            e = jnp.where(gid == float(g), logit[r:r + 1, :], e)
        el.append(e)
    emax = functools.reduce(jnp.maximum, el)
    pe = [jnp.exp(e - emax) for e in el]
    idx1 = jnp.full_like(emax, MOE_EPG - 1)
    for j in range(MOE_EPG - 2, -1, -1):
        idx1 = jnp.where(el[j] == emax, float(j), idx1)
    el2 = [jnp.where(idx1 == float(j), -jnp.inf, el[j]) for j in range(MOE_EPG)]
    emax2 = functools.reduce(jnp.maximum, el2)
    idx2 = jnp.full_like(emax, MOE_EPG - 1)
    for j in range(MOE_EPG - 2, -1, -1):
        idx2 = jnp.where(el2[j] == emax2, float(j), idx2)
    sel = [(idx1 == float(j)) | (idx2 == float(j)) for j in range(MOE_EPG)]
    den = functools.reduce(lambda a, b: a + b, [jnp.where(sel[j], pe[j], 0.0) for j in range(MOE_EPG)])
    cw = [jnp.where(sel[j], pe[j] * (g_w / den), 0.0) for j in range(MOE_EPG)]

    row8 = lax.broadcasted_iota(jnp.int32, (8, BLK), 0)
    onehot = jnp.where(row8.astype(F32) == gid, 1.0, 0.0)
    within = _dot(onehot.astype(BF16), tri_ref[...])
    carry = carry_ref[...]
    rank = jnp.sum(onehot * (within + carry[:, 0:1]), axis=0, keepdims=True)
    cum_ref[...] = carry
    new_carry = carry + jnp.sum(onehot, axis=1, keepdims=True)
    carry_ref[...] = new_carry
    cumhi_ref[...] = new_carry
    tot_ref[...] = new_carry

    il_ref[...] = jnp.where(row8 == 0, gid, jnp.where(row8 == 1, rank, 0.0))
    rowl = lax.broadcasted_iota(jnp.int32, (LANES, BLK), 0)
    m = jnp.zeros((LANES, BLK), F32)
    for j in range(MOE_EPG):
        hi = cw[j].astype(BF16).astype(F32)
        m = jnp.where(rowl == j, hi, m)
        m = jnp.where(rowl == MOE_EPG + j, cw[j] - hi, m)
    m = jnp.where(rowl == INFO_GID, gid, m)
    m = jnp.where(rowl == INFO_RANK, rank, m)
    it_ref[...] = m.T


def _moe_route(xp, xs, g, modr, wr_t, br, cond_of_tile):
    d = D_MODEL
    npt = xp.shape[0] // BLK
    nt = npt + xs.shape[0] // BLK
    n = nt * BLK
    tri = jnp.asarray(np.triu(np.ones((BLK, BLK), np.float32), 1), BF16)
    full = lambda shape: pl.BlockSpec(shape, lambda i: (0,) * len(shape))
    return pl.pallas_call(
        functools.partial(_route_kernel, n_prompt_tiles=npt),
        grid=(nt,),
        in_specs=_two_stream_specs(npt, d) + [
            full((1, d)),
            _mod_spec(cond_of_tile, 3), _mod_spec(cond_of_tile, 4),
            full((ROUTE_ROWS, d)), full((ROUTE_ROWS, 1)), full((BLK, BLK)),
        ],
        out_specs=[
            pl.BlockSpec((BLK, d), lambda i: (i, 0)),
            pl.BlockSpec((8, BLK), lambda i: (0, i)),
            pl.BlockSpec((BLK, LANES), lambda i: (i, 0)),
            pl.BlockSpec((None, 8, LANES), lambda i: (i, 0, 0)),
            pl.BlockSpec((None, 8, LANES), lambda i: (i, 0, 0)),
            pl.BlockSpec((8, LANES), lambda i: (0, 0)),
        ],
        out_shape=[
            jax.ShapeDtypeStruct((n, d), BF16),
            jax.ShapeDtypeStruct((8, n), F32),
            jax.ShapeDtypeStruct((n, LANES), F32),
            jax.ShapeDtypeStruct((nt, 8, LANES), F32),
            jax.ShapeDtypeStruct((nt, 8, LANES), F32),
            jax.ShapeDtypeStruct((8, LANES), F32),
        ],
        scratch_shapes=[pltpu.VMEM((8, LANES), F32)],
        compiler_params=_cparams(("arbitrary",)),
        name="moe_route",
    )(xp, xs, g.reshape(1, d), modr, modr, wr_t, br, tri)


def _sorted_pos(gid, rank, rstart_ref):
    p = rank
    for g in range(MOE_GROUPS):
        p = p + jnp.where(gid == float(g), rstart_ref[g].astype(F32), 0.0)
    return p


def _gather_tile(tbl_ref, h_ref, il_ref, it_ref, hs_ref, cws_ref, a, used, *, n_tiles):
    win = GATHER_WIN * BLK
    rstart_ref = tbl_ref

    @pl.when(jnp.logical_not(used))
    def _():
        hs_ref[...] = jnp.zeros(hs_ref.shape, hs_ref.dtype)
        cws_ref[...] = jnp.zeros(cws_ref.shape, cws_ref.dtype)

    @pl.when(used)
    def _():
        dest = (lax.broadcasted_iota(jnp.int32, (BLK, 1), 0) + a * BLK).astype(F32)
        src_tile = lax.broadcasted_iota(jnp.int32, (1, win), 1) // BLK
        clo = tbl_ref[TBL_CLO * LANES + a]
        n_win = (tbl_ref[TBL_CHI * LANES + a] - clo + GATHER_WIN) // GATHER_WIN

        def window(w):
            first = clo + w * GATHER_WIN
            c0 = jnp.minimum(first, n_tiles - GATHER_WIN)
            rows = pl.ds(pl.multiple_of(c0 * BLK, BLK), win)
            info = il_ref[:, rows]
            p = _sorted_pos(info[0:1, :], info[1:2, :], rstart_ref)
            p = jnp.where(src_tile + c0 >= first, p, -1.0)
            onehot = jnp.where(dest == p, 1.0, 0.0).astype(BF16)
            r = _dot(onehot, it_ref[rows, :].astype(BF16))
            return _dot(onehot, h_ref[rows, :]), r + pltpu.roll(r, LANES - MOE_EPG, 1)

        dh, dc = window(0)
        hs_ref[...] = dh.astype(BF16)
        cws_ref[...] = dc

        def body(w, carry):
            dh, dc = window(w)
            hs_ref[...] = (hs_ref[...].astype(F32) + dh).astype(BF16)
            cws_ref[...] += dc
            return carry

        lax.fori_loop(1, n_win, body, 0)


def _gather_kernel(tbl_ref, h_ref, il_ref, it_ref, hs_ref, cws_ref, *, n_tiles, fill_unused):
    s = pl.program_id(0)
    nt = tbl_ref[TBL_SNT * LANES + s]

    def tile(j, carry):
        rows = pl.ds(pl.multiple_of(j * BLK, BLK), BLK)
        _gather_tile(tbl_ref, h_ref, il_ref, it_ref, hs_ref.at[rows, :], cws_ref.at[rows, :],
                     s * SUPER + j, j < nt, n_tiles=n_tiles)
        return carry

    lax.fori_loop(0, SUPER if fill_unused else jnp.where(nt > 0, SUPER, 0), tile, 0)


def _moe_gather(h, info_lane, info_tok, tbl, bufs, n_super):
    n, d = h.shape
    nt = n // BLK
    first = bufs is None
    assert nt >= GATHER_WIN and n_super * SUPER <= LANES
    out_block = (lambda s, tbl_r: (s, 0)) if first else (lambda s, tbl_r: (tbl_r[TBL_SBLK * LANES + s], 0))
    carried = [] if first else [pl.BlockSpec(memory_space=pl.ANY)] * 2

    def body(tbl_ref, h_ref, il_ref, it_ref, *rest):
        _gather_kernel(tbl_ref, h_ref, il_ref, it_ref, *rest[len(carried):], n_tiles=nt, fill_unused=first)

    rows = n_super * SUPER_ROWS
    return pl.pallas_call(
        body,
        grid_spec=pltpu.PrefetchScalarGridSpec(
            num_scalar_prefetch=1,
            grid=(n_super,),
            in_specs=[
                pl.BlockSpec((n, d), lambda a, *_: (0, 0), **_RESIDENT),
                pl.BlockSpec((8, n), lambda a, *_: (0, 0), **_RESIDENT),
                pl.BlockSpec((n, LANES), lambda a, *_: (0, 0), **_RESIDENT),
            ] + carried,
            out_specs=[pl.BlockSpec((SUPER_ROWS, d), out_block), pl.BlockSpec((SUPER_ROWS, LANES), out_block)],
        ),
        out_shape=[jax.ShapeDtypeStruct((rows, d), BF16), jax.ShapeDtypeStruct((rows, LANES), F32)],
        input_output_aliases={} if first else {4: 0, 5: 1},
        compiler_params=_cparams(("arbitrary",)),
        name="moe_gather",
    )(tbl, h, info_lane, info_tok, *(() if first else bufs))


def _moe_mlp_kernel(tbl_ref, hs_ref, cws_ref, wg_ref, wu_ref, wd_ref, ys_ref,
                    acc_ref, wgb_ref, wub_ref, wdb_ref, *, fill_unused):
    s = pl.program_id(0)
    k = pl.program_id(1)
    nt = tbl_ref[TBL_SNT * LANES + s]

    if fill_unused:
        @pl.when((nt == 0) & (k == 0))
        def _():
            ys_ref[...] = jnp.zeros_like(ys_ref)

    @pl.when(nt > 0)
    def _():
        wgb_ref[...] = wg_ref[...].astype(BF16)
        wub_ref[...] = wu_ref[...].astype(BF16)
        wdb_ref[...] = wd_ref[...].astype(BF16)

    @pl.when((s == 0) & (k == 0))
    def _():
        acc_ref[...] = jnp.zeros_like(acc_ref)

    def skipped(j):
        rows = slice(j * EXPERT_TILES * BLK, (j + 1) * EXPERT_TILES * BLK)
        ys_ref[rows, :] = jnp.zeros((EXPERT_TILES * BLK, D_MODEL), BF16)

    def block(j):
        rows = slice(j * EXPERT_TILES * BLK, (j + 1) * EXPERT_TILES * BLK)
        hsub = hs_ref[rows, :]
        cws = cws_ref[rows, :]
        cwk = jnp.zeros((EXPERT_TILES * BLK, 1), F32)
        for kk in range(MOE_EPG):
            cwk = jnp.where(k == kk, cws[:, kk:kk + 1], cwk)
        y = None
        for fh in range(MOE_D_FF // FF_PART):
            fc = slice(fh * FF_PART, (fh + 1) * FF_PART)
            gate = _dot(hsub, wgb_ref[:, fc])
            up = _dot(hsub, wub_ref[:, fc])
            act = ((_silu(gate) * up) * cwk).astype(BF16)
            part = _dot(act, wdb_ref[fc, :])
            y = part if y is None else y + part
        total = jnp.where(k == 0, 0.0, acc_ref[rows, :]) + y
        acc_ref[rows, :] = total
        ys_ref[rows, :] = total.astype(BF16)

    def maybe_block(j):
        pl.when(j * EXPERT_TILES < nt)(functools.partial(block, j))
        pl.when((j * EXPERT_TILES >= nt) & (nt > 0) & (k == 0))(functools.partial(skipped, j))

    n_blocks = SUPER // EXPERT_TILES
    n_straight = STRAIGHT_TILES // EXPERT_TILES

    @pl.when(nt >= STRAIGHT_TILES)
    def _():
        for j in range(n_straight):
            block(j)

    @pl.when(nt < STRAIGHT_TILES)
    def _():
        for j in range(n_straight):
            maybe_block(j)

    for j in range(n_straight, n_blocks):
        maybe_block(j)


def _moe_mlp(hs, cws, w_gate, w_up, w_down, layer, tbl, ys_buf):
    d = hs.shape[1]
    n_super = hs.shape[0] // SUPER_ROWS
    first = ys_buf is None
    carried = [] if first else [pl.BlockSpec(memory_space=pl.ANY)]

    def body(tbl_ref, hs_ref, cws_ref, wg_ref, wu_ref, wd_ref, *rest):
        _moe_mlp_kernel(tbl_ref, hs_ref, cws_ref, wg_ref, wu_ref, wd_ref, *rest[len(carried):], fill_unused=first)

    def widx(s, k, tbl_r):
        kk = jnp.where(tbl_r[TBL_SNT * LANES + s] > 0, k, MOE_EPG - 1)
        return (layer * MOE_EXPERTS + tbl_r[TBL_SGROUP * LANES + s] * MOE_EPG + kk, 0, 0)

    rows_idx = lambda s, k, tbl_r: (tbl_r[TBL_SBLK * LANES + s], 0)
    out_idx = (lambda s, k, tbl_r: (s, 0)) if first else rows_idx
    return pl.pallas_call(
        body,
        grid_spec=pltpu.PrefetchScalarGridSpec(
            num_scalar_prefetch=1,
            grid=(n_super, MOE_EPG),
            in_specs=[
                pl.BlockSpec((SUPER_ROWS, d), rows_idx),
                pl.BlockSpec((SUPER_ROWS, LANES), rows_idx),
                pl.BlockSpec((None, d, MOE_D_FF), widx),
                pl.BlockSpec((None, d, MOE_D_FF), widx),
                pl.BlockSpec((None, MOE_D_FF, d), widx),
            ] + carried,
            out_specs=pl.BlockSpec((SUPER_ROWS, d), out_idx),
            scratch_shapes=[
                pltpu.VMEM((SUPER_ROWS, d), F32),
                pltpu.VMEM((d, MOE_D_FF), BF16),
                pltpu.VMEM((d, MOE_D_FF), BF16),
                pltpu.VMEM((MOE_D_FF, d), BF16),
            ],
        ),
        out_shape=jax.ShapeDtypeStruct(hs.shape, BF16),
        input_output_aliases={} if first else {6: 0},
        compiler_params=_cparams(("arbitrary", "arbitrary")),
        name="moe_experts",
    )(tbl, hs, cws, w_gate, w_up, w_down, *(() if first else (ys_buf,)))


N_SRC = 2 * MOE_GROUPS


def _unsort_kernel(tbl_ref, xp_ref, xs_ref, it_ref, gate_ref, *rest, n_prompt_tiles):
    ys_refs, op_ref, os_ref = rest[:N_SRC], rest[N_SRC], rest[N_SRC + 1]
    t = pl.program_id(0)
    info = it_ref[...]
    p = _sorted_pos(info[:, INFO_GID:INFO_GID + 1], info[:, INFO_RANK:INFO_RANK + 1], tbl_ref)
    lane = lax.broadcasted_iota(jnp.int32, (1, BLK), 1).astype(F32)
    slot = lambda m: tbl_ref[(TBL_UT + t) * LANES + m]

    def take(m):
        a = slot(m)
        onehot = jnp.where(p - (a * BLK).astype(F32) == lane, 1.0, 0.0).astype(BF16)
        return _dot(onehot, ys_refs[m][...])

    def stream(x_ref, o_ref):
        first = functools.reduce(lambda u, w: u + w, [take(m) for m in range(0, N_SRC, 2)])
        o_ref[...] = x_ref[...] + gate_ref[...] * first
        for m in range(1, N_SRC, 2):
            @pl.when(slot(m) >= 0)
            def _():
                o_ref[...] += gate_ref[...] * take(m)

    pl.when(t < n_prompt_tiles)(functools.partial(stream, xp_ref, op_ref))
    pl.when(t >= n_prompt_tiles)(functools.partial(stream, xs_ref, os_ref))


def _moe_unsort(xp, xs, info_tok, modr, ys, tbl, cond_of_tile):
    d = D_MODEL
    npt = xp.shape[0] // BLK
    nt = npt + xs.shape[0] // BLK

    def ys_spec(m):
        return pl.BlockSpec((BLK, d), lambda t, tbl_r: (jnp.maximum(tbl_r[(TBL_UT + t) * LANES + m], 0), 0))

    return pl.pallas_call(
        functools.partial(_unsort_kernel, n_prompt_tiles=npt),
        grid_spec=pltpu.PrefetchScalarGridSpec(
            num_scalar_prefetch=1,
            grid=(nt,),
            in_specs=_two_stream_specs(npt, d) + [
                pl.BlockSpec((BLK, LANES), lambda t, *_: (t, 0)),
                _mod_spec(cond_of_tile, 5),
            ] + [ys_spec(m) for m in range(N_SRC)],
            out_specs=_two_stream_specs(npt, d),
        ),
        out_shape=[jax.ShapeDtypeStruct(xp.shape, F32), jax.ShapeDtypeStruct(xs.shape, F32)],
        compiler_params=_cparams(("arbitrary",)),
        name="moe_unsort_residual",
    )(tbl, xp, xs, info_tok, modr, *([ys] * N_SRC))


def _tables_kernel(lo_ref, hi_ref, tot_ref, tbl_ref, *, n_tiles, n_super):
    one = lambda cond: jnp.where(cond, 1.0, 0.0)
    groups = range(MOE_GROUPS)
    lane = lax.broadcasted_iota(jnp.int32, (1, LANES), 1).astype(F32)
    tot = [tot_ref[g:g + 1, :] for g in groups]
    pick = lambda vals, idx: sum(jnp.where(idx == float(g), vals[g], 0.0) for g in groups)

    n_sup = [sum(one(tot[g] > float(m * SUPER_ROWS)) for m in range(n_super)) for g in groups]
    sup_start, sup_end, run = [], [], 0.0
    for g in groups:
        sup_start.append(run + 0.0 * tot[g])
        run = run + n_sup[g]
        sup_end.append(run)
    n_used = sup_end[-1]
    rstart = [sup_start[g] * float(SUPER_ROWS) for g in groups]
    group_of = lambda s: jnp.minimum(sum(one(s >= sup_end[g]) for g in groups), float(MOE_GROUPS - 1))

    sg, snt, written = [], [], []
    for m in range(n_super):
        used = float(m) < n_used
        g_m = group_of(jnp.where(used, float(m), n_used - 1.0))
        rows_left = pick(tot, g_m) - (float(m) - pick(sup_start, g_m)) * float(SUPER_ROWS)
        nt_m = jnp.where(used, jnp.clip(jnp.floor((rows_left + float(BLK - 1)) * (1.0 / BLK)), 0.0, float(SUPER)), 0.0)
        sg.append(g_m)
        snt.append(nt_m)
        written.append(float(EXPERT_TILES) * jnp.floor((nt_m + float(EXPERT_TILES - 1)) * (1.0 / EXPERT_TILES)))
    sblk = [jnp.minimum(float(m), n_used - 1.0) for m in range(n_super)]
    by_super_lane = lambda vals: sum(jnp.where(lane == float(m), vals[m], 0.0) for m in range(n_super))
    last_written = sum(jnp.where(n_used - 1.0 == float(m), float(m * SUPER) + written[m] - 1.0, 0.0)
                       for m in range(n_super))

    sa = sum(one(lane >= float(m * SUPER)) for m in range(1, n_super))
    ja = lane - sa * float(SUPER)
    at_tile = lambda vals: sum(jnp.where(sa == float(m), vals[m], 0.0) for m in range(n_super))
    snt_a, ga, written_a = at_tile(snt), at_tile(sg), at_tile(written)
    code = jnp.where(ja < snt_a, float(CODE_GATHER), jnp.where(ja < written_a, float(CODE_ZERO), float(CODE_SKIP)))
    oblk = jnp.where(ja < written_a, lane,
                     jnp.where(written_a > 0.0, sa * float(SUPER) + written_a - 1.0, last_written))
    r0 = ((sa - pick(sup_start, ga)) * float(SUPER) + ja) * float(BLK)
    lo = [lo_ref[:, g, :] for g in groups]
    hi = [hi_ref[:, g, :] for g in groups]
    clo = jnp.sum(one(pick(hi, ga) <= r0), axis=0, keepdims=True)
    chi = jnp.sum(one(pick(lo, ga) < r0 + float(BLK)), axis=0, keepdims=True) - 1.0
    clo = jnp.clip(clo, 0.0, float(n_tiles - 1))
    chi = jnp.clip(chi, clo, float(n_tiles - 1))

    gm = one(lane >= 2.0) + one(lane >= 4.0) + one(lane >= 6.0)
    first = pick(rstart, gm) + pick(lo, gm)
    last = pick(rstart, gm) + pick(hi, gm) - 1.0
    t0 = jnp.floor(first * (1.0 / BLK))
    t1 = jnp.floor(last * (1.0 / BLK))
    has = pick(hi, gm) > pick(lo, gm)
    a0 = jnp.where(has, t0, -1.0)
    a1 = jnp.where(has, jnp.where(t1 != t0, t1, -1.0), -1.0)
    ut = jnp.where(lane >= float(N_SRC), -1.0, jnp.where(lane - 2.0 * gm == 0.0, a0, a1))

    def put(row, v):
        tbl_ref[row:row + 1, :] = v.astype(jnp.int32)

    put(TBL_RSTART, sum(jnp.where(lane == float(g), rstart[g], 0.0) for g in groups))
    put(TBL_SGROUP, by_super_lane(sg))
    put(TBL_SNT, by_super_lane(snt))
    put(TBL_SBLK, by_super_lane(sblk))
    put(TBL_CODE, code)
    put(TBL_CLO, clo)
    put(TBL_CHI, chi)
    put(TBL_OBLK, oblk)
    tbl_ref[TBL_UT:TBL_UT + n_tiles, :] = ut.astype(jnp.int32)


def _moe_tables(cum_lo, cum_hi, tot, n_tiles, n_super):
    assert n_super * SUPER <= LANES
    tbl = pl.pallas_call(
        functools.partial(_tables_kernel, n_tiles=n_tiles, n_super=n_super),
        out_shape=jax.ShapeDtypeStruct((TBL_UT + n_tiles, LANES), jnp.int32),
        name="moe_tables",
    )(cum_lo, cum_hi, tot)
    return tbl.reshape(-1)


def _moe_layer(xp, xs, bufs, g, modr, wr_t, br, w_gate, w_up, w_down, layer, cond_of_tile):
    n = xp.shape[0] + xs.shape[0]
    nt = n // BLK
    n_super = (n - 1) // SUPER_ROWS + MOE_GROUPS
    h, info_lane, info_tok, cum_lo, cum_hi, tot = _moe_route(xp, xs, g, modr, wr_t, br, cond_of_tile)
    tbl = _moe_tables(cum_lo, cum_hi, tot, nt, n_super)
    hs, cws = _moe_gather(h, info_lane, info_tok, tbl, None if bufs is None else bufs[:2], n_super)
    ys = _moe_mlp(hs, cws, w_gate, w_up, w_down, layer, tbl, None if bufs is None else bufs[2])
    xp, xs = _moe_unsort(xp, xs, info_tok, modr, ys, tbl, cond_of_tile)
    return xp, xs, (hs, cws, ys)


def kernel(x_prompt, x_sample, c, cache_k, cache_v, state_hgrn, c_ctx, norm_g, w_ada, b_ada, hgrn_w_in, hgrn_lb_logits, hgrn_onorm_g, hgrn_w_out, attn_w_qkv, attn_qn_g, attn_kn_g, attn_lambda, attn_subln_g, attn_w_out, moe_w_group, moe_b_group, moe_w_expert, moe_b_expert, moe_w_gate, moe_w_up, moe_w_down):
    n_prompt_seq, seq, d = x_prompt.shape
    dec_batch, dec_seq, _ = x_sample.shape
    n_prompt = n_prompt_seq * seq
    assert d == D_MODEL and seq == BLK and dec_seq % BLK == 0
    assert 1 + dec_batch <= N_COND

    xp = x_prompt.reshape(n_prompt, d)
    xs = x_sample.reshape(dec_batch * dec_seq, d)
    cond = jnp.zeros((N_COND, d), F32).at[0].set(c_ctx).at[1:1 + dec_batch].set(c)
    mod = _modulation(cond, w_ada, b_ada)

    lbs = jnp.cumsum(jax.nn.softmax(hgrn_lb_logits.astype(F32), axis=0), axis=0)
    lbs = lbs - lbs[0:1]

    cond_prompt = lambda b: 0
    cond_sample = lambda b: 1 + b
    npt, spt = n_prompt // BLK, dec_seq // BLK
    cond_tile = lambda i: jnp.where(i < npt, 0, 1 + (i - npt) // spt)

    w_in = hgrn_w_in.astype(BF16)
    w_hout = hgrn_w_out.astype(BF16)
    w_qkv = attn_w_qkv.astype(BF16)
    w_aout = attn_w_out.astype(BF16)
    wr_t = jnp.zeros((DEPTH, ROUTE_ROWS, d), F32)
    wr_t = wr_t.at[:, :MOE_GROUPS].set(moe_w_group.transpose(0, 2, 1))
    wr_t = wr_t.at[:, MOE_GROUPS:MOE_GROUPS + MOE_EXPERTS].set(moe_w_expert.transpose(0, 2, 1)).astype(BF16)
    br = jnp.zeros((DEPTH, ROUTE_ROWS, 1), F32)
    br = br.at[:, :MOE_GROUPS, 0].set(moe_b_group).at[:, MOE_GROUPS:MOE_GROUPS + MOE_EXPERTS, 0].set(moe_b_expert)
    w_gate = moe_w_gate.reshape(DEPTH * MOE_EXPERTS, d, MOE_D_FF)
    w_up = moe_w_up.reshape(DEPTH * MOE_EXPERTS, d, MOE_D_FF)
    w_down = moe_w_down.reshape(DEPTH * MOE_EXPERTS, MOE_D_FF, d)

    sfin, kv = None, None
    bufs = None
    for i in range(DEPTH):
        j = i // 2
        modr = mod[i].reshape(N_COND * N_MOD, 1, d)
        if i % 2 == 0:
            common = (norm_g[i, 0], modr, w_in[j], lbs[j], hgrn_onorm_g[j], w_hout[j])
            xp, sfin = _hgrn_layer(xp, n_prompt_seq, seq, cond_prompt, *common, None, sfin, j)
            xs, _ = _hgrn_layer(xs, dec_batch, dec_seq, cond_sample, *common, state_hgrn[:, j], None, j)
        else:
            common = (norm_g[i, 0], modr, w_qkv[j], attn_qn_g[j], attn_kn_g[j], attn_lambda[j], attn_subln_g[j],
                      w_aout[j], i)
            xp, kv = _attn_layer(xp, n_prompt_seq, seq, cond_prompt, *common, None, None, kv, j)
            xs, _ = _attn_layer(xs, dec_batch, dec_seq, cond_sample, *common, cache_k[:, j], cache_v[:, j], None, j)
        xp, xs, bufs = _moe_layer(xp, xs, bufs, norm_g[i, 1], modr, wr_t[i], br[i], w_gate, w_up, w_down, i, cond_tile)

    new_k = kv[0].reshape(n_prompt_seq, DEPTH // 2, seq, HEADS, 2, QK_DIM)
    new_v = kv[1].reshape(n_prompt_seq, DEPTH // 2, seq, HEADS, HEAD_DIM)
    return (xp.reshape(n_prompt_seq, seq, d), xs.reshape(dec_batch, dec_seq, d), new_k, new_v, sfin)
```

```python
import functools
import math

import numpy as np
import jax
import jax.numpy as jnp
from jax import lax
from jax.experimental import pallas as pl
from jax.experimental.pallas import tpu as pltpu

F32 = jnp.float32
BF16 = jnp.bfloat16

D_MODEL = 1024
DEPTH = 4
GRID_W = 64
HEADS = 8
HEAD_DIM = 128
QK_DIM = 64
ROPE_THETA = 10000.0
MOE_GROUPS = 4
MOE_EPG = 4
MOE_EXPERTS = MOE_GROUPS * MOE_EPG
MOE_D_FF = 512
EPS = 1e-6
N_COND = 8
N_MOD = 6
HGRN_PARTS = 5

LANES = 128
BLK = 256
CHUNK = 32
N_CHUNK = BLK // CHUNK
HEADS_PER_STEP = 2
SCAN_HEADS_PER_STEP = 4
SCAN_HEADS_PER_BODY = 8
ATTN_HEADS_PER_BODY = 8
ATTN_SEQS_PER_STEP = 2
SUPER = 12
SUPER_ROWS = SUPER * BLK
STRAIGHT_TILES = 8
TBL_RSTART, TBL_SGROUP, TBL_SNT, TBL_SBLK, TBL_CLO, TBL_CHI = range(6)
TBL_UT = 8
EXPERT_TILES = 2
FF_PART = 256
GATHER_WIN = 6
EXP2_CLAMP = 115.0
SCAN_GUARD = 100.0
VMEM_LIMIT = 56 * 1024 * 1024
N_SLOTS = DEPTH // 2
_RESIDENT = dict(pipeline_mode=pl.Buffered(1))


def _cparams(sem):
    return pltpu.CompilerParams(dimension_semantics=sem, vmem_limit_bytes=VMEM_LIMIT)


def _silu(x):
    return x * jax.nn.sigmoid(x)


def _dot(a, b):
    return jnp.dot(a, b, preferred_element_type=F32)


def _dot_nt(a, b):
    return lax.dot_general(a, b, (((1,), (1,)), ((), ())), preferred_element_type=F32)


def _dot_tn(a, b):
    return lax.dot_general(a, b, (((0,), (0,)), ((), ())), preferred_element_type=F32)


def _lane_block(i, width):
    return _block_at(i * width, width)


def _block_at(start, width):
    if isinstance(start, int):
        return slice(start, start + width)
    return pl.ds(pl.multiple_of(start, width), width)


def _for_each_group(n_groups, body):
    if n_groups == 1:
        body(0, 0)
    else:
        lax.fori_loop(0, n_groups, body, 0)


def _mod_kernel(c_ref, w_ref, b_ref, o_ref):
    o_ref[...] = _dot(_silu(c_ref[...]), w_ref[...]) + b_ref[...]


def _modulation(cond, w_ada, b_ada):
    tn = 1536
    nj = (N_MOD * D_MODEL) // tn
    return pl.pallas_call(
        _mod_kernel,
        grid=(DEPTH, nj),
        in_specs=[
            pl.BlockSpec((N_COND, D_MODEL), lambda l, j: (0, 0)),
            pl.BlockSpec((None, D_MODEL, tn), lambda l, j: (l, 0, j)),
            pl.BlockSpec((None, 1, tn), lambda l, j: (l, 0, j)),
        ],
        out_specs=pl.BlockSpec((None, N_COND, tn), lambda l, j: (l, 0, j)),
        out_shape=jax.ShapeDtypeStruct((DEPTH, N_COND, N_MOD * D_MODEL), F32),
        compiler_params=_cparams(("parallel", "parallel")),
        name="modulation",
    )(cond, w_ada, b_ada.reshape(DEPTH, 1, N_MOD * D_MODEL))


def _norm_mod(x, g, sc, sh):
    ms = jnp.mean(x * x, axis=-1, keepdims=True)
    return (x * lax.rsqrt(ms + EPS) * g) * (1.0 + sc) + sh


def _mod_spec(cond_of_step, which):
    return pl.BlockSpec((None, 1, D_MODEL), lambda i, *_: (cond_of_step(i) * N_MOD + which, 0, 0))


def _scan_constants():
    t = np.arange(BLK)
    out = []
    for rev in (False, True):
        u = (BLK - 1 - t) if rev else t
        ut, us = u[:, None], u[None, :]
        cums = (us <= ut).astype(np.float32)
        lev = np.where(us > ut, 0,
              np.where(ut // CHUNK == us // CHUNK, 1,
              np.where(ut // 64 == us // 64, 2,
              np.where(ut // 128 == us // 128, 3, 4)))).astype(np.int32)
        out += [jnp.asarray(cums, BF16), jnp.asarray(lev)]
    return out


def _rows_to_block(rows, rev):
    order = rows[::-1] if rev else rows
    return jnp.concatenate([jnp.broadcast_to(r, (CHUNK, LANES)) for r in order], axis=0)


def _trunc_bf16(x):
    return lax.bitcast_convert_type(lax.bitcast_convert_type(x, jnp.int32) & jnp.int32(-65536), F32)


def _scan_gates(z, lb, cums):
    sig = jax.nn.sigmoid(z)
    f = lb + (1.0 - lb) * sig
    logf = jnp.log2(f)
    k = (1.0 - lb) * (1.0 - sig)
    hi32 = _trunc_bf16(logf)
    bb = _dot(cums, jnp.concatenate([hi32.astype(BF16), (logf - hi32).astype(BF16)], axis=1))
    return k, bb[:, :LANES] + bb[:, LANES:]


def _scan_att_exact(q, k, b, rev):
    ti = lax.broadcasted_iota(jnp.int32, (BLK, BLK), 0)
    si = lax.broadcasted_iota(jnp.int32, (BLK, BLK), 1)
    ri = lax.broadcasted_iota(jnp.int32, (BLK, 1), 0)
    if rev:
        ti, si, ri = BLK - 1 - ti, BLK - 1 - si, BLK - 1 - ri
    h1 = _trunc_bf16(b)
    h2 = _trunc_bf16(b - h1)
    pieces = jnp.concatenate([h1.astype(BF16), h2.astype(BF16), (b - h1 - h2).astype(BF16)], axis=1)
    att = jnp.where(ti == si, jnp.sum(q * k, axis=-1, keepdims=True), 0.0)
    g = 2
    while g <= BLK:
        h = g // 2
        at_mid = jnp.where(((ti & -g) + (h - 1)) == si, 1.0, 0.0).astype(BF16)
        bb = _dot(at_mid, pieces)
        b_r = bb[:, :LANES] + bb[:, LANES:2 * LANES] + bb[:, 2 * LANES:]
        after = (ri & h) != 0
        qg = jnp.where(after, q * jnp.exp2(jnp.minimum(b - b_r, 0.0)), 0.0).astype(BF16)
        kg = jnp.where(after, 0.0, k * jnp.exp2(jnp.minimum(b_r - b, 0.0))).astype(BF16)
        prod = _dot_nt(qg, kg)
        att = jnp.where((ti & -g) == (si & -g),
                        jnp.where((ti & h) != 0, jnp.where((si & h) == 0, prod, att), att), att)
        g *= 2
    return att


def _scan_prep(q, z, lb, cums, rev, with_inter):
    k, b = _scan_gates(z, lb, cums)

    e_row, m_row = (0, CHUNK // 2) if rev else (CHUNK - 1, CHUNK // 2 - 1)
    ends, mids = [], []
    for j in range(N_CHUNK):
        ends.append(b[j * CHUNK + e_row:j * CHUNK + e_row + 1, :])
        mids.append(b[j * CHUNK + m_row:j * CHUNK + m_row + 1, :])
    if rev:
        ends, mids = ends[::-1], mids[::-1]
    zero = jnp.zeros((1, LANES), F32)
    one = jnp.ones((1, LANES), F32)
    pres = [zero] + ends[:-1]
    b_pre = _rows_to_block(pres, rev)
    b_end = _rows_to_block(ends, rev)
    b_mid = _rows_to_block(mids, rev)

    qd = q * jnp.exp2(b - b_pre)
    ku = k * jnp.exp2(b_end - b)
    qm = q * jnp.exp2(jnp.clip(b - b_mid, -EXP2_CLAMP, EXP2_CLAMP))
    km = k * jnp.exp2(jnp.clip(b_mid - b, -EXP2_CLAMP, EXP2_CLAMP))

    levels = []
    for nc in (2, 4, 8):
        fq, fk = [], []
        for ju in range(N_CHUNK):
            r = (ju // nc) * nc + nc // 2 - 1
            if ju % nc >= nc // 2:
                fq.append(one if nc == 2 else jnp.exp2(pres[ju] - ends[r]))
                fk.append(zero)
            else:
                fq.append(zero)
                fk.append(one if nc == 2 else jnp.exp2(ends[r] - ends[ju]))
        levels.append(((qd * _rows_to_block(fq, rev)).astype(BF16), (ku * _rows_to_block(fk, rev)).astype(BF16)))

    last = ends[-1]
    qh = (qd * _rows_to_block([jnp.exp2(p) for p in pres], rev)).astype(BF16) if with_inter else None
    kh = (ku * _rows_to_block([jnp.exp2(last - e) for e in ends], rev)).astype(BF16)
    risk = functools.reduce(jnp.maximum, [jnp.maximum(p - m, m - e) for p, m, e in zip(pres, mids, ends)])
    return (qm.astype(BF16), km.astype(BF16)), levels, qh, kh, last, risk


def _scan_att(prep, lev):
    att = jnp.where(lev == 1, _dot_nt(*prep[0]), 0.0)
    for level, (ql, kl) in enumerate(prep[1], start=2):
        att = jnp.where(lev == level, _dot_nt(ql, kl), att)
    return att


def _scan_att_bidir(prep_f, prep_b, lev_f, lev_b):
    att = jnp.where(lev_f == 1, _dot_nt(*prep_f[0]), 0.0) + jnp.where(lev_b == 1, _dot_nt(*prep_b[0]), 0.0)
    lev = jnp.maximum(lev_f, lev_b)
    for level, ((qf, kf), (qb, kb)) in enumerate(zip(prep_f[1], prep_b[1]), start=2):
        both = _dot_nt(jnp.concatenate([qf, qb], axis=1), jnp.concatenate([kf, kb], axis=1))
        att = jnp.where(lev == level, both, att)
    return att


def _scan_state(prep, v, st_prev):
    ut = _dot_tn(v.astype(BF16), prep[3])
    return ut if st_prev is None else st_prev * jnp.exp2(prep[4]) + ut


def _scan_finish(o, g, on):
    ms = jnp.mean(o * o, axis=-1, keepdims=True)
    return ((o * lax.rsqrt(ms + EPS) * on) * _silu(g)).astype(BF16)


def _slot_view(ref, slot, owns_all_slots):
    if not owns_all_slots:
        return ref
    for s in range(ref.shape[0]):
        if s != slot:
            ref[s] = jnp.zeros(ref.shape[1:], ref.dtype)
    return ref.at[slot]


def _hgrn_kernel(*refs, n_blocks, has_state, slot, owns_all_slots, hps):
    it = iter(refs)
    x_ref, g_ref, sh_ref, sc_ref, gate_ref, win_ref, lb_ref, on_ref = (next(it) for _ in range(8))
    cf_ref, lf_ref, cb_ref, lvb_ref = (next(it) for _ in range(4))
    s0_ref = next(it) if has_state else None
    wout_ref, xo_ref = next(it), next(it)
    sfin_ref = None if has_state else next(it)
    h_ref, proj_ref, og_ref = next(it), next(it), next(it)
    oacc_ref, st_ref = (next(it), next(it)) if has_state else (None, None)

    h_ref[...] = _norm_mod(x_ref[...], g_ref[...], sc_ref[...], sh_ref[...]).astype(BF16)
    on = on_ref[...]
    if not has_state:
        sfin_ref = _slot_view(sfin_ref, slot, owns_all_slots)

    group_w = hps * HEAD_DIM

    def part(i, p, rows=slice(None)):
        return proj_ref[rows, p * group_w + i * HEAD_DIM:p * group_w + (i + 1) * HEAD_DIM]

    def pair(hp, carry):
        for p in range(HGRN_PARTS):
            cols = _block_at(p * D_MODEL + hp * group_w, group_w)
            proj_ref[:, p * group_w:(p + 1) * group_w] = _dot(h_ref[...], win_ref[:, cols])
        heads = [hp * hps + i for i in range(hps)]
        lbs = [lb_ref[:, _lane_block(hd, HEAD_DIM)] for hd in heads]

        def dyn_part(i, p, rows=slice(None)):
            return proj_ref[rows, pl.ds(pl.multiple_of(p * group_w + i * HEAD_DIM, HEAD_DIM), HEAD_DIM)]

        if not has_state:
            risk = jnp.zeros((1, LANES), F32)
            for i, hd in enumerate(heads):
                q, v = part(i, 0), part(i, 1)
                prep_f = _scan_prep(q, part(i, 2), lbs[i][0:1, :], cf_ref[...], False, False)
                prep_b = _scan_prep(q, part(i, 3), lbs[i][1:2, :], cb_ref[...], True, False)
                risk = jnp.maximum(risk, jnp.maximum(prep_f[5], prep_b[5]))
                att = _scan_att_bidir(prep_f, prep_b, lf_ref[...], lvb_ref[...])
                o = _dot(att.astype(BF16), v.astype(BF16))
                sfin_ref[0, hd] = _scan_state(prep_f, v, None).T
                sfin_ref[1, hd] = _scan_state(prep_b, v, None).T
                og_ref[:, _lane_block(hd, HEAD_DIM)] = _scan_finish(o, part(i, 4), on)

            @pl.when(jnp.max(risk) > SCAN_GUARD)
            def _():
                def exact_head(i, c2):
                    hd = hp * hps + i
                    lb = lb_ref[:, _lane_block(hd, HEAD_DIM)]
                    q = dyn_part(i, 0)
                    k_f, b_f = _scan_gates(dyn_part(i, 2), lb[0:1, :], cf_ref[...])
                    k_b, b_b = _scan_gates(dyn_part(i, 3), lb[1:2, :], cb_ref[...])
                    att = _scan_att_exact(q, k_f, b_f, False) + _scan_att_exact(q, k_b, b_b, True)
                    o = _dot(att.astype(BF16), dyn_part(i, 1).astype(BF16))
                    og_ref[:, _lane_block(hd, HEAD_DIM)] = _scan_finish(o, dyn_part(i, 4), on)
                    return c2

                lax.fori_loop(0, hps, exact_head, 0)
        else:
            def one_head(i, hd, lb, tb, exact, get):
                rf = pl.ds(pl.multiple_of(tb * BLK, BLK), BLK)
                rb = pl.ds(pl.multiple_of((n_blocks - 1 - tb) * BLK, BLK), BLK)
                cols = (slice(i * HEAD_DIM, (i + 1) * HEAD_DIM) if isinstance(i, int)
                        else pl.ds(pl.multiple_of(i * HEAD_DIM, HEAD_DIM), HEAD_DIM))
                q_f, q_b, v_f, v_b = get(i, 0, rf), get(i, 0, rb), get(i, 1, rf), get(i, 1, rb)
                prep_f = _scan_prep(q_f, get(i, 2, rf), lb[0:1, :], cf_ref[...], False, True)
                prep_b = _scan_prep(q_b, get(i, 3, rb), lb[1:2, :], cb_ref[...], True, True)
                if exact:
                    k_f, b_f = _scan_gates(get(i, 2, rf), lb[0:1, :], cf_ref[...])
                    k_b, b_b = _scan_gates(get(i, 3, rb), lb[1:2, :], cb_ref[...])
                    att_f = _scan_att_exact(q_f, k_f, b_f, False)
                    att_b = _scan_att_exact(q_b, k_b, b_b, True)
                else:
                    att_f, att_b = _scan_att(prep_f, lf_ref[...]), _scan_att(prep_b, lvb_ref[...])
                st_f, st_b = st_ref[2 * i], st_ref[2 * i + 1]
                of = _dot(att_f.astype(BF16), v_f.astype(BF16)) + _dot_nt(prep_f[2], st_f.astype(BF16))
                ob = _dot(att_b.astype(BF16), v_b.astype(BF16)) + _dot_nt(prep_b[2], st_b.astype(BF16))
                st_ref[2 * i] = _scan_state(prep_f, v_f, st_f)
                st_ref[2 * i + 1] = _scan_state(prep_b, v_b, st_b)

                @pl.when(2 * tb < n_blocks)
                def _():
                    oacc_ref[rf, cols] = of
                    oacc_ref[rb, cols] = ob

                @pl.when(2 * tb >= n_blocks)
                def _():
                    oacc_ref[rf, cols] += of
                    oacc_ref[rb, cols] += ob

                return jnp.maximum(prep_f[5], prep_b[5])

            for i, hd in enumerate(heads):
                st_ref[2 * i] = s0_ref[0, hd].T
                st_ref[2 * i + 1] = s0_ref[1, hd].T

            def body(tb, risk):
                for i, hd in enumerate(heads):
                    risk = jnp.maximum(risk, one_head(i, hd, lbs[i], tb, False, part))
                return risk

            risk = lax.fori_loop(0, n_blocks, body, jnp.zeros((1, LANES), F32))

            @pl.when(jnp.max(risk) > SCAN_GUARD)
            def _():
                def exact_head(i, c2):
                    hd = hp * hps + i
                    st_ref[2 * i] = s0_ref[0, hd].T
                    st_ref[2 * i + 1] = s0_ref[1, hd].T
                    lb = lb_ref[:, _lane_block(hd, HEAD_DIM)]

                    def exact_body(tb, c3):
                        one_head(i, hd, lb, tb, True, dyn_part)
                        return c3

                    lax.fori_loop(0, n_blocks, exact_body, 0)
                    return c2

                lax.fori_loop(0, hps, exact_head, 0)

            for i, hd in enumerate(heads):
                cols = slice(i * HEAD_DIM, (i + 1) * HEAD_DIM)
                og_ref[:, _lane_block(hd, HEAD_DIM)] = _scan_finish(oacc_ref[:, cols], part(i, 4), on)
        return carry

    _for_each_group(HEADS // hps, pair)
    xo_ref[...] = x_ref[...] + gate_ref[...] * _dot(og_ref[...], wout_ref[...])


def _hgrn_layer(x, n_seq, t, cond_of_seq, g, modr, w_in, lbs_j, onorm_g, w_out, s0, sfin_prev, slot):
    d = D_MODEL
    has_state = s0 is not None
    n_blocks = t // BLK
    assert n_blocks == 1 or n_blocks % 2 == 0
    full = lambda shape, **kw: pl.BlockSpec(shape, lambda b, *_: (0,) * len(shape), **kw)
    in_specs = [
        pl.BlockSpec((t, d), lambda b: (b, 0)),
        full((1, d)),
        _mod_spec(cond_of_seq, 0), _mod_spec(cond_of_seq, 1), _mod_spec(cond_of_seq, 2),
        full((d, HGRN_PARTS * d), **_RESIDENT),
        full((2, d)),
        full((1, HEAD_DIM)),
    ] + [full((BLK, BLK))] * 4
    args = [x, g.reshape(1, d), modr, modr, modr, w_in, lbs_j, onorm_g.reshape(1, HEAD_DIM)] + _scan_constants()
    state_block = (None, 2, HEADS, HEAD_DIM, HEAD_DIM)
    if has_state:
        in_specs.append(pl.BlockSpec(state_block, lambda b: (b, 0, 0, 0, 0)))
        args.append(s0)
    in_specs.append(full((d, d), **_RESIDENT))
    args.append(w_out)
    out_specs = [pl.BlockSpec((t, d), lambda b: (b, 0))]
    out_shape = [jax.ShapeDtypeStruct((n_seq * t, d), F32)]
    hps = SCAN_HEADS_PER_STEP if has_state else SCAN_HEADS_PER_BODY
    group_w = hps * HEAD_DIM
    scratch = [pltpu.VMEM((t, d), BF16), pltpu.VMEM((t, HGRN_PARTS * group_w), F32), pltpu.VMEM((t, d), BF16)]
    aliases = {}
    if has_state:
        scratch += [pltpu.VMEM((t, group_w), F32), pltpu.VMEM((2 * hps, HEAD_DIM, HEAD_DIM), F32)]
    else:
        state_dims = (2, HEADS, HEAD_DIM, HEAD_DIM)
        if sfin_prev is None:
            out_specs.append(pl.BlockSpec((None, N_SLOTS) + state_dims, lambda b: (b, 0, 0, 0, 0, 0)))
        else:
            out_specs.append(pl.BlockSpec((None, None) + state_dims, lambda b: (b, slot, 0, 0, 0, 0)))
            in_specs.append(pl.BlockSpec(memory_space=pl.ANY))
            args.append(sfin_prev)
            aliases = {len(args) - 1: 1}
        out_shape.append(jax.ShapeDtypeStruct((n_seq, N_SLOTS) + state_dims, F32))

    def body(*refs):
        if sfin_prev is not None:
            n_in = len(args)
            refs = refs[:n_in - 1] + refs[n_in:]
        _hgrn_kernel(*refs, n_blocks=n_blocks, has_state=has_state, slot=slot, owns_all_slots=sfin_prev is None,
                     hps=hps)

    out = pl.pallas_call(
        body,
        grid=(n_seq,),
        in_specs=in_specs,
        out_specs=out_specs,
        out_shape=out_shape,
        scratch_shapes=scratch,
        input_output_aliases=aliases,
        compiler_params=_cparams(("parallel",)),
        name="hgrn_layer_sample" if has_state else "hgrn_layer_prompt",
    )(*args)
    return out if not has_state else (out[0], None)


def _rope_tables(t_lat):
    rows = t_lat // GRID_W
    row = jnp.repeat(jnp.arange(rows), GRID_W).astype(F32)
    col = jnp.tile(jnp.arange(GRID_W), rows).astype(F32)
    half = QK_DIM // 2
    inv_freq = ROPE_THETA ** (-jnp.arange(0, half, 2, dtype=F32) / half)
    ang_row = row[:, None] * inv_freq
    ang_col = col[:, None] * inv_freq

    def part(ang):
        c, s = jnp.cos(ang), jnp.sin(ang)
        return jnp.concatenate([c, c], axis=1), jnp.concatenate([-s, s], axis=1)

    cr, sr = part(ang_row)
    cc, sc = part(ang_col)
    cos64 = jnp.concatenate([cr, cc], axis=1)
    sin64 = jnp.concatenate([sr, sc], axis=1)
    return jnp.concatenate([cos64, cos64], axis=1), jnp.concatenate([sin64, sin64], axis=1)


def _rope(x, cos, sin):
    lane = lax.broadcasted_iota(jnp.int32, x.shape, 1)
    first = (lane % (QK_DIM // 2)) < (QK_DIM // 4)
    swapped = jnp.where(first, pltpu.roll(x, LANES - QK_DIM // 4, 1), pltpu.roll(x, QK_DIM // 4, 1))
    return x * cos + swapped * sin


def _attn_kernel(*refs, n_qblk, n_sub, rope, cache, emit_kv, lam_init, slot, owns_all_slots):
    it = iter(refs)
    x_ref, g_ref, sh_ref, sc_ref, gate_ref, wqkv_ref = (next(it) for _ in range(6))
    qg_ref, kg_ref, lam_ref, sg_ref, seg_ref = (next(it) for _ in range(5))
    cos_ref = sin_ref = ck_ref = cv_ref = nk_ref = nv_ref = None
    if rope:
        cos_ref, sin_ref = next(it), next(it)
    if cache:
        ck_ref, cv_ref = next(it), next(it)
    wout_ref, xo_ref = next(it), next(it)
    if emit_kv:
        nk_all, nv_all = next(it), next(it)
        nk_refs = [_slot_view(nk_all.at[u], slot, owns_all_slots) for u in range(n_sub)]
        nv_refs = [_slot_view(nv_all.at[u], slot, owns_all_slots) for u in range(n_sub)]
    h_ref, qkv_ref, oa_ref = next(it), next(it), next(it)
    t = n_qblk * BLK

    d = D_MODEL
    for u in range(n_sub):
        seq = slice(u * t, (u + 1) * t)
        h_ref[seq, :] = _norm_mod(x_ref[seq, :], g_ref[...], sc_ref[...], sh_ref[...]).astype(BF16)
        qkv_ref[seq, :] = _dot(h_ref[seq, :], wqkv_ref[...])
    seg = seg_ref[...]
    lp = lam_ref[...]
    lam = (jnp.exp(jnp.sum(lp[0:1, :] * lp[1:2, :], axis=-1, keepdims=True))
           - jnp.exp(jnp.sum(lp[2:3, :] * lp[3:4, :], axis=-1, keepdims=True)) + lam_init)
    lane = lax.broadcasted_iota(jnp.int32, (1, LANES), 1)
    comp0 = lane < QK_DIM
    scale = QK_DIM ** -0.5

    def split(a):
        return [jnp.where(comp0, a, 0.0).astype(BF16), jnp.where(comp0, 0.0, a).astype(BF16)]

    def one_head(hd, u):
        hcol = _lane_block(hd, HEAD_DIM)
        seq = slice(u * t, (u + 1) * t)
        k = qkv_ref[seq, _block_at(d + hd * HEAD_DIM, HEAD_DIM)]
        v = qkv_ref[seq, _block_at(2 * d + hd * HEAD_DIM, HEAD_DIM)]
        kn = k * lax.rsqrt(_dot(k * k, seg) + EPS) * kg_ref[...]
        if emit_kv:
            nk_refs[u][:, hcol] = kn
            nv_refs[u][:, hcol] = v
        if rope:
            kn = _rope(kn, cos_ref[...], sin_ref[...])
        ks = split(kn)
        vb = v.astype(BF16)
        if cache:
            cks = split(ck_ref[:, hcol])
            cvb = cv_ref[:, hcol].astype(BF16)
        for qi in range(n_qblk):
            pos = slice(qi * BLK, (qi + 1) * BLK)
            rows = slice(u * t + qi * BLK, u * t + (qi + 1) * BLK)
            q = qkv_ref[rows, hcol]
            qn = q * lax.rsqrt(_dot(q * q, seg) + EPS) * qg_ref[...]
            if rope:
                qn = _rope(qn, cos_ref[pos, :], sin_ref[pos, :])
            qb = (qn * scale).astype(BF16)
            a_self, a_cache = None, None
            for c in range(2):
                s = _dot_nt(qb, ks[c])
                m = jnp.max(s, axis=-1, keepdims=True)
                if cache:
                    sc = _dot_nt(qb, cks[c])
                    m = jnp.maximum(m, jnp.max(sc, axis=-1, keepdims=True))
                p = jnp.exp(s - m)
                den = jnp.sum(p, axis=-1, keepdims=True)
                if cache:
                    pc = jnp.exp(sc - m)
                    den = den + jnp.sum(pc, axis=-1, keepdims=True)
                w = (1.0 / den) if c == 0 else (-lam / den)
                a_self = p * w if c == 0 else a_self + p * w
                if cache:
                    a_cache = pc * w if c == 0 else a_cache + pc * w
            o = _dot(a_self.astype(BF16), vb)
            if cache:
                o = o + _dot(a_cache.astype(BF16), cvb)
            ms = jnp.mean(o * o, axis=-1, keepdims=True)
            o = (o * lax.rsqrt(ms + EPS) * sg_ref[...]) * (1.0 - lam_init)
            oa_ref[rows, hcol] = o.astype(BF16)

    heads_per_body = ATTN_HEADS_PER_BODY if n_qblk == 1 else HEADS_PER_STEP

    def group(hg, carry):
        for u in range(n_sub):
            for i in range(heads_per_body):
                one_head(hg * heads_per_body + i, u)
        return carry

    _for_each_group(HEADS // heads_per_body, group)
    for u in range(n_sub):
        seq = slice(u * t, (u + 1) * t)
        xo_ref[seq, :] = x_ref[seq, :] + gate_ref[...] * _dot(oa_ref[seq, :], wout_ref[...])


def _attn_layer(x, n_seq, t, cond_of_seq, g, modr, w_qkv, qn_g, kn_g, lam_p, subln_g, w_out, layer_idx,
                cache_k_j, cache_v_j, kv_prev, slot):
    d = D_MODEL
    cache = cache_k_j is not None
    lam_init = 0.8 - 0.6 * math.exp(-0.3 * layer_idx)
    qg = jnp.tile(qn_g.reshape(1, QK_DIM), (1, 2))
    kg = jnp.tile(kn_g.reshape(1, QK_DIM), (1, 2))
    li = np.arange(LANES)
    seg = jnp.asarray((li[:, None] // QK_DIM == li[None, :] // QK_DIM).astype(np.float32) / QK_DIM)
    full = lambda shape, **kw: pl.BlockSpec(shape, lambda b, *_: (0,) * len(shape), **kw)
    n_sub = 1 if cache else ATTN_SEQS_PER_STEP
    assert n_seq % n_sub == 0
    rows = n_sub * t
    in_specs = [
        pl.BlockSpec((rows, d), lambda b: (b, 0)),
        full((1, d)),
        _mod_spec(cond_of_seq, 0), _mod_spec(cond_of_seq, 1), _mod_spec(cond_of_seq, 2),
        full((d, 3 * d), **_RESIDENT),
        full((1, LANES)), full((1, LANES)), full((4, QK_DIM)), full((1, HEAD_DIM)), full((LANES, LANES)),
    ]
    args = [x, g.reshape(1, d), modr, modr, modr, w_qkv, qg, kg, lam_p, subln_g.reshape(1, HEAD_DIM), seg]
    if cache:
        cos, sin = _rope_tables(t)
        past = cache_k_j.shape[1]
        in_specs += [full((t, LANES)), full((t, LANES)),
                     pl.BlockSpec((None, past, d), lambda b: (b, 0, 0)),
                     pl.BlockSpec((None, past, d), lambda b: (b, 0, 0))]
        args += [cos, sin, cache_k_j.reshape(n_seq, past, d), cache_v_j.reshape(n_seq, past, d)]
    in_specs.append(full((d, d), **_RESIDENT))
    args.append(w_out)
    out_specs = [pl.BlockSpec((rows, d), lambda b: (b, 0))]
    out_shape = [jax.ShapeDtypeStruct((n_seq * t, d), F32)]
    aliases = {}
    n_carried = 0
    if not cache:
        if kv_prev is None:
            kv_spec = pl.BlockSpec((n_sub, N_SLOTS, t, d), lambda b: (b, 0, 0, 0))
        else:
            kv_spec = pl.BlockSpec((n_sub, None, t, d), lambda b: (b, slot, 0, 0))
            in_specs += [pl.BlockSpec(memory_space=pl.ANY)] * 2
            args += list(kv_prev)
            aliases = {len(args) - 2: 1, len(args) - 1: 2}
            n_carried = 2
        out_specs += [kv_spec, kv_spec]
        out_shape += [jax.ShapeDtypeStruct((n_seq, N_SLOTS, t, d), F32)] * 2

    def body(*refs):
        n_in = len(args)
        refs = refs[:n_in - n_carried] + refs[n_in:]
        _attn_kernel(*refs, n_qblk=t // BLK, n_sub=n_sub, rope=cache, cache=cache, emit_kv=not cache,
                     lam_init=lam_init, slot=slot, owns_all_slots=kv_prev is None)

    out = pl.pallas_call(
        body,
        grid=(n_seq // n_sub,),
        in_specs=in_specs,
        out_specs=out_specs,
        out_shape=out_shape,
        scratch_shapes=[pltpu.VMEM((rows, d), BF16), pltpu.VMEM((rows, 3 * d), F32), pltpu.VMEM((rows, d), BF16)],
        input_output_aliases=aliases,
        compiler_params=_cparams(("parallel",)),
        name="attn_layer_sample" if cache else "attn_layer_prompt",
    )(*args)
    return (out[0], None) if cache else (out[0], (out[1], out[2]))


ROUTE_ROWS = 32
INFO_GID, INFO_RANK = 8, 9


def _two_stream_specs(n_prompt_tiles, width):
    return [pl.BlockSpec((BLK, width), lambda i, *_: (jnp.minimum(i, n_prompt_tiles - 1), 0)),
            pl.BlockSpec((BLK, width), lambda i, *_: (jnp.maximum(i - n_prompt_tiles, 0), 0))]


def _route_kernel(xp_ref, xs_ref, g_ref, sh_ref, sc_ref, wr_ref, br_ref, tri_ref,
                  h_ref, il_ref, it_ref, cum_ref, cumhi_ref, tot_ref, carry_ref, *, n_prompt_tiles):
    i = pl.program_id(0)

    @pl.when(i == 0)
    def _():
        carry_ref[...] = jnp.zeros_like(carry_ref)

    @pl.when(i < n_prompt_tiles)
    def _():
        h_ref[...] = _norm_mod(xp_ref[...], g_ref[...], sc_ref[...], sh_ref[...]).astype(BF16)

    @pl.when(i >= n_prompt_tiles)
    def _():
        h_ref[...] = _norm_mod(xs_ref[...], g_ref[...], sc_ref[...], sh_ref[...]).astype(BF16)

    logit = _dot_nt(wr_ref[...], h_ref[...]) + br_ref[...]
    gl = [logit[g:g + 1, :] for g in range(MOE_GROUPS)]
    gmax = functools.reduce(jnp.maximum, gl)
    gz = functools.reduce(lambda a, b: a + b, [jnp.exp(x - gmax) for x in gl])
    g_w = 1.0 / gz
    gid = jnp.full_like(gmax, MOE_GROUPS - 1)
    for g in range(MOE_GROUPS - 2, -1, -1):
        gid = jnp.where(gl[g] == gmax, float(g), gid)
    el = []
    for j in range(MOE_EPG):
        e = logit[MOE_GROUPS + j:MOE_GROUPS + j + 1, :]
        for g in range(1, MOE_GROUPS):
            r = MOE_GROUPS + g * MOE_EPG + j
            e = jnp.where(gid == float(g), logit[r:r + 1, :], e)
        el.append(e)
    emax = functools.reduce(jnp.maximum, el)
    pe = [jnp.exp(e - emax) for e in el]
    idx1 = jnp.full_like(emax, MOE_EPG - 1)
    for j in range(MOE_EPG - 2, -1, -1):
        idx1 = jnp.where(el[j] == emax, float(j), idx1)
    el2 = [jnp.where(idx1 == float(j), -jnp.inf, el[j]) for j in range(MOE_EPG)]
    emax2 = functools.reduce(jnp.maximum, el2)
    idx2 = jnp.full_like(emax, MOE_EPG - 1)
    for j in range(MOE_EPG - 2, -1, -1):
        idx2 = jnp.where(el2[j] == emax2, float(j), idx2)
    sel = [(idx1 == float(j)) | (idx2 == float(j)) for j in range(MOE_EPG)]
    den = functools.reduce(lambda a, b: a + b, [jnp.where(sel[j], pe[j], 0.0) for j in range(MOE_EPG)])
    cw = [jnp.where(sel[j], pe[j] * (g_w / den), 0.0) for j in range(MOE_EPG)]

    row8 = lax.broadcasted_iota(jnp.int32, (8, BLK), 0)
    onehot = jnp.where(row8.astype(F32) == gid, 1.0, 0.0)
    within = _dot(onehot.astype(BF16), tri_ref[...])
    carry = carry_ref[...]
    rank = jnp.sum(onehot * (within + carry[:, 0:1]), axis=0, keepdims=True)
    cum_ref[...] = carry
    new_carry = carry + jnp.sum(onehot, axis=1, keepdims=True)
    carry_ref[...] = new_carry
    cumhi_ref[...] = new_carry
    tot_ref[...] = new_carry

    il_ref[...] = jnp.where(row8 == 0, gid, jnp.where(row8 == 1, rank, 0.0))
    rowl = lax.broadcasted_iota(jnp.int32, (LANES, BLK), 0)
    m = jnp.zeros((LANES, BLK), F32)
    for j in range(MOE_EPG):
        hi = cw[j].astype(BF16).astype(F32)
        m = jnp.where(rowl == j, hi, m)
        m = jnp.where(rowl == MOE_EPG + j, cw[j] - hi, m)
    m = jnp.where(rowl == INFO_GID, gid, m)
    m = jnp.where(rowl == INFO_RANK, rank, m)
    it_ref[...] = m.T


def _moe_route(xp, xs, g, modr, wr_t, br, cond_of_tile):
    d = D_MODEL
    npt = xp.shape[0] // BLK
    nt = npt + xs.shape[0] // BLK
    n = nt * BLK
    tri = jnp.asarray(np.triu(np.ones((BLK, BLK), np.float32), 1), BF16)
    full = lambda shape: pl.BlockSpec(shape, lambda i: (0,) * len(shape))
    return pl.pallas_call(
        functools.partial(_route_kernel, n_prompt_tiles=npt),
        grid=(nt,),
        in_specs=_two_stream_specs(npt, d) + [
            full((1, d)),
            _mod_spec(cond_of_tile, 3), _mod_spec(cond_of_tile, 4),
            full((ROUTE_ROWS, d)), full((ROUTE_ROWS, 1)), full((BLK, BLK)),
        ],
        out_specs=[
            pl.BlockSpec((BLK, d), lambda i: (i, 0)),
            pl.BlockSpec((8, BLK), lambda i: (0, i)),
            pl.BlockSpec((BLK, LANES), lambda i: (i, 0)),
            pl.BlockSpec((None, 8, LANES), lambda i: (i, 0, 0)),
            pl.BlockSpec((None, 8, LANES), lambda i: (i, 0, 0)),
            pl.BlockSpec((8, LANES), lambda i: (0, 0)),
        ],
        out_shape=[
            jax.ShapeDtypeStruct((n, d), BF16),
            jax.ShapeDtypeStruct((8, n), F32),
            jax.ShapeDtypeStruct((n, LANES), F32),
            jax.ShapeDtypeStruct((nt, 8, LANES), F32),
            jax.ShapeDtypeStruct((nt, 8, LANES), F32),
            jax.ShapeDtypeStruct((8, LANES), F32),
        ],
        scratch_shapes=[pltpu.VMEM((8, LANES), F32)],
        compiler_params=_cparams(("arbitrary",)),
        name="moe_route",
    )(xp, xs, g.reshape(1, d), modr, modr, wr_t, br, tri)


def _sorted_pos(gid, rank, rstart_ref):
    p = rank
    for g in range(MOE_GROUPS):
        p = p + jnp.where(gid == float(g), rstart_ref[g].astype(F32), 0.0)
    return p


def _gather_tile(tbl_ref, h_ref, il_ref, it_ref, hs_ref, cws_ref, a, used, *, n_tiles):
    win = GATHER_WIN * BLK
    rstart_ref = tbl_ref

    @pl.when(jnp.logical_not(used))
    def _():
        hs_ref[...] = jnp.zeros(hs_ref.shape, hs_ref.dtype)
        cws_ref[...] = jnp.zeros(cws_ref.shape, cws_ref.dtype)

    @pl.when(used)
    def _():
        dest = (lax.broadcasted_iota(jnp.int32, (BLK, 1), 0) + a * BLK).astype(F32)
        src_tile = lax.broadcasted_iota(jnp.int32, (1, win), 1) // BLK
        clo = tbl_ref[TBL_CLO * LANES + a]
        n_win = (tbl_ref[TBL_CHI * LANES + a] - clo + GATHER_WIN) // GATHER_WIN

        def window(w):
            first = clo + w * GATHER_WIN
            c0 = jnp.minimum(first, n_tiles - GATHER_WIN)
            rows = pl.ds(pl.multiple_of(c0 * BLK, BLK), win)
            info = il_ref[:, rows]
            p = _sorted_pos(info[0:1, :], info[1:2, :], rstart_ref)
            p = jnp.where(src_tile + c0 >= first, p, -1.0)
            onehot = jnp.where(dest == p, 1.0, 0.0).astype(BF16)
            r = _dot(onehot, it_ref[rows, :].astype(BF16))
            return _dot(onehot, h_ref[rows, :]), r + pltpu.roll(r, LANES - MOE_EPG, 1)

        dh, dc = window(0)
        hs_ref[...] = dh.astype(BF16)
        cws_ref[...] = dc

        def body(w, carry):
            dh, dc = window(w)
            hs_ref[...] = (hs_ref[...].astype(F32) + dh).astype(BF16)
            cws_ref[...] += dc
            return carry

        lax.fori_loop(1, n_win, body, 0)


def _gather_kernel(tbl_ref, h_ref, il_ref, it_ref, hs_ref, cws_ref, *, n_tiles, fill_unused):
    s = pl.program_id(0)
    nt = tbl_ref[TBL_SNT * LANES + s]

    def tile(j, carry):
        rows = pl.ds(pl.multiple_of(j * BLK, BLK), BLK)
        _gather_tile(tbl_ref, h_ref, il_ref, it_ref, hs_ref.at[rows, :], cws_ref.at[rows, :],
                     s * SUPER + j, j < nt, n_tiles=n_tiles)
        return carry

    lax.fori_loop(0, SUPER if fill_unused else jnp.where(nt > 0, SUPER, 0), tile, 0)


def _moe_gather(h, info_lane, info_tok, tbl, bufs, n_super):
    n, d = h.shape
    nt = n // BLK
    first = bufs is None
    assert nt >= GATHER_WIN and n_super * SUPER <= LANES
    out_block = (lambda s, tbl_r: (s, 0)) if first else (lambda s, tbl_r: (tbl_r[TBL_SBLK * LANES + s], 0))
    carried = [] if first else [pl.BlockSpec(memory_space=pl.ANY)] * 2

    def body(tbl_ref, h_ref, il_ref, it_ref, *rest):
        _gather_kernel(tbl_ref, h_ref, il_ref, it_ref, *rest[len(carried):], n_tiles=nt, fill_unused=first)

    rows = n_super * SUPER_ROWS
    return pl.pallas_call(
        body,
        grid_spec=pltpu.PrefetchScalarGridSpec(
            num_scalar_prefetch=1,
            grid=(n_super,),
            in_specs=[
                pl.BlockSpec((n, d), lambda a, *_: (0, 0), **_RESIDENT),
                pl.BlockSpec((8, n), lambda a, *_: (0, 0), **_RESIDENT),
                pl.BlockSpec((n, LANES), lambda a, *_: (0, 0), **_RESIDENT),
            ] + carried,
            out_specs=[pl.BlockSpec((SUPER_ROWS, d), out_block), pl.BlockSpec((SUPER_ROWS, LANES), out_block)],
        ),
        out_shape=[jax.ShapeDtypeStruct((rows, d), BF16), jax.ShapeDtypeStruct((rows, LANES), F32)],
        input_output_aliases={} if first else {4: 0, 5: 1},
        compiler_params=_cparams(("arbitrary",)),
        name="moe_gather",
    )(tbl, h, info_lane, info_tok, *(() if first else bufs))


def _moe_mlp_kernel(tbl_ref, hs_ref, cws_ref, wg_ref, wu_ref, wd_ref, ys_ref,
                    acc_ref, wgb_ref, wub_ref, wdb_ref, *, fill_unused):
    s = pl.program_id(0)
    k = pl.program_id(1)
    nt = tbl_ref[TBL_SNT * LANES + s]

    if fill_unused:
        @pl.when((nt == 0) & (k == 0))
        def _():
            ys_ref[...] = jnp.zeros_like(ys_ref)

    @pl.when(nt > 0)
    def _():
        wgb_ref[...] = wg_ref[...].astype(BF16)
        wub_ref[...] = wu_ref[...].astype(BF16)
        wdb_ref[...] = wd_ref[...].astype(BF16)

    @pl.when((s == 0) & (k == 0))
    def _():
        acc_ref[...] = jnp.zeros_like(acc_ref)

    def skipped(j):
        rows = slice(j * EXPERT_TILES * BLK, (j + 1) * EXPERT_TILES * BLK)
        ys_ref[rows, :] = jnp.zeros((EXPERT_TILES * BLK, D_MODEL), BF16)

    def block(j):
        rows = slice(j * EXPERT_TILES * BLK, (j + 1) * EXPERT_TILES * BLK)
        hsub = hs_ref[rows, :]
        cws = cws_ref[rows, :]
        cwk = jnp.zeros((EXPERT_TILES * BLK, 1), F32)
        for kk in range(MOE_EPG):
            cwk = jnp.where(k == kk, cws[:, kk:kk + 1], cwk)
        y = None
        for fh in range(MOE_D_FF // FF_PART):
            fc = slice(fh * FF_PART, (fh + 1) * FF_PART)
            gate = _dot(hsub, wgb_ref[:, fc])
            up = _dot(hsub, wub_ref[:, fc])
            act = ((_silu(gate) * up) * cwk).astype(BF16)
            part = _dot(act, wdb_ref[fc, :])
            y = part if y is None else y + part
        total = jnp.where(k == 0, 0.0, acc_ref[rows, :]) + y
        acc_ref[rows, :] = total
        ys_ref[rows, :] = total.astype(BF16)

    def maybe_block(j):
        pl.when(j * EXPERT_TILES < nt)(functools.partial(block, j))
        pl.when((j * EXPERT_TILES >= nt) & (nt > 0) & (k == 0))(functools.partial(skipped, j))

    n_blocks = SUPER // EXPERT_TILES
    n_straight = STRAIGHT_TILES // EXPERT_TILES

    @pl.when(nt >= STRAIGHT_TILES)
    def _():
        for j in range(n_straight):
            block(j)

    @pl.when(nt < STRAIGHT_TILES)
    def _():
        for j in range(n_straight):
            maybe_block(j)

    for j in range(n_straight, n_blocks):
        maybe_block(j)


def _moe_mlp(hs, cws, w_gate, w_up, w_down, layer, tbl, ys_buf):
    d = hs.shape[1]
    n_super = hs.shape[0] // SUPER_ROWS
    first = ys_buf is None
    carried = [] if first else [pl.BlockSpec(memory_space=pl.ANY)]

    def body(tbl_ref, hs_ref, cws_ref, wg_ref, wu_ref, wd_ref, *rest):
        _moe_mlp_kernel(tbl_ref, hs_ref, cws_ref, wg_ref, wu_ref, wd_ref, *rest[len(carried):], fill_unused=first)

    def widx(s, k, tbl_r):
        kk = jnp.where(tbl_r[TBL_SNT * LANES + s] > 0, k, MOE_EPG - 1)
        return (layer * MOE_EXPERTS + tbl_r[TBL_SGROUP * LANES + s] * MOE_EPG + kk, 0, 0)

    rows_idx = lambda s, k, tbl_r: (tbl_r[TBL_SBLK * LANES + s], 0)
    out_idx = (lambda s, k, tbl_r: (s, 0)) if first else rows_idx
    return pl.pallas_call(
        body,
        grid_spec=pltpu.PrefetchScalarGridSpec(
            num_scalar_prefetch=1,
            grid=(n_super, MOE_EPG),
            in_specs=[
                pl.BlockSpec((SUPER_ROWS, d), rows_idx),
                pl.BlockSpec((SUPER_ROWS, LANES), rows_idx),
                pl.BlockSpec((None, d, MOE_D_FF), widx),
                pl.BlockSpec((None, d, MOE_D_FF), widx),
                pl.BlockSpec((None, MOE_D_FF, d), widx),
            ] + carried,
            out_specs=pl.BlockSpec((SUPER_ROWS, d), out_idx),
            scratch_shapes=[
                pltpu.VMEM((SUPER_ROWS, d), F32),
                pltpu.VMEM((d, MOE_D_FF), BF16),
                pltpu.VMEM((d, MOE_D_FF), BF16),
                pltpu.VMEM((MOE_D_FF, d), BF16),
            ],
        ),
        out_shape=jax.ShapeDtypeStruct(hs.shape, BF16),
        input_output_aliases={} if first else {6: 0},
        compiler_params=_cparams(("arbitrary", "arbitrary")),
        name="moe_experts",
    )(tbl, hs, cws, w_gate, w_up, w_down, *(() if first else (ys_buf,)))


N_SRC = 2 * MOE_GROUPS


def _unsort_kernel(tbl_ref, xp_ref, xs_ref, it_ref, gate_ref, *rest, n_prompt_tiles):
    ys_refs, op_ref, os_ref = rest[:N_SRC], rest[N_SRC], rest[N_SRC + 1]
    t = pl.program_id(0)
    info = it_ref[...]
    p = _sorted_pos(info[:, INFO_GID:INFO_GID + 1], info[:, INFO_RANK:INFO_RANK + 1], tbl_ref)
    lane = lax.broadcasted_iota(jnp.int32, (1, BLK), 1).astype(F32)
    slot = lambda m: tbl_ref[(TBL_UT + t) * LANES + m]

    def take(m):
        a = slot(m)
        onehot = jnp.where(p - (a * BLK).astype(F32) == lane, 1.0, 0.0).astype(BF16)
        return _dot(onehot, ys_refs[m][...])

    def stream(x_ref, o_ref):
        first = functools.reduce(lambda u, w: u + w, [take(m) for m in range(0, N_SRC, 2)])
        o_ref[...] = x_ref[...] + gate_ref[...] * first
        for m in range(1, N_SRC, 2):
            @pl.when(slot(m) >= 0)
            def _():
                o_ref[...] += gate_ref[...] * take(m)

    pl.when(t < n_prompt_tiles)(functools.partial(stream, xp_ref, op_ref))
    pl.when(t >= n_prompt_tiles)(functools.partial(stream, xs_ref, os_ref))


def _moe_unsort(xp, xs, info_tok, modr, ys, tbl, cond_of_tile):
    d = D_MODEL
    npt = xp.shape[0] // BLK
    nt = npt + xs.shape[0] // BLK

    def ys_spec(m):
        return pl.BlockSpec((BLK, d), lambda t, tbl_r: (jnp.maximum(tbl_r[(TBL_UT + t) * LANES + m], 0), 0))

    return pl.pallas_call(
        functools.partial(_unsort_kernel, n_prompt_tiles=npt),
        grid_spec=pltpu.PrefetchScalarGridSpec(
            num_scalar_prefetch=1,
            grid=(nt,),
            in_specs=_two_stream_specs(npt, d) + [
                pl.BlockSpec((BLK, LANES), lambda t, *_: (t, 0)),
                _mod_spec(cond_of_tile, 5),
            ] + [ys_spec(m) for m in range(N_SRC)],
            out_specs=_two_stream_specs(npt, d),
        ),
        out_shape=[jax.ShapeDtypeStruct(xp.shape, F32), jax.ShapeDtypeStruct(xs.shape, F32)],
        compiler_params=_cparams(("arbitrary",)),
        name="moe_unsort_residual",
    )(tbl, xp, xs, info_tok, modr, *([ys] * N_SRC))


def _tables_kernel(lo_ref, hi_ref, tot_ref, tbl_ref, *, n_tiles, n_super):
    one = lambda cond: jnp.where(cond, 1.0, 0.0)
    groups = range(MOE_GROUPS)
    lane = lax.broadcasted_iota(jnp.int32, (1, LANES), 1).astype(F32)
    tot = [tot_ref[g:g + 1, :] for g in groups]
    pick = lambda vals, idx: sum(jnp.where(idx == float(g), vals[g], 0.0) for g in groups)

    n_sup = [sum(one(tot[g] > float(m * SUPER_ROWS)) for m in range(n_super)) for g in groups]
    sup_start, sup_end, run = [], [], 0.0
    for g in groups:
        sup_start.append(run + 0.0 * tot[g])
        run = run + n_sup[g]
        sup_end.append(run)
    n_used = sup_end[-1]
    rstart = [sup_start[g] * float(SUPER_ROWS) for g in groups]
    group_of = lambda s: jnp.minimum(sum(one(s >= sup_end[g]) for g in groups), float(MOE_GROUPS - 1))

    sg, snt = [], []
    for m in range(n_super):
        used = float(m) < n_used
        g_m = group_of(jnp.where(used, float(m), n_used - 1.0))
        rows_left = pick(tot, g_m) - (float(m) - pick(sup_start, g_m)) * float(SUPER_ROWS)
        nt_m = jnp.where(used, jnp.clip(jnp.floor((rows_left + float(BLK - 1)) * (1.0 / BLK)), 0.0, float(SUPER)), 0.0)
        sg.append(g_m)
        snt.append(nt_m)
    sblk = [jnp.minimum(float(m), n_used - 1.0) for m in range(n_super)]
    by_super_lane = lambda vals: sum(jnp.where(lane == float(m), vals[m], 0.0) for m in range(n_super))

    sa = sum(one(lane >= float(m * SUPER)) for m in range(1, n_super))
    ja = lane - sa * float(SUPER)
    ga = sum(jnp.where(sa == float(m), sg[m], 0.0) for m in range(n_super))
    r0 =((sa - pick(sup_start, ga)) * float(SUPER) + ja) * float(BLK)
    lo = [lo_ref[:, g, :] for g in groups]
    hi = [hi_ref[:, g, :] for g in groups]
    clo = jnp.sum(one(pick(hi, ga) <= r0), axis=0, keepdims=True)
    chi = jnp.sum(one(pick(lo, ga) < r0 + float(BLK)), axis=0, keepdims=True) - 1.0
    clo = jnp.clip(clo, 0.0, float(n_tiles - 1))
    chi = jnp.clip(chi, clo, float(n_tiles - 1))

    gm = one(lane >= 2.0) + one(lane >= 4.0) + one(lane >= 6.0)
    first = pick(rstart, gm) + pick(lo, gm)
    last = pick(rstart, gm) + pick(hi, gm) - 1.0
    t0 = jnp.floor(first * (1.0 / BLK))
    t1 = jnp.floor(last * (1.0 / BLK))
    has = pick(hi, gm) > pick(lo, gm)
    a0 = jnp.where(has, t0, -1.0)
    a1 = jnp.where(has, jnp.where(t1 != t0, t1, -1.0), -1.0)
    ut = jnp.where(lane >= float(N_SRC), -1.0, jnp.where(lane - 2.0 * gm == 0.0, a0, a1))

    def put(row, v):
        tbl_ref[row:row + 1, :] = v.astype(jnp.int32)

    put(TBL_RSTART, sum(jnp.where(lane == float(g), rstart[g], 0.0) for g in groups))
    put(TBL_SGROUP, by_super_lane(sg))
    put(TBL_SNT, by_super_lane(snt))
    put(TBL_SBLK, by_super_lane(sblk))
    put(TBL_CLO, clo)
    put(TBL_CHI, chi)
    tbl_ref[TBL_CHI + 1:TBL_UT, :] = jnp.zeros((TBL_UT - TBL_CHI - 1, LANES), jnp.int32)
    tbl_ref[TBL_UT:TBL_UT + n_tiles, :] = ut.astype(jnp.int32)


def _moe_tables(cum_lo, cum_hi, tot, n_tiles, n_super):
    assert n_super * SUPER <= LANES
    tbl = pl.pallas_call(
        functools.partial(_tables_kernel, n_tiles=n_tiles, n_super=n_super),
        out_shape=jax.ShapeDtypeStruct((TBL_UT + n_tiles, LANES), jnp.int32),
        name="moe_tables",
    )(cum_lo, cum_hi, tot)
    return tbl.reshape(-1)


def _moe_layer(xp, xs, bufs, g, modr, wr_t, br, w_gate, w_up, w_down, layer, cond_of_tile):
    n = xp.shape[0] + xs.shape[0]
    nt = n // BLK
    n_super = (n - 1) // SUPER_ROWS + MOE_GROUPS
    h, info_lane, info_tok, cum_lo, cum_hi, tot = _moe_route(xp, xs, g, modr, wr_t, br, cond_of_tile)
    tbl = _moe_tables(cum_lo, cum_hi, tot, nt, n_super)
    hs, cws = _moe_gather(h, info_lane, info_tok, tbl, None if bufs is None else bufs[:2], n_super)
    ys = _moe_mlp(hs, cws, w_gate, w_up, w_down, layer, tbl, None if bufs is None else bufs[2])
    xp, xs = _moe_unsort(xp, xs, info_tok, modr, ys, tbl, cond_of_tile)
    return xp, xs, (hs, cws, ys)


def kernel(x_prompt, x_sample, c, cache_k, cache_v, state_hgrn, c_ctx, norm_g, w_ada, b_ada, hgrn_w_in, hgrn_lb_logits, hgrn_onorm_g, hgrn_w_out, attn_w_qkv, attn_qn_g, attn_kn_g, attn_lambda, attn_subln_g, attn_w_out, moe_w_group, moe_b_group, moe_w_expert, moe_b_expert, moe_w_gate, moe_w_up, moe_w_down):
    n_prompt_seq, seq, d = x_prompt.shape
    dec_batch, dec_seq, _ = x_sample.shape
    n_prompt = n_prompt_seq * seq
    assert d == D_MODEL and seq == BLK and dec_seq % BLK == 0
    assert 1 + dec_batch <= N_COND

    xp = x_prompt.reshape(n_prompt, d)
    xs = x_sample.reshape(dec_batch * dec_seq, d)
    cond = jnp.zeros((N_COND, d), F32).at[0].set(c_ctx).at[1:1 + dec_batch].set(c)
    mod = _modulation(cond, w_ada, b_ada)

    lbs = jnp.cumsum(jax.nn.softmax(hgrn_lb_logits.astype(F32), axis=0), axis=0)
    lbs = lbs - lbs[0:1]

    cond_prompt = lambda b: 0
    cond_sample = lambda b: 1 + b
    npt, spt = n_prompt // BLK, dec_seq // BLK
    cond_tile = lambda i: jnp.where(i < npt, 0, 1 + (i - npt) // spt)

    w_in = hgrn_w_in.astype(BF16)
    w_hout = hgrn_w_out.astype(BF16)
    w_qkv = attn_w_qkv.astype(BF16)
    w_aout = attn_w_out.astype(BF16)
    wr_t = jnp.zeros((DEPTH, ROUTE_ROWS, d), F32)
    wr_t = wr_t.at[:, :MOE_GROUPS].set(moe_w_group.transpose(0, 2, 1))
    wr_t = wr_t.at[:, MOE_GROUPS:MOE_GROUPS + MOE_EXPERTS].set(moe_w_expert.transpose(0, 2, 1)).astype(BF16)
    br = jnp.zeros((DEPTH, ROUTE_ROWS, 1), F32)
    br = br.at[:, :MOE_GROUPS, 0].set(moe_b_group).at[:, MOE_GROUPS:MOE_GROUPS + MOE_EXPERTS, 0].set(moe_b_expert)
    w_gate = moe_w_gate.reshape(DEPTH * MOE_EXPERTS, d, MOE_D_FF)
    w_up = moe_w_up.reshape(DEPTH * MOE_EXPERTS, d, MOE_D_FF)
    w_down = moe_w_down.reshape(DEPTH * MOE_EXPERTS, MOE_D_FF, d)

    sfin, kv = None, None
    bufs = None
    for i in range(DEPTH):
        j = i // 2
        modr = mod[i].reshape(N_COND * N_MOD, 1, d)
        if i % 2 == 0:
            common = (norm_g[i, 0], modr, w_in[j], lbs[j], hgrn_onorm_g[j], w_hout[j])
            xp, sfin = _hgrn_layer(xp, n_prompt_seq, seq, cond_prompt, *common, None, sfin, j)
            xs, _ = _hgrn_layer(xs, dec_batch, dec_seq, cond_sample, *common, state_hgrn[:, j], None, j)
        else:
            common = (norm_g[i, 0], modr, w_qkv[j], attn_qn_g[j], attn_kn_g[j], attn_lambda[j], attn_subln_g[j],
                      w_aout[j], i)
            xp, kv = _attn_layer(xp, n_prompt_seq, seq, cond_prompt, *common, None, None, kv, j)
            xs, _ = _attn_layer(xs, dec_batch, dec_seq, cond_sample, *common, cache_k[:, j], cache_v[:, j], None, j)
        xp, xs, bufs = _moe_layer(xp, xs, bufs, norm_g[i, 1], modr, wr_t[i], br[i], w_gate, w_up, w_down, i, cond_tile)

    new_k = kv[0].reshape(n_prompt_seq, DEPTH // 2, seq, HEADS, 2, QK_DIM)
    new_v = kv[1].reshape(n_prompt_seq, DEPTH // 2, seq, HEADS, HEAD_DIM)
    return (xp.reshape(n_prompt_seq, seq, d), xs.reshape(dec_batch, dec_seq, d), new_k, new_v, sfin)
```

```python
import functools
import math

import numpy as np
import jax
import jax.numpy as jnp
from jax import lax
from jax.experimental import pallas as pl
from jax.experimental.pallas import tpu as pltpu

F32 = jnp.float32
BF16 = jnp.bfloat16

D_MODEL = 1024
DEPTH = 4
GRID_W = 64
HEADS = 8
HEAD_DIM = 128
QK_DIM = 64
ROPE_THETA = 10000.0
MOE_GROUPS = 4
MOE_EPG = 4
MOE_EXPERTS = MOE_GROUPS * MOE_EPG
MOE_D_FF = 512
EPS = 1e-6
N_COND = 8
N_MOD = 6
HGRN_PARTS = 5

LANES = 128
BLK = 256
CHUNK = 32
N_CHUNK = BLK // CHUNK
HEADS_PER_STEP = 2
SCAN_HEADS_PER_STEP = 4
SCAN_HEADS_PER_BODY = 8
ATTN_HEADS_PER_BODY = 8
ATTN_SEQS_PER_STEP = 2
SUPER = 12
SUPER_ROWS = SUPER * BLK
STRAIGHT_TILES = 8
TBL_RSTART, TBL_SGROUP, TBL_SNT, TBL_SBLK, TBL_CLO, TBL_CHI = range(6)
TBL_UT = 8
EXPERT_TILES = 2
FF_PART = 256
GATHER_WIN = 6
EXP2_CLAMP = 115.0
SCAN_GUARD = 100.0
VMEM_LIMIT = 56 * 1024 * 1024
N_SLOTS = DEPTH // 2
_RESIDENT = dict(pipeline_mode=pl.Buffered(1))


def _cparams(sem):
    return pltpu.CompilerParams(dimension_semantics=sem, vmem_limit_bytes=VMEM_LIMIT)


def _silu(x):
    return x * jax.nn.sigmoid(x)


def _dot(a, b):
    return jnp.dot(a, b, preferred_element_type=F32)


def _dot_nt(a, b):
    return lax.dot_general(a, b, (((1,), (1,)), ((), ())), preferred_element_type=F32)


def _dot_tn(a, b):
    return lax.dot_general(a, b, (((0,), (0,)), ((), ())), preferred_element_type=F32)


def _lane_block(i, width):
    return _block_at(i * width, width)


def _block_at(start, width):
    if isinstance(start, int):
        return slice(start, start + width)
    return pl.ds(pl.multiple_of(start, width), width)


def _for_each_group(n_groups, body):
    if n_groups == 1:
        body(0, 0)
    else:
        lax.fori_loop(0, n_groups, body, 0)


def _mod_kernel(c_ref, w_ref, b_ref, o_ref):
    o_ref[...] = _dot(_silu(c_ref[...]), w_ref[...]) + b_ref[...]


def _modulation(cond, w_ada, b_ada):
    tn = 1536
    nj = (N_MOD * D_MODEL) // tn
    return pl.pallas_call(
        _mod_kernel,
        grid=(DEPTH, nj),
        in_specs=[
            pl.BlockSpec((N_COND, D_MODEL), lambda l, j: (0, 0)),
            pl.BlockSpec((None, D_MODEL, tn), lambda l, j: (l, 0, j)),
            pl.BlockSpec((None, 1, tn), lambda l, j: (l, 0, j)),
        ],
        out_specs=pl.BlockSpec((None, N_COND, tn), lambda l, j: (l, 0, j)),
        out_shape=jax.ShapeDtypeStruct((DEPTH, N_COND, N_MOD * D_MODEL), F32),
        compiler_params=_cparams(("parallel", "parallel")),
        name="modulation",
    )(cond, w_ada, b_ada.reshape(DEPTH, 1, N_MOD * D_MODEL))


def _norm_mod(x, g, sc, sh):
    ms = jnp.mean(x * x, axis=-1, keepdims=True)
    return (x * lax.rsqrt(ms + EPS) * g) * (1.0 + sc) + sh


def _mod_spec(cond_of_step, which):
    return pl.BlockSpec((None, 1, D_MODEL), lambda i, *_: (cond_of_step(i) * N_MOD + which, 0, 0))


def _scan_constants():
    t = np.arange(BLK)
    out = []
    for rev in (False, True):
        u = (BLK - 1 - t) if rev else t
        ut, us = u[:, None], u[None, :]
        cums = (us <= ut).astype(np.float32)
        lev = np.where(us > ut, 0,
              np.where(ut // CHUNK == us // CHUNK, 1,
              np.where(ut // 64 == us // 64, 2,
              np.where(ut // 128 == us // 128, 3, 4)))).astype(np.int32)
        out += [jnp.asarray(cums, BF16), jnp.asarray(lev)]
    return out


def _rows_to_block(rows, rev):
    order = rows[::-1] if rev else rows
    return jnp.concatenate([jnp.broadcast_to(r, (CHUNK, LANES)) for r in order], axis=0)


def _trunc_bf16(x):
    return lax.bitcast_convert_type(lax.bitcast_convert_type(x, jnp.int32) & jnp.int32(-65536), F32)


def _scan_gates(z, lb, cums):
    sig = jax.nn.sigmoid(z)
    f = lb + (1.0 - lb) * sig
    logf = jnp.log2(f)
    k = (1.0 - lb) * (1.0 - sig)
    hi32 = _trunc_bf16(logf)
    bb = _dot(cums, jnp.concatenate([hi32.astype(BF16), (logf - hi32).astype(BF16)], axis=1))
    return k, bb[:, :LANES] + bb[:, LANES:]


def _scan_att_exact(q, k, b, rev):
    ti = lax.broadcasted_iota(jnp.int32, (BLK, BLK), 0)
    si = lax.broadcasted_iota(jnp.int32, (BLK, BLK), 1)
    ri = lax.broadcasted_iota(jnp.int32, (BLK, 1), 0)
    if rev:
        ti, si, ri = BLK - 1 - ti, BLK - 1 - si, BLK - 1 - ri
    h1 = _trunc_bf16(b)
    h2 = _trunc_bf16(b - h1)
    pieces = jnp.concatenate([h1.astype(BF16), h2.astype(BF16), (b - h1 - h2).astype(BF16)], axis=1)
    att = jnp.where(ti == si, jnp.sum(q * k, axis=-1, keepdims=True), 0.0)
    g = 2
    while g <= BLK:
        h = g // 2
        at_mid = jnp.where(((ti & -g) + (h - 1)) == si, 1.0, 0.0).astype(BF16)
        bb = _dot(at_mid, pieces)
        b_r = bb[:, :LANES] + bb[:, LANES:2 * LANES] + bb[:, 2 * LANES:]
        after = (ri & h) != 0
        qg = jnp.where(after, q * jnp.exp2(jnp.minimum(b - b_r, 0.0)), 0.0).astype(BF16)
        kg = jnp.where(after, 0.0, k * jnp.exp2(jnp.minimum(b_r - b, 0.0))).astype(BF16)
        prod = _dot_nt(qg, kg)
        att = jnp.where((ti & -g) == (si & -g),
                        jnp.where((ti & h) != 0, jnp.where((si & h) == 0, prod, att), att), att)
        g *= 2
    return att


def _scan_prep(q, z, lb, cums, rev, with_inter):
    k, b = _scan_gates(z, lb, cums)

    e_row, m_row = (0, CHUNK // 2) if rev else (CHUNK - 1, CHUNK // 2 - 1)
    ends, mids = [], []
    for j in range(N_CHUNK):
        ends.append(b[j * CHUNK + e_row:j * CHUNK + e_row + 1, :])
        mids.append(b[j * CHUNK + m_row:j * CHUNK + m_row + 1, :])
    if rev:
        ends, mids = ends[::-1], mids[::-1]
    zero = jnp.zeros((1, LANES), F32)
    one = jnp.ones((1, LANES), F32)
    pres = [zero] + ends[:-1]
    b_pre = _rows_to_block(pres, rev)
    b_end = _rows_to_block(ends, rev)
    b_mid = _rows_to_block(mids, rev)

    qd = q * jnp.exp2(b - b_pre)
    ku = k * jnp.exp2(b_end - b)
    qm = q * jnp.exp2(jnp.clip(b - b_mid, -EXP2_CLAMP, EXP2_CLAMP))
    km = k * jnp.exp2(jnp.clip(b_mid - b, -EXP2_CLAMP, EXP2_CLAMP))

    levels = []
    for nc in (2, 4, 8):
        fq, fk = [], []
        for ju in range(N_CHUNK):
            r = (ju // nc) * nc + nc // 2 - 1
            if ju % nc >= nc // 2:
                fq.append(one if nc == 2 else jnp.exp2(pres[ju] - ends[r]))
                fk.append(zero)
            else:
                fq.append(zero)
                fk.append(one if nc == 2 else jnp.exp2(ends[r] - ends[ju]))
        levels.append(((qd * _rows_to_block(fq, rev)).astype(BF16), (ku * _rows_to_block(fk, rev)).astype(BF16)))

    last = ends[-1]
    qh = (qd * _rows_to_block([jnp.exp2(p) for p in pres], rev)).astype(BF16) if with_inter else None
    kh = (ku * _rows_to_block([jnp.exp2(last - e) for e in ends], rev)).astype(BF16)
    risk = functools.reduce(jnp.maximum, [jnp.maximum(p - m, m - e) for p, m, e in zip(pres, mids, ends)])
    return (qm.astype(BF16), km.astype(BF16)), levels, qh, kh, last, risk


def _scan_att(prep, lev):
    att = jnp.where(lev == 1, _dot_nt(*prep[0]), 0.0)
    for level, (ql, kl) in enumerate(prep[1], start=2):
        att = jnp.where(lev == level, _dot_nt(ql, kl), att)
    return att


def _scan_att_bidir(prep_f, prep_b, lev_f, lev_b):
    att = jnp.where(lev_f == 1, _dot_nt(*prep_f[0]), 0.0) + jnp.where(lev_b == 1, _dot_nt(*prep_b[0]), 0.0)
    lev = jnp.maximum(lev_f, lev_b)
    for level, ((qf, kf), (qb, kb)) in enumerate(zip(prep_f[1], prep_b[1]), start=2):
        both = _dot_nt(jnp.concatenate([qf, qb], axis=1), jnp.concatenate([kf, kb], axis=1))
        att = jnp.where(lev == level, both, att)
    return att


def _scan_state(prep, v, st_prev):
    ut = _dot_tn(v.astype(BF16), prep[3])
    return ut if st_prev is None else st_prev * jnp.exp2(prep[4]) + ut


def _scan_finish(o, g, on):
    ms = jnp.mean(o * o, axis=-1, keepdims=True)
    return ((o * lax.rsqrt(ms + EPS) * on) * _silu(g)).astype(BF16)


def _slot_view(ref, slot, owns_all_slots):
    if not owns_all_slots:
        return ref
    for s in range(ref.shape[0]):
        if s != slot:
            ref[s] = jnp.zeros(ref.shape[1:], ref.dtype)
    return ref.at[slot]


def _hgrn_kernel(*refs, n_blocks, has_state, slot, owns_all_slots, hps):
    it = iter(refs)
    x_ref, g_ref, sh_ref, sc_ref, gate_ref, win_ref, lb_ref, on_ref = (next(it) for _ in range(8))
    cf_ref, lf_ref, cb_ref, lvb_ref = (next(it) for _ in range(4))
    s0_ref = next(it) if has_state else None
    wout_ref, xo_ref = next(it), next(it)
    sfin_ref = None if has_state else next(it)
    h_ref, proj_ref, og_ref = next(it), next(it), next(it)
    oacc_ref, st_ref = (next(it), next(it)) if has_state else (None, None)

    h_ref[...] = _norm_mod(x_ref[...], g_ref[...], sc_ref[...], sh_ref[...]).astype(BF16)
    on = on_ref[...]
    if not has_state:
        sfin_ref = _slot_view(sfin_ref, slot, owns_all_slots)

    group_w = hps * HEAD_DIM

    def part(i, p, rows=slice(None)):
        return proj_ref[rows, p * group_w + i * HEAD_DIM:p * group_w + (i + 1) * HEAD_DIM]

    def pair(hp, carry):
        for p in range(HGRN_PARTS):
            cols = _block_at(p * D_MODEL + hp * group_w, group_w)
            proj_ref[:, p * group_w:(p + 1) * group_w] = _dot(h_ref[...], win_ref[:, cols])
        heads = [hp * hps + i for i in range(hps)]
        lbs = [lb_ref[:, _lane_block(hd, HEAD_DIM)] for hd in heads]

        def dyn_part(i, p, rows=slice(None)):
            return proj_ref[rows, pl.ds(pl.multiple_of(p * group_w + i * HEAD_DIM, HEAD_DIM), HEAD_DIM)]

        if not has_state:
            risk = jnp.zeros((1, LANES), F32)
            for i, hd in enumerate(heads):
                q, v = part(i, 0), part(i, 1)
                prep_f = _scan_prep(q, part(i, 2), lbs[i][0:1, :], cf_ref[...], False, False)
                prep_b = _scan_prep(q, part(i, 3), lbs[i][1:2, :], cb_ref[...], True, False)
                risk = jnp.maximum(risk, jnp.maximum(prep_f[5], prep_b[5]))
                att = _scan_att_bidir(prep_f, prep_b, lf_ref[...], lvb_ref[...])
                o = _dot(att.astype(BF16), v.astype(BF16))
                sfin_ref[0, hd] = _scan_state(prep_f, v, None).T
                sfin_ref[1, hd] = _scan_state(prep_b, v, None).T
                og_ref[:, _lane_block(hd, HEAD_DIM)] = _scan_finish(o, part(i, 4), on)

            @pl.when(jnp.max(risk) > SCAN_GUARD)
            def _():
                def exact_head(i, c2):
                    hd = hp * hps + i
                    lb = lb_ref[:, _lane_block(hd, HEAD_DIM)]
                    q = dyn_part(i, 0)
                    k_f, b_f = _scan_gates(dyn_part(i, 2), lb[0:1, :], cf_ref[...])
                    k_b, b_b = _scan_gates(dyn_part(i, 3), lb[1:2, :], cb_ref[...])
                    att = _scan_att_exact(q, k_f, b_f, False) + _scan_att_exact(q, k_b, b_b, True)
                    o = _dot(att.astype(BF16), dyn_part(i, 1).astype(BF16))
                    og_ref[:, _lane_block(hd, HEAD_DIM)] = _scan_finish(o, dyn_part(i, 4), on)
                    return c2

                lax.fori_loop(0, hps, exact_head, 0)
        else:
            def one_head(i, hd, lb, tb, exact, get):
                rf = pl.ds(pl.multiple_of(tb * BLK, BLK), BLK)
                rb = pl.ds(pl.multiple_of((n_blocks - 1 - tb) * BLK, BLK), BLK)
                cols = (slice(i * HEAD_DIM, (i + 1) * HEAD_DIM) if isinstance(i, int)
                        else pl.ds(pl.multiple_of(i * HEAD_DIM, HEAD_DIM), HEAD_DIM))
                q_f, q_b, v_f, v_b = get(i, 0, rf), get(i, 0, rb), get(i, 1, rf), get(i, 1, rb)
                prep_f = _scan_prep(q_f, get(i, 2, rf), lb[0:1, :], cf_ref[...], False, True)
                prep_b = _scan_prep(q_b, get(i, 3, rb), lb[1:2, :], cb_ref[...], True, True)
                if exact:
                    k_f, b_f = _scan_gates(get(i, 2, rf), lb[0:1, :], cf_ref[...])
                    k_b, b_b = _scan_gates(get(i, 3, rb), lb[1:2, :], cb_ref[...])
                    att_f = _scan_att_exact(q_f, k_f, b_f, False)
                    att_b = _scan_att_exact(q_b, k_b, b_b, True)
                else:
                    att_f, att_b = _scan_att(prep_f, lf_ref[...]), _scan_att(prep_b, lvb_ref[...])
                st_f, st_b = st_ref[2 * i], st_ref[2 * i + 1]
                of = _dot(att_f.astype(BF16), v_f.astype(BF16)) + _dot_nt(prep_f[2], st_f.astype(BF16))
                ob = _dot(att_b.astype(BF16), v_b.astype(BF16)) + _dot_nt(prep_b[2], st_b.astype(BF16))
                st_ref[2 * i] = _scan_state(prep_f, v_f, st_f)
                st_ref[2 * i + 1] = _scan_state(prep_b, v_b, st_b)

                @pl.when(2 * tb < n_blocks)
                def _():
                    oacc_ref[rf, cols] = of
                    oacc_ref[rb, cols] = ob

                @pl.when(2 * tb >= n_blocks)
                def _():
                    oacc_ref[rf, cols] += of
                    oacc_ref[rb, cols] += ob

                return jnp.maximum(prep_f[5], prep_b[5])

            for i, hd in enumerate(heads):
                st_ref[2 * i] = s0_ref[0, hd].T
                st_ref[2 * i + 1] = s0_ref[1, hd].T

            def body(tb, risk):
                for i, hd in enumerate(heads):
                    risk = jnp.maximum(risk, one_head(i, hd, lbs[i], tb, False, part))
                return risk

            risk = lax.fori_loop(0, n_blocks, body, jnp.zeros((1, LANES), F32))

            @pl.when(jnp.max(risk) > SCAN_GUARD)
            def _():
                def exact_head(i, c2):
                    hd = hp * hps + i
                    st_ref[2 * i] = s0_ref[0, hd].T
                    st_ref[2 * i + 1] = s0_ref[1, hd].T
                    lb = lb_ref[:, _lane_block(hd, HEAD_DIM)]

                    def exact_body(tb, c3):
                        one_head(i, hd, lb, tb, True, dyn_part)
                        return c3

                    lax.fori_loop(0, n_blocks, exact_body, 0)
                    return c2

                lax.fori_loop(0, hps, exact_head, 0)

            for i, hd in enumerate(heads):
                cols = slice(i * HEAD_DIM, (i + 1) * HEAD_DIM)
                og_ref[:, _lane_block(hd, HEAD_DIM)] = _scan_finish(oacc_ref[:, cols], part(i, 4), on)
        return carry

    _for_each_group(HEADS // hps, pair)
    xo_ref[...] = x_ref[...] + gate_ref[...] * _dot(og_ref[...], wout_ref[...])


def _hgrn_layer(x, n_seq, t, cond_of_seq, g, modr, w_in, lbs_j, onorm_g, w_out, s0, sfin_prev, slot):
    d = D_MODEL
    has_state = s0 is not None
    n_blocks = t // BLK
    assert n_blocks == 1 or n_blocks % 2 == 0
    full = lambda shape, **kw: pl.BlockSpec(shape, lambda b, *_: (0,) * len(shape), **kw)
    in_specs = [
        pl.BlockSpec((t, d), lambda b: (b, 0)),
        full((1, d)),
        _mod_spec(cond_of_seq, 0), _mod_spec(cond_of_seq, 1), _mod_spec(cond_of_seq, 2),
        full((d, HGRN_PARTS * d), **_RESIDENT),
        full((2, d)),
        full((1, HEAD_DIM)),
    ] + [full((BLK, BLK))] * 4
    args = [x, g.reshape(1, d), modr, modr, modr, w_in, lbs_j, onorm_g.reshape(1, HEAD_DIM)] + _scan_constants()
    state_block = (None, 2, HEADS, HEAD_DIM, HEAD_DIM)
    if has_state:
        in_specs.append(pl.BlockSpec(state_block, lambda b: (b, 0, 0, 0, 0)))
        args.append(s0)
    in_specs.append(full((d, d), **_RESIDENT))
    args.append(w_out)
    out_specs = [pl.BlockSpec((t, d), lambda b: (b, 0))]
    out_shape = [jax.ShapeDtypeStruct((n_seq * t, d), F32)]
    hps = SCAN_HEADS_PER_STEP if has_state else SCAN_HEADS_PER_BODY
    group_w = hps * HEAD_DIM
    scratch = [pltpu.VMEM((t, d), BF16), pltpu.VMEM((t, HGRN_PARTS * group_w), F32), pltpu.VMEM((t, d), BF16)]
    aliases = {}
    if has_state:
        scratch += [pltpu.VMEM((t, group_w), F32), pltpu.VMEM((2 * hps, HEAD_DIM, HEAD_DIM), F32)]
    else:
        state_dims = (2, HEADS, HEAD_DIM, HEAD_DIM)
        if sfin_prev is None:
            out_specs.append(pl.BlockSpec((None, N_SLOTS) + state_dims, lambda b: (b, 0, 0, 0, 0, 0)))
        else:
            out_specs.append(pl.BlockSpec((None, None) + state_dims, lambda b: (b, slot, 0, 0, 0, 0)))
            in_specs.append(pl.BlockSpec(memory_space=pl.ANY))
            args.append(sfin_prev)
            aliases = {len(args) - 1: 1}
        out_shape.append(jax.ShapeDtypeStruct((n_seq, N_SLOTS) + state_dims, F32))

    def body(*refs):
        if sfin_prev is not None:
            n_in = len(args)
            refs = refs[:n_in - 1] + refs[n_in:]
        _hgrn_kernel(*refs, n_blocks=n_blocks, has_state=has_state, slot=slot, owns_all_slots=sfin_prev is None,
                     hps=hps)

    out = pl.pallas_call(
        body,
        grid=(n_seq,),
        in_specs=in_specs,
        out_specs=out_specs,
        out_shape=out_shape,
        scratch_shapes=scratch,
        input_output_aliases=aliases,
        compiler_params=_cparams(("parallel",)),
        name="hgrn_layer_sample" if has_state else "hgrn_layer_prompt",
    )(*args)
    return out if not has_state else (out[0], None)


def _rope_tables(t_lat):
    rows = t_lat // GRID_W
    row = jnp.repeat(jnp.arange(rows), GRID_W).astype(F32)
    col = jnp.tile(jnp.arange(GRID_W), rows).astype(F32)
    half = QK_DIM // 2
    inv_freq = ROPE_THETA ** (-jnp.arange(0, half, 2, dtype=F32) / half)
    ang_row = row[:, None] * inv_freq
    ang_col = col[:, None] * inv_freq

    def part(ang):
        c, s = jnp.cos(ang), jnp.sin(ang)
        return jnp.concatenate([c, c], axis=1), jnp.concatenate([-s, s], axis=1)

    cr, sr = part(ang_row)
    cc, sc = part(ang_col)
    cos64 = jnp.concatenate([cr, cc], axis=1)
    sin64 = jnp.concatenate([sr, sc], axis=1)
    return jnp.concatenate([cos64, cos64], axis=1), jnp.concatenate([sin64, sin64], axis=1)


def _rope(x, cos, sin):
    lane = lax.broadcasted_iota(jnp.int32, x.shape, 1)
    first = (lane % (QK_DIM // 2)) < (QK_DIM // 4)
    swapped = jnp.where(first, pltpu.roll(x, LANES - QK_DIM // 4, 1), pltpu.roll(x, QK_DIM // 4, 1))
    return x * cos + swapped * sin


def _attn_kernel(*refs, n_qblk, n_sub, rope, cache, emit_kv, lam_init, slot, owns_all_slots):
    it = iter(refs)
    x_ref, g_ref, sh_ref, sc_ref, gate_ref, wqkv_ref = (next(it) for _ in range(6))
    qg_ref, kg_ref, lam_ref, sg_ref, seg_ref = (next(it) for _ in range(5))
    cos_ref = sin_ref = ck_ref = cv_ref = nk_ref = nv_ref = None
    if rope:
        cos_ref, sin_ref = next(it), next(it)
    if cache:
        ck_ref, cv_ref = next(it), next(it)
    wout_ref, xo_ref = next(it), next(it)
    if emit_kv:
        nk_all, nv_all = next(it), next(it)
        nk_refs = [_slot_view(nk_all.at[u], slot, owns_all_slots) for u in range(n_sub)]
        nv_refs = [_slot_view(nv_all.at[u], slot, owns_all_slots) for u in range(n_sub)]
    h_ref, qkv_ref, oa_ref = next(it), next(it), next(it)
    t = n_qblk * BLK

    d = D_MODEL
    for u in range(n_sub):
        seq = slice(u * t, (u + 1) * t)
        h_ref[seq, :] = _norm_mod(x_ref[seq, :], g_ref[...], sc_ref[...], sh_ref[...]).astype(BF16)
        qkv_ref[seq, :] = _dot(h_ref[seq, :], wqkv_ref[...])
    seg = seg_ref[...]
    lp = lam_ref[...]
    lam = (jnp.exp(jnp.sum(lp[0:1, :] * lp[1:2, :], axis=-1, keepdims=True))
           - jnp.exp(jnp.sum(lp[2:3, :] * lp[3:4, :], axis=-1, keepdims=True)) + lam_init)
    lane = lax.broadcasted_iota(jnp.int32, (1, LANES), 1)
    comp0 = lane < QK_DIM
    scale = QK_DIM ** -0.5

    def split(a):
        return [jnp.where(comp0, a, 0.0).astype(BF16), jnp.where(comp0, 0.0, a).astype(BF16)]

    def one_head(hd, u):
        hcol = _lane_block(hd, HEAD_DIM)
        seq = slice(u * t, (u + 1) * t)
        k = qkv_ref[seq, _block_at(d + hd * HEAD_DIM, HEAD_DIM)]
        v = qkv_ref[seq, _block_at(2 * d + hd * HEAD_DIM, HEAD_DIM)]
        kn = k * lax.rsqrt(_dot(k * k, seg) + EPS) * kg_ref[...]
        if emit_kv:
            nk_refs[u][:, hcol] = kn
            nv_refs[u][:, hcol] = v
        if rope:
            kn = _rope(kn, cos_ref[...], sin_ref[...])
        ks = split(kn)
        vb = v.astype(BF16)
        if cache:
            cks = split(ck_ref[:, hcol])
            cvb = cv_ref[:, hcol].astype(BF16)
        q_rows = BLK // 2 if n_qblk == 1 else BLK
        for qi in range(t // q_rows):
            pos = slice(qi * q_rows, (qi + 1) * q_rows)
            rows = slice(u * t + qi * q_rows, u * t + (qi + 1) * q_rows)
            q = qkv_ref[rows, hcol]
            qn = q * lax.rsqrt(_dot(q * q, seg) + EPS) * qg_ref[...]
            if rope:
                qn = _rope(qn, cos_ref[pos, :], sin_ref[pos, :])
            qb = (qn * scale).astype(BF16)
            a_self, a_cache = None, None
            for c in range(2):
                s = _dot_nt(qb, ks[c])
                m = jnp.max(s, axis=-1, keepdims=True)
                if cache:
                    sc = _dot_nt(qb, cks[c])
                    m = jnp.maximum(m, jnp.max(sc, axis=-1, keepdims=True))
                p = jnp.exp(s - m)
                den = jnp.sum(p, axis=-1, keepdims=True)
                if cache:
                    pc = jnp.exp(sc - m)
                    den = den + jnp.sum(pc, axis=-1, keepdims=True)
                w = (1.0 / den) if c == 0 else (-lam / den)
                a_self = p * w if c == 0 else a_self + p * w
                if cache:
                    a_cache = pc * w if c == 0 else a_cache + pc * w
            o = _dot(a_self.astype(BF16), vb)
            if cache:
                o = o + _dot(a_cache.astype(BF16), cvb)
            ms = jnp.mean(o * o, axis=-1, keepdims=True)
            o = (o * lax.rsqrt(ms + EPS) * sg_ref[...]) * (1.0 - lam_init)
            oa_ref[rows, hcol] = o.astype(BF16)

    heads_per_body = ATTN_HEADS_PER_BODY if n_qblk == 1 else HEADS_PER_STEP

    def group(hg, carry):
        for u in range(n_sub):
            for i in range(heads_per_body):
                one_head(hg * heads_per_body + i, u)
        return carry

    _for_each_group(HEADS // heads_per_body, group)
    for u in range(n_sub):
        seq = slice(u * t, (u + 1) * t)
        xo_ref[seq, :] = x_ref[seq, :] + gate_ref[...] * _dot(oa_ref[seq, :], wout_ref[...])


def _attn_layer(x, n_seq, t, cond_of_seq, g, modr, w_qkv, qn_g, kn_g, lam_p, subln_g, w_out, layer_idx,
                cache_k_j, cache_v_j, kv_prev, slot):
    d = D_MODEL
    cache = cache_k_j is not None
    lam_init = 0.8 - 0.6 * math.exp(-0.3 * layer_idx)
    qg = jnp.tile(qn_g.reshape(1, QK_DIM), (1, 2))
    kg = jnp.tile(kn_g.reshape(1, QK_DIM), (1, 2))
    li = np.arange(LANES)
    seg = jnp.asarray((li[:, None] // QK_DIM == li[None, :] // QK_DIM).astype(np.float32) / QK_DIM)
    full = lambda shape, **kw: pl.BlockSpec(shape, lambda b, *_: (0,) * len(shape), **kw)
    n_sub = 1 if cache else ATTN_SEQS_PER_STEP
    assert n_seq % n_sub == 0
    rows = n_sub * t
    in_specs = [
        pl.BlockSpec((rows, d), lambda b: (b, 0)),
        full((1, d)),
        _mod_spec(cond_of_seq, 0), _mod_spec(cond_of_seq, 1), _mod_spec(cond_of_seq, 2),
        full((d, 3 * d), **_RESIDENT),
        full((1, LANES)), full((1, LANES)), full((4, QK_DIM)), full((1, HEAD_DIM)), full((LANES, LANES)),
    ]
    args = [x, g.reshape(1, d), modr, modr, modr, w_qkv, qg, kg, lam_p, subln_g.reshape(1, HEAD_DIM), seg]
    if cache:
        cos, sin = _rope_tables(t)
        past = cache_k_j.shape[1]
        in_specs += [full((t, LANES)), full((t, LANES)),
                     pl.BlockSpec((None, past, d), lambda b: (b, 0, 0)),
                     pl.BlockSpec((None, past, d), lambda b: (b, 0, 0))]
        args += [cos, sin, cache_k_j.reshape(n_seq, past, d), cache_v_j.reshape(n_seq, past, d)]
    in_specs.append(full((d, d), **_RESIDENT))
    args.append(w_out)
    out_specs = [pl.BlockSpec((rows, d), lambda b: (b, 0))]
    out_shape = [jax.ShapeDtypeStruct((n_seq * t, d), F32)]
    aliases = {}
    n_carried = 0
    if not cache:
        if kv_prev is None:
            kv_spec = pl.BlockSpec((n_sub, N_SLOTS, t, d), lambda b: (b, 0, 0, 0))
        else:
            kv_spec = pl.BlockSpec((n_sub, None, t, d), lambda b: (b, slot, 0, 0))
            in_specs += [pl.BlockSpec(memory_space=pl.ANY)] * 2
            args += list(kv_prev)
            aliases = {len(args) - 2: 1, len(args) - 1: 2}
            n_carried = 2
        out_specs += [kv_spec, kv_spec]
        out_shape += [jax.ShapeDtypeStruct((n_seq, N_SLOTS, t, d), F32)] * 2

    def body(*refs):
        n_in = len(args)
        refs = refs[:n_in - n_carried] + refs[n_in:]
        _attn_kernel(*refs, n_qblk=t // BLK, n_sub=n_sub, rope=cache, cache=cache, emit_kv=not cache,
                     lam_init=lam_init, slot=slot, owns_all_slots=kv_prev is None)

    out = pl.pallas_call(
        body,
        grid=(n_seq // n_sub,),
        in_specs=in_specs,
        out_specs=out_specs,
        out_shape=out_shape,
        scratch_shapes=[pltpu.VMEM((rows, d), BF16), pltpu.VMEM((rows, 3 * d), F32), pltpu.VMEM((rows, d), BF16)],
        input_output_aliases=aliases,
        compiler_params=_cparams(("parallel",)),
        name="attn_layer_sample" if cache else "attn_layer_prompt",
    )(*args)
    return (out[0], None) if cache else (out[0], (out[1], out[2]))


ROUTE_ROWS = 32
INFO_GID, INFO_RANK = 8, 9


def _two_stream_specs(n_prompt_tiles, width):
    return [pl.BlockSpec((BLK, width), lambda i, *_: (jnp.minimum(i, n_prompt_tiles - 1), 0)),
            pl.BlockSpec((BLK, width), lambda i, *_: (jnp.maximum(i - n_prompt_tiles, 0), 0))]


def _route_kernel(xp_ref, xs_ref, g_ref, sh_ref, sc_ref, wr_ref, br_ref, tri_ref,
                  h_ref, il_ref, it_ref, cum_ref, cumhi_ref, tot_ref, carry_ref, *, n_prompt_tiles):
    i = pl.program_id(0)

    @pl.when(i == 0)
    def _():
        carry_ref[...] = jnp.zeros_like(carry_ref)

    @pl.when(i < n_prompt_tiles)
    def _():
        h_ref[...] = _norm_mod(xp_ref[...], g_ref[...], sc_ref[...], sh_ref[...]).astype(BF16)

    @pl.when(i >= n_prompt_tiles)
    def _():
        h_ref[...] = _norm_mod(xs_ref[...], g_ref[...], sc_ref[...], sh_ref[...]).astype(BF16)

    logit = _dot_nt(wr_ref[...], h_ref[...]) + br_ref[...]
    gl = [logit[g:g + 1, :] for g in range(MOE_GROUPS)]
    gmax = functools.reduce(jnp.maximum, gl)
    gz = functools.reduce(lambda a, b: a + b, [jnp.exp(x - gmax) for x in gl])
    g_w = 1.0 / gz
    gid = jnp.full_like(gmax, MOE_GROUPS - 1)
    for g in range(MOE_GROUPS - 2, -1, -1):
        gid = jnp.where(gl[g] == gmax, float(g), gid)
    el = []
    for j in range(MOE_EPG):
        e = logit[MOE_GROUPS + j:MOE_GROUPS + j + 1, :]
        for g in range(1, MOE_GROUPS):
            r = MOE_GROUPS + g * MOE_EPG + j
            e = jnp.where(gid == float(g), logit[r:r + 1, :], e)
        el.append(e)
    emax = functools.reduce(jnp.maximum, el)
    pe = [jnp.exp(e - emax) for e in el]
    idx1 = jnp.full_like(emax, MOE_EPG - 1)
    for j in range(MOE_EPG - 2, -1, -1):
        idx1 = jnp.where(el[j] == emax, float(j), idx1)
    el2 = [jnp.where(idx1 == float(j), -jnp.inf, el[j]) for j in range(MOE_EPG)]
    emax2 = functools.reduce(jnp.maximum, el2)
    idx2 = jnp.full_like(emax, MOE_EPG - 1)
    for j in range(MOE_EPG - 2, -1, -1):
        idx2 = jnp.where(el2[j] == emax2, float(j), idx2)
    sel = [(idx1 == float(j)) | (idx2 == float(j)) for j in range(MOE_EPG)]
    den = functools.reduce(lambda a, b: a + b, [jnp.where(sel[j], pe[j], 0.0) for j in range(MOE_EPG)])
    cw = [jnp.where(sel[j], pe[j] * (g_w / den), 0.0) for j in range(MOE_EPG)]

    row8 = lax.broadcasted_iota(jnp.int32, (8, BLK), 0)
    onehot = jnp.where(row8.astype(F32) == gid, 1.0, 0.0)
    within = _dot(onehot.astype(BF16), tri_ref[...])
    carry = carry_ref[...]
    rank = jnp.sum(onehot * (within + carry[:, 0:1]), axis=0, keepdims=True)
    cum_ref[...] = carry
    new_carry = carry + jnp.sum(onehot, axis=1, keepdims=True)
    carry_ref[...] = new_carry
    cumhi_ref[...] = new_carry
    tot_ref[...] = new_carry

    il_ref[...] = jnp.where(row8 == 0, gid, jnp.where(row8 == 1, rank, 0.0))
    rowl = lax.broadcasted_iota(jnp.int32, (LANES, BLK), 0)
    m = jnp.zeros((LANES, BLK), F32)
    for j in range(MOE_EPG):
        hi = cw[j].astype(BF16).astype(F32)
        m = jnp.where(rowl == j, hi, m)
        m = jnp.where(rowl == MOE_EPG + j, cw[j] - hi, m)
    m = jnp.where(rowl == INFO_GID, gid, m)
    m = jnp.where(rowl == INFO_RANK, rank, m)
    it_ref[...] = m.T


def _moe_route(xp, xs, g, modr, wr_t, br, cond_of_tile):
    d = D_MODEL
    npt = xp.shape[0] // BLK
    nt = npt + xs.shape[0] // BLK
    n = nt * BLK
    tri = jnp.asarray(np.triu(np.ones((BLK, BLK), np.float32), 1), BF16)
    full = lambda shape: pl.BlockSpec(shape, lambda i: (0,) * len(shape))
    return pl.pallas_call(
        functools.partial(_route_kernel, n_prompt_tiles=npt),
        grid=(nt,),
        in_specs=_two_stream_specs(npt, d) + [
            full((1, d)),
            _mod_spec(cond_of_tile, 3), _mod_spec(cond_of_tile, 4),
            full((ROUTE_ROWS, d)), full((ROUTE_ROWS, 1)), full((BLK, BLK)),
        ],
        out_specs=[
            pl.BlockSpec((BLK, d), lambda i: (i, 0)),
            pl.BlockSpec((8, BLK), lambda i: (0, i)),
            pl.BlockSpec((BLK, LANES), lambda i: (i, 0)),
            pl.BlockSpec((None, 8, LANES), lambda i: (i, 0, 0)),
            pl.BlockSpec((None, 8, LANES), lambda i: (i, 0, 0)),
            pl.BlockSpec((8, LANES), lambda i: (0, 0)),
        ],
        out_shape=[
            jax.ShapeDtypeStruct((n, d), BF16),
            jax.ShapeDtypeStruct((8, n), F32),
            jax.ShapeDtypeStruct((n, LANES), F32),
            jax.ShapeDtypeStruct((nt, 8, LANES), F32),
            jax.ShapeDtypeStruct((nt, 8, LANES), F32),
            jax.ShapeDtypeStruct((8, LANES), F32),
        ],
        scratch_shapes=[pltpu.VMEM((8, LANES), F32)],
        compiler_params=_cparams(("arbitrary",)),
        name="moe_route",
    )(xp, xs, g.reshape(1, d), modr, modr, wr_t, br, tri)


def _sorted_pos(gid, rank, rstart_ref):
    p = rank
    for g in range(MOE_GROUPS):
        p = p + jnp.where(gid == float(g), rstart_ref[g].astype(F32), 0.0)
    return p


def _gather_tile(tbl_ref, h_ref, il_ref, it_ref, hs_ref, cws_ref, a, used, *, n_tiles):
    win = GATHER_WIN * BLK
    rstart_ref = tbl_ref

    @pl.when(jnp.logical_not(used))
    def _():
        hs_ref[...] = jnp.zeros(hs_ref.shape, hs_ref.dtype)
        cws_ref[...] = jnp.zeros(cws_ref.shape, cws_ref.dtype)

    @pl.when(used)
    def _():
        dest = (lax.broadcasted_iota(jnp.int32, (BLK, 1), 0) + a * BLK).astype(F32)
        src_tile = lax.broadcasted_iota(jnp.int32, (1, win), 1) // BLK
        clo = tbl_ref[TBL_CLO * LANES + a]
        n_win = (tbl_ref[TBL_CHI * LANES + a] - clo + GATHER_WIN) // GATHER_WIN

        def window(w):
            first = clo + w * GATHER_WIN
            c0 = jnp.minimum(first, n_tiles - GATHER_WIN)
            rows = pl.ds(pl.multiple_of(c0 * BLK, BLK), win)
            info = il_ref[:, rows]
            p = _sorted_pos(info[0:1, :], info[1:2, :], rstart_ref)
            p = jnp.where(src_tile + c0 >= first, p, -1.0)
            onehot = jnp.where(dest == p, 1.0, 0.0).astype(BF16)
            r = _dot(onehot, it_ref[rows, :].astype(BF16))
            return _dot(onehot, h_ref[rows, :]), r + pltpu.roll(r, LANES - MOE_EPG, 1)

        dh, dc = window(0)
        hs_ref[...] = dh.astype(BF16)
        cws_ref[...] = dc

        def body(w, carry):
            dh, dc = window(w)
            hs_ref[...] = (hs_ref[...].astype(F32) + dh).astype(BF16)
            cws_ref[...] += dc
            return carry

        lax.fori_loop(1, n_win, body, 0)


def _gather_kernel(tbl_ref, h_ref, il_ref, it_ref, hs_ref, cws_ref, *, n_tiles, fill_unused):
    s = pl.program_id(0)
    nt = tbl_ref[TBL_SNT * LANES + s]

    def tile(j, carry):
        rows = pl.ds(pl.multiple_of(j * BLK, BLK), BLK)
        _gather_tile(tbl_ref, h_ref, il_ref, it_ref, hs_ref.at[rows, :], cws_ref.at[rows, :],
                     s * SUPER + j, j < nt, n_tiles=n_tiles)
        return carry

    lax.fori_loop(0, SUPER if fill_unused else jnp.where(nt > 0, SUPER, 0), tile, 0)


def _moe_gather(h, info_lane, info_tok, tbl, bufs, n_super):
    n, d = h.shape
    nt = n // BLK
    first = bufs is None
    assert nt >= GATHER_WIN and n_super * SUPER <= LANES
    out_block = (lambda s, tbl_r: (s, 0)) if first else (lambda s, tbl_r: (tbl_r[TBL_SBLK * LANES + s], 0))
    carried = [] if first else [pl.BlockSpec(memory_space=pl.ANY)] * 2

    def body(tbl_ref, h_ref, il_ref, it_ref, *rest):
        _gather_kernel(tbl_ref, h_ref, il_ref, it_ref, *rest[len(carried):], n_tiles=nt, fill_unused=first)

    rows = n_super * SUPER_ROWS
    return pl.pallas_call(
        body,
        grid_spec=pltpu.PrefetchScalarGridSpec(
            num_scalar_prefetch=1,
            grid=(n_super,),
            in_specs=[
                pl.BlockSpec((n, d), lambda a, *_: (0, 0), **_RESIDENT),
                pl.BlockSpec((8, n), lambda a, *_: (0, 0), **_RESIDENT),
                pl.BlockSpec((n, LANES), lambda a, *_: (0, 0), **_RESIDENT),
            ] + carried,
            out_specs=[pl.BlockSpec((SUPER_ROWS, d), out_block), pl.BlockSpec((SUPER_ROWS, LANES), out_block)],
        ),
        out_shape=[jax.ShapeDtypeStruct((rows, d), BF16), jax.ShapeDtypeStruct((rows, LANES), F32)],
        input_output_aliases={} if first else {4: 0, 5: 1},
        compiler_params=_cparams(("arbitrary",)),
        name="moe_gather",
    )(tbl, h, info_lane, info_tok, *(() if first else bufs))


def _moe_mlp_kernel(tbl_ref, hs_ref, cws_ref, wg_ref, wu_ref, wd_ref, ys_ref,
                    acc_ref, wgb_ref, wub_ref, wdb_ref, *, fill_unused):
    s = pl.program_id(0)
    k = pl.program_id(1)
    nt = tbl_ref[TBL_SNT * LANES + s]

    if fill_unused:
        @pl.when((nt == 0) & (k == 0))
        def _():
            ys_ref[...] = jnp.zeros_like(ys_ref)

    @pl.when(nt > 0)
    def _():
        wgb_ref[...] = wg_ref[...].astype(BF16)
        wub_ref[...] = wu_ref[...].astype(BF16)
        wdb_ref[...] = wd_ref[...].astype(BF16)

    @pl.when((s == 0) & (k == 0))
    def _():
        acc_ref[...] = jnp.zeros_like(acc_ref)

    def skipped(j):
        rows = slice(j * EXPERT_TILES * BLK, (j + 1) * EXPERT_TILES * BLK)
        ys_ref[rows, :] = jnp.zeros((EXPERT_TILES * BLK, D_MODEL), BF16)

    def block(j):
        rows = slice(j * EXPERT_TILES * BLK, (j + 1) * EXPERT_TILES * BLK)
        hsub = hs_ref[rows, :]
        cws = cws_ref[rows, :]
        cwk = jnp.zeros((EXPERT_TILES * BLK, 1), F32)
        for kk in range(MOE_EPG):
            cwk = jnp.where(k == kk, cws[:, kk:kk + 1], cwk)
        y = None
        for fh in range(MOE_D_FF // FF_PART):
            fc = slice(fh * FF_PART, (fh + 1) * FF_PART)
            gate = _dot(hsub, wgb_ref[:, fc])
            up = _dot(hsub, wub_ref[:, fc])
            act = ((_silu(gate) * up) * cwk).astype(BF16)
            part = _dot(act, wdb_ref[fc, :])
            y = part if y is None else y + part
        total = jnp.where(k == 0, 0.0, acc_ref[rows, :]) + y
        acc_ref[rows, :] = total
        ys_ref[rows, :] = total.astype(BF16)

    def maybe_block(j):
        pl.when(j * EXPERT_TILES < nt)(functools.partial(block, j))
        pl.when((j * EXPERT_TILES >= nt) & (nt > 0) & (k == 0))(functools.partial(skipped, j))

    n_blocks = SUPER // EXPERT_TILES
    n_straight = STRAIGHT_TILES // EXPERT_TILES

    @pl.when(nt >= STRAIGHT_TILES)
    def _():
        for j in range(n_straight):
            block(j)

    @pl.when(nt < STRAIGHT_TILES)
    def _():
        for j in range(n_straight):
            maybe_block(j)

    for j in range(n_straight, n_blocks):
        maybe_block(j)


def _moe_mlp(hs, cws, w_gate, w_up, w_down, layer, tbl, ys_buf):
    d = hs.shape[1]
    n_super = hs.shape[0] // SUPER_ROWS
    first = ys_buf is None
    carried = [] if first else [pl.BlockSpec(memory_space=pl.ANY)]

    def body(tbl_ref, hs_ref, cws_ref, wg_ref, wu_ref, wd_ref, *rest):
        _moe_mlp_kernel(tbl_ref, hs_ref, cws_ref, wg_ref, wu_ref, wd_ref, *rest[len(carried):], fill_unused=first)

    def widx(s, k, tbl_r):
        kk = jnp.where(tbl_r[TBL_SNT * LANES + s] > 0, k, MOE_EPG - 1)
        return (layer * MOE_EXPERTS + tbl_r[TBL_SGROUP * LANES + s] * MOE_EPG + kk, 0, 0)

    rows_idx = lambda s, k, tbl_r: (tbl_r[TBL_SBLK * LANES + s], 0)
    out_idx = (lambda s, k, tbl_r: (s, 0)) if first else rows_idx
    return pl.pallas_call(
        body,
        grid_spec=pltpu.PrefetchScalarGridSpec(
            num_scalar_prefetch=1,
            grid=(n_super, MOE_EPG),
            in_specs=[
                pl.BlockSpec((SUPER_ROWS, d), rows_idx),
                pl.BlockSpec((SUPER_ROWS, LANES), rows_idx),
                pl.BlockSpec((None, d, MOE_D_FF), widx),
                pl.BlockSpec((None, d, MOE_D_FF), widx),
                pl.BlockSpec((None, MOE_D_FF, d), widx),
            ] + carried,
            out_specs=pl.BlockSpec((SUPER_ROWS, d), out_idx),
            scratch_shapes=[
                pltpu.VMEM((SUPER_ROWS, d), F32),
                pltpu.VMEM((d, MOE_D_FF), BF16),
                pltpu.VMEM((d, MOE_D_FF), BF16),
                pltpu.VMEM((MOE_D_FF, d), BF16),
            ],
        ),
        out_shape=jax.ShapeDtypeStruct(hs.shape, BF16),
        input_output_aliases={} if first else {6: 0},
        compiler_params=_cparams(("arbitrary", "arbitrary")),
        name="moe_experts",
    )(tbl, hs, cws, w_gate, w_up, w_down, *(() if first else (ys_buf,)))


N_SRC = 2 * MOE_GROUPS


def _unsort_kernel(tbl_ref, xp_ref, xs_ref, it_ref, gate_ref, *rest, n_prompt_tiles):
    ys_refs, op_ref, os_ref = rest[:N_SRC], rest[N_SRC], rest[N_SRC + 1]
    t = pl.program_id(0)
    info = it_ref[...]
    p = _sorted_pos(info[:, INFO_GID:INFO_GID + 1], info[:, INFO_RANK:INFO_RANK + 1], tbl_ref)
    lane = lax.broadcasted_iota(jnp.int32, (1, BLK), 1).astype(F32)
    slot = lambda m: tbl_ref[(TBL_UT + t) * LANES + m]

    def take(m):
        a = slot(m)
        onehot = jnp.where(p - (a * BLK).astype(F32) == lane, 1.0, 0.0).astype(BF16)
        return _dot(onehot, ys_refs[m][...])

    def stream(x_ref, o_ref):
        first = functools.reduce(lambda u, w: u + w, [take(m) for m in range(0, N_SRC, 2)])
        o_ref[...] = x_ref[...] + gate_ref[...] * first
        for m in range(1, N_SRC, 2):
            @pl.when(slot(m) >= 0)
            def _():
                o_ref[...] += gate_ref[...] * take(m)

    pl.when(t < n_prompt_tiles)(functools.partial(stream, xp_ref, op_ref))
    pl.when(t >= n_prompt_tiles)(functools.partial(stream, xs_ref, os_ref))


def _moe_unsort(xp, xs, info_tok, modr, ys, tbl, cond_of_tile):
    d = D_MODEL
    npt = xp.shape[0] // BLK
    nt = npt + xs.shape[0] // BLK

    def ys_spec(m):
        return pl.BlockSpec((BLK, d), lambda t, tbl_r: (jnp.maximum(tbl_r[(TBL_UT + t) * LANES + m], 0), 0))

    return pl.pallas_call(
        functools.partial(_unsort_kernel, n_prompt_tiles=npt),
        grid_spec=pltpu.PrefetchScalarGridSpec(
            num_scalar_prefetch=1,
            grid=(nt,),
            in_specs=_two_stream_specs(npt, d) + [
                pl.BlockSpec((BLK, LANES), lambda t, *_: (t, 0)),
                _mod_spec(cond_of_tile, 5),
            ] + [ys_spec(m) for m in range(N_SRC)],
            out_specs=_two_stream_specs(npt, d),
        ),
        out_shape=[jax.ShapeDtypeStruct(xp.shape, F32), jax.ShapeDtypeStruct(xs.shape, F32)],
        compiler_params=_cparams(("arbitrary",)),
        name="moe_unsort_residual",
    )(tbl, xp, xs, info_tok, modr, *([ys] * N_SRC))


def _tables_kernel(lo_ref, hi_ref, tot_ref, tbl_ref, *, n_tiles, n_super):
    one = lambda cond: jnp.where(cond, 1.0, 0.0)
    groups = range(MOE_GROUPS)
    lane = lax.broadcasted_iota(jnp.int32, (1, LANES), 1).astype(F32)
    tot = [tot_ref[g:g + 1, :] for g in groups]
    pick = lambda vals, idx: sum(jnp.where(idx == float(g), vals[g], 0.0) for g in groups)

    n_sup = [sum(one(tot[g] > float(m * SUPER_ROWS)) for m in range(n_super)) for g in groups]
    sup_start, sup_end, run = [], [], 0.0
    for g in groups:
        sup_start.append(run + 0.0 * tot[g])
        run = run + n_sup[g]
        sup_end.append(run)
    n_used = sup_end[-1]
    rstart = [sup_start[g] * float(SUPER_ROWS) for g in groups]
    group_of = lambda s: jnp.minimum(sum(one(s >= sup_end[g]) for g in groups), float(MOE_GROUPS - 1))

    sg, snt = [], []
    for m in range(n_super):
        used = float(m) < n_used
        g_m = group_of(jnp.where(used, float(m), n_used - 1.0))
        rows_left = pick(tot, g_m) - (float(m) - pick(sup_start, g_m)) * float(SUPER_ROWS)
        nt_m = jnp.where(used, jnp.clip(jnp.floor((rows_left + float(BLK - 1)) * (1.0 / BLK)), 0.0, float(SUPER)), 0.0)
        sg.append(g_m)
        snt.append(nt_m)
    sblk = [jnp.minimum(float(m), n_used - 1.0) for m in range(n_super)]
    by_super_lane = lambda vals: sum(jnp.where(lane == float(m), vals[m], 0.0) for m in range(n_super))

    sa = sum(one(lane >= float(m * SUPER)) for m in range(1, n_super))
    ja = lane - sa * float(SUPER)
    ga = sum(jnp.where(sa == float(m), sg[m], 0.0) for m in range(n_super))
    r0 =((sa - pick(sup_start, ga)) * float(SUPER) + ja) * float(BLK)
    lo = [lo_ref[:, g, :] for g in groups]
    hi = [hi_ref[:, g, :] for g in groups]
    clo = jnp.sum(one(pick(hi, ga) <= r0), axis=0, keepdims=True)
    chi = jnp.sum(one(pick(lo, ga) < r0 + float(BLK)), axis=0, keepdims=True) - 1.0
    clo = jnp.clip(clo, 0.0, float(n_tiles - 1))
    chi = jnp.clip(chi, clo, float(n_tiles - 1))

    gm = one(lane >= 2.0) + one(lane >= 4.0) + one(lane >= 6.0)
    first = pick(rstart, gm) + pick(lo, gm)
    last = pick(rstart, gm) + pick(hi, gm) - 1.0
    t0 = jnp.floor(first * (1.0 / BLK))
    t1 = jnp.floor(last * (1.0 / BLK))
    has = pick(hi, gm) > pick(lo, gm)
    a0 = jnp.where(has, t0, -1.0)
    a1 = jnp.where(has, jnp.where(t1 != t0, t1, -1.0), -1.0)
    ut = jnp.where(lane >= float(N_SRC), -1.0, jnp.where(lane - 2.0 * gm == 0.0, a0, a1))

    def put(row, v):
        tbl_ref[row:row + 1, :] = v.astype(jnp.int32)

    put(TBL_RSTART, sum(jnp.where(lane == float(g), rstart[g], 0.0) for g in groups))
    put(TBL_SGROUP, by_super_lane(sg))
    put(TBL_SNT, by_super_lane(snt))
    put(TBL_SBLK, by_super_lane(sblk))
    put(TBL_CLO, clo)
    put(TBL_CHI, chi)
    tbl_ref[TBL_CHI + 1:TBL_UT, :] = jnp.zeros((TBL_UT - TBL_CHI - 1, LANES), jnp.int32)
    tbl_ref[TBL_UT:TBL_UT + n_tiles, :] = ut.astype(jnp.int32)


def _moe_tables(cum_lo, cum_hi, tot, n_tiles, n_super):
    assert n_super * SUPER <= LANES
    tbl = pl.pallas_call(
        functools.partial(_tables_kernel, n_tiles=n_tiles, n_super=n_super),
        out_shape=jax.ShapeDtypeStruct((TBL_UT + n_tiles, LANES), jnp.int32),
        name="moe_tables",
    )(cum_lo, cum_hi, tot)
    return tbl.reshape(-1)


def _moe_layer(xp, xs, bufs, g, modr, wr_t, br, w_gate, w_up, w_down, layer, cond_of_tile):
    n = xp.shape[0] + xs.shape[0]
    nt = n // BLK
    n_super = (n - 1) // SUPER_ROWS + MOE_GROUPS
    h, info_lane, info_tok, cum_lo, cum_hi, tot = _moe_route(xp, xs, g, modr, wr_t, br, cond_of_tile)
    tbl = _moe_tables(cum_lo, cum_hi, tot, nt, n_super)
    hs, cws = _moe_gather(h, info_lane, info_tok, tbl, None if bufs is None else bufs[:2], n_super)
    ys = _moe_mlp(hs, cws, w_gate, w_up, w_down, layer, tbl, None if bufs is None else bufs[2])
    xp, xs = _moe_unsort(xp, xs, info_tok, modr, ys, tbl, cond_of_tile)
    return xp, xs, (hs, cws, ys)


def kernel(x_prompt, x_sample, c, cache_k, cache_v, state_hgrn, c_ctx, norm_g, w_ada, b_ada, hgrn_w_in, hgrn_lb_logits, hgrn_onorm_g, hgrn_w_out, attn_w_qkv, attn_qn_g, attn_kn_g, attn_lambda, attn_subln_g, attn_w_out, moe_w_group, moe_b_group, moe_w_expert, moe_b_expert, moe_w_gate, moe_w_up, moe_w_down):
    n_prompt_seq, seq, d = x_prompt.shape
    dec_batch, dec_seq, _ = x_sample.shape
    n_prompt = n_prompt_seq * seq
    assert d == D_MODEL and seq == BLK and dec_seq % BLK == 0
    assert 1 + dec_batch <= N_COND

    xp = x_prompt.reshape(n_prompt, d)
    xs = x_sample.reshape(dec_batch * dec_seq, d)
    cond = jnp.zeros((N_COND, d), F32).at[0].set(c_ctx).at[1:1 + dec_batch].set(c)
    mod = _modulation(cond, w_ada, b_ada)

    lbs = jnp.cumsum(jax.nn.softmax(hgrn_lb_logits.astype(F32), axis=0), axis=0)
    lbs = lbs - lbs[0:1]

    cond_prompt = lambda b: 0
    cond_sample = lambda b: 1 + b
    npt, spt = n_prompt // BLK, dec_seq // BLK
    cond_tile = lambda i: jnp.where(i < npt, 0, 1 + (i - npt) // spt)

    w_in = hgrn_w_in.astype(BF16)
    w_hout = hgrn_w_out.astype(BF16)
    w_qkv = attn_w_qkv.astype(BF16)
    w_aout = attn_w_out.astype(BF16)
    wr_t = jnp.zeros((DEPTH, ROUTE_ROWS, d), F32)
    wr_t = wr_t.at[:, :MOE_GROUPS].set(moe_w_group.transpose(0, 2, 1))
    wr_t = wr_t.at[:, MOE_GROUPS:MOE_GROUPS + MOE_EXPERTS].set(moe_w_expert.transpose(0, 2, 1)).astype(BF16)
    br = jnp.zeros((DEPTH, ROUTE_ROWS, 1), F32)
    br = br.at[:, :MOE_GROUPS, 0].set(moe_b_group).at[:, MOE_GROUPS:MOE_GROUPS + MOE_EXPERTS, 0].set(moe_b_expert)
    w_gate = moe_w_gate.reshape(DEPTH * MOE_EXPERTS, d, MOE_D_FF)
    w_up = moe_w_up.reshape(DEPTH * MOE_EXPERTS, d, MOE_D_FF)
    w_down = moe_w_down.reshape(DEPTH * MOE_EXPERTS, MOE_D_FF, d)

    sfin, kv = None, None
    bufs = None
    for i in range(DEPTH):
        j = i // 2
        modr = mod[i].reshape(N_COND * N_MOD, 1, d)
        if i % 2 == 0:
            common = (norm_g[i, 0], modr, w_in[j], lbs[j], hgrn_onorm_g[j], w_hout[j])
            xp, sfin = _hgrn_layer(xp, n_prompt_seq, seq, cond_prompt, *common, None, sfin, j)
            xs, _ = _hgrn_layer(xs, dec_batch, dec_seq, cond_sample, *common, state_hgrn[:, j], None, j)
        else:
            common = (norm_g[i, 0], modr, w_qkv[j], attn_qn_g[j], attn_kn_g[j], attn_lambda[j], attn_subln_g[j],
                      w_aout[j], i)
            xp, kv = _attn_layer(xp, n_prompt_seq, seq, cond_prompt, *common, None, None, kv, j)
            xs, _ = _attn_layer(xs, dec_batch, dec_seq, cond_sample, *common, cache_k[:, j], cache_v[:, j], None, j)
        xp, xs, bufs = _moe_layer(xp, xs, bufs, norm_g[i, 1], modr, wr_t[i], br[i], w_gate, w_up, w_down, i, cond_tile)

    new_k = kv[0].reshape(n_prompt_seq, DEPTH // 2, seq, HEADS, 2, QK_DIM)
    new_v = kv[1].reshape(n_prompt_seq, DEPTH // 2, seq, HEADS, HEAD_DIM)
    return (xp.reshape(n_prompt_seq, seq, d), xs.reshape(dec_batch, dec_seq, d), new_k, new_v, sfin)
```
